```python
import jax, jax.numpy as jnp
from jax import lax
import numpy as np

D_MODEL = 1024
BATCH = 16
SEQ = 2048
DEPTH = 1
DEC_BATCH = 32
DEC_SEQ = 32
PAST_LEN = 4096

CHUNK = 64
EPS = 1e-6
H_A = 8
Q_LORA = 384
KV_LORA = 256
NOPE = 64
ROPE = 32
V_A = 64
ROPE_THETA = 10000.0
Q_BLOCK = 128
MLA_SCALE = (NOPE + ROPE) ** -0.5
H_B = 8
DH_B = 64
BAND_PREV = 8
BAND = (BAND_PREV + 1) * CHUNK
WINDOW_B = BAND_PREV * CHUNK
REL_CLIP = 256
BAND_SCALE = DH_B ** -0.5
PEER_HEADS = 8
N_KEYS = 128
N_EXPERTS = N_KEYS * N_KEYS
D_KEY = 256
TOPK = 16
PEER_BLOCK = 128
PLE_DIM = 256
IN_SPLITS = (Q_LORA, KV_LORA, ROPE, H_B * DH_B, H_B * DH_B, H_B * DH_B, D_MODEL, D_MODEL)
IN_COLS = sum(IN_SPLITS)

kernel_name = "hybrid_mla_band_peer_stream_step"


def rmsnorm(x, g):
    xf = x.astype(jnp.float32)
    y = xf * lax.rsqrt(jnp.mean(xf * xf, axis=-1, keepdims=True) + EPS)
    return (y * g.astype(jnp.float32)).astype(x.dtype)


def rope(x, pos):
    half = ROPE // 2
    freqs = ROPE_THETA ** (-jnp.arange(half, dtype=jnp.float32) / half)
    ang = pos.astype(jnp.float32)[:, None] * freqs[None, :]
    shape = (ang.shape[0],) + (1,) * (x.ndim - 3) + (half,)
    cos = jnp.cos(ang).reshape(shape)
    sin = jnp.sin(ang).reshape(shape)
    xf = x.astype(jnp.float32)
    x1, x2 = xf[..., :half], xf[..., half:]
    return jnp.concatenate([x1 * cos - x2 * sin, x1 * sin + x2 * cos], axis=-1).astype(x.dtype)


def mixer_inputs(h, pos, w_in, g_q, g_kv, w_uq, w_uk):
    B, T, _ = h.shape
    offsets = [int(o) for o in np.cumsum(IN_SPLITS)[:-1]]
    c_q, c_kv, k_r, q_b, k_b, v_b, gate_a, gate_b = jnp.split(h @ w_in, offsets, axis=-1)
    q = (rmsnorm(c_q, g_q) @ w_uq).reshape(B, T, H_A, NOPE + ROPE)
    q_rope = rope(q[..., NOPE:], pos)
    q_lat = jnp.einsum('bthn,chn->bthc', q[..., :NOPE], w_uk)
    c_kv = rmsnorm(c_kv, g_kv)
    k_r = rope(k_r, pos)
    q_b = q_b.reshape(B, T, H_B, DH_B)
    k_b = k_b.reshape(B, T, H_B, DH_B)
    v_b = v_b.reshape(B, T, H_B, DH_B)
    return q_lat, q_rope, c_kv, k_r, q_b, k_b, v_b, gate_a, gate_b


def mla_attend(q_lat, q_rope, c_kv, k_rope, mask):
    s = (jnp.einsum('bqhc,bkc->bhqk', q_lat, c_kv)
         + jnp.einsum('bqhr,bkr->bhqk', q_rope, k_rope)).astype(jnp.float32) * MLA_SCALE
    if mask is not None:
        s = jnp.where(mask, s, -1e30)
    w = jax.nn.softmax(s, axis=-1).astype(c_kv.dtype)
    return jnp.einsum('bhqk,bkc->bqhc', w, c_kv)


def mla_prompt(q_lat, q_rope, c_kv, k_rope):
    B, S = q_lat.shape[:2]
    nb = S // Q_BLOCK
    k_chunk = jnp.arange(S) // CHUNK
    ql = q_lat.reshape(B, nb, Q_BLOCK, H_A, KV_LORA).swapaxes(0, 1)
    qr = q_rope.reshape(B, nb, Q_BLOCK, H_A, ROPE).swapaxes(0, 1)

    def one_block(args):
        i, ql_b, qr_b = args
        q_chunk = (i * Q_BLOCK + jnp.arange(Q_BLOCK)) // CHUNK
        mask = k_chunk[None, :] <= q_chunk[:, None]
        return mla_attend(ql_b, qr_b, c_kv, k_rope, mask)

    o = lax.map(one_block, (jnp.arange(nb), ql, qr))
    return o.swapaxes(0, 1).reshape(B, S, H_A, KV_LORA)


def rel_bias_lookup(table, q_pos, k_pos):
    rel = jnp.clip(q_pos[:, None] - k_pos[None, :], -REL_CLIP, REL_CLIP) + REL_CLIP
    return table[:, rel]


def band_attend(q, k, v, bias, mask):
    s = jnp.einsum('bqhd,bkhd->bhqk', q, k).astype(jnp.float32) * BAND_SCALE + bias.astype(jnp.float32)
    if mask is not None:
        s = jnp.where(mask, s, -1e30)
    w = jax.nn.softmax(s, axis=-1).astype(v.dtype)
    return jnp.einsum('bhqk,bkhd->bqhd', w, v)


def band_prompt(q, k, v, table):
    B, S = q.shape[:2]
    nc = S // CHUNK
    pad = ((0, 0), (WINDOW_B, 0), (0, 0), (0, 0))
    k_pad = jnp.pad(k, pad)
    v_pad = jnp.pad(v, pad)
    q_c = q.reshape(B, nc, CHUNK, H_B, DH_B).swapaxes(0, 1)
    bias = rel_bias_lookup(table, jnp.arange(CHUNK) + WINDOW_B, jnp.arange(BAND))

    def one_chunk(args):
        c, q_blk = args
        start = c * CHUNK
        k_blk = lax.dynamic_slice_in_dim(k_pad, start, BAND, axis=1)
        v_blk = lax.dynamic_slice_in_dim(v_pad, start, BAND, axis=1)
        valid = (start - WINDOW_B + jnp.arange(BAND)) >= 0
        return band_attend(q_blk, k_blk, v_blk, bias, valid[None, :])

    o = lax.map(one_chunk, (jnp.arange(nc), q_c))
    return o.swapaxes(0, 1).reshape(B, S, H_B * DH_B)


def band_sample(q, k, v, cache_k, cache_v, table, past_len):
    B, T = q.shape[:2]
    Lc = cache_k.shape[1]
    k_all = jnp.concatenate([cache_k, k], axis=1)
    v_all = jnp.concatenate([cache_v, v], axis=1)
    q_pos = past_len + jnp.arange(T)
    k_pos = past_len - Lc + jnp.arange(Lc + T)
    bias = rel_bias_lookup(table, q_pos, k_pos)
    o = band_attend(q, k_all, v_all, bias, None)
    return o.reshape(B, T, H_B * DH_B), k_all[:, T:], v_all[:, T:]


def peer_ffn(h, w_query, sub_keys, expert_u, expert_v):
    B, T, D = h.shape
    n = B * T
    nb = -(-n // PEER_BLOCK)
    x = jnp.pad(h.reshape(n, D), ((0, nb * PEER_BLOCK - n), (0, 0))).reshape(nb, PEER_BLOCK, D)

    def one_block(xb):
        q = (xb @ w_query).reshape(PEER_BLOCK, PEER_HEADS, 2, D_KEY // 2)
        s = jnp.einsum('nhpd,hpkd->nhpk', q, sub_keys)
        top_s, top_i = lax.top_k(s, TOPK)
        cand = top_s[:, :, 0, :, None] + top_s[:, :, 1, None, :]
        best_s, best_c = lax.top_k(cand.reshape(PEER_BLOCK, PEER_HEADS, TOPK * TOPK), TOPK)
        i1 = jnp.take_along_axis(top_i[:, :, 0], best_c // TOPK, axis=-1)
        i2 = jnp.take_along_axis(top_i[:, :, 1], best_c % TOPK, axis=-1)
        idx = i1 * N_KEYS + i2
        g = jax.nn.softmax(best_s.astype(jnp.float32), axis=-1).astype(xb.dtype)
        a = jax.nn.gelu(jnp.einsum('nd,nhkd->nhk', xb, expert_u[idx]), approximate=False)
        return jnp.einsum('nhk,nhkd->nd', g * a, expert_v[idx])

    y = lax.map(one_block, x).reshape(nb * PEER_BLOCK, D)[:n]
    return y.reshape(B, T, D)


def trunk_layer(x, p, past, W):
    (g_mix, w_in, g_q, g_kv, w_uq, w_uk, w_uv, rel_tab, w_a_proj, w_b_proj, w_out,
     g_ffn, w_query, sub_keys, eu, ev, g_ple, w_ple_gate, w_ple_proj) = W
    B, T, _ = x.shape
    start = 0 if past is None else past[0].shape[1]
    pos = start + jnp.arange(T)
    h = rmsnorm(x, g_mix)
    q_lat, q_rope, c_kv, k_r, q_b, k_b, v_b, gate_a, gate_b = mixer_inputs(h, pos, w_in, g_q, g_kv, w_uq, w_uk)
    if past is None:
        o_lat = mla_prompt(q_lat, q_rope, c_kv, k_r)
        o_b = band_prompt(q_b, k_b, v_b, rel_tab)
        wb = min(WINDOW_B, T)
        new_bk, new_bv = k_b[:, T - wb:], v_b[:, T - wb:]
    else:
        cache_lat, cache_kr, cache_bk, cache_bv = past
        o_lat = mla_attend(q_lat, q_rope,
                           jnp.concatenate([cache_lat, c_kv], axis=1),
                           jnp.concatenate([cache_kr, k_r], axis=1), None)
        o_b, new_bk, new_bv = band_sample(q_b, k_b, v_b, cache_bk, cache_bv, rel_tab, start)
    o_a = jnp.einsum('bthc,chv->bthv', o_lat, w_uv).reshape(B, T, H_A * V_A)
    merged = jax.nn.sigmoid(gate_a) * (o_a @ w_a_proj) + jax.nn.sigmoid(gate_b) * (o_b @ w_b_proj)
    x = x + merged @ w_out
    x = x + peer_ffn(rmsnorm(x, g_ffn), w_query, sub_keys, eu, ev)
    x = x + jax.nn.sigmoid(rmsnorm(x, g_ple) @ w_ple_gate) * (p @ w_ple_proj)
    return x, (c_kv, k_r, new_bk, new_bv)


def setup_inputs(seed: int = 0) -> dict:
    key = jax.random.key(seed)
    ks = iter(jax.random.split(key, 40))
    nrm = lambda shape, s=1.0: jax.random.normal(next(ks), shape, jnp.float32) * s
    gain = lambda shape: 1.0 + nrm(shape, 0.05)
    wb = min(WINDOW_B, PAST_LEN)
    return {
        "x_prompt": nrm((BATCH, SEQ, D_MODEL)),
        "x_sample": nrm((DEC_BATCH, DEC_SEQ, D_MODEL)),
        "cache_latent": nrm((DEPTH, DEC_BATCH, PAST_LEN, KV_LORA)),
        "cache_krope": nrm((DEPTH, DEC_BATCH, PAST_LEN, ROPE)),
        "cache_band_k": nrm((DEPTH, DEC_BATCH, wb, H_B, DH_B)),
        "cache_band_v": nrm((DEPTH, DEC_BATCH, wb, H_B, DH_B)),
        "p_prompt": nrm((DEPTH, BATCH, SEQ, PLE_DIM)),
        "p_sample": nrm((DEPTH, DEC_BATCH, DEC_SEQ, PLE_DIM)),
        "g_mix": gain((DEPTH, D_MODEL)),
        "w_in": nrm((DEPTH, D_MODEL, IN_COLS), D_MODEL ** -0.5),
        "g_q_lora": gain((DEPTH, Q_LORA)),
        "g_kv_lora": gain((DEPTH, KV_LORA)),
        "w_uq": nrm((DEPTH, Q_LORA, H_A * (NOPE + ROPE)), Q_LORA ** -0.5),
        "w_uk": nrm((DEPTH, KV_LORA, H_A, NOPE), KV_LORA ** -0.5),
        "w_uv": nrm((DEPTH, KV_LORA, H_A, V_A), KV_LORA ** -0.5),
        "rel_bias": nrm((DEPTH, H_B, 2 * REL_CLIP + 1), 0.1),
        "w_a_proj": nrm((DEPTH, H_A * V_A, D_MODEL), (H_A * V_A) ** -0.5),
        "w_b_proj": nrm((DEPTH, H_B * DH_B, D_MODEL), (H_B * DH_B) ** -0.5),
        "w_out": nrm((DEPTH, D_MODEL, D_MODEL), D_MODEL ** -0.5),
        "g_ffn": gain((DEPTH, D_MODEL)),
        "w_query": nrm((DEPTH, D_MODEL, PEER_HEADS * D_KEY), D_MODEL ** -0.5),
        "sub_keys": nrm((DEPTH, PEER_HEADS, 2, N_KEYS, D_KEY // 2), (D_KEY // 2) ** -0.5),
        "expert_u": nrm((DEPTH, N_EXPERTS, D_MODEL), D_MODEL ** -0.5),
        "expert_v": nrm((DEPTH, N_EXPERTS, D_MODEL), 0.5),
        "g_ple": gain((DEPTH, D_MODEL)),
        "w_ple_gate": nrm((DEPTH, D_MODEL, D_MODEL), D_MODEL ** -0.5),
        "w_ple_proj": nrm((DEPTH, PLE_DIM, D_MODEL), PLE_DIM ** -0.5),
        "g_final": gain((D_MODEL,)),
    }


def reference(x_prompt, x_sample, cache_latent, cache_krope, cache_band_k, cache_band_v,
              p_prompt, p_sample, g_mix, w_in, g_q_lora, g_kv_lora, w_uq, w_uk, w_uv,
              rel_bias, w_a_proj, w_b_proj, w_out, g_ffn, w_query, sub_keys, expert_u,
              expert_v, g_ple, w_ple_gate, w_ple_proj, g_final):
    xp, xs = x_prompt, x_sample
    st_p, st_s = [], []
    for i in range(DEPTH):
        W = (g_mix[i], w_in[i], g_q_lora[i], g_kv_lora[i], w_uq[i], w_uk[i], w_uv[i],
             rel_bias[i], w_a_proj[i], w_b_proj[i], w_out[i], g_ffn[i], w_query[i],
             sub_keys[i], expert_u[i], expert_v[i], g_ple[i], w_ple_gate[i], w_ple_proj[i])
        xp, sp = trunk_layer(xp, p_prompt[i], None, W)
        xs, ss = trunk_layer(xs, p_sample[i],
                             (cache_latent[i], cache_krope[i], cache_band_k[i], cache_band_v[i]), W)
        st_p.append(sp)
        st_s.append(ss)
    y_prompt = rmsnorm(xp, g_final)
    y_sample = rmsnorm(xs, g_final)
    latent_prompt = jnp.stack([s[0] for s in st_p])
    krope_prompt = jnp.stack([s[1] for s in st_p])
    band_k_prompt = jnp.stack([s[2] for s in st_p])
    band_v_prompt = jnp.stack([s[3] for s in st_p])
    latent_sample = jnp.stack([s[0] for s in st_s])
    krope_sample = jnp.stack([s[1] for s in st_s])
    band_k_sample = jnp.stack([s[2] for s in st_s])
    band_v_sample = jnp.stack([s[3] for s in st_s])
    return (y_prompt, y_sample, latent_prompt, krope_prompt, band_k_prompt, band_v_prompt,
            latent_sample, krope_sample, band_k_sample, band_v_sample)
```

```python
import functools
import jax
import jax.numpy as jnp
import numpy as np
from jax import lax
from jax.experimental import pallas as pl
from jax.experimental.pallas import tpu as pltpu

D_MODEL = 1024
BATCH = 16
SEQ = 2048
DEC_BATCH = 32
DEC_SEQ = 32
PAST_LEN = 4096
CHUNK = 64
EPS = 1e-6
H_A = 8
Q_LORA = 384
KV_LORA = 256
NOPE = 64
ROPE = 32
V_A = 64
ROPE_THETA = 10000.0
MLA_SCALE = (NOPE + ROPE) ** -0.5
H_B = 8
DH_B = 64
BAND_PREV = 8
BAND = (BAND_PREV + 1) * CHUNK
WINDOW_B = BAND_PREV * CHUNK
REL_CLIP = 256
BAND_SCALE = DH_B ** -0.5
PEER_HEADS = 8
N_KEYS = 128
N_EXPERTS = N_KEYS * N_KEYS
TOPK = 16
PLE_DIM = 256

N_P = BATCH * SEQ
N_S = DEC_BATCH * DEC_SEQ
N_TOK = N_P + N_S
HB = H_B * DH_B
HALF = D_MODEL // 2
SLOTS = PEER_HEADS * TOPK
LANES = 128
TM = 256
TQ = 128
TB = 32
STRIDE = 136
VMEM_LIMIT = 56 * 1024 * 1024
NEG = -1e30
BF16 = jnp.bfloat16
F32 = jnp.float32
NT = (((1,), (1,)), ((), ()))


def _params(n_axes=1):
    return pltpu.CompilerParams(dimension_semantics=("arbitrary",) * n_axes,
                                vmem_limit_bytes=VMEM_LIMIT)


def _const(shape):
    nd = len(shape)
    return pl.BlockSpec(shape, lambda *_: (0,) * nd, pipeline_mode=pl.Buffered(1))


def _rows(width, tile=TM):
    return pl.BlockSpec((tile, width), lambda i: (i, 0))


def _rms(x, g):
    return x * lax.rsqrt(jnp.mean(x * x, axis=-1, keepdims=True) + EPS) * g


def _dot(a, b):
    return jnp.dot(a, b, preferred_element_type=F32)


def _softmax_rows(s_list):
    m = functools.reduce(jnp.maximum, [jnp.max(s, axis=-1, keepdims=True) for s in s_list])
    p_list = [jnp.exp(s - m) for s in s_list]
    inv = 1.0 / functools.reduce(jnp.add, [jnp.sum(p, axis=-1, keepdims=True) for p in p_list])
    return [(p * inv).astype(BF16) for p in p_list]


def _inproj_kernel(x_ref, gmix_ref, w1_ref, w2_ref, w3_ref, gq_ref, gkv_ref, wuq1_ref, wuq2_ref,
                   wukp_ref, wuv_ref, cos_ref, sin_ref,
                   lat_ref, kr_ref, q16_ref, k16_ref, v16_ref, qb16_ref, kb16_ref, vb16_ref,
                   kb32_ref, vb32_ref, sga_ref, sgb_ref):
    h = _rms(x_ref[...], gmix_ref[...]).astype(BF16)
    cos = cos_ref[...]
    sin = sin_ref[...]
    z1 = _dot(h, w1_ref[...])
    k128 = z1[:, 640:768] * cos + z1[:, 768:896] * sin
    kr_ref[...] = k128[:, NOPE:NOPE + ROPE]
    cqn = _rms(z1[:, :Q_LORA], gq_ref[...]).astype(BF16)
    cos8 = jnp.concatenate([cos] * H_A, axis=1)
    sin8 = jnp.concatenate([sin] * H_A, axis=1)
    q = _dot(cqn, wuq1_ref[...]) * cos8 + _dot(cqn, wuq2_ref[...]) * sin8
    q16_ref[...] = q.astype(BF16)
    ckvn = _rms(z1[:, Q_LORA:Q_LORA + KV_LORA], gkv_ref[...])
    lat_ref[...] = ckvn
    ckvn16 = ckvn.astype(BF16)
    kk = _dot(ckvn16, wukp_ref[...]) + jnp.concatenate([k128] * H_A, axis=1)
    k16_ref[...] = kk.astype(BF16)
    v16_ref[...] = _dot(ckvn16, wuv_ref[...]).astype(BF16)
    z2 = _dot(h, w2_ref[...])
    qb16_ref[...] = z2[:, :HB].astype(BF16)
    kb = z2[:, HB:2 * HB]
    vb = z2[:, 2 * HB:]
    kb32_ref[...] = kb
    vb32_ref[...] = vb
    kb16_ref[...] = kb.astype(BF16)
    vb16_ref[...] = vb.astype(BF16)
    z3 = _dot(h, w3_ref[...])
    sga_ref[...] = jax.nn.sigmoid(z3[:, :D_MODEL])
    sgb_ref[...] = jax.nn.sigmoid(z3[:, D_MODEL:])


def _inproj(x, gmix, w1, w2, w3, gq, gkv, wuq1, wuq2, wukp, wuv, cos, sin):
    n = x.shape[0]
    widths = [(KV_LORA, F32), (ROPE, F32), (H_A * LANES, BF16), (H_A * LANES, BF16), (H_A * V_A, BF16),
              (HB, BF16), (HB, BF16), (HB, BF16), (HB, F32), (HB, F32), (D_MODEL, F32), (D_MODEL, F32)]
    return pl.pallas_call(
        _inproj_kernel,
        out_shape=[jax.ShapeDtypeStruct((n, w), d) for w, d in widths],
        grid=(n // TM,),
        in_specs=[_rows(D_MODEL), _const(gmix.shape), _const(w1.shape), _const(w2.shape), _const(w3.shape),
                  _const(gq.shape), _const(gkv.shape), _const(wuq1.shape), _const(wuq2.shape),
                  _const(wukp.shape), _const(wuv.shape), _rows(LANES), _rows(LANES)],
        out_specs=[_rows(w) for w, _ in widths],
        compiler_params=_params(),
        name="inproj",
    )(x, gmix, w1, w2, w3, gq, gkv, wuq1, wuq2, wukp, wuv, cos, sin)


def _pair_select(o_even, o_odd):
    lane = lax.broadcasted_iota(jnp.int32, o_even.shape, 1)
    return jnp.where(lane < V_A, o_even, o_odd)


def _mla_prompt_kernel(q_ref, k_ref, v_ref, o_ref):
    i = pl.program_id(1)
    q_chunk = (i * TQ + lax.broadcasted_iota(jnp.int32, (TQ, SEQ), 0)) // CHUNK
    k_chunk = lax.broadcasted_iota(jnp.int32, (TQ, SEQ), 1) // CHUNK
    mask = k_chunk <= q_chunk
    outs = []
    for pair in range(H_A // 2):
        v2 = v_ref[:, pair * LANES:(pair + 1) * LANES]
        o2 = []
        for e in range(2):
            hd = 2 * pair + e
            qh = q_ref[:, hd * LANES:(hd + 1) * LANES]
            kh = k_ref[:, hd * LANES:(hd + 1) * LANES]
            s = lax.dot_general(qh, kh, NT, preferred_element_type=F32) * MLA_SCALE
            (w,) = _softmax_rows([jnp.where(mask, s, NEG)])
            o2.append(_dot(w, v2))
        outs.append(_pair_select(o2[0], o2[1]))
    o_ref[...] = jnp.concatenate(outs, axis=1).astype(BF16)


def _mla_prompt(q16, k16, v16):
    nq = SEQ // TQ
    return pl.pallas_call(
        _mla_prompt_kernel,
        out_shape=jax.ShapeDtypeStruct((N_P, H_A * V_A), BF16),
        grid=(BATCH, nq),
        in_specs=[pl.BlockSpec((TQ, H_A * LANES), lambda b, i: (b * nq + i, 0)),
                  pl.BlockSpec((SEQ, H_A * LANES), lambda b, i: (b, 0)),
                  pl.BlockSpec((SEQ, H_A * V_A), lambda b, i: (b, 0))],
        out_specs=pl.BlockSpec((TQ, H_A * V_A), lambda b, i: (b * nq + i, 0)),
        compiler_params=_params(2),
        name="mla_prompt",
    )(q16, k16, v16)


def _mla_sample_kernel(q_ref, clat_ref, ckr_ref, nlat_ref, nkr_ref, wukt_ref, wuvh_ref, o_ref):
    q = q_ref[...].astype(F32)
    qlat, qrope = [], []
    for hd in range(H_A):
        qn = q[:, hd * LANES:hd * LANES + NOPE].astype(BF16)
        qlat.append(_dot(qn, wukt_ref[hd]).astype(BF16))
        qrope.append(q[:, hd * LANES + NOPE:hd * LANES + NOPE + ROPE].astype(BF16))
    qlat = jnp.concatenate(qlat, axis=0)
    qrope = jnp.concatenate(qrope, axis=0)
    clat = clat_ref[...].astype(BF16)
    ckr = ckr_ref[...].astype(BF16)
    nlat = nlat_ref[...].astype(BF16)
    nkr = nkr_ref[...].astype(BF16)
    s_c = (lax.dot_general(qlat, clat, NT, preferred_element_type=F32)
           + lax.dot_general(qrope, ckr, NT, preferred_element_type=F32)) * MLA_SCALE
    s_n = (lax.dot_general(qlat, nlat, NT, preferred_element_type=F32)
           + lax.dot_general(qrope, nkr, NT, preferred_element_type=F32)) * MLA_SCALE
    w_c, w_n = _softmax_rows([s_c, s_n])
    olat = (_dot(w_c, clat) + _dot(w_n, nlat)).astype(BF16)
    outs = [_dot(olat[hd * DEC_SEQ:(hd + 1) * DEC_SEQ, :], wuvh_ref[hd]) for hd in range(H_A)]
    o_ref[...] = jnp.concatenate(outs, axis=1).astype(BF16)


def _mla_sample(q16, cache_lat, cache_kr, lat, kr, wukt, wuvh):
    off = N_P // DEC_SEQ
    return pl.pallas_call(
        _mla_sample_kernel,
        out_shape=jax.ShapeDtypeStruct((N_S, H_A * V_A), BF16),
        grid=(DEC_BATCH,),
        in_specs=[pl.BlockSpec((DEC_SEQ, H_A * LANES), lambda b: (off + b, 0)),
                  pl.BlockSpec((None, PAST_LEN, KV_LORA), lambda b: (b, 0, 0)),
                  pl.BlockSpec((None, PAST_LEN, ROPE), lambda b: (b, 0, 0)),
                  pl.BlockSpec((DEC_SEQ, KV_LORA), lambda b: (off + b, 0)),
                  pl.BlockSpec((DEC_SEQ, ROPE), lambda b: (off + b, 0)),
                  _const(wukt.shape), _const(wuvh.shape)],
        out_specs=pl.BlockSpec((DEC_SEQ, H_A * V_A), lambda b: (b, 0)),
        compiler_params=_params(),
        name="mla_sample",
    )(q16, cache_lat, cache_kr, lat, kr, wukt, wuvh)


def _band_heads(q, blocks, bias_refs, valid):
    lane = lax.broadcasted_iota(jnp.int32, (q.shape[0], LANES), 1)
    outs = []
    for pair in range(H_B // 2):
        sl = slice(pair * LANES, (pair + 1) * LANES)
        q2 = q[:, sl]
        o2 = []
        for e in range(2):
            hd = 2 * pair + e
            own = (lane >= DH_B) if e else (lane < DH_B)
            qm = jnp.where(own, q2, jnp.zeros_like(q2))
            scores = []
            for (k, _), b_ref in zip(blocks, bias_refs):
                s = lax.dot_general(qm, k[:, sl], NT, preferred_element_type=F32) * BAND_SCALE + b_ref[hd]
                scores.append(s)
            if valid is not None:
                scores = [jnp.where(valid, s, NEG) for s in scores]
            ws = _softmax_rows(scores)
            o2.append(functools.reduce(jnp.add, [_dot(w, v[:, sl]) for w, (_, v) in zip(ws, blocks)]))
        outs.append(_pair_select(o2[0], o2[1]))
    return jnp.concatenate(outs, axis=1).astype(BF16)


def _band_prompt_kernel(q_ref, k_ref, v_ref, bias_ref, o_ref):
    c = pl.program_id(1)
    start = pl.multiple_of(c * CHUNK, CHUNK)
    k = k_ref[pl.ds(start, BAND), :]
    v = v_ref[pl.ds(start, BAND), :]
    valid = (start - WINDOW_B + lax.broadcasted_iota(jnp.int32, (CHUNK, BAND), 1)) >= 0
    o_ref[...] = _band_heads(q_ref[...], [(k, v)], [bias_ref], valid)


def _band_prompt(qb16, kpad, vpad, bias):
    nc = SEQ // CHUNK
    return pl.pallas_call(
        _band_prompt_kernel,
        out_shape=jax.ShapeDtypeStruct((N_P, HB), BF16),
        grid=(BATCH, nc),
        in_specs=[pl.BlockSpec((CHUNK, HB), lambda b, c: (b * nc + c, 0)),
                  pl.BlockSpec((None, SEQ + WINDOW_B, HB), lambda b, c: (b, 0, 0)),
                  pl.BlockSpec((None, SEQ + WINDOW_B, HB), lambda b, c: (b, 0, 0)),
                  _const(bias.shape)],
        out_specs=pl.BlockSpec((CHUNK, HB), lambda b, c: (b * nc + c, 0)),
        compiler_params=_params(2),
        name="band_prompt",
    )(qb16, kpad, vpad, bias)


def _band_sample_kernel(q_ref, ck_ref, cv_ref, nk16_ref, nv16_ref, nk32_ref, nv32_ref, bias_c_ref, bias_n_ref,
                        o_ref, bk_ref, bv_ref):
    ck = ck_ref[...]
    cv = cv_ref[...]
    blocks = [(ck.astype(BF16), cv.astype(BF16)), (nk16_ref[...], nv16_ref[...])]
    o_ref[...] = _band_heads(q_ref[...], blocks, [bias_c_ref, bias_n_ref], None)
    keep = WINDOW_B - DEC_SEQ
    bk_ref[0:keep, :] = ck[DEC_SEQ:, :]
    bk_ref[keep:, :] = nk32_ref[...]
    bv_ref[0:keep, :] = cv[DEC_SEQ:, :]
    bv_ref[keep:, :] = nv32_ref[...]


def _band_sample(qb16, cache_k, cache_v, kb16, vb16, kb32, vb32, bias_c, bias_n):
    off = N_P // DEC_SEQ
    new = lambda: pl.BlockSpec((DEC_SEQ, HB), lambda b: (off + b, 0))
    cache = lambda: pl.BlockSpec((None, WINDOW_B, HB), lambda b: (b, 0, 0))
    return pl.pallas_call(
        _band_sample_kernel,
        out_shape=[jax.ShapeDtypeStruct((N_S, HB), BF16),
                   jax.ShapeDtypeStruct((DEC_BATCH, WINDOW_B, HB), F32),
                   jax.ShapeDtypeStruct((DEC_BATCH, WINDOW_B, HB), F32)],
        grid=(DEC_BATCH,),
        in_specs=[new(), cache(), cache(), new(), new(), new(), new(),
                  _const(bias_c.shape), _const(bias_n.shape)],
        out_specs=[pl.BlockSpec((DEC_SEQ, HB), lambda b: (b, 0)), cache(), cache()],
        compiler_params=_params(),
        name="band_sample",
    )(qb16, cache_k, cache_v, kb16, vb16, kb32, vb32, bias_c, bias_n)


def _mix_out_kernel(oa_ref, ob_ref, sga_ref, sgb_ref, x_ref, wa_ref, wb_ref, wout_ref, gffn_ref,
                    x1_ref, h2_ref, h16_ref):
    merged = sga_ref[...] * _dot(oa_ref[...], wa_ref[...]) + sgb_ref[...] * _dot(ob_ref[...], wb_ref[...])
    x1 = x_ref[...] + _dot(merged.astype(BF16), wout_ref[...])
    x1_ref[...] = x1
    h2 = _rms(x1, gffn_ref[...])
    h2_ref[...] = h2
    h16_ref[...] = h2.astype(BF16)


def _mix_out(oa, ob, sga, sgb, x, wa, wb, wout, gffn):
    n = x.shape[0]
    return pl.pallas_call(
        _mix_out_kernel,
        out_shape=[jax.ShapeDtypeStruct((n, D_MODEL), F32), jax.ShapeDtypeStruct((n, D_MODEL), F32),
                   jax.ShapeDtypeStruct((n, D_MODEL), BF16)],
        grid=(n // TM,),
        in_specs=[_rows(H_A * V_A), _rows(HB), _rows(D_MODEL), _rows(D_MODEL), _rows(D_MODEL),
                  _const(wa.shape), _const(wb.shape), _const(wout.shape), _const(gffn.shape)],
        out_specs=[_rows(D_MODEL)] * 3,
        compiler_params=_params(),
        name="mix_out",
    )(oa, ob, sga, sgb, x, wa, wb, wout, gffn)


def _top16(s, n_rows):
    row = lax.broadcasted_iota(jnp.int32, (n_rows, TQ), 0).astype(F32)
    vals, ids = [], []
    for _ in range(TOPK):
        m = jnp.max(s, axis=0, keepdims=True)
        i = jnp.min(jnp.where(s == m, row, float(n_rows)), axis=0, keepdims=True)
        vals.append(m)
        ids.append(i)
        s = jnp.where(row == i, -jnp.inf, s)
    return jnp.concatenate(vals, axis=0), jnp.concatenate(ids, axis=0)


def _peer_topk_kernel(h_ref, wqt_ref, keys_ref, idx_ref, g_ref, qt_ref):
    qt_ref[...] = lax.dot_general(wqt_ref[...], h_ref[...], NT, preferred_element_type=F32).astype(BF16)

    def head(hd, carry):
        tops = []
        for p in range(2):
            hp = hd * 2 + p
            qs = qt_ref[pl.ds(pl.multiple_of(hp * N_KEYS, N_KEYS), N_KEYS), :]
            tops.append(_top16(_dot(keys_ref[hp], qs), N_KEYS))
        (s1, i1), (s2, i2) = tops
        cand = jnp.concatenate([s1[a:a + 1, :] + s2 for a in range(TOPK)], axis=0)
        cidx = jnp.concatenate([i1[a:a + 1, :] * float(N_KEYS) + i2 for a in range(TOPK)], axis=0)
        row = lax.broadcasted_iota(jnp.int32, (TOPK * TOPK, TQ), 0).astype(F32)
        vals, ids = [], []
        for _ in range(TOPK):
            m = jnp.max(cand, axis=0, keepdims=True)
            c = jnp.min(jnp.where(cand == m, row, float(TOPK * TOPK)), axis=0, keepdims=True)
            sel = row == c
            vals.append(m)
            ids.append(jnp.sum(jnp.where(sel, cidx, 0.0), axis=0, keepdims=True))
            cand = jnp.where(sel, -jnp.inf, cand)
        best = jnp.concatenate(vals, axis=0)
        e = jnp.exp(best - best[0:1, :])
        g = e / jnp.sum(e, axis=0, keepdims=True)
        base = pl.multiple_of(hd * TOPK, TOPK)
        g_ref[pl.ds(base, TOPK), :] = g
        idx_ref[pl.ds(base, TOPK), :] = (jnp.concatenate(ids, axis=0) * 4.0).astype(jnp.int32)
        return carry

    lax.fori_loop(0, PEER_HEADS, head, 0)


def _peer_topk(h16, wqt, keys):
    n = h16.shape[0]
    nt = n // TQ
    return pl.pallas_call(
        _peer_topk_kernel,
        out_shape=[jax.ShapeDtypeStruct((nt, SLOTS, TQ), jnp.int32),
                   jax.ShapeDtypeStruct((nt, SLOTS, TQ), F32)],
        grid=(nt,),
        in_specs=[_rows(D_MODEL, TQ), _const(wqt.shape), _const(keys.shape)],
        out_specs=[pl.BlockSpec((None, SLOTS, TQ), lambda i: (i, 0, 0))] * 2,
        scratch_shapes=[pltpu.VMEM((2 * PEER_HEADS * N_KEYS, TQ), BF16)],
        compiler_params=_params(),
        name="peer_topk",
    )(h16, wqt, keys)


def _pack_table(t):
    b = lax.bitcast_convert_type(t.astype(BF16), jnp.uint16).astype(jnp.uint32)
    packed = (b[:, :HALF] << 16) | b[:, HALF:]
    return packed.reshape(N_EXPERTS * 4, LANES)


def _unpack(w):
    hi = lax.bitcast_convert_type(w & jnp.uint32(0xFFFF0000), F32)
    lo = lax.bitcast_convert_type(w << 16, F32)
    return hi, lo


def _peer_u_kernel(idx_ref, x_ref, g_ref, tab, c_ref, prod, rbuf, abuf):
    @pl.when(pl.program_id(0) == 0)
    def _():
        rbuf[...] = jnp.zeros_like(rbuf)

    ones = jnp.ones((8, LANES), BF16)

    def token(t, carry):
        r = rbuf[(t + 1) % 2]
        r1 = r.astype(BF16)
        r2 = (r - r1.astype(F32)).astype(BF16)
        s = (lax.dot_general(ones, r1, NT, preferred_element_type=F32)
             + lax.dot_general(ones, r2, NT, preferred_element_type=F32))
        abuf[pl.ds(t, 1), :] = s[0:1, :]

        tt = jnp.minimum(t, TB - 1)
        xt = x_ref[tt]
        x_a = xt[0:4, :]
        x_b = xt[4:8, :]
        for j in range(SLOTS):
            i4 = pl.multiple_of(idx_ref[tt, j], 4)
            hi, lo = _unpack(tab[pl.ds(i4, 4), :])
            prod[pl.ds(j, 4, stride=STRIDE), :] = hi * x_a + lo * x_b
        rbuf[t % 2] = (prod[pl.ds(0, SLOTS), :] + prod[pl.ds(STRIDE, SLOTS), :]
                       + prod[pl.ds(2 * STRIDE, SLOTS), :] + prod[pl.ds(3 * STRIDE, SLOTS), :])
        return carry

    lax.fori_loop(0, TB + 1, token, 0)
    a = abuf[pl.ds(1, TB), :]
    gelu = 0.5 * a * (1.0 + lax.erf(a * (2.0 ** -0.5)))
    c_ref[...] = g_ref[...] * gelu


def _peer_u(idx, x, g, tab_u):
    n = x.shape[0]
    nb = n // TB
    return pl.pallas_call(
        _peer_u_kernel,
        out_shape=jax.ShapeDtypeStruct((n, SLOTS), F32),
        grid=(nb,),
        in_specs=[pl.BlockSpec((None, TB, SLOTS), lambda i: (i, 0, 0), memory_space=pltpu.SMEM),
                  pl.BlockSpec((TB, 8, LANES), lambda i: (i, 0, 0)),
                  pl.BlockSpec((TB, SLOTS), lambda i: (i, 0)),
                  _const(tab_u.shape)],
        out_specs=pl.BlockSpec((TB, SLOTS), lambda i: (i, 0)),
        scratch_shapes=[pltpu.VMEM((4 * STRIDE, LANES), F32),
                        pltpu.VMEM((2, SLOTS, LANES), F32),
                        pltpu.VMEM((TB + 8, SLOTS), F32)],
        compiler_params=_params(),
        name="peer_u",
    )(idx.reshape(nb, TB, SLOTS), x.reshape(n, 8, LANES), g, tab_u)


def _peer_v_kernel(idx_ref, c_ref, tab, y_ref):
    def token(t, carry):
        acc_a = jnp.zeros((4, LANES), F32)
        acc_b = jnp.zeros((4, LANES), F32)
        for j in range(SLOTS):
            i4 = pl.multiple_of(idx_ref[t, j], 4)
            hi, lo = _unpack(tab[pl.ds(i4, 4), :])
            c = c_ref[t, j]
            acc_a = acc_a + c * hi
            acc_b = acc_b + c * lo
        y_ref[t] = jnp.concatenate([acc_a, acc_b], axis=0)
        return carry

    lax.fori_loop(0, TB, token, 0)


def _peer_v(idx, c, tab_v):
    n = c.shape[0]
    nb = n // TB
    y = pl.pallas_call(
        _peer_v_kernel,
        out_shape=jax.ShapeDtypeStruct((n, 8, LANES), F32),
        grid=(nb,),
        in_specs=[pl.BlockSpec((None, TB, SLOTS), lambda i: (i, 0, 0), memory_space=pltpu.SMEM),
                  pl.BlockSpec((None, TB, SLOTS), lambda i: (i, 0, 0), memory_space=pltpu.SMEM),
                  _const(tab_v.shape)],
        out_specs=pl.BlockSpec((TB, 8, LANES), lambda i: (i, 0, 0)),
        compiler_params=_params(),
        name="peer_v",
    )(idx.reshape(nb, TB, SLOTS), c.reshape(nb, TB, SLOTS), tab_v)
    return y.reshape(n, D_MODEL)


def _final_kernel(x1_ref, y_ref, p_ref, gple_ref, wg_ref, wp_ref, gfin_ref, o_ref):
    x2 = x1_ref[...] + y_ref[...]
    gate = jax.nn.sigmoid(_dot(_rms(x2, gple_ref[...]).astype(BF16), wg_ref[...]))
    x3 = x2 + gate * _dot(p_ref[...].astype(BF16), wp_ref[...])
    o_ref[...] = _rms(x3, gfin_ref[...])


def _final(x1, y, p, gple, wg, wp, gfin):
    n = x1.shape[0]
    return pl.pallas_call(
        _final_kernel,
        out_shape=jax.ShapeDtypeStruct((n, D_MODEL), F32),
        grid=(n // TM,),
        in_specs=[_rows(D_MODEL), _rows(D_MODEL), _rows(PLE_DIM), _const(gple.shape), _const(wg.shape),
                  _const(wp.shape), _const(gfin.shape)],
        out_specs=_rows(D_MODEL),
        compiler_params=_params(),
        name="ple_final",
    )(x1, y, p, gple, wg, wp, gfin)


def _rope_tables():
    half = ROPE // 2
    freqs = ROPE_THETA ** (-np.arange(half, dtype=np.float32) / half)
    pos = np.concatenate([np.tile(np.arange(SEQ), BATCH), np.tile(PAST_LEN + np.arange(DEC_SEQ), DEC_BATCH)])
    ang = jnp.asarray(pos, F32)[:, None] * jnp.asarray(freqs, F32)[None, :]
    cos, sin = jnp.cos(ang), jnp.sin(ang)
    n = pos.shape[0]
    pad = jnp.zeros((n, LANES - NOPE - ROPE), F32)
    cos_t = jnp.concatenate([jnp.ones((n, NOPE), F32), cos, cos, pad], axis=1)
    sin_t = jnp.concatenate([jnp.zeros((n, NOPE), F32), sin, sin, pad], axis=1)
    return cos_t, sin_t


def _rot_cols(w):
    half = ROPE // 2
    return jnp.concatenate([-w[..., half:], w[..., :half]], axis=-1)


def _rel_bias(table, q_pos, k_pos):
    rel = np.clip(q_pos[:, None] - k_pos[None, :], -REL_CLIP, REL_CLIP) + REL_CLIP
    return table[:, rel]


def kernel(x_prompt, x_sample, cache_latent, cache_krope, cache_band_k, cache_band_v, p_prompt, p_sample, g_mix, w_in, g_q_lora, g_kv_lora, w_uq, w_uk, w_uv, rel_bias, w_a_proj, w_b_proj, w_out, g_ffn, w_query, sub_keys, expert_u, expert_v, g_ple, w_ple_gate, w_ple_proj, g_final):
    w = w_in[0]
    o_kr = Q_LORA + KV_LORA
    o_qb = o_kr + ROPE
    o_ga = o_qb + 3 * HB
    w_kr = w[:, o_kr:o_qb]
    z64 = jnp.zeros((D_MODEL, NOPE), F32)
    z32 = jnp.zeros((D_MODEL, LANES - NOPE - ROPE), F32)
    w1 = jnp.concatenate([w[:, :o_kr], z64, w_kr, z32, z64, _rot_cols(w_kr), z32], axis=1).astype(BF16)
    w2 = w[:, o_qb:o_ga].astype(BF16)
    w3 = w[:, o_ga:].astype(BF16)
    wq = w_uq[0].reshape(Q_LORA, H_A, NOPE + ROPE)
    zq64 = jnp.zeros((Q_LORA, H_A, NOPE), F32)
    zq32 = jnp.zeros((Q_LORA, H_A, LANES - NOPE - ROPE), F32)
    wuq1 = jnp.concatenate([wq, zq32], axis=-1).reshape(Q_LORA, H_A * LANES).astype(BF16)
    wuq2 = jnp.concatenate([zq64, _rot_cols(wq[..., NOPE:]), zq32], axis=-1).reshape(Q_LORA, H_A * LANES).astype(BF16)
    wukp = jnp.concatenate([w_uk[0], jnp.zeros((KV_LORA, H_A, LANES - NOPE), F32)], axis=-1)
    wukp = wukp.reshape(KV_LORA, H_A * LANES).astype(BF16)
    wuv = w_uv[0].reshape(KV_LORA, H_A * V_A).astype(BF16)
    wukt = jnp.transpose(w_uk[0], (1, 2, 0)).astype(BF16)
    wuvh = jnp.transpose(w_uv[0], (1, 0, 2)).astype(BF16)
    cos_t, sin_t = _rope_tables()
    row = lambda g: g.reshape(1, -1)

    x = jnp.concatenate([x_prompt.reshape(N_P, D_MODEL), x_sample.reshape(N_S, D_MODEL)], axis=0)
    p = jnp.concatenate([p_prompt.reshape(N_P, PLE_DIM), p_sample.reshape(N_S, PLE_DIM)], axis=0)

    (lat, kr, q16, k16, v16, qb16, kb16, vb16, kb32, vb32, sga, sgb) = _inproj(
        x, row(g_mix[0]), w1, w2, w3, row(g_q_lora[0]), row(g_kv_lora[0]), wuq1, wuq2, wukp, wuv, cos_t, sin_t)

    oa_p = _mla_prompt(q16, k16, v16)
    oa_s = _mla_sample(q16, cache_latent[0], cache_krope[0], lat, kr, wukt, wuvh)

    pad = ((0, 0), (WINDOW_B, 0), (0, 0))
    kpad = jnp.pad(kb16[:N_P].reshape(BATCH, SEQ, HB), pad)
    vpad = jnp.pad(vb16[:N_P].reshape(BATCH, SEQ, HB), pad)
    tab = rel_bias[0]
    bias_p = _rel_bias(tab, np.arange(CHUNK) + WINDOW_B, np.arange(BAND))
    bias_s = _rel_bias(tab, PAST_LEN + np.arange(DEC_SEQ), PAST_LEN - WINDOW_B + np.arange(WINDOW_B + DEC_SEQ))
    ob_p = _band_prompt(qb16, kpad, vpad, bias_p)
    ob_s, bk_s, bv_s = _band_sample(qb16, cache_band_k[0].reshape(DEC_BATCH, WINDOW_B, HB),
                                    cache_band_v[0].reshape(DEC_BATCH, WINDOW_B, HB),
                                    kb16, vb16, kb32, vb32, bias_s[:, :, :WINDOW_B], bias_s[:, :, WINDOW_B:])

    oa = jnp.concatenate([oa_p, oa_s], axis=0)
    ob = jnp.concatenate([ob_p, ob_s], axis=0)
    x1, h2, h16 = _mix_out(oa, ob, sga, sgb, x, w_a_proj[0].astype(BF16), w_b_proj[0].astype(BF16),
                           w_out[0].astype(BF16), row(g_ffn[0]))
    wqt = jnp.transpose(w_query[0]).astype(BF16)
    keys = sub_keys[0].reshape(2 * PEER_HEADS, N_KEYS, N_KEYS).astype(BF16)
    idx_t, g_t = _peer_topk(h16, wqt, keys)
    idx = jnp.swapaxes(idx_t, 1, 2).reshape(N_TOK, SLOTS)
    gw = jnp.swapaxes(g_t, 1, 2).reshape(N_TOK, SLOTS)

    c = _peer_u(idx, h2, gw, _pack_table(expert_u[0]))
    y = _peer_v(idx, c, _pack_table(expert_v[0]))

    out = _final(x1, y, p, row(g_ple[0]), w_ple_gate[0].astype(BF16), w_ple_proj[0].astype(BF16), row(g_final))

    y_prompt = out[:N_P].reshape(BATCH, SEQ, D_MODEL)
    y_sample = out[N_P:].reshape(DEC_BATCH, DEC_SEQ, D_MODEL)
    lat_p = lat[:N_P].reshape(1, BATCH, SEQ, KV_LORA)
    kr_p = kr[:N_P].reshape(1, BATCH, SEQ, ROPE)
    bk_p = kb32[:N_P].reshape(BATCH, SEQ, H_B, DH_B)[None, :, SEQ - WINDOW_B:]
    bv_p = vb32[:N_P].reshape(BATCH, SEQ, H_B, DH_B)[None, :, SEQ - WINDOW_B:]
    lat_s = lat[N_P:].reshape(1, DEC_BATCH, DEC_SEQ, KV_LORA)
    kr_s = kr[N_P:].reshape(1, DEC_BATCH, DEC_SEQ, ROPE)
    bk_s = bk_s.reshape(1, DEC_BATCH, WINDOW_B, H_B, DH_B)
    bv_s = bv_s.reshape(1, DEC_BATCH, WINDOW_B, H_B, DH_B)
    return (y_prompt, y_sample, lat_p, kr_p, bk_p, bv_p, lat_s, kr_s, bk_s, bv_s)
```

```python
import functools
import jax
import jax.numpy as jnp
import numpy as np
from jax import lax
from jax.experimental import pallas as pl
from jax.experimental.pallas import tpu as pltpu

D_MODEL = 1024
BATCH = 16
SEQ = 2048
DEC_BATCH = 32
DEC_SEQ = 32
PAST_LEN = 4096
CHUNK = 64
EPS = 1e-6
H_A = 8
Q_LORA = 384
KV_LORA = 256
NOPE = 64
ROPE = 32
V_A = 64
ROPE_THETA = 10000.0
MLA_SCALE = (NOPE + ROPE) ** -0.5
H_B = 8
DH_B = 64
BAND_PREV = 8
BAND = (BAND_PREV + 1) * CHUNK
WINDOW_B = BAND_PREV * CHUNK
REL_CLIP = 256
BAND_SCALE = DH_B ** -0.5
PEER_HEADS = 8
N_KEYS = 128
N_EXPERTS = N_KEYS * N_KEYS
TOPK = 16
PLE_DIM = 256

N_P = BATCH * SEQ
N_S = DEC_BATCH * DEC_SEQ
N_TOK = N_P + N_S
HB = H_B * DH_B
HALF = D_MODEL // 2
SLOTS = PEER_HEADS * TOPK
LANES = 128
SUBLANES = 8
TM = 256
TQ = 128
KV_STEP = 512
TB = 32
ROW_WORDS = HALF // LANES
STRIDE = 136
HEADS_PER_TRIP = 2
IDX_GROUP = 16
VMEM_LIMIT = 56 * 1024 * 1024
NEG = -1e30
BF16 = jnp.bfloat16
F32 = jnp.float32
NT = (((1,), (1,)), ((), ()))


def _params(n_axes=1):
    return pltpu.CompilerParams(dimension_semantics=("arbitrary",) * n_axes,
                                vmem_limit_bytes=VMEM_LIMIT)


def _const(shape):
    nd = len(shape)
    return pl.BlockSpec(shape, lambda *_: (0,) * nd, pipeline_mode=pl.Buffered(1))


def _rows(width, tile=TM):
    return pl.BlockSpec((tile, width), lambda i: (i, 0))


def _rms(x, g):
    return x * lax.rsqrt(jnp.mean(x * x, axis=-1, keepdims=True) + EPS) * g


def _dot(a, b):
    return jnp.dot(a, b, preferred_element_type=F32)


def _softmax_rows(s_list):
    m = functools.reduce(jnp.maximum, [jnp.max(s, axis=-1, keepdims=True) for s in s_list])
    p_list = [jnp.exp(s - m) for s in s_list]
    inv = 1.0 / functools.reduce(jnp.add, [jnp.sum(p, axis=-1, keepdims=True) for p in p_list])
    return [(p * inv).astype(BF16) for p in p_list]


def _inproj_kernel(x_ref, gmix_ref, w1_ref, w2_ref, w3_ref, gq_ref, gkv_ref, wuq1_ref, wuq2_ref,
                   wukp_ref, wuv_ref, cos_ref, sin_ref,
                   lat_ref, kr_ref, q16_ref, k16_ref, v16_ref, qb16_ref, kb16_ref, vb16_ref,
                   kb32_ref, vb32_ref, sga_ref, sgb_ref):
    h = _rms(x_ref[...], gmix_ref[...]).astype(BF16)
    cos = cos_ref[...]
    sin = sin_ref[...]
    z1 = _dot(h, w1_ref[...])
    k128 = z1[:, 640:768] * cos + z1[:, 768:896] * sin
    kr_ref[...] = k128[:, NOPE:NOPE + ROPE]
    cqn = _rms(z1[:, :Q_LORA], gq_ref[...]).astype(BF16)
    cos8 = jnp.concatenate([cos] * H_A, axis=1)
    sin8 = jnp.concatenate([sin] * H_A, axis=1)
    q = _dot(cqn, wuq1_ref[...]) * cos8 + _dot(cqn, wuq2_ref[...]) * sin8
    q16_ref[...] = q.astype(BF16)
    ckvn = _rms(z1[:, Q_LORA:Q_LORA + KV_LORA], gkv_ref[...])
    lat_ref[...] = ckvn
    ckvn16 = ckvn.astype(BF16)
    kk = _dot(ckvn16, wukp_ref[...]) + jnp.concatenate([k128] * H_A, axis=1)
    k16_ref[...] = kk.astype(BF16)
    v16_ref[...] = _dot(ckvn16, wuv_ref[...]).astype(BF16)
    z2 = _dot(h, w2_ref[...])
    qb16_ref[...] = z2[:, :HB].astype(BF16)
    kb = z2[:, HB:2 * HB]
    vb = z2[:, 2 * HB:]
    kb32_ref[...] = kb
    vb32_ref[...] = vb
    kb16_ref[...] = kb.astype(BF16)
    vb16_ref[...] = vb.astype(BF16)
    z3 = _dot(h, w3_ref[...])
    sga_ref[...] = jax.nn.sigmoid(z3[:, :D_MODEL])
    sgb_ref[...] = jax.nn.sigmoid(z3[:, D_MODEL:])


def _inproj(x, gmix, w1, w2, w3, gq, gkv, wuq1, wuq2, wukp, wuv, cos, sin):
    n = x.shape[0]
    widths = [(KV_LORA, F32), (ROPE, F32), (H_A * LANES, BF16), (H_A * LANES, BF16), (H_A * V_A, BF16),
              (HB, BF16), (HB, BF16), (HB, BF16), (HB, F32), (HB, F32), (D_MODEL, F32), (D_MODEL, F32)]
    return pl.pallas_call(
        _inproj_kernel,
        out_shape=[jax.ShapeDtypeStruct((n, w), d) for w, d in widths],
        grid=(n // TM,),
        in_specs=[_rows(D_MODEL), _const(gmix.shape), _const(w1.shape), _const(w2.shape), _const(w3.shape),
                  _const(gq.shape), _const(gkv.shape), _const(wuq1.shape), _const(wuq2.shape),
                  _const(wukp.shape), _const(wuv.shape), _rows(LANES), _rows(LANES)],
        out_specs=[_rows(w) for w, _ in widths],
        compiler_params=_params(),
        name="inproj",
    )(x, gmix, w1, w2, w3, gq, gkv, wuq1, wuq2, wukp, wuv, cos, sin)


def _pair_select(o_even, o_odd):
    lane = lax.broadcasted_iota(jnp.int32, o_even.shape, 1)
    return jnp.where(lane < V_A, o_even, o_odd)


def _mla_prompt_tile(q_ref, k_ref, v_ref, o_ref, i, nk):
    q_chunk = (i * TQ + lax.broadcasted_iota(jnp.int32, (TQ, nk), 0)) // CHUNK
    k_chunk = lax.broadcasted_iota(jnp.int32, (TQ, nk), 1) // CHUNK
    mask = k_chunk <= q_chunk
    outs = []
    for pair in range(H_A // 2):
        v2 = v_ref[0:nk, pair * LANES:(pair + 1) * LANES]
        o2 = []
        for e in range(2):
            hd = 2 * pair + e
            qh = q_ref[:, hd * LANES:(hd + 1) * LANES]
            kh = k_ref[0:nk, hd * LANES:(hd + 1) * LANES]
            s = lax.dot_general(qh, kh, NT, preferred_element_type=F32) * MLA_SCALE
            (w,) = _softmax_rows([jnp.where(mask, s, NEG)])
            o2.append(_dot(w, v2))
        outs.append(_pair_select(o2[0], o2[1]))
    o_ref[...] = jnp.concatenate(outs, axis=1).astype(BF16)


def _mla_prompt_kernel(q_ref, k_ref, v_ref, o_ref):
    i = pl.program_id(1)
    tiles_per_step = KV_STEP // TQ
    for grp in range(SEQ // KV_STEP):
        @pl.when(i // tiles_per_step == grp)
        def _():
            _mla_prompt_tile(q_ref, k_ref, v_ref, o_ref, i, (grp + 1) * KV_STEP)


def _mla_prompt(q16, k16, v16):
    nq = SEQ // TQ
    return pl.pallas_call(
        _mla_prompt_kernel,
        out_shape=jax.ShapeDtypeStruct((N_P, H_A * V_A), BF16),
        grid=(BATCH, nq),
        in_specs=[pl.BlockSpec((TQ, H_A * LANES), lambda b, i: (b * nq + i, 0)),
                  pl.BlockSpec((SEQ, H_A * LANES), lambda b, i: (b, 0)),
                  pl.BlockSpec((SEQ, H_A * V_A), lambda b, i: (b, 0))],
        out_specs=pl.BlockSpec((TQ, H_A * V_A), lambda b, i: (b * nq + i, 0)),
        compiler_params=_params(2),
        name="mla_prompt",
    )(q16, k16, v16)


def _mla_sample_kernel(q_ref, clat_ref, ckr_ref, nlat_ref, nkr_ref, wukt_ref, wuvh_ref, o_ref):
    q = q_ref[...].astype(F32)
    qlat, qrope = [], []
    for hd in range(H_A):
        qn = q[:, hd * LANES:hd * LANES + NOPE].astype(BF16)
        qlat.append(_dot(qn, wukt_ref[hd]).astype(BF16))
        qrope.append(q[:, hd * LANES + NOPE:hd * LANES + NOPE + ROPE].astype(BF16))
    qlat = jnp.concatenate(qlat, axis=0)
    qrope = jnp.concatenate(qrope, axis=0)
    clat = clat_ref[...].astype(BF16)
    ckr = ckr_ref[...].astype(BF16)
    nlat = nlat_ref[...].astype(BF16)
    nkr = nkr_ref[...].astype(BF16)
    s_c = (lax.dot_general(qlat, clat, NT, preferred_element_type=F32)
           + lax.dot_general(qrope, ckr, NT, preferred_element_type=F32)) * MLA_SCALE
    s_n = (lax.dot_general(qlat, nlat, NT, preferred_element_type=F32)
           + lax.dot_general(qrope, nkr, NT, preferred_element_type=F32)) * MLA_SCALE
    w_c, w_n = _softmax_rows([s_c, s_n])
    olat = (_dot(w_c, clat) + _dot(w_n, nlat)).astype(BF16)
    outs = [_dot(olat[hd * DEC_SEQ:(hd + 1) * DEC_SEQ, :], wuvh_ref[hd]) for hd in range(H_A)]
    o_ref[...] = jnp.concatenate(outs, axis=1).astype(BF16)


def _mla_sample(q16, cache_lat, cache_kr, lat, kr, wukt, wuvh):
    off = N_P // DEC_SEQ
    return pl.pallas_call(
        _mla_sample_kernel,
        out_shape=jax.ShapeDtypeStruct((N_S, H_A * V_A), BF16),
        grid=(DEC_BATCH,),
        in_specs=[pl.BlockSpec((DEC_SEQ, H_A * LANES), lambda b: (off + b, 0)),
                  pl.BlockSpec((None, PAST_LEN, KV_LORA), lambda b: (b, 0, 0)),
                  pl.BlockSpec((None, PAST_LEN, ROPE), lambda b: (b, 0, 0)),
                  pl.BlockSpec((DEC_SEQ, KV_LORA), lambda b: (off + b, 0)),
                  pl.BlockSpec((DEC_SEQ, ROPE), lambda b: (off + b, 0)),
                  _const(wukt.shape), _const(wuvh.shape)],
        out_specs=pl.BlockSpec((DEC_SEQ, H_A * V_A), lambda b: (b, 0)),
        compiler_params=_params(),
        name="mla_sample",
    )(q16, cache_lat, cache_kr, lat, kr, wukt, wuvh)


def _band_heads(q, blocks, bias_refs, valid):
    lane = lax.broadcasted_iota(jnp.int32, (q.shape[0], LANES), 1)
    outs = []
    for pair in range(H_B // 2):
        sl = slice(pair * LANES, (pair + 1) * LANES)
        q2 = q[:, sl]
        o2 = []
        for e in range(2):
            hd = 2 * pair + e
            own = (lane >= DH_B) if e else (lane < DH_B)
            qm = jnp.where(own, q2, jnp.zeros_like(q2))
            scores = []
            for (k, _), b_ref in zip(blocks, bias_refs):
                s = lax.dot_general(qm, k[:, sl], NT, preferred_element_type=F32) * BAND_SCALE + b_ref[hd]
                scores.append(s)
            if valid is not None:
                scores = [jnp.where(valid, s, NEG) for s in scores]
            ws = _softmax_rows(scores)
            o2.append(functools.reduce(jnp.add, [_dot(w, v[:, sl]) for w, (_, v) in zip(ws, blocks)]))
        outs.append(_pair_select(o2[0], o2[1]))
    return jnp.concatenate(outs, axis=1).astype(BF16)


def _band_prompt_kernel(q_ref, k_ref, v_ref, bias_ref, o_ref):
    c = pl.program_id(1)
    start = pl.multiple_of(c * CHUNK, CHUNK)
    k = k_ref[pl.ds(start, BAND), :]
    v = v_ref[pl.ds(start, BAND), :]
    valid = (start - WINDOW_B + lax.broadcasted_iota(jnp.int32, (CHUNK, BAND), 1)) >= 0
    o_ref[...] = _band_heads(q_ref[...], [(k, v)], [bias_ref], valid)


def _band_prompt(qb16, kpad, vpad, bias):
    nc = SEQ // CHUNK
    return pl.pallas_call(
        _band_prompt_kernel,
        out_shape=jax.ShapeDtypeStruct((N_P, HB), BF16),
        grid=(BATCH, nc),
        in_specs=[pl.BlockSpec((CHUNK, HB), lambda b, c: (b * nc + c, 0)),
                  pl.BlockSpec((None, SEQ + WINDOW_B, HB), lambda b, c: (b, 0, 0)),
                  pl.BlockSpec((None, SEQ + WINDOW_B, HB), lambda b, c: (b, 0, 0)),
                  _const(bias.shape)],
        out_specs=pl.BlockSpec((CHUNK, HB), lambda b, c: (b * nc + c, 0)),
        compiler_params=_params(2),
        name="band_prompt",
    )(qb16, kpad, vpad, bias)


def _band_sample_kernel(q_ref, ck_ref, cv_ref, nk16_ref, nv16_ref, nk32_ref, nv32_ref, bias_c_ref, bias_n_ref,
                        o_ref, bk_ref, bv_ref):
    ck = ck_ref[...]
    cv = cv_ref[...]
    blocks = [(ck.astype(BF16), cv.astype(BF16)), (nk16_ref[...], nv16_ref[...])]
    o_ref[...] = _band_heads(q_ref[...], blocks, [bias_c_ref, bias_n_ref], None)
    keep = WINDOW_B - DEC_SEQ
    bk_ref[0:keep, :] = ck[DEC_SEQ:, :]
    bk_ref[keep:, :] = nk32_ref[...]
    bv_ref[0:keep, :] = cv[DEC_SEQ:, :]
    bv_ref[keep:, :] = nv32_ref[...]


def _band_sample(qb16, cache_k, cache_v, kb16, vb16, kb32, vb32, bias_c, bias_n):
    off = N_P // DEC_SEQ
    new = lambda: pl.BlockSpec((DEC_SEQ, HB), lambda b: (off + b, 0))
    cache = lambda: pl.BlockSpec((None, WINDOW_B, HB), lambda b: (b, 0, 0))
    return pl.pallas_call(
        _band_sample_kernel,
        out_shape=[jax.ShapeDtypeStruct((N_S, HB), BF16),
                   jax.ShapeDtypeStruct((DEC_BATCH, WINDOW_B, HB), F32),
                   jax.ShapeDtypeStruct((DEC_BATCH, WINDOW_B, HB), F32)],
        grid=(DEC_BATCH,),
        in_specs=[new(), cache(), cache(), new(), new(), new(), new(),
                  _const(bias_c.shape), _const(bias_n.shape)],
        out_specs=[pl.BlockSpec((DEC_SEQ, HB), lambda b: (b, 0)), cache(), cache()],
        compiler_params=_params(),
        name="band_sample",
    )(qb16, cache_k, cache_v, kb16, vb16, kb32, vb32, bias_c, bias_n)


def _mix_out_kernel(oa_ref, ob_ref, sga_ref, sgb_ref, x_ref, wa_ref, wb_ref, wout_ref, gffn_ref,
                    x1_ref, h2_ref, h16_ref):
    merged = sga_ref[...] * _dot(oa_ref[...], wa_ref[...]) + sgb_ref[...] * _dot(ob_ref[...], wb_ref[...])
    x1 = x_ref[...] + _dot(merged.astype(BF16), wout_ref[...])
    x1_ref[...] = x1
    h2 = _rms(x1, gffn_ref[...])
    h2_ref[...] = h2
    h16_ref[...] = h2.astype(BF16)


def _mix_out(oa, ob, sga, sgb, x, wa, wb, wout, gffn):
    n = x.shape[0]
    return pl.pallas_call(
        _mix_out_kernel,
        out_shape=[jax.ShapeDtypeStruct((n, D_MODEL), F32), jax.ShapeDtypeStruct((n, D_MODEL), F32),
                   jax.ShapeDtypeStruct((n, D_MODEL), BF16)],
        grid=(n // TM,),
        in_specs=[_rows(H_A * V_A), _rows(HB), _rows(D_MODEL), _rows(D_MODEL), _rows(D_MODEL),
                  _const(wa.shape), _const(wb.shape), _const(wout.shape), _const(gffn.shape)],
        out_specs=[_rows(D_MODEL)] * 3,
        compiler_params=_params(),
        name="mix_out",
    )(oa, ob, sga, sgb, x, wa, wb, wout, gffn)


def _top16(s, n_rows):
    row = lax.broadcasted_iota(jnp.int32, (n_rows, TQ), 0).astype(F32)
    vals, ids = [], []
    for _ in range(TOPK):
        m = jnp.max(s, axis=0, keepdims=True)
        i = jnp.min(jnp.where(s == m, row, float(n_rows)), axis=0, keepdims=True)
        vals.append(m)
        ids.append(i)
        s = jnp.where(row == i, -jnp.inf, s)
    return jnp.concatenate(vals, axis=0), jnp.concatenate(ids, axis=0)


def _pair_candidates(f1, f2):
    h = SUBLANES
    blocks = [f1(0, 1, 0, h), f1(0, 1, h, 2 * h)]
    blocks += [f1(a, a + 1, 0, h) for a in range(1, h)]
    blocks += [f2(h, 2 * h, 0, 1)]
    return jnp.concatenate(blocks, axis=0)


def _peer_topk_kernel(h_ref, wqt_ref, keys_ref, idx_ref, g_ref, qt_ref):
    qt_ref[...] = lax.dot_general(wqt_ref[...], h_ref[...], NT, preferred_element_type=F32).astype(BF16)
    r8 = lax.broadcasted_iota(jnp.int32, (SUBLANES, TQ), 0).astype(F32)
    flat = lambda a0, a1, b0, b1: (r8 + float(b0)) + float(TOPK * a0)
    flat_t = lambda a0, a1, b0, b1: (r8 + float(a0)) * float(TOPK) + float(b0)
    cflat = _pair_candidates(flat, flat_t)

    def one_head(hd):
        tops = []
        for p in range(2):
            hp = hd * 2 + p
            qs = qt_ref[pl.ds(pl.multiple_of(hp * N_KEYS, N_KEYS), N_KEYS), :]
            tops.append(_top16(_dot(keys_ref[hp], qs), N_KEYS))
        (s1, i1), (s2, i2) = tops
        add = lambda x, y: (lambda a0, a1, b0, b1: x[a0:a1, :] + y[b0:b1, :])
        cand = _pair_candidates(add(s1, s2), add(s1, s2))
        e1 = i1 * float(N_KEYS)
        cidx = _pair_candidates(add(e1, i2), add(e1, i2))
        vals, ids = [], []
        for _ in range(TOPK):
            m = jnp.max(cand, axis=0, keepdims=True)
            c = jnp.min(jnp.where(cand == m, cflat, float(TOPK * TOPK)), axis=0, keepdims=True)
            sel = cflat == c
            vals.append(m)
            ids.append(jnp.sum(jnp.where(sel, cidx, 0.0), axis=0, keepdims=True))
            cand = jnp.where(sel, -jnp.inf, cand)
        best = jnp.concatenate(vals, axis=0)
        e = jnp.exp(best - best[0:1, :])
        g = e / jnp.sum(e, axis=0, keepdims=True)
        base = pl.multiple_of(hd * TOPK, TOPK)
        g_ref[pl.ds(base, TOPK), :] = g
        idx_ref[pl.ds(base, TOPK), :] = (jnp.concatenate(ids, axis=0) * float(ROW_WORDS)).astype(jnp.int32)

    def head_group(grp, carry):
        for e in range(HEADS_PER_TRIP):
            one_head(grp * HEADS_PER_TRIP + e)
        return carry

    lax.fori_loop(0, PEER_HEADS // HEADS_PER_TRIP, head_group, 0)


def _peer_topk(h16, wqt, keys):
    n = h16.shape[0]
    nt = n // TQ
    return pl.pallas_call(
        _peer_topk_kernel,
        out_shape=[jax.ShapeDtypeStruct((nt, SLOTS, TQ), jnp.int32),
                   jax.ShapeDtypeStruct((nt, SLOTS, TQ), F32)],
        grid=(nt,),
        in_specs=[_rows(D_MODEL, TQ), _const(wqt.shape), _const(keys.shape)],
        out_specs=[pl.BlockSpec((None, SLOTS, TQ), lambda i: (i, 0, 0))] * 2,
        scratch_shapes=[pltpu.VMEM((2 * PEER_HEADS * N_KEYS, TQ), BF16)],
        compiler_params=_params(),
        name="peer_topk",
    )(h16, wqt, keys)


def _pack_table(t):
    b = lax.bitcast_convert_type(t.astype(BF16), jnp.uint16).astype(jnp.uint32)
    packed = (b[:, :HALF] << 16) | b[:, HALF:]
    return packed.reshape(N_EXPERTS * ROW_WORDS, LANES)


def _unpack(w):
    hi = lax.bitcast_convert_type(w & jnp.uint32(0xFFFF0000), F32)
    lo = lax.bitcast_convert_type(w << 16, F32)
    return hi, lo


def _smem_row(ref, t):
    out = []
    for grp in range(SLOTS // IDX_GROUP):
        sub = ref.at[0, pl.ds(t * SLOTS + grp * IDX_GROUP, IDX_GROUP)]
        out.extend(sub[k] for k in range(IDX_GROUP))
    return out


def _peer_u_kernel(idx_ref, x_ref, g_ref, tab, c_ref, prod, rbuf, abuf):
    @pl.when(pl.program_id(0) == 0)
    def _():
        rbuf[...] = jnp.zeros_like(rbuf)

    ones = jnp.ones((SUBLANES, LANES), BF16)

    def token(t, carry):
        r = rbuf[(t + 1) % 2]
        r1 = r.astype(BF16)
        r2 = (r - r1.astype(F32)).astype(BF16)
        s = (lax.dot_general(ones, r1, NT, preferred_element_type=F32)
             + lax.dot_general(ones, r2, NT, preferred_element_type=F32))
        abuf[pl.ds(t, 1), :] = s[0:1, :]

        tt = jnp.minimum(t, TB - 1)
        xt = x_ref[tt]
        x_a = xt[0:ROW_WORDS, :]
        x_b = xt[ROW_WORDS:, :]
        for j, i in enumerate(_smem_row(idx_ref, tt)):
            hi, lo = _unpack(tab[pl.ds(pl.multiple_of(i, ROW_WORDS), ROW_WORDS), :])
            prod[pl.ds(j, ROW_WORDS, stride=STRIDE), :] = hi * x_a + lo * x_b
        rbuf[t % 2] = (prod[pl.ds(0, SLOTS), :] + prod[pl.ds(STRIDE, SLOTS), :]
                       + prod[pl.ds(2 * STRIDE, SLOTS), :] + prod[pl.ds(3 * STRIDE, SLOTS), :])
        return carry

    lax.fori_loop(0, TB + 1, token, 0)
    a = abuf[pl.ds(1, TB), :]
    gelu = 0.5 * a * (1.0 + lax.erf(a * (2.0 ** -0.5)))
    c_ref[...] = g_ref[...] * gelu


def _peer_u(idx, x, g, tab_u):
    n = x.shape[0]
    nb = n // TB
    return pl.pallas_call(
        _peer_u_kernel,
        out_shape=jax.ShapeDtypeStruct((n, SLOTS), F32),
        grid=(nb,),
        in_specs=[pl.BlockSpec((None, 1, TB * SLOTS), lambda i: (i, 0, 0), memory_space=pltpu.SMEM),
                  pl.BlockSpec((TB, SUBLANES, LANES), lambda i: (i, 0, 0)),
                  pl.BlockSpec((TB, SLOTS), lambda i: (i, 0)),
                  _const(tab_u.shape)],
        out_specs=pl.BlockSpec((TB, SLOTS), lambda i: (i, 0)),
        scratch_shapes=[pltpu.VMEM((ROW_WORDS * STRIDE, LANES), F32),
                        pltpu.VMEM((2, SLOTS, LANES), F32),
                        pltpu.VMEM((TB + 8, SLOTS), F32)],
        compiler_params=_params(),
        name="peer_u",
    )(idx.reshape(nb, 1, TB * SLOTS), x.reshape(n, SUBLANES, LANES), g, tab_u)


def _peer_v_kernel(idx_ref, c_ref, tab, y_ref):
    def token(t, carry):
        acc_a = jnp.zeros((ROW_WORDS, LANES), F32)
        acc_b = jnp.zeros((ROW_WORDS, LANES), F32)
        for i, c in zip(_smem_row(idx_ref, t), _smem_row(c_ref, t)):
            hi, lo = _unpack(tab[pl.ds(pl.multiple_of(i, ROW_WORDS), ROW_WORDS), :])
            acc_a = acc_a + c * hi
            acc_b = acc_b + c * lo
        y_ref[t] = jnp.concatenate([acc_a, acc_b], axis=0)
        return carry

    lax.fori_loop(0, TB, token, 0)


def _peer_v(idx, c, tab_v):
    n = c.shape[0]
    nb = n // TB
    y = pl.pallas_call(
        _peer_v_kernel,
        out_shape=jax.ShapeDtypeStruct((n, SUBLANES, LANES), F32),
        grid=(nb,),
        in_specs=[pl.BlockSpec((None, 1, TB * SLOTS), lambda i: (i, 0, 0), memory_space=pltpu.SMEM),
                  pl.BlockSpec((None, 1, TB * SLOTS), lambda i: (i, 0, 0), memory_space=pltpu.SMEM),
                  _const(tab_v.shape)],
        out_specs=pl.BlockSpec((TB, SUBLANES, LANES), lambda i: (i, 0, 0)),
        compiler_params=_params(),
        name="peer_v",
    )(idx.reshape(nb, 1, TB * SLOTS), c.reshape(nb, 1, TB * SLOTS), tab_v)
    return y.reshape(n, D_MODEL)


def _final_kernel(x1_ref, y_ref, p_ref, gple_ref, wg_ref, wp_ref, gfin_ref, o_ref):
    x2 = x1_ref[...] + y_ref[...]
    gate = jax.nn.sigmoid(_dot(_rms(x2, gple_ref[...]).astype(BF16), wg_ref[...]))
    x3 = x2 + gate * _dot(p_ref[...].astype(BF16), wp_ref[...])
    o_ref[...] = _rms(x3, gfin_ref[...])


def _final(x1, y, p, gple, wg, wp, gfin):
    n = x1.shape[0]
    return pl.pallas_call(
        _final_kernel,
        out_shape=jax.ShapeDtypeStruct((n, D_MODEL), F32),
        grid=(n // TM,),
        in_specs=[_rows(D_MODEL), _rows(D_MODEL), _rows(PLE_DIM), _const(gple.shape), _const(wg.shape),
                  _const(wp.shape), _const(gfin.shape)],
        out_specs=_rows(D_MODEL),
        compiler_params=_params(),
        name="ple_final",
    )(x1, y, p, gple, wg, wp, gfin)


def _rope_tables():
    half = ROPE // 2
    freqs = ROPE_THETA ** (-np.arange(half, dtype=np.float32) / half)
    pos = np.concatenate([np.tile(np.arange(SEQ), BATCH), np.tile(PAST_LEN + np.arange(DEC_SEQ), DEC_BATCH)])
    ang = jnp.asarray(pos, F32)[:, None] * jnp.asarray(freqs, F32)[None, :]
    cos, sin = jnp.cos(ang), jnp.sin(ang)
    n = pos.shape[0]
    pad = jnp.zeros((n, LANES - NOPE - ROPE), F32)
    cos_t = jnp.concatenate([jnp.ones((n, NOPE), F32), cos, cos, pad], axis=1)
    sin_t = jnp.concatenate([jnp.zeros((n, NOPE), F32), sin, sin, pad], axis=1)
    return cos_t, sin_t


def _rot_cols(w):
    half = ROPE // 2
    return jnp.concatenate([-w[..., half:], w[..., :half]], axis=-1)


def _rel_bias(table, q_pos, k_pos):
    rel = np.clip(q_pos[:, None] - k_pos[None, :], -REL_CLIP, REL_CLIP) + REL_CLIP
    return table[:, rel]


def kernel(x_prompt, x_sample, cache_latent, cache_krope, cache_band_k, cache_band_v, p_prompt, p_sample, g_mix, w_in, g_q_lora, g_kv_lora, w_uq, w_uk, w_uv, rel_bias, w_a_proj, w_b_proj, w_out, g_ffn, w_query, sub_keys, expert_u, expert_v, g_ple, w_ple_gate, w_ple_proj, g_final):
    w = w_in[0]
    o_kr = Q_LORA + KV_LORA
    o_qb = o_kr + ROPE
    o_ga = o_qb + 3 * HB
    w_kr = w[:, o_kr:o_qb]
    z64 = jnp.zeros((D_MODEL, NOPE), F32)
    z32 = jnp.zeros((D_MODEL, LANES - NOPE - ROPE), F32)
    w1 = jnp.concatenate([w[:, :o_kr], z64, w_kr, z32, z64, _rot_cols(w_kr), z32], axis=1).astype(BF16)
    w2 = w[:, o_qb:o_ga].astype(BF16)
    w3 = w[:, o_ga:].astype(BF16)
    wq = w_uq[0].reshape(Q_LORA, H_A, NOPE + ROPE)
    zq64 = jnp.zeros((Q_LORA, H_A, NOPE), F32)
    zq32 = jnp.zeros((Q_LORA, H_A, LANES - NOPE - ROPE), F32)
    wuq1 = jnp.concatenate([wq, zq32], axis=-1).reshape(Q_LORA, H_A * LANES).astype(BF16)
    wuq2 = jnp.concatenate([zq64, _rot_cols(wq[..., NOPE:]), zq32], axis=-1).reshape(Q_LORA, H_A * LANES).astype(BF16)
    wukp = jnp.concatenate([w_uk[0], jnp.zeros((KV_LORA, H_A, LANES - NOPE), F32)], axis=-1)
    wukp = wukp.reshape(KV_LORA, H_A * LANES).astype(BF16)
    wuv = w_uv[0].reshape(KV_LORA, H_A * V_A).astype(BF16)
    wukt = jnp.transpose(w_uk[0], (1, 2, 0)).astype(BF16)
    wuvh = jnp.transpose(w_uv[0], (1, 0, 2)).astype(BF16)
    cos_t, sin_t = _rope_tables()
    row = lambda g: g.reshape(1, -1)

    x = jnp.concatenate([x_prompt.reshape(N_P, D_MODEL), x_sample.reshape(N_S, D_MODEL)], axis=0)
    p = jnp.concatenate([p_prompt.reshape(N_P, PLE_DIM), p_sample.reshape(N_S, PLE_DIM)], axis=0)

    (lat, kr, q16, k16, v16, qb16, kb16, vb16, kb32, vb32, sga, sgb) = _inproj(
        x, row(g_mix[0]), w1, w2, w3, row(g_q_lora[0]), row(g_kv_lora[0]), wuq1, wuq2, wukp, wuv, cos_t, sin_t)

    oa_p = _mla_prompt(q16, k16, v16)
    oa_s = _mla_sample(q16, cache_latent[0], cache_krope[0], lat, kr, wukt, wuvh)

    pad = ((0, 0), (WINDOW_B, 0), (0, 0))
    kpad = jnp.pad(kb16[:N_P].reshape(BATCH, SEQ, HB), pad)
    vpad = jnp.pad(vb16[:N_P].reshape(BATCH, SEQ, HB), pad)
    tab = rel_bias[0]
    bias_p = _rel_bias(tab, np.arange(CHUNK) + WINDOW_B, np.arange(BAND))
    bias_s = _rel_bias(tab, PAST_LEN + np.arange(DEC_SEQ), PAST_LEN - WINDOW_B + np.arange(WINDOW_B + DEC_SEQ))
    ob_p = _band_prompt(qb16, kpad, vpad, bias_p)
    ob_s, bk_s, bv_s = _band_sample(qb16, cache_band_k[0].reshape(DEC_BATCH, WINDOW_B, HB),
                                    cache_band_v[0].reshape(DEC_BATCH, WINDOW_B, HB),
                                    kb16, vb16, kb32, vb32, bias_s[:, :, :WINDOW_B], bias_s[:, :, WINDOW_B:])

    oa = jnp.concatenate([oa_p, oa_s], axis=0)
    ob = jnp.concatenate([ob_p, ob_s], axis=0)
    x1, h2, h16 = _mix_out(oa, ob, sga, sgb, x, w_a_proj[0].astype(BF16), w_b_proj[0].astype(BF16),
                           w_out[0].astype(BF16), row(g_ffn[0]))
    wqt = jnp.transpose(w_query[0]).astype(BF16)
    keys = sub_keys[0].reshape(2 * PEER_HEADS, N_KEYS, N_KEYS).astype(BF16)
    idx_t, g_t = _peer_topk(h16, wqt, keys)
    idx = jnp.swapaxes(idx_t, 1, 2).reshape(N_TOK, SLOTS)
    gw = jnp.swapaxes(g_t, 1, 2).reshape(N_TOK, SLOTS)

    c = _peer_u(idx, h2, gw, _pack_table(expert_u[0]))
    y = _peer_v(idx, c, _pack_table(expert_v[0]))

    out = _final(x1, y, p, row(g_ple[0]), w_ple_gate[0].astype(BF16), w_ple_proj[0].astype(BF16), row(g_final))

    y_prompt = out[:N_P].reshape(BATCH, SEQ, D_MODEL)
    y_sample = out[N_P:].reshape(DEC_BATCH, DEC_SEQ, D_MODEL)
    lat_p = lat[:N_P].reshape(1, BATCH, SEQ, KV_LORA)
    kr_p = kr[:N_P].reshape(1, BATCH, SEQ, ROPE)
    bk_p = kb32[:N_P].reshape(BATCH, SEQ, H_B, DH_B)[None, :, SEQ - WINDOW_B:]
    bv_p = vb32[:N_P].reshape(BATCH, SEQ, H_B, DH_B)[None, :, SEQ - WINDOW_B:]
    lat_s = lat[N_P:].reshape(1, DEC_BATCH, DEC_SEQ, KV_LORA)
    kr_s = kr[N_P:].reshape(1, DEC_BATCH, DEC_SEQ, ROPE)
    bk_s = bk_s.reshape(1, DEC_BATCH, WINDOW_B, H_B, DH_B)
    bv_s = bv_s.reshape(1, DEC_BATCH, WINDOW_B, H_B, DH_B)
    return (y_prompt, y_sample, lat_p, kr_p, bk_p, bv_p, lat_s, kr_s, bk_s, bv_s)
```

```python
import functools
import jax
import jax.numpy as jnp
import numpy as np
from jax import lax
from jax.experimental import pallas as pl
from jax.experimental.pallas import tpu as pltpu

D_MODEL = 1024
BATCH = 16
SEQ = 2048
DEC_BATCH = 32
DEC_SEQ = 32
PAST_LEN = 4096
CHUNK = 64
EPS = 1e-6
H_A = 8
Q_LORA = 384
KV_LORA = 256
NOPE = 64
ROPE = 32
V_A = 64
ROPE_THETA = 10000.0
MLA_SCALE = (NOPE + ROPE) ** -0.5
H_B = 8
DH_B = 64
BAND_PREV = 8
BAND = (BAND_PREV + 1) * CHUNK
WINDOW_B = BAND_PREV * CHUNK
REL_CLIP = 256
BAND_SCALE = DH_B ** -0.5
PEER_HEADS = 8
N_KEYS = 128
N_EXPERTS = N_KEYS * N_KEYS
TOPK = 16
PLE_DIM = 256

N_P = BATCH * SEQ
N_S = DEC_BATCH * DEC_SEQ
N_TOK = N_P + N_S
HB = H_B * DH_B
HALF = D_MODEL // 2
SLOTS = PEER_HEADS * TOPK
LANES = 128
SUBLANES = 8
TM = 256
TQ = 128
KV_STEP = 512
TB = 32
ROW_WORDS = HALF // LANES
STRIDE = 136
HEADS_PER_TRIP = 2
IDX_GROUP = 16
VMEM_LIMIT = 56 * 1024 * 1024
NEG = -1e30
BF16 = jnp.bfloat16
F32 = jnp.float32
NT = (((1,), (1,)), ((), ()))


def _params(n_axes=1):
    return pltpu.CompilerParams(dimension_semantics=("arbitrary",) * n_axes,
                                vmem_limit_bytes=VMEM_LIMIT)


def _const(shape):
    nd = len(shape)
    return pl.BlockSpec(shape, lambda *_: (0,) * nd, pipeline_mode=pl.Buffered(1))


def _rows(width, tile=TM):
    return pl.BlockSpec((tile, width), lambda i: (i, 0))


def _rms(x, g):
    return x * lax.rsqrt(jnp.mean(x * x, axis=-1, keepdims=True) + EPS) * g


def _dot(a, b):
    return jnp.dot(a, b, preferred_element_type=F32)


def _softmax_rows(s_list):
    m = functools.reduce(jnp.maximum, [jnp.max(s, axis=-1, keepdims=True) for s in s_list])
    p_list = [jnp.exp(s - m) for s in s_list]
    inv = 1.0 / functools.reduce(jnp.add, [jnp.sum(p, axis=-1, keepdims=True) for p in p_list])
    return [(p * inv).astype(BF16) for p in p_list]


def _inproj_kernel(x_ref, gmix_ref, w1_ref, w2_ref, w3_ref, gq_ref, gkv_ref, wuq1_ref, wuq2_ref,
                   wukp_ref, wuv_ref, cos_ref, sin_ref,
                   lat_ref, kr_ref, q16_ref, k16_ref, v16_ref, qb16_ref, kb16_ref, vb16_ref,
                   kb32_ref, vb32_ref, sga_ref, sgb_ref):
    h = _rms(x_ref[...], gmix_ref[...]).astype(BF16)
    cos = cos_ref[...]
    sin = sin_ref[...]
    z1 = _dot(h, w1_ref[...])
    k128 = z1[:, 640:768] * cos + z1[:, 768:896] * sin
    kr_ref[...] = k128[:, NOPE:NOPE + ROPE]
    cqn = _rms(z1[:, :Q_LORA], gq_ref[...]).astype(BF16)
    cos8 = jnp.concatenate([cos] * H_A, axis=1)
    sin8 = jnp.concatenate([sin] * H_A, axis=1)
    q = _dot(cqn, wuq1_ref[...]) * cos8 + _dot(cqn, wuq2_ref[...]) * sin8
    q16_ref[...] = q.astype(BF16)
    ckvn = _rms(z1[:, Q_LORA:Q_LORA + KV_LORA], gkv_ref[...])
    lat_ref[...] = ckvn
    ckvn16 = ckvn.astype(BF16)
    kk = _dot(ckvn16, wukp_ref[...]) + jnp.concatenate([k128] * H_A, axis=1)
    k16_ref[...] = kk.astype(BF16)
    v16_ref[...] = _dot(ckvn16, wuv_ref[...]).astype(BF16)
    z2 = _dot(h, w2_ref[...])
    qb16_ref[...] = z2[:, :HB].astype(BF16)
    kb = z2[:, HB:2 * HB]
    vb = z2[:, 2 * HB:]
    kb32_ref[...] = kb
    vb32_ref[...] = vb
    kb16_ref[...] = kb.astype(BF16)
    vb16_ref[...] = vb.astype(BF16)
    z3 = _dot(h, w3_ref[...])
    sga_ref[...] = jax.nn.sigmoid(z3[:, :D_MODEL])
    sgb_ref[...] = jax.nn.sigmoid(z3[:, D_MODEL:])


def _rope_rows():
    per_seq = SEQ // TM
    return pl.BlockSpec((TM, LANES), lambda i: (jnp.where(i < N_P // TM, i % per_seq, per_seq), 0))


def _inproj(x, gmix, w1, w2, w3, gq, gkv, wuq1, wuq2, wukp, wuv, cos, sin):
    n = x.shape[0]
    widths = [(KV_LORA, F32), (ROPE, F32), (H_A * LANES, BF16), (H_A * LANES, BF16), (H_A * V_A, BF16),
              (HB, BF16), (HB, BF16), (HB, BF16), (HB, F32), (HB, F32), (D_MODEL, F32), (D_MODEL, F32)]
    return pl.pallas_call(
        _inproj_kernel,
        out_shape=[jax.ShapeDtypeStruct((n, w), d) for w, d in widths],
        grid=(n // TM,),
        in_specs=[_rows(D_MODEL), _const(gmix.shape), _const(w1.shape), _const(w2.shape), _const(w3.shape),
                  _const(gq.shape), _const(gkv.shape), _const(wuq1.shape), _const(wuq2.shape),
                  _const(wukp.shape), _const(wuv.shape), _rope_rows(), _rope_rows()],
        out_specs=[_rows(w) for w, _ in widths],
        compiler_params=_params(),
        name="inproj",
    )(x, gmix, w1, w2, w3, gq, gkv, wuq1, wuq2, wukp, wuv, cos, sin)


def _pair_select(o_even, o_odd):
    lane = lax.broadcasted_iota(jnp.int32, o_even.shape, 1)
    return jnp.where(lane < V_A, o_even, o_odd)


def _mla_prompt_tile(q_ref, k_ref, v_ref, o_ref, i, nk):
    q_chunk = (i * TQ + lax.broadcasted_iota(jnp.int32, (TQ, nk), 0)) // CHUNK
    k_chunk = lax.broadcasted_iota(jnp.int32, (TQ, nk), 1) // CHUNK
    mask = k_chunk <= q_chunk
    scores = []
    for hd in range(H_A):
        qh = q_ref[:, hd * LANES:(hd + 1) * LANES]
        kh = k_ref[0:nk, hd * LANES:(hd + 1) * LANES]
        s = lax.dot_general(qh, kh, NT, preferred_element_type=F32) * MLA_SCALE
        scores.append(jnp.where(mask, s, NEG))
    weights = [_softmax_rows([s])[0] for s in scores]
    outs = [_dot(weights[hd], v_ref[0:nk, (hd // 2) * LANES:(hd // 2 + 1) * LANES]) for hd in range(H_A)]
    pairs = [_pair_select(outs[2 * pr], outs[2 * pr + 1]) for pr in range(H_A // 2)]
    o_ref[...] = jnp.concatenate(pairs, axis=1).astype(BF16)


def _mla_prompt_kernel(q_ref, k_ref, v_ref, o_ref):
    i = pl.program_id(1)
    tiles_per_step = KV_STEP // TQ
    for grp in range(SEQ // KV_STEP):
        @pl.when(i // tiles_per_step == grp)
        def _():
            _mla_prompt_tile(q_ref, k_ref, v_ref, o_ref, i, (grp + 1) * KV_STEP)


def _mla_prompt(q16, k16, v16):
    nq = SEQ // TQ
    return pl.pallas_call(
        _mla_prompt_kernel,
        out_shape=jax.ShapeDtypeStruct((N_P, H_A * V_A), BF16),
        grid=(BATCH, nq),
        in_specs=[pl.BlockSpec((TQ, H_A * LANES), lambda b, i: (b * nq + i, 0)),
                  pl.BlockSpec((SEQ, H_A * LANES), lambda b, i: (b, 0)),
                  pl.BlockSpec((SEQ, H_A * V_A), lambda b, i: (b, 0))],
        out_specs=pl.BlockSpec((TQ, H_A * V_A), lambda b, i: (b * nq + i, 0)),
        compiler_params=_params(2),
        name="mla_prompt",
    )(q16, k16, v16)


def _mla_sample_kernel(q_ref, clat_ref, ckr_ref, nlat_ref, nkr_ref, wukt_ref, wuvh_ref, o_ref):
    q = q_ref[...].astype(F32)
    qlat, qrope = [], []
    for hd in range(H_A):
        qn = q[:, hd * LANES:hd * LANES + NOPE].astype(BF16)
        qlat.append(_dot(qn, wukt_ref[hd]).astype(BF16))
        qrope.append(q[:, hd * LANES + NOPE:hd * LANES + NOPE + ROPE].astype(BF16))
    qlat = jnp.concatenate(qlat, axis=0)
    qrope = jnp.concatenate(qrope, axis=0)
    clat = clat_ref[...].astype(BF16)
    ckr = ckr_ref[...].astype(BF16)
    nlat = nlat_ref[...].astype(BF16)
    nkr = nkr_ref[...].astype(BF16)
    s_c = (lax.dot_general(qlat, clat, NT, preferred_element_type=F32)
           + lax.dot_general(qrope, ckr, NT, preferred_element_type=F32)) * MLA_SCALE
    s_n = (lax.dot_general(qlat, nlat, NT, preferred_element_type=F32)
           + lax.dot_general(qrope, nkr, NT, preferred_element_type=F32)) * MLA_SCALE
    w_c, w_n = _softmax_rows([s_c, s_n])
    olat = (_dot(w_c, clat) + _dot(w_n, nlat)).astype(BF16)
    outs = [_dot(olat[hd * DEC_SEQ:(hd + 1) * DEC_SEQ, :], wuvh_ref[hd]) for hd in range(H_A)]
    o_ref[...] = jnp.concatenate(outs, axis=1).astype(BF16)


def _mla_sample(q16, cache_lat, cache_kr, lat, kr, wukt, wuvh):
    off = N_P // DEC_SEQ
    return pl.pallas_call(
        _mla_sample_kernel,
        out_shape=jax.ShapeDtypeStruct((N_S, H_A * V_A), BF16),
        grid=(DEC_BATCH,),
        in_specs=[pl.BlockSpec((DEC_SEQ, H_A * LANES), lambda b: (off + b, 0)),
                  pl.BlockSpec((None, PAST_LEN, KV_LORA), lambda b: (b, 0, 0)),
                  pl.BlockSpec((None, PAST_LEN, ROPE), lambda b: (b, 0, 0)),
                  pl.BlockSpec((DEC_SEQ, KV_LORA), lambda b: (off + b, 0)),
                  pl.BlockSpec((DEC_SEQ, ROPE), lambda b: (off + b, 0)),
                  _const(wukt.shape), _const(wuvh.shape)],
        out_specs=pl.BlockSpec((DEC_SEQ, H_A * V_A), lambda b: (b, 0)),
        compiler_params=_params(),
        name="mla_sample",
    )(q16, cache_lat, cache_kr, lat, kr, wukt, wuvh)


def _band_heads(q, blocks, bias_refs, valid):
    lane = lax.broadcasted_iota(jnp.int32, (q.shape[0], LANES), 1)
    scores = []
    for hd in range(H_B):
        sl = slice((hd // 2) * LANES, (hd // 2 + 1) * LANES)
        q2 = q[:, sl]
        own = (lane >= DH_B) if hd % 2 else (lane < DH_B)
        qm = jnp.where(own, q2, jnp.zeros_like(q2))
        ss = [lax.dot_general(qm, k[:, sl], NT, preferred_element_type=F32) * BAND_SCALE + b_ref[hd]
              for (k, _), b_ref in zip(blocks, bias_refs)]
        if valid is not None:
            ss = [jnp.where(valid, s, NEG) for s in ss]
        scores.append(ss)
    weights = [_softmax_rows(ss) for ss in scores]
    outs = []
    for hd in range(H_B):
        sl = slice((hd // 2) * LANES, (hd // 2 + 1) * LANES)
        outs.append(functools.reduce(jnp.add, [_dot(w, v[:, sl]) for w, (_, v) in zip(weights[hd], blocks)]))
    pairs = [_pair_select(outs[2 * pr], outs[2 * pr + 1]) for pr in range(H_B // 2)]
    return jnp.concatenate(pairs, axis=1).astype(BF16)


def _band_prompt_kernel(q_ref, k_ref, v_ref, bias_ref, o_ref):
    c = pl.program_id(1)
    start = pl.multiple_of(c * CHUNK, CHUNK)
    k = k_ref[pl.ds(start, BAND), :]
    v = v_ref[pl.ds(start, BAND), :]
    valid = (start - WINDOW_B + lax.broadcasted_iota(jnp.int32, (CHUNK, BAND), 1)) >= 0
    o_ref[...] = _band_heads(q_ref[...], [(k, v)], [bias_ref], valid)


def _band_prompt(qb16, kpad, vpad, bias):
    nc = SEQ // CHUNK
    return pl.pallas_call(
        _band_prompt_kernel,
        out_shape=jax.ShapeDtypeStruct((N_P, HB), BF16),
        grid=(BATCH, nc),
        in_specs=[pl.BlockSpec((CHUNK, HB), lambda b, c: (b * nc + c, 0)),
                  pl.BlockSpec((None, SEQ + WINDOW_B, HB), lambda b, c: (b, 0, 0)),
                  pl.BlockSpec((None, SEQ + WINDOW_B, HB), lambda b, c: (b, 0, 0)),
                  _const(bias.shape)],
        out_specs=pl.BlockSpec((CHUNK, HB), lambda b, c: (b * nc + c, 0)),
        compiler_params=_params(2),
        name="band_prompt",
    )(qb16, kpad, vpad, bias)


def _band_sample_kernel(q_ref, ck_ref, cv_ref, nk16_ref, nv16_ref, nk32_ref, nv32_ref, bias_c_ref, bias_n_ref,
                        o_ref, bk_ref, bv_ref):
    ck = ck_ref[...]
    cv = cv_ref[...]
    blocks = [(ck.astype(BF16), cv.astype(BF16)), (nk16_ref[...], nv16_ref[...])]
    o_ref[...] = _band_heads(q_ref[...], blocks, [bias_c_ref, bias_n_ref], None)
    keep = WINDOW_B - DEC_SEQ
    bk_ref[0:keep, :] = ck[DEC_SEQ:, :]
    bk_ref[keep:, :] = nk32_ref[...]
    bv_ref[0:keep, :] = cv[DEC_SEQ:, :]
    bv_ref[keep:, :] = nv32_ref[...]


def _band_sample(qb16, cache_k, cache_v, kb16, vb16, kb32, vb32, bias_c, bias_n):
    off = N_P // DEC_SEQ
    new = lambda: pl.BlockSpec((DEC_SEQ, HB), lambda b: (off + b, 0))
    cache = lambda: pl.BlockSpec((None, WINDOW_B, HB), lambda b: (b, 0, 0))
    return pl.pallas_call(
        _band_sample_kernel,
        out_shape=[jax.ShapeDtypeStruct((N_S, HB), BF16),
                   jax.ShapeDtypeStruct((DEC_BATCH, WINDOW_B, HB), F32),
                   jax.ShapeDtypeStruct((DEC_BATCH, WINDOW_B, HB), F32)],
        grid=(DEC_BATCH,),
        in_specs=[new(), cache(), cache(), new(), new(), new(), new(),
                  _const(bias_c.shape), _const(bias_n.shape)],
        out_specs=[pl.BlockSpec((DEC_SEQ, HB), lambda b: (b, 0)), cache(), cache()],
        compiler_params=_params(),
        name="band_sample",
    )(qb16, cache_k, cache_v, kb16, vb16, kb32, vb32, bias_c, bias_n)


def _mix_out_kernel(oa_ref, ob_ref, sga_ref, sgb_ref, x_ref, wa_ref, wb_ref, wout_ref, gffn_ref,
                    x1_ref, h2_ref, h16_ref):
    merged = sga_ref[...] * _dot(oa_ref[...], wa_ref[...]) + sgb_ref[...] * _dot(ob_ref[...], wb_ref[...])
    x1 = x_ref[...] + _dot(merged.astype(BF16), wout_ref[...])
    x1_ref[...] = x1
    h2 = _rms(x1, gffn_ref[...])
    h2_ref[...] = h2
    h16_ref[...] = h2.astype(BF16)


def _mix_out(oa, ob, sga, sgb, x, wa, wb, wout, gffn):
    n = x.shape[0]
    return pl.pallas_call(
        _mix_out_kernel,
        out_shape=[jax.ShapeDtypeStruct((n, D_MODEL), F32), jax.ShapeDtypeStruct((n, D_MODEL), F32),
                   jax.ShapeDtypeStruct((n, D_MODEL), BF16)],
        grid=(n // TM,),
        in_specs=[_rows(H_A * V_A), _rows(HB), _rows(D_MODEL), _rows(D_MODEL), _rows(D_MODEL),
                  _const(wa.shape), _const(wb.shape), _const(wout.shape), _const(gffn.shape)],
        out_specs=[_rows(D_MODEL)] * 3,
        compiler_params=_params(),
        name="mix_out",
    )(oa, ob, sga, sgb, x, wa, wb, wout, gffn)


def _top16(s, n_rows):
    row = lax.broadcasted_iota(jnp.int32, (n_rows, TQ), 0).astype(F32)
    vals, ids = [], []
    for _ in range(TOPK):
        m = jnp.max(s, axis=0, keepdims=True)
        i = jnp.min(jnp.where(s == m, row, float(n_rows)), axis=0, keepdims=True)
        vals.append(m)
        ids.append(i)
        s = jnp.where(row == i, -jnp.inf, s)
    return jnp.concatenate(vals, axis=0), jnp.concatenate(ids, axis=0)


def _pair_candidates(f1, f2):
    h = SUBLANES
    blocks = [f1(0, 1, 0, h), f1(0, 1, h, 2 * h)]
    blocks += [f1(a, a + 1, 0, h) for a in range(1, h)]
    blocks += [f2(h, 2 * h, 0, 1)]
    return jnp.concatenate(blocks, axis=0)


def _peer_topk_kernel(h_ref, wqt_ref, keys_ref, idx_ref, g_ref, qt_ref):
    qt_ref[...] = lax.dot_general(wqt_ref[...], h_ref[...], NT, preferred_element_type=F32).astype(BF16)
    r8 = lax.broadcasted_iota(jnp.int32, (SUBLANES, TQ), 0).astype(F32)
    flat = lambda a0, a1, b0, b1: (r8 + float(b0)) + float(TOPK * a0)
    flat_t = lambda a0, a1, b0, b1: (r8 + float(a0)) * float(TOPK) + float(b0)
    cflat = _pair_candidates(flat, flat_t)

    def one_head(hd):
        tops = []
        for p in range(2):
            hp = hd * 2 + p
            qs = qt_ref[pl.ds(pl.multiple_of(hp * N_KEYS, N_KEYS), N_KEYS), :]
            tops.append(_top16(_dot(keys_ref[hp], qs), N_KEYS))
        (s1, i1), (s2, i2) = tops
        add = lambda x, y: (lambda a0, a1, b0, b1: x[a0:a1, :] + y[b0:b1, :])
        cand = _pair_candidates(add(s1, s2), add(s1, s2))
        e1 = i1 * float(N_KEYS)
        cidx = _pair_candidates(add(e1, i2), add(e1, i2))
        vals, ids = [], []
        for _ in range(TOPK):
            m = jnp.max(cand, axis=0, keepdims=True)
            c = jnp.min(jnp.where(cand == m, cflat, float(TOPK * TOPK)), axis=0, keepdims=True)
            sel = cflat == c
            vals.append(m)
            ids.append(jnp.sum(jnp.where(sel, cidx, 0.0), axis=0, keepdims=True))
            cand = jnp.where(sel, -jnp.inf, cand)
        best = jnp.concatenate(vals, axis=0)
        e = jnp.exp(best - best[0:1, :])
        g = e / jnp.sum(e, axis=0, keepdims=True)
        base = pl.multiple_of(hd * TOPK, TOPK)
        g_ref[pl.ds(base, TOPK), :] = g
        idx_ref[pl.ds(base, TOPK), :] = (jnp.concatenate(ids, axis=0) * float(ROW_WORDS)).astype(jnp.int32)

    def head_group(grp, carry):
        for e in range(HEADS_PER_TRIP):
            one_head(grp * HEADS_PER_TRIP + e)
        return carry

    lax.fori_loop(0, PEER_HEADS // HEADS_PER_TRIP, head_group, 0)


def _peer_topk(h16, wqt, keys):
    n = h16.shape[0]
    nt = n // TQ
    return pl.pallas_call(
        _peer_topk_kernel,
        out_shape=[jax.ShapeDtypeStruct((nt, SLOTS, TQ), jnp.int32),
                   jax.ShapeDtypeStruct((nt, SLOTS, TQ), F32)],
        grid=(nt,),
        in_specs=[_rows(D_MODEL, TQ), _const(wqt.shape), _const(keys.shape)],
        out_specs=[pl.BlockSpec((None, SLOTS, TQ), lambda i: (i, 0, 0))] * 2,
        scratch_shapes=[pltpu.VMEM((2 * PEER_HEADS * N_KEYS, TQ), BF16)],
        compiler_params=_params(),
        name="peer_topk",
    )(h16, wqt, keys)


def _pack_table(t):
    b = lax.bitcast_convert_type(t.astype(BF16), jnp.uint16).astype(jnp.uint32)
    packed = (b[:, :HALF] << 16) | b[:, HALF:]
    return packed.reshape(N_EXPERTS * ROW_WORDS, LANES)


def _unpack(w):
    hi = lax.bitcast_convert_type(w & jnp.uint32(0xFFFF0000), F32)
    lo = lax.bitcast_convert_type(w << 16, F32)
    return hi, lo


def _smem_row(ref, t):
    out = []
    for grp in range(SLOTS // IDX_GROUP):
        sub = ref.at[0, pl.ds(t * SLOTS + grp * IDX_GROUP, IDX_GROUP)]
        out.extend(sub[k] for k in range(IDX_GROUP))
    return out


def _peer_u_kernel(idx_ref, x_ref, g_ref, tab, c_ref, prod, rbuf, abuf):
    @pl.when(pl.program_id(0) == 0)
    def _():
        rbuf[...] = jnp.zeros_like(rbuf)

    ones = jnp.ones((SUBLANES, LANES), BF16)

    def token(t, carry):
        r = rbuf[(t + 1) % 2]
        r1 = r.astype(BF16)
        r2 = (r - r1.astype(F32)).astype(BF16)
        s = (lax.dot_general(ones, r1, NT, preferred_element_type=F32)
             + lax.dot_general(ones, r2, NT, preferred_element_type=F32))
        abuf[pl.ds(t, 1), :] = s[0:1, :]

        tt = jnp.minimum(t, TB - 1)
        xt = x_ref[tt]
        x_a = xt[0:ROW_WORDS, :]
        x_b = xt[ROW_WORDS:, :]
        for j, i in enumerate(_smem_row(idx_ref, tt)):
            hi, lo = _unpack(tab[pl.ds(pl.multiple_of(i, ROW_WORDS), ROW_WORDS), :])
            prod[pl.ds(j, ROW_WORDS, stride=STRIDE), :] = hi * x_a + lo * x_b
        rbuf[t % 2] = (prod[pl.ds(0, SLOTS), :] + prod[pl.ds(STRIDE, SLOTS), :]
                       + prod[pl.ds(2 * STRIDE, SLOTS), :] + prod[pl.ds(3 * STRIDE, SLOTS), :])
        return carry

    lax.fori_loop(0, TB + 1, token, 0)
    a = abuf[pl.ds(1, TB), :]
    gelu = 0.5 * a * (1.0 + lax.erf(a * (2.0 ** -0.5)))
    c_ref[...] = g_ref[...] * gelu


def _peer_u(idx, x, g, tab_u):
    n = x.shape[0]
    nb = n // TB
    return pl.pallas_call(
        _peer_u_kernel,
        out_shape=jax.ShapeDtypeStruct((n, SLOTS), F32),
        grid=(nb,),
        in_specs=[pl.BlockSpec((None, 1, TB * SLOTS), lambda i: (i, 0, 0), memory_space=pltpu.SMEM),
                  pl.BlockSpec((TB, SUBLANES, LANES), lambda i: (i, 0, 0)),
                  pl.BlockSpec((TB, SLOTS), lambda i: (i, 0)),
                  _const(tab_u.shape)],
        out_specs=pl.BlockSpec((TB, SLOTS), lambda i: (i, 0)),
        scratch_shapes=[pltpu.VMEM((ROW_WORDS * STRIDE, LANES), F32),
                        pltpu.VMEM((2, SLOTS, LANES), F32),
                        pltpu.VMEM((TB + 8, SLOTS), F32)],
        compiler_params=_params(),
        name="peer_u",
    )(idx.reshape(nb, 1, TB * SLOTS), x.reshape(n, SUBLANES, LANES), g, tab_u)


def _peer_v_kernel(idx_ref, c_ref, tab, y_ref):
    def token(t, carry):
        acc_a = jnp.zeros((ROW_WORDS, LANES), F32)
        acc_b = jnp.zeros((ROW_WORDS, LANES), F32)
        for j in range(SLOTS):
            i = pl.multiple_of(idx_ref[t, j], ROW_WORDS)
            hi, lo = _unpack(tab[pl.ds(i, ROW_WORDS), :])
            c = c_ref[t, j]
            acc_a = acc_a + c * hi
            acc_b = acc_b + c * lo
        y_ref[t] = jnp.concatenate([acc_a, acc_b], axis=0)
        return carry

    lax.fori_loop(0, TB, token, 0)


def _peer_v(idx, c, tab_v):
    n = c.shape[0]
    nb = n // TB
    y = pl.pallas_call(
        _peer_v_kernel,
        out_shape=jax.ShapeDtypeStruct((n, SUBLANES, LANES), F32),
        grid=(nb,),
        in_specs=[pl.BlockSpec((None, TB, SLOTS), lambda i: (i, 0, 0), memory_space=pltpu.SMEM),
                  pl.BlockSpec((None, TB, SLOTS), lambda i: (i, 0, 0), memory_space=pltpu.SMEM),
                  _const(tab_v.shape)],
        out_specs=pl.BlockSpec((TB, SUBLANES, LANES), lambda i: (i, 0, 0)),
        compiler_params=_params(),
        name="peer_v",
    )(idx.reshape(nb, TB, SLOTS), c.reshape(nb, TB, SLOTS), tab_v)
    return y.reshape(n, D_MODEL)


def _final_kernel(x1_ref, y_ref, p_ref, gple_ref, wg_ref, wp_ref, gfin_ref, o_ref):
    x2 = x1_ref[...] + y_ref[...]
    gate = jax.nn.sigmoid(_dot(_rms(x2, gple_ref[...]).astype(BF16), wg_ref[...]))
    x3 = x2 + gate * _dot(p_ref[...].astype(BF16), wp_ref[...])
    o_ref[...] = _rms(x3, gfin_ref[...])


def _final(x1, y, p, gple, wg, wp, gfin):
    n = x1.shape[0]
    return pl.pallas_call(
        _final_kernel,
        out_shape=jax.ShapeDtypeStruct((n, D_MODEL), F32),
        grid=(n // TM,),
        in_specs=[_rows(D_MODEL), _rows(D_MODEL), _rows(PLE_DIM), _const(gple.shape), _const(wg.shape),
                  _const(wp.shape), _const(gfin.shape)],
        out_specs=_rows(D_MODEL),
        compiler_params=_params(),
        name="ple_final",
    )(x1, y, p, gple, wg, wp, gfin)


def _rope_tables():
    half = ROPE // 2
    freqs = ROPE_THETA ** (-np.arange(half, dtype=np.float32) / half)
    pos = np.concatenate([np.arange(SEQ), np.tile(PAST_LEN + np.arange(DEC_SEQ), TM // DEC_SEQ)])
    ang = jnp.asarray(pos, F32)[:, None] * jnp.asarray(freqs, F32)[None, :]
    cos, sin = jnp.cos(ang), jnp.sin(ang)
    n = pos.shape[0]
    pad = jnp.zeros((n, LANES - NOPE - ROPE), F32)
    cos_t = jnp.concatenate([jnp.ones((n, NOPE), F32), cos, cos, pad], axis=1)
    sin_t = jnp.concatenate([jnp.zeros((n, NOPE), F32), sin, sin, pad], axis=1)
    return cos_t, sin_t


def _rot_cols(w):
    half = ROPE // 2
    return jnp.concatenate([-w[..., half:], w[..., :half]], axis=-1)


def _rel_bias(table, n_q, n_k):
    diag = np.arange(-(n_q - 1), n_k)
    line = table[:, np.clip(WINDOW_B - diag, -REL_CLIP, REL_CLIP) + REL_CLIP]
    return jnp.stack([line[:, n_q - 1 - i:n_q - 1 - i + n_k] for i in range(n_q)], axis=1)


def kernel(x_prompt, x_sample, cache_latent, cache_krope, cache_band_k, cache_band_v, p_prompt, p_sample, g_mix, w_in, g_q_lora, g_kv_lora, w_uq, w_uk, w_uv, rel_bias, w_a_proj, w_b_proj, w_out, g_ffn, w_query, sub_keys, expert_u, expert_v, g_ple, w_ple_gate, w_ple_proj, g_final):
    w = w_in[0]
    o_kr = Q_LORA + KV_LORA
    o_qb = o_kr + ROPE
    o_ga = o_qb + 3 * HB
    w_kr = w[:, o_kr:o_qb]
    z64 = jnp.zeros((D_MODEL, NOPE), F32)
    z32 = jnp.zeros((D_MODEL, LANES - NOPE - ROPE), F32)
    w1 = jnp.concatenate([w[:, :o_kr], z64, w_kr, z32, z64, _rot_cols(w_kr), z32], axis=1).astype(BF16)
    w2 = w[:, o_qb:o_ga].astype(BF16)
    w3 = w[:, o_ga:].astype(BF16)
    wq = w_uq[0].reshape(Q_LORA, H_A, NOPE + ROPE)
    zq64 = jnp.zeros((Q_LORA, H_A, NOPE), F32)
    zq32 = jnp.zeros((Q_LORA, H_A, LANES - NOPE - ROPE), F32)
    wuq1 = jnp.concatenate([wq, zq32], axis=-1).reshape(Q_LORA, H_A * LANES).astype(BF16)
    wuq2 = jnp.concatenate([zq64, _rot_cols(wq[..., NOPE:]), zq32], axis=-1).reshape(Q_LORA, H_A * LANES).astype(BF16)
    wukp = jnp.concatenate([w_uk[0], jnp.zeros((KV_LORA, H_A, LANES - NOPE), F32)], axis=-1)
    wukp = wukp.reshape(KV_LORA, H_A * LANES).astype(BF16)
    wuv = w_uv[0].reshape(KV_LORA, H_A * V_A).astype(BF16)
    wukt = jnp.transpose(w_uk[0], (1, 2, 0)).astype(BF16)
    wuvh = jnp.transpose(w_uv[0], (1, 0, 2)).astype(BF16)
    cos_t, sin_t = _rope_tables()
    row = lambda g: g.reshape(1, -1)

    x = jnp.concatenate([x_prompt.reshape(N_P, D_MODEL), x_sample.reshape(N_S, D_MODEL)], axis=0)
    p = jnp.concatenate([p_prompt.reshape(N_P, PLE_DIM), p_sample.reshape(N_S, PLE_DIM)], axis=0)

    (lat, kr, q16, k16, v16, qb16, kb16, vb16, kb32, vb32, sga, sgb) = _inproj(
        x, row(g_mix[0]), w1, w2, w3, row(g_q_lora[0]), row(g_kv_lora[0]), wuq1, wuq2, wukp, wuv, cos_t, sin_t)

    oa_p = _mla_prompt(q16, k16, v16)
    oa_s = _mla_sample(q16, cache_latent[0], cache_krope[0], lat, kr, wukt, wuvh)

    pad = ((0, 0), (WINDOW_B, 0), (0, 0))
    kpad = jnp.pad(kb16[:N_P].reshape(BATCH, SEQ, HB), pad)
    vpad = jnp.pad(vb16[:N_P].reshape(BATCH, SEQ, HB), pad)
    tab = rel_bias[0]
    bias_p = _rel_bias(tab, CHUNK, BAND)
    bias_s = _rel_bias(tab, DEC_SEQ, WINDOW_B + DEC_SEQ)
    ob_p = _band_prompt(qb16, kpad, vpad, bias_p)
    ob_s, bk_s, bv_s = _band_sample(qb16, cache_band_k[0].reshape(DEC_BATCH, WINDOW_B, HB),
                                    cache_band_v[0].reshape(DEC_BATCH, WINDOW_B, HB),
                                    kb16, vb16, kb32, vb32, bias_s[:, :, :WINDOW_B], bias_s[:, :, WINDOW_B:])

    oa = jnp.concatenate([oa_p, oa_s], axis=0)
    ob = jnp.concatenate([ob_p, ob_s], axis=0)
    x1, h2, h16 = _mix_out(oa, ob, sga, sgb, x, w_a_proj[0].astype(BF16), w_b_proj[0].astype(BF16),
                           w_out[0].astype(BF16), row(g_ffn[0]))
    wqt = jnp.transpose(w_query[0]).astype(BF16)
    keys = sub_keys[0].reshape(2 * PEER_HEADS, N_KEYS, N_KEYS).astype(BF16)
    idx_t, g_t = _peer_topk(h16, wqt, keys)
    idx = jnp.swapaxes(idx_t, 1, 2).reshape(N_TOK, SLOTS)
    gw = jnp.swapaxes(g_t, 1, 2).reshape(N_TOK, SLOTS)

    c = _peer_u(idx, h2, gw, _pack_table(expert_u[0]))
    y = _peer_v(idx, c, _pack_table(expert_v[0]))

    out = _final(x1, y, p, row(g_ple[0]), w_ple_gate[0].astype(BF16), w_ple_proj[0].astype(BF16), row(g_final))

    y_prompt = out[:N_P].reshape(BATCH, SEQ, D_MODEL)
    y_sample = out[N_P:].reshape(DEC_BATCH, DEC_SEQ, D_MODEL)
    lat_p = lat[:N_P].reshape(1, BATCH, SEQ, KV_LORA)
    kr_p = kr[:N_P].reshape(1, BATCH, SEQ, ROPE)
    bk_p = kb32[:N_P].reshape(BATCH, SEQ, H_B, DH_B)[None, :, SEQ - WINDOW_B:]
    bv_p = vb32[:N_P].reshape(BATCH, SEQ, H_B, DH_B)[None, :, SEQ - WINDOW_B:]
    lat_s = lat[N_P:].reshape(1, DEC_BATCH, DEC_SEQ, KV_LORA)
    kr_s = kr[N_P:].reshape(1, DEC_BATCH, DEC_SEQ, ROPE)
    bk_s = bk_s.reshape(1, DEC_BATCH, WINDOW_B, H_B, DH_B)
    bv_s = bv_s.reshape(1, DEC_BATCH, WINDOW_B, H_B, DH_B)
    return (y_prompt, y_sample, lat_p, kr_p, bk_p, bv_p, lat_s, kr_s, bk_s, bv_s)
```

```python
import functools
import jax
import jax.numpy as jnp
import numpy as np
from jax import lax
from jax.experimental import pallas as pl
from jax.experimental.pallas import tpu as pltpu

D_MODEL = 1024
BATCH = 16
SEQ = 2048
DEC_BATCH = 32
DEC_SEQ = 32
PAST_LEN = 4096
CHUNK = 64
EPS = 1e-6
H_A = 8
Q_LORA = 384
KV_LORA = 256
NOPE = 64
ROPE = 32
V_A = 64
ROPE_THETA = 10000.0
MLA_SCALE = (NOPE + ROPE) ** -0.5
H_B = 8
DH_B = 64
BAND_PREV = 8
BAND = (BAND_PREV + 1) * CHUNK
WINDOW_B = BAND_PREV * CHUNK
REL_CLIP = 256
BAND_SCALE = DH_B ** -0.5
PEER_HEADS = 8
N_KEYS = 128
N_EXPERTS = N_KEYS * N_KEYS
TOPK = 16
PLE_DIM = 256

N_P = BATCH * SEQ
N_S = DEC_BATCH * DEC_SEQ
N_TOK = N_P + N_S
HB = H_B * DH_B
HALF = D_MODEL // 2
SLOTS = PEER_HEADS * TOPK
LANES = 128
SUBLANES = 8
TM = 256
TQ = 128
KV_STEP = 512
TB = 64
CREP_RING = 4
ROW_WORDS = HALF // LANES
PAIRS = SLOTS // 2
STRIDE = 72
HEADS_PER_TRIP = 2
IDX_GROUP = 16
VMEM_LIMIT = 56 * 1024 * 1024
NEG = -1e30
BF16 = jnp.bfloat16
F32 = jnp.float32
NT = (((1,), (1,)), ((), ()))


def _params(n_axes=1):
    return pltpu.CompilerParams(dimension_semantics=("arbitrary",) * n_axes,
                                vmem_limit_bytes=VMEM_LIMIT)


def _const(shape):
    nd = len(shape)
    return pl.BlockSpec(shape, lambda *_: (0,) * nd, pipeline_mode=pl.Buffered(1))


def _rows(width, tile=TM):
    return pl.BlockSpec((tile, width), lambda i: (i, 0))


def _rms(x, g):
    return x * lax.rsqrt(jnp.mean(x * x, axis=-1, keepdims=True) + EPS) * g


def _dot(a, b):
    return jnp.dot(a, b, preferred_element_type=F32)


def _softmax_rows(s_list):
    m = functools.reduce(jnp.maximum, [jnp.max(s, axis=-1, keepdims=True) for s in s_list])
    p_list = [jnp.exp(s - m) for s in s_list]
    inv = 1.0 / functools.reduce(jnp.add, [jnp.sum(p, axis=-1, keepdims=True) for p in p_list])
    return [(p * inv).astype(BF16) for p in p_list]


def _inproj_kernel(x_ref, gmix_ref, w1_ref, w2_ref, w3_ref, gq_ref, gkv_ref, wuq1_ref, wuq2_ref,
                   wukp_ref, wuv_ref, cos_ref, sin_ref,
                   lat_ref, kr_ref, q16_ref, k16_ref, v16_ref, qb16_ref, kb16_ref, vb16_ref,
                   kb32_ref, vb32_ref, sga_ref, sgb_ref):
    h = _rms(x_ref[...], gmix_ref[...]).astype(BF16)
    cos = cos_ref[...]
    sin = sin_ref[...]
    z1 = _dot(h, w1_ref[...])
    k128 = z1[:, 640:768] * cos + z1[:, 768:896] * sin
    kr_ref[...] = k128[:, NOPE:NOPE + ROPE]
    cqn = _rms(z1[:, :Q_LORA], gq_ref[...]).astype(BF16)
    cos8 = jnp.concatenate([cos] * H_A, axis=1)
    sin8 = jnp.concatenate([sin] * H_A, axis=1)
    q = _dot(cqn, wuq1_ref[...]) * cos8 + _dot(cqn, wuq2_ref[...]) * sin8
    q16_ref[...] = q.astype(BF16)
    ckvn = _rms(z1[:, Q_LORA:Q_LORA + KV_LORA], gkv_ref[...])
    lat_ref[...] = ckvn
    ckvn16 = ckvn.astype(BF16)
    kk = _dot(ckvn16, wukp_ref[...]) + jnp.concatenate([k128] * H_A, axis=1)
    k16_ref[...] = kk.astype(BF16)
    v16_ref[...] = _dot(ckvn16, wuv_ref[...]).astype(BF16)
    z2 = _dot(h, w2_ref[...])
    qb16_ref[...] = z2[:, :HB].astype(BF16)
    kb = z2[:, HB:2 * HB]
    vb = z2[:, 2 * HB:]
    kb32_ref[...] = kb
    vb32_ref[...] = vb
    kb16_ref[...] = kb.astype(BF16)
    vb16_ref[...] = vb.astype(BF16)
    z3 = _dot(h, w3_ref[...])
    sga_ref[...] = jax.nn.sigmoid(z3[:, :D_MODEL])
    sgb_ref[...] = jax.nn.sigmoid(z3[:, D_MODEL:])


def _rope_rows():
    per_seq = SEQ // TM
    return pl.BlockSpec((TM, LANES), lambda i: (jnp.where(i < N_P // TM, i % per_seq, per_seq), 0))


def _inproj(x, gmix, w1, w2, w3, gq, gkv, wuq1, wuq2, wukp, wuv, cos, sin):
    n = x.shape[0]
    widths = [(KV_LORA, F32), (ROPE, F32), (H_A * LANES, BF16), (H_A * LANES, BF16), (H_A * V_A, BF16),
              (HB, BF16), (HB, BF16), (HB, BF16), (HB, F32), (HB, F32), (D_MODEL, F32), (D_MODEL, F32)]
    return pl.pallas_call(
        _inproj_kernel,
        out_shape=[jax.ShapeDtypeStruct((n, w), d) for w, d in widths],
        grid=(n // TM,),
        in_specs=[_rows(D_MODEL), _const(gmix.shape), _const(w1.shape), _const(w2.shape), _const(w3.shape),
                  _const(gq.shape), _const(gkv.shape), _const(wuq1.shape), _const(wuq2.shape),
                  _const(wukp.shape), _const(wuv.shape), _rope_rows(), _rope_rows()],
        out_specs=[_rows(w) for w, _ in widths],
        compiler_params=_params(),
        name="inproj",
    )(x, gmix, w1, w2, w3, gq, gkv, wuq1, wuq2, wukp, wuv, cos, sin)


def _pair_select(o_even, o_odd):
    lane = lax.broadcasted_iota(jnp.int32, o_even.shape, 1)
    return jnp.where(lane < V_A, o_even, o_odd)


def _mla_prompt_tile(q_ref, k_ref, v_ref, o_ref, i, nk):
    q_chunk = (i * TQ + lax.broadcasted_iota(jnp.int32, (TQ, nk), 0)) // CHUNK
    k_chunk = lax.broadcasted_iota(jnp.int32, (TQ, nk), 1) // CHUNK
    mask = k_chunk <= q_chunk
    scores = []
    for hd in range(H_A):
        qh = q_ref[:, hd * LANES:(hd + 1) * LANES]
        kh = k_ref[0:nk, hd * LANES:(hd + 1) * LANES]
        s = lax.dot_general(qh, kh, NT, preferred_element_type=F32) * MLA_SCALE
        scores.append(jnp.where(mask, s, NEG))
    weights = [_softmax_rows([s])[0] for s in scores]
    outs = [_dot(weights[hd], v_ref[0:nk, (hd // 2) * LANES:(hd // 2 + 1) * LANES]) for hd in range(H_A)]
    pairs = [_pair_select(outs[2 * pr], outs[2 * pr + 1]) for pr in range(H_A // 2)]
    o_ref[...] = jnp.concatenate(pairs, axis=1).astype(BF16)


def _mla_prompt_kernel(q_ref, k_ref, v_ref, o_ref):
    i = pl.program_id(1)
    tiles_per_step = KV_STEP // TQ
    for grp in range(SEQ // KV_STEP):
        @pl.when(i // tiles_per_step == grp)
        def _():
            _mla_prompt_tile(q_ref, k_ref, v_ref, o_ref, i, (grp + 1) * KV_STEP)


def _mla_prompt(q16, k16, v16):
    nq = SEQ // TQ
    return pl.pallas_call(
        _mla_prompt_kernel,
        out_shape=jax.ShapeDtypeStruct((N_P, H_A * V_A), BF16),
        grid=(BATCH, nq),
        in_specs=[pl.BlockSpec((TQ, H_A * LANES), lambda b, i: (b * nq + i, 0)),
                  pl.BlockSpec((SEQ, H_A * LANES), lambda b, i: (b, 0)),
                  pl.BlockSpec((SEQ, H_A * V_A), lambda b, i: (b, 0))],
        out_specs=pl.BlockSpec((TQ, H_A * V_A), lambda b, i: (b * nq + i, 0)),
        compiler_params=_params(2),
        name="mla_prompt",
    )(q16, k16, v16)


def _mla_sample_kernel(q_ref, clat_ref, ckr_ref, nlat_ref, nkr_ref, wukt_ref, wuvh_ref, o_ref):
    q = q_ref[...].astype(F32)
    qlat, qrope = [], []
    for hd in range(H_A):
        qn = q[:, hd * LANES:hd * LANES + NOPE].astype(BF16)
        qlat.append(_dot(qn, wukt_ref[hd]).astype(BF16))
        qrope.append(q[:, hd * LANES + NOPE:hd * LANES + NOPE + ROPE].astype(BF16))
    qlat = jnp.concatenate(qlat, axis=0)
    qrope = jnp.concatenate(qrope, axis=0)
    clat = clat_ref[...].astype(BF16)
    ckr = ckr_ref[...].astype(BF16)
    nlat = nlat_ref[...].astype(BF16)
    nkr = nkr_ref[...].astype(BF16)
    s_c = (lax.dot_general(qlat, clat, NT, preferred_element_type=F32)
           + lax.dot_general(qrope, ckr, NT, preferred_element_type=F32)) * MLA_SCALE
    s_n = (lax.dot_general(qlat, nlat, NT, preferred_element_type=F32)
           + lax.dot_general(qrope, nkr, NT, preferred_element_type=F32)) * MLA_SCALE
    w_c, w_n = _softmax_rows([s_c, s_n])
    olat = (_dot(w_c, clat) + _dot(w_n, nlat)).astype(BF16)
    outs = [_dot(olat[hd * DEC_SEQ:(hd + 1) * DEC_SEQ, :], wuvh_ref[hd]) for hd in range(H_A)]
    o_ref[...] = jnp.concatenate(outs, axis=1).astype(BF16)


def _mla_sample(q16, cache_lat, cache_kr, lat, kr, wukt, wuvh):
    off = N_P // DEC_SEQ
    return pl.pallas_call(
        _mla_sample_kernel,
        out_shape=jax.ShapeDtypeStruct((N_S, H_A * V_A), BF16),
        grid=(DEC_BATCH,),
        in_specs=[pl.BlockSpec((DEC_SEQ, H_A * LANES), lambda b: (off + b, 0)),
                  pl.BlockSpec((None, PAST_LEN, KV_LORA), lambda b: (b, 0, 0)),
                  pl.BlockSpec((None, PAST_LEN, ROPE), lambda b: (b, 0, 0)),
                  pl.BlockSpec((DEC_SEQ, KV_LORA), lambda b: (off + b, 0)),
                  pl.BlockSpec((DEC_SEQ, ROPE), lambda b: (off + b, 0)),
                  _const(wukt.shape), _const(wuvh.shape)],
        out_specs=pl.BlockSpec((DEC_SEQ, H_A * V_A), lambda b: (b, 0)),
        compiler_params=_params(),
        name="mla_sample",
    )(q16, cache_lat, cache_kr, lat, kr, wukt, wuvh)


def _band_heads(q, blocks, bias_refs, valid):
    lane = lax.broadcasted_iota(jnp.int32, (q.shape[0], LANES), 1)
    scores = []
    for hd in range(H_B):
        sl = slice((hd // 2) * LANES, (hd // 2 + 1) * LANES)
        q2 = q[:, sl]
        own = (lane >= DH_B) if hd % 2 else (lane < DH_B)
        qm = jnp.where(own, q2, jnp.zeros_like(q2))
        ss = [lax.dot_general(qm, k[:, sl], NT, preferred_element_type=F32) * BAND_SCALE + b_ref[hd]
              for (k, _), b_ref in zip(blocks, bias_refs)]
        if valid is not None:
            ss = [jnp.where(valid, s, NEG) for s in ss]
        scores.append(ss)
    weights = [_softmax_rows(ss) for ss in scores]
    outs = []
    for hd in range(H_B):
        sl = slice((hd // 2) * LANES, (hd // 2 + 1) * LANES)
        outs.append(functools.reduce(jnp.add, [_dot(w, v[:, sl]) for w, (_, v) in zip(weights[hd], blocks)]))
    pairs = [_pair_select(outs[2 * pr], outs[2 * pr + 1]) for pr in range(H_B // 2)]
    return jnp.concatenate(pairs, axis=1).astype(BF16)


def _band_prompt_kernel(q_ref, k_ref, v_ref, bias_ref, o_ref):
    c = pl.program_id(1)
    start = pl.multiple_of(c * CHUNK, CHUNK)
    k = k_ref[pl.ds(start, BAND), :]
    v = v_ref[pl.ds(start, BAND), :]
    valid = (start - WINDOW_B + lax.broadcasted_iota(jnp.int32, (CHUNK, BAND), 1)) >= 0
    o_ref[...] = _band_heads(q_ref[...], [(k, v)], [bias_ref], valid)


def _band_prompt(qb16, kpad, vpad, bias):
    nc = SEQ // CHUNK
    return pl.pallas_call(
        _band_prompt_kernel,
        out_shape=jax.ShapeDtypeStruct((N_P, HB), BF16),
        grid=(BATCH, nc),
        in_specs=[pl.BlockSpec((CHUNK, HB), lambda b, c: (b * nc + c, 0)),
                  pl.BlockSpec((None, SEQ + WINDOW_B, HB), lambda b, c: (b, 0, 0)),
                  pl.BlockSpec((None, SEQ + WINDOW_B, HB), lambda b, c: (b, 0, 0)),
                  _const(bias.shape)],
        out_specs=pl.BlockSpec((CHUNK, HB), lambda b, c: (b * nc + c, 0)),
        compiler_params=_params(2),
        name="band_prompt",
    )(qb16, kpad, vpad, bias)


def _band_sample_kernel(q_ref, ck_ref, cv_ref, nk16_ref, nv16_ref, nk32_ref, nv32_ref, bias_c_ref, bias_n_ref,
                        o_ref, bk_ref, bv_ref):
    ck = ck_ref[...]
    cv = cv_ref[...]
    blocks = [(ck.astype(BF16), cv.astype(BF16)), (nk16_ref[...], nv16_ref[...])]
    o_ref[...] = _band_heads(q_ref[...], blocks, [bias_c_ref, bias_n_ref], None)
    keep = WINDOW_B - DEC_SEQ
    bk_ref[0:keep, :] = ck[DEC_SEQ:, :]
    bk_ref[keep:, :] = nk32_ref[...]
    bv_ref[0:keep, :] = cv[DEC_SEQ:, :]
    bv_ref[keep:, :] = nv32_ref[...]


def _band_sample(qb16, cache_k, cache_v, kb16, vb16, kb32, vb32, bias_c, bias_n):
    off = N_P // DEC_SEQ
    new = lambda: pl.BlockSpec((DEC_SEQ, HB), lambda b: (off + b, 0))
    cache = lambda: pl.BlockSpec((None, WINDOW_B, HB), lambda b: (b, 0, 0))
    return pl.pallas_call(
        _band_sample_kernel,
        out_shape=[jax.ShapeDtypeStruct((N_S, HB), BF16),
                   jax.ShapeDtypeStruct((DEC_BATCH, WINDOW_B, HB), F32),
                   jax.ShapeDtypeStruct((DEC_BATCH, WINDOW_B, HB), F32)],
        grid=(DEC_BATCH,),
        in_specs=[new(), cache(), cache(), new(), new(), new(), new(),
                  _const(bias_c.shape), _const(bias_n.shape)],
        out_specs=[pl.BlockSpec((DEC_SEQ, HB), lambda b: (b, 0)), cache(), cache()],
        compiler_params=_params(),
        name="band_sample",
    )(qb16, cache_k, cache_v, kb16, vb16, kb32, vb32, bias_c, bias_n)


def _mix_out_kernel(oa_ref, ob_ref, sga_ref, sgb_ref, x_ref, wa_ref, wb_ref, wout_ref, gffn_ref,
                    x1_ref, h2_ref, h16_ref):
    merged = sga_ref[...] * _dot(oa_ref[...], wa_ref[...]) + sgb_ref[...] * _dot(ob_ref[...], wb_ref[...])
    x1 = x_ref[...] + _dot(merged.astype(BF16), wout_ref[...])
    x1_ref[...] = x1
    h2 = _rms(x1, gffn_ref[...])
    h2_ref[...] = h2
    h16_ref[...] = h2.astype(BF16)


def _mix_out(oa, ob, sga, sgb, x, wa, wb, wout, gffn):
    n = x.shape[0]
    return pl.pallas_call(
        _mix_out_kernel,
        out_shape=[jax.ShapeDtypeStruct((n, D_MODEL), F32), jax.ShapeDtypeStruct((n, D_MODEL), F32),
                   jax.ShapeDtypeStruct((n, D_MODEL), BF16)],
        grid=(n // TM,),
        in_specs=[_rows(H_A * V_A), _rows(HB), _rows(D_MODEL), _rows(D_MODEL), _rows(D_MODEL),
                  _const(wa.shape), _const(wb.shape), _const(wout.shape), _const(gffn.shape)],
        out_specs=[_rows(D_MODEL)] * 3,
        compiler_params=_params(),
        name="mix_out",
    )(oa, ob, sga, sgb, x, wa, wb, wout, gffn)


def _top16(s, n_rows):
    row = lax.broadcasted_iota(jnp.int32, (n_rows, TQ), 0).astype(F32)
    vals, ids = [], []
    for _ in range(TOPK):
        m = jnp.max(s, axis=0, keepdims=True)
        i = jnp.min(jnp.where(s == m, row, float(n_rows)), axis=0, keepdims=True)
        vals.append(m)
        ids.append(i)
        s = jnp.where(row == i, -jnp.inf, s)
    return jnp.concatenate(vals, axis=0), jnp.concatenate(ids, axis=0)


def _pair_candidates(f1, f2):
    h = SUBLANES
    blocks = [f1(0, 1, 0, h), f1(0, 1, h, 2 * h)]
    blocks += [f1(a, a + 1, 0, h) for a in range(1, h)]
    blocks += [f2(h, 2 * h, 0, 1)]
    return jnp.concatenate(blocks, axis=0)


def _peer_topk_kernel(h_ref, wqt_ref, keys_ref, idx_ref, g_ref, qt_ref):
    qt_ref[...] = lax.dot_general(wqt_ref[...], h_ref[...], NT, preferred_element_type=F32).astype(BF16)
    r8 = lax.broadcasted_iota(jnp.int32, (SUBLANES, TQ), 0).astype(F32)
    flat = lambda a0, a1, b0, b1: (r8 + float(b0)) + float(TOPK * a0)
    flat_t = lambda a0, a1, b0, b1: (r8 + float(a0)) * float(TOPK) + float(b0)
    cflat = _pair_candidates(flat, flat_t)

    def one_head(hd):
        tops = []
        for p in range(2):
            hp = hd * 2 + p
            qs = qt_ref[pl.ds(pl.multiple_of(hp * N_KEYS, N_KEYS), N_KEYS), :]
            tops.append(_top16(_dot(keys_ref[hp], qs), N_KEYS))
        (s1, i1), (s2, i2) = tops
        add = lambda x, y: (lambda a0, a1, b0, b1: x[a0:a1, :] + y[b0:b1, :])
        cand = _pair_candidates(add(s1, s2), add(s1, s2))
        e1 = i1 * float(N_KEYS)
        cidx = _pair_candidates(add(e1, i2), add(e1, i2))
        vals, ids = [], []
        for _ in range(TOPK):
            m = jnp.max(cand, axis=0, keepdims=True)
            c = jnp.min(jnp.where(cand == m, cflat, float(TOPK * TOPK)), axis=0, keepdims=True)
            sel = cflat == c
            vals.append(m)
            ids.append(jnp.sum(jnp.where(sel, cidx, 0.0), axis=0, keepdims=True))
            cand = jnp.where(sel, -jnp.inf, cand)
        best = jnp.concatenate(vals, axis=0)
        e = jnp.exp(best - best[0:1, :])
        g = e / jnp.sum(e, axis=0, keepdims=True)
        base = pl.multiple_of(hd * TOPK, TOPK)
        g_ref[pl.ds(base, TOPK), :] = g
        idx_ref[pl.ds(base, TOPK), :] = (jnp.concatenate(ids, axis=0) * float(ROW_WORDS)).astype(jnp.int32)

    def head_group(grp, carry):
        for e in range(HEADS_PER_TRIP):
            one_head(grp * HEADS_PER_TRIP + e)
        return carry

    lax.fori_loop(0, PEER_HEADS // HEADS_PER_TRIP, head_group, 0)


def _peer_topk(h16, wqt, keys):
    n = h16.shape[0]
    nt = n // TQ
    return pl.pallas_call(
        _peer_topk_kernel,
        out_shape=[jax.ShapeDtypeStruct((nt, SLOTS, TQ), jnp.int32),
                   jax.ShapeDtypeStruct((nt, SLOTS, TQ), F32)],
        grid=(nt,),
        in_specs=[_rows(D_MODEL, TQ), _const(wqt.shape), _const(keys.shape)],
        out_specs=[pl.BlockSpec((None, SLOTS, TQ), lambda i: (i, 0, 0))] * 2,
        scratch_shapes=[pltpu.VMEM((2 * PEER_HEADS * N_KEYS, TQ), BF16)],
        compiler_params=_params(),
        name="peer_topk",
    )(h16, wqt, keys)


def _pack_table(t):
    b = lax.bitcast_convert_type(t.astype(BF16), jnp.uint16).astype(jnp.uint32)
    packed = (b[:, :HALF] << 16) | b[:, HALF:]
    return packed.reshape(N_EXPERTS * ROW_WORDS, LANES)


def _unpack(w):
    hi = lax.bitcast_convert_type(w & jnp.uint32(0xFFFF0000), F32)
    lo = lax.bitcast_convert_type(w << 16, F32)
    return hi, lo


def _slot_pairs(idx_ref, t):
    out = []
    for grp in range(PAIRS // IDX_GROUP):
        lo = idx_ref.at[0, pl.ds(t * SLOTS + grp * IDX_GROUP, IDX_GROUP)]
        hi = idx_ref.at[0, pl.ds(t * SLOTS + PAIRS + grp * IDX_GROUP, IDX_GROUP)]
        out.extend((lo[k], hi[k]) for k in range(IDX_GROUP))
    return out


def _load_pair(tab, ia, ib):
    wa = tab[pl.ds(pl.multiple_of(ia, ROW_WORDS), ROW_WORDS), :]
    wb = tab[pl.ds(pl.multiple_of(ib, ROW_WORDS), ROW_WORDS), :]
    return _unpack(jnp.concatenate([wa, wb], axis=0))


def _peer_u_kernel(idx_ref, x_ref, g_ref, tab, c_ref, prod_a, prod_b, r_hi, r_lo, abuf):
    @pl.when(pl.program_id(0) == 0)
    def _():
        r_hi[...] = jnp.zeros_like(r_hi)
        r_lo[...] = jnp.zeros_like(r_lo)

    ones = jnp.ones((SUBLANES, LANES), BF16)

    def reduce(slot, out_row):
        s = (lax.dot_general(ones, r_hi[slot], NT, preferred_element_type=F32)
             + lax.dot_general(ones, r_lo[slot], NT, preferred_element_type=F32))
        abuf[pl.ds(out_row, 1), :] = s[0:1, :]

    def gather(t, slot, prod):
        xt = x_ref[t]
        x_hi = jnp.concatenate([xt[0:ROW_WORDS, :]] * 2, axis=0)
        x_lo = jnp.concatenate([xt[ROW_WORDS:, :]] * 2, axis=0)
        for j, (ia, ib) in enumerate(_slot_pairs(idx_ref, t)):
            hi, lo = _load_pair(tab, ia, ib)
            prod[pl.ds(j, SUBLANES, stride=STRIDE), :] = hi * x_hi + lo * x_lo
        halves = [functools.reduce(jnp.add, [prod[pl.ds((h * ROW_WORDS + s) * STRIDE, PAIRS), :]
                                             for s in range(ROW_WORDS)]) for h in range(2)]
        r = jnp.concatenate(halves, axis=0)
        hi16 = r.astype(BF16)
        r_hi[slot] = hi16
        r_lo[slot] = (r - hi16.astype(F32)).astype(BF16)

    def two_tokens(p, carry):
        t0 = 2 * p
        reduce(0, t0)
        reduce(1, t0 + 1)
        gather(t0, 0, prod_a)
        gather(t0 + 1, 1, prod_b)
        return carry

    lax.fori_loop(0, TB // 2, two_tokens, 0)
    reduce(0, TB)
    reduce(1, TB + 1)
    a = abuf[pl.ds(2, TB), :]
    gelu = 0.5 * a * (1.0 + lax.erf(a * (2.0 ** -0.5)))
    c_ref[...] = g_ref[...] * gelu


def _peer_u(idx, x, g, tab_u):
    n = x.shape[0]
    nb = n // TB
    return pl.pallas_call(
        _peer_u_kernel,
        out_shape=jax.ShapeDtypeStruct((n, SLOTS), F32),
        grid=(nb,),
        in_specs=[pl.BlockSpec((None, 1, TB * SLOTS), lambda i: (i, 0, 0), memory_space=pltpu.SMEM),
                  pl.BlockSpec((TB, SUBLANES, LANES), lambda i: (i, 0, 0)),
                  pl.BlockSpec((TB, SLOTS), lambda i: (i, 0)),
                  _const(tab_u.shape)],
        out_specs=pl.BlockSpec((TB, SLOTS), lambda i: (i, 0)),
        scratch_shapes=[pltpu.VMEM((SUBLANES * STRIDE, LANES), F32),
                        pltpu.VMEM((SUBLANES * STRIDE, LANES), F32),
                        pltpu.VMEM((2, SLOTS, LANES), BF16),
                        pltpu.VMEM((2, SLOTS, LANES), BF16),
                        pltpu.VMEM((TB + 8, SLOTS), F32)],
        compiler_params=_params(),
        name="peer_u",
    )(idx.reshape(nb, 1, TB * SLOTS), x.reshape(n, SUBLANES, LANES), g, tab_u)


def _peer_v_kernel(idx_ref, c_ref, tab, y_ref, *crep):
    row = lax.broadcasted_iota(jnp.int32, (SUBLANES, LANES), 0)
    eye = (lax.broadcasted_iota(jnp.int32, (SLOTS, LANES), 0)
           == lax.broadcasted_iota(jnp.int32, (SLOTS, LANES), 1))

    def spread(t, crep):
        col = jnp.sum(jnp.where(eye, c_ref[pl.ds(t, 1), :], 0.0), axis=1, keepdims=True)
        crep[...] = jnp.broadcast_to(col, (SLOTS, LANES))

    def token(t, crep):
        acc_hi = jnp.zeros((SUBLANES, LANES), F32)
        acc_lo = jnp.zeros((SUBLANES, LANES), F32)
        for j, (ia, ib) in enumerate(_slot_pairs(idx_ref, t)):
            hi, lo = _load_pair(tab, ia, ib)
            cm = jnp.where(row < ROW_WORDS, crep[pl.ds(j, 1), :], crep[pl.ds(PAIRS + j, 1), :])
            acc_hi = acc_hi + cm * hi
            acc_lo = acc_lo + cm * lo
        y_ref[t] = jnp.concatenate([acc_hi[0:ROW_WORDS, :] + acc_hi[ROW_WORDS:, :],
                                    acc_lo[0:ROW_WORDS, :] + acc_lo[ROW_WORDS:, :]], axis=0)

    ahead = CREP_RING // 2
    for k in range(ahead):
        spread(k, crep[k])

    def ring_trip(p, carry):
        t0 = CREP_RING * p
        for k in range(CREP_RING):
            spread(jnp.minimum(t0 + k + ahead, TB - 1), crep[(k + ahead) % CREP_RING])
            token(t0 + k, crep[k])
        return carry

    lax.fori_loop(0, TB // CREP_RING, ring_trip, 0)


def _peer_v(idx, c, tab_v):
    n = c.shape[0]
    nb = n // TB
    y = pl.pallas_call(
        _peer_v_kernel,
        out_shape=jax.ShapeDtypeStruct((n, SUBLANES, LANES), F32),
        grid=(nb,),
        in_specs=[pl.BlockSpec((None, 1, TB * SLOTS), lambda i: (i, 0, 0), memory_space=pltpu.SMEM),
                  pl.BlockSpec((TB, SLOTS), lambda i: (i, 0)),
                  _const(tab_v.shape)],
        out_specs=pl.BlockSpec((TB, SUBLANES, LANES), lambda i: (i, 0, 0)),
        scratch_shapes=[pltpu.VMEM((SLOTS, LANES), F32)] * CREP_RING,
        compiler_params=_params(),
        name="peer_v",
    )(idx.reshape(nb, 1, TB * SLOTS), c, tab_v)
    return y.reshape(n, D_MODEL)


def _final_kernel(x1_ref, y_ref, p_ref, gple_ref, wg_ref, wp_ref, gfin_ref, o_ref):
    x2 = x1_ref[...] + y_ref[...]
    gate = jax.nn.sigmoid(_dot(_rms(x2, gple_ref[...]).astype(BF16), wg_ref[...]))
    x3 = x2 + gate * _dot(p_ref[...].astype(BF16), wp_ref[...])
    o_ref[...] = _rms(x3, gfin_ref[...])


def _final(x1, y, p, gple, wg, wp, gfin):
    n = x1.shape[0]
    return pl.pallas_call(
        _final_kernel,
        out_shape=jax.ShapeDtypeStruct((n, D_MODEL), F32),
        grid=(n // TM,),
        in_specs=[_rows(D_MODEL), _rows(D_MODEL), _rows(PLE_DIM), _const(gple.shape), _const(wg.shape),
                  _const(wp.shape), _const(gfin.shape)],
        out_specs=_rows(D_MODEL),
        compiler_params=_params(),
        name="ple_final",
    )(x1, y, p, gple, wg, wp, gfin)


def _rope_tables():
    half = ROPE // 2
    freqs = ROPE_THETA ** (-np.arange(half, dtype=np.float32) / half)
    pos = np.concatenate([np.arange(SEQ), np.tile(PAST_LEN + np.arange(DEC_SEQ), TM // DEC_SEQ)])
    ang = jnp.asarray(pos, F32)[:, None] * jnp.asarray(freqs, F32)[None, :]
    cos, sin = jnp.cos(ang), jnp.sin(ang)
    n = pos.shape[0]
    pad = jnp.zeros((n, LANES - NOPE - ROPE), F32)
    cos_t = jnp.concatenate([jnp.ones((n, NOPE), F32), cos, cos, pad], axis=1)
    sin_t = jnp.concatenate([jnp.zeros((n, NOPE), F32), sin, sin, pad], axis=1)
    return cos_t, sin_t


def _rot_cols(w):
    half = ROPE // 2
    return jnp.concatenate([-w[..., half:], w[..., :half]], axis=-1)


def _rel_bias(table, n_q, n_k):
    diag = np.arange(-(n_q - 1), n_k)
    line = table[:, np.clip(WINDOW_B - diag, -REL_CLIP, REL_CLIP) + REL_CLIP]
    return jnp.stack([line[:, n_q - 1 - i:n_q - 1 - i + n_k] for i in range(n_q)], axis=1)


def kernel(x_prompt, x_sample, cache_latent, cache_krope, cache_band_k, cache_band_v, p_prompt, p_sample, g_mix, w_in, g_q_lora, g_kv_lora, w_uq, w_uk, w_uv, rel_bias, w_a_proj, w_b_proj, w_out, g_ffn, w_query, sub_keys, expert_u, expert_v, g_ple, w_ple_gate, w_ple_proj, g_final):
    w = w_in[0]
    o_kr = Q_LORA + KV_LORA
    o_qb = o_kr + ROPE
    o_ga = o_qb + 3 * HB
    w_kr = w[:, o_kr:o_qb]
    z64 = jnp.zeros((D_MODEL, NOPE), F32)
    z32 = jnp.zeros((D_MODEL, LANES - NOPE - ROPE), F32)
    w1 = jnp.concatenate([w[:, :o_kr], z64, w_kr, z32, z64, _rot_cols(w_kr), z32], axis=1).astype(BF16)
    w2 = w[:, o_qb:o_ga].astype(BF16)
    w3 = w[:, o_ga:].astype(BF16)
    wq = w_uq[0].reshape(Q_LORA, H_A, NOPE + ROPE)
    zq64 = jnp.zeros((Q_LORA, H_A, NOPE), F32)
    zq32 = jnp.zeros((Q_LORA, H_A, LANES - NOPE - ROPE), F32)
    wuq1 = jnp.concatenate([wq, zq32], axis=-1).reshape(Q_LORA, H_A * LANES).astype(BF16)
    wuq2 = jnp.concatenate([zq64, _rot_cols(wq[..., NOPE:]), zq32], axis=-1).reshape(Q_LORA, H_A * LANES).astype(BF16)
    wukp = jnp.concatenate([w_uk[0], jnp.zeros((KV_LORA, H_A, LANES - NOPE), F32)], axis=-1)
    wukp = wukp.reshape(KV_LORA, H_A * LANES).astype(BF16)
    wuv = w_uv[0].reshape(KV_LORA, H_A * V_A).astype(BF16)
    wukt = jnp.transpose(w_uk[0], (1, 2, 0)).astype(BF16)
    wuvh = jnp.transpose(w_uv[0], (1, 0, 2)).astype(BF16)
    cos_t, sin_t = _rope_tables()
    row = lambda g: g.reshape(1, -1)

    x = jnp.concatenate([x_prompt.reshape(N_P, D_MODEL), x_sample.reshape(N_S, D_MODEL)], axis=0)
    p = jnp.concatenate([p_prompt.reshape(N_P, PLE_DIM), p_sample.reshape(N_S, PLE_DIM)], axis=0)

    (lat, kr, q16, k16, v16, qb16, kb16, vb16, kb32, vb32, sga, sgb) = _inproj(
        x, row(g_mix[0]), w1, w2, w3, row(g_q_lora[0]), row(g_kv_lora[0]), wuq1, wuq2, wukp, wuv, cos_t, sin_t)

    oa_p = _mla_prompt(q16, k16, v16)
    oa_s = _mla_sample(q16, cache_latent[0], cache_krope[0], lat, kr, wukt, wuvh)

    pad = ((0, 0), (WINDOW_B, 0), (0, 0))
    kpad = jnp.pad(kb16[:N_P].reshape(BATCH, SEQ, HB), pad)
    vpad = jnp.pad(vb16[:N_P].reshape(BATCH, SEQ, HB), pad)
    tab = rel_bias[0]
    bias_p = _rel_bias(tab, CHUNK, BAND)
    bias_s = _rel_bias(tab, DEC_SEQ, WINDOW_B + DEC_SEQ)
    ob_p = _band_prompt(qb16, kpad, vpad, bias_p)
    ob_s, bk_s, bv_s = _band_sample(qb16, cache_band_k[0].reshape(DEC_BATCH, WINDOW_B, HB),
                                    cache_band_v[0].reshape(DEC_BATCH, WINDOW_B, HB),
                                    kb16, vb16, kb32, vb32, bias_s[:, :, :WINDOW_B], bias_s[:, :, WINDOW_B:])

    oa = jnp.concatenate([oa_p, oa_s], axis=0)
    ob = jnp.concatenate([ob_p, ob_s], axis=0)
    x1, h2, h16 = _mix_out(oa, ob, sga, sgb, x, w_a_proj[0].astype(BF16), w_b_proj[0].astype(BF16),
                           w_out[0].astype(BF16), row(g_ffn[0]))
    wqt = jnp.transpose(w_query[0]).astype(BF16)
    keys = sub_keys[0].reshape(2 * PEER_HEADS, N_KEYS, N_KEYS).astype(BF16)
    idx_t, g_t = _peer_topk(h16, wqt, keys)
    idx = jnp.swapaxes(idx_t, 1, 2).reshape(N_TOK, SLOTS)
    gw = jnp.swapaxes(g_t, 1, 2).reshape(N_TOK, SLOTS)

    c = _peer_u(idx, h2, gw, _pack_table(expert_u[0]))
    y = _peer_v(idx, c, _pack_table(expert_v[0]))

    out = _final(x1, y, p, row(g_ple[0]), w_ple_gate[0].astype(BF16), w_ple_proj[0].astype(BF16), row(g_final))

    y_prompt = out[:N_P].reshape(BATCH, SEQ, D_MODEL)
    y_sample = out[N_P:].reshape(DEC_BATCH, DEC_SEQ, D_MODEL)
    lat_p = lat[:N_P].reshape(1, BATCH, SEQ, KV_LORA)
    kr_p = kr[:N_P].reshape(1, BATCH, SEQ, ROPE)
    bk_p = kb32[:N_P].reshape(BATCH, SEQ, H_B, DH_B)[None, :, SEQ - WINDOW_B:]
    bv_p = vb32[:N_P].reshape(BATCH, SEQ, H_B, DH_B)[None, :, SEQ - WINDOW_B:]
    lat_s = lat[N_P:].reshape(1, DEC_BATCH, DEC_SEQ, KV_LORA)
    kr_s = kr[N_P:].reshape(1, DEC_BATCH, DEC_SEQ, ROPE)
    bk_s = bk_s.reshape(1, DEC_BATCH, WINDOW_B, H_B, DH_B)
    bv_s = bv_s.reshape(1, DEC_BATCH, WINDOW_B, H_B, DH_B)
    return (y_prompt, y_sample, lat_p, kr_p, bk_p, bv_p, lat_s, kr_s, bk_s, bv_s)
```

```python
import functools
import jax
import jax.numpy as jnp
import numpy as np
from jax import lax
from jax.experimental import pallas as pl
from jax.experimental.pallas import tpu as pltpu

D_MODEL = 1024
BATCH = 16
SEQ = 2048
DEC_BATCH = 32
DEC_SEQ = 32
PAST_LEN = 4096
CHUNK = 64
EPS = 1e-6
H_A = 8
Q_LORA = 384
KV_LORA = 256
NOPE = 64
ROPE = 32
V_A = 64
ROPE_THETA = 10000.0
MLA_SCALE = (NOPE + ROPE) ** -0.5
H_B = 8
DH_B = 64
BAND_PREV = 8
BAND = (BAND_PREV + 1) * CHUNK
WINDOW_B = BAND_PREV * CHUNK
REL_CLIP = 256
BAND_SCALE = DH_B ** -0.5
PEER_HEADS = 8
N_KEYS = 128
N_EXPERTS = N_KEYS * N_KEYS
TOPK = 16
PLE_DIM = 256

N_P = BATCH * SEQ
N_S = DEC_BATCH * DEC_SEQ
N_TOK = N_P + N_S
HB = H_B * DH_B
HALF = D_MODEL // 2
SLOTS = PEER_HEADS * TOPK
LANES = 128
SUBLANES = 8
TM = 256
TQ = 128
KV_STEP = 512
TB = 64
CREP_RING = 4
ROW_WORDS = HALF // LANES
PAIRS = SLOTS // 2
STRIDE = 72
HEADS_PER_TRIP = 2
IDX_GROUP = 16
VMEM_LIMIT = 56 * 1024 * 1024
NEG = -1e30
BF16 = jnp.bfloat16
F32 = jnp.float32
NT = (((1,), (1,)), ((), ()))


def _params(n_axes=1):
    return pltpu.CompilerParams(dimension_semantics=("arbitrary",) * n_axes,
                                vmem_limit_bytes=VMEM_LIMIT)


def _const(shape):
    nd = len(shape)
    return pl.BlockSpec(shape, lambda *_: (0,) * nd, pipeline_mode=pl.Buffered(1))


def _rows(width, tile=TM):
    return pl.BlockSpec((tile, width), lambda i: (i, 0))


def _rms(x, g):
    return x * lax.rsqrt(jnp.mean(x * x, axis=-1, keepdims=True) + EPS) * g


def _dot(a, b):
    return jnp.dot(a, b, preferred_element_type=F32)


def _softmax_rows(s_list):
    m = functools.reduce(jnp.maximum, [jnp.max(s, axis=-1, keepdims=True) for s in s_list])
    p_list = [jnp.exp(s - m) for s in s_list]
    inv = 1.0 / functools.reduce(jnp.add, [jnp.sum(p, axis=-1, keepdims=True) for p in p_list])
    return [(p * inv).astype(BF16) for p in p_list]


def _inproj_kernel(x_ref, gmix_ref, w1_ref, w2_ref, w3_ref, gq_ref, gkv_ref, wuq1_ref, wuq2_ref,
                   wukp_ref, wuv_ref, cos_ref, sin_ref,
                   lat_ref, kr_ref, q16_ref, k16_ref, v16_ref, qb16_ref, kb16_ref, vb16_ref,
                   kb32_ref, vb32_ref, sga_ref, sgb_ref):
    h = _rms(x_ref[...], gmix_ref[...]).astype(BF16)
    cos = cos_ref[...]
    sin = sin_ref[...]
    z1 = _dot(h, w1_ref[...])
    k128 = z1[:, 640:768] * cos + z1[:, 768:896] * sin
    kr_ref[...] = k128[:, NOPE:NOPE + ROPE]
    cqn = _rms(z1[:, :Q_LORA], gq_ref[...]).astype(BF16)
    cos8 = jnp.concatenate([cos] * H_A, axis=1)
    sin8 = jnp.concatenate([sin] * H_A, axis=1)
    q = _dot(cqn, wuq1_ref[...]) * cos8 + _dot(cqn, wuq2_ref[...]) * sin8
    q16_ref[...] = q.astype(BF16)
    ckvn = _rms(z1[:, Q_LORA:Q_LORA + KV_LORA], gkv_ref[...])
    lat_ref[...] = ckvn
    ckvn16 = ckvn.astype(BF16)
    kk = _dot(ckvn16, wukp_ref[...]) + jnp.concatenate([k128] * H_A, axis=1)
    k16_ref[...] = kk.astype(BF16)
    v16_ref[...] = _dot(ckvn16, wuv_ref[...]).astype(BF16)
    z2 = _dot(h, w2_ref[...])
    qb16_ref[...] = z2[:, :HB].astype(BF16)
    kb = z2[:, HB:2 * HB]
    vb = z2[:, 2 * HB:]
    kb32_ref[...] = kb
    vb32_ref[...] = vb
    kb16_ref[...] = kb.astype(BF16)
    vb16_ref[...] = vb.astype(BF16)
    z3 = _dot(h, w3_ref[...])
    sga_ref[...] = jax.nn.sigmoid(z3[:, :D_MODEL])
    sgb_ref[...] = jax.nn.sigmoid(z3[:, D_MODEL:])


def _rope_rows():
    per_seq = SEQ // TM
    return pl.BlockSpec((TM, LANES), lambda i: (jnp.where(i < N_P // TM, i % per_seq, per_seq), 0))


def _inproj(x, gmix, w1, w2, w3, gq, gkv, wuq1, wuq2, wukp, wuv, cos, sin):
    n = x.shape[0]
    widths = [(KV_LORA, F32), (ROPE, F32), (H_A * LANES, BF16), (H_A * LANES, BF16), (H_A * V_A, BF16),
              (HB, BF16), (HB, BF16), (HB, BF16), (HB, F32), (HB, F32), (D_MODEL, F32), (D_MODEL, F32)]
    return pl.pallas_call(
        _inproj_kernel,
        out_shape=[jax.ShapeDtypeStruct((n, w), d) for w, d in widths],
        grid=(n // TM,),
        in_specs=[_rows(D_MODEL), _const(gmix.shape), _const(w1.shape), _const(w2.shape), _const(w3.shape),
                  _const(gq.shape), _const(gkv.shape), _const(wuq1.shape), _const(wuq2.shape),
                  _const(wukp.shape), _const(wuv.shape), _rope_rows(), _rope_rows()],
        out_specs=[_rows(w) for w, _ in widths],
        compiler_params=_params(),
        name="inproj",
    )(x, gmix, w1, w2, w3, gq, gkv, wuq1, wuq2, wukp, wuv, cos, sin)


def _pair_select(o_even, o_odd):
    lane = lax.broadcasted_iota(jnp.int32, o_even.shape, 1)
    return jnp.where(lane < V_A, o_even, o_odd)


def _mla_prompt_tile(q_ref, k_ref, v_ref, o_ref, i, nk):
    q_chunk = (i * TQ + lax.broadcasted_iota(jnp.int32, (TQ, nk), 0)) // CHUNK
    k_chunk = lax.broadcasted_iota(jnp.int32, (TQ, nk), 1) // CHUNK
    mask = k_chunk <= q_chunk
    scores = []
    for hd in range(H_A):
        qh = q_ref[:, hd * LANES:(hd + 1) * LANES]
        kh = k_ref[0:nk, hd * LANES:(hd + 1) * LANES]
        s = lax.dot_general(qh, kh, NT, preferred_element_type=F32) * MLA_SCALE
        scores.append(jnp.where(mask, s, NEG))
    weights = [_softmax_rows([s])[0] for s in scores]
    outs = [_dot(weights[hd], v_ref[0:nk, (hd // 2) * LANES:(hd // 2 + 1) * LANES]) for hd in range(H_A)]
    pairs = [_pair_select(outs[2 * pr], outs[2 * pr + 1]) for pr in range(H_A // 2)]
    o_ref[...] = jnp.concatenate(pairs, axis=1).astype(BF16)


def _mla_prompt_kernel(q_ref, k_ref, v_ref, o_ref):
    i = pl.program_id(1)
    tiles_per_step = KV_STEP // TQ
    for grp in range(SEQ // KV_STEP):
        @pl.when(i // tiles_per_step == grp)
        def _():
            _mla_prompt_tile(q_ref, k_ref, v_ref, o_ref, i, (grp + 1) * KV_STEP)


def _mla_prompt(q16, k16, v16):
    nq = SEQ // TQ
    return pl.pallas_call(
        _mla_prompt_kernel,
        out_shape=jax.ShapeDtypeStruct((N_P, H_A * V_A), BF16),
        grid=(BATCH, nq),
        in_specs=[pl.BlockSpec((TQ, H_A * LANES), lambda b, i: (b * nq + i, 0)),
                  pl.BlockSpec((SEQ, H_A * LANES), lambda b, i: (b, 0)),
                  pl.BlockSpec((SEQ, H_A * V_A), lambda b, i: (b, 0))],
        out_specs=pl.BlockSpec((TQ, H_A * V_A), lambda b, i: (b * nq + i, 0)),
        compiler_params=_params(2),
        name="mla_prompt",
    )(q16, k16, v16)


def _mla_sample_kernel(q_ref, clat_ref, ckr_ref, nlat_ref, nkr_ref, wukt_ref, wuvh_ref, o_ref):
    q = q_ref[...].astype(F32)
    qlat, qrope = [], []
    for hd in range(H_A):
        qn = q[:, hd * LANES:hd * LANES + NOPE].astype(BF16)
        qlat.append(_dot(qn, wukt_ref[hd]).astype(BF16))
        qrope.append(q[:, hd * LANES + NOPE:hd * LANES + NOPE + ROPE].astype(BF16))
    qlat = jnp.concatenate(qlat, axis=0)
    qrope = jnp.concatenate(qrope, axis=0)
    clat = clat_ref[...].astype(BF16)
    ckr = ckr_ref[...].astype(BF16)
    nlat = nlat_ref[...].astype(BF16)
    nkr = nkr_ref[...].astype(BF16)
    s_c = (lax.dot_general(qlat, clat, NT, preferred_element_type=F32)
           + lax.dot_general(qrope, ckr, NT, preferred_element_type=F32)) * MLA_SCALE
    s_n = (lax.dot_general(qlat, nlat, NT, preferred_element_type=F32)
           + lax.dot_general(qrope, nkr, NT, preferred_element_type=F32)) * MLA_SCALE
    w_c, w_n = _softmax_rows([s_c, s_n])
    olat = (_dot(w_c, clat) + _dot(w_n, nlat)).astype(BF16)
    outs = [_dot(olat[hd * DEC_SEQ:(hd + 1) * DEC_SEQ, :], wuvh_ref[hd]) for hd in range(H_A)]
    o_ref[...] = jnp.concatenate(outs, axis=1).astype(BF16)


def _mla_sample(q16, cache_lat, cache_kr, lat, kr, wukt, wuvh):
    off = N_P // DEC_SEQ
    return pl.pallas_call(
        _mla_sample_kernel,
        out_shape=jax.ShapeDtypeStruct((N_S, H_A * V_A), BF16),
        grid=(DEC_BATCH,),
        in_specs=[pl.BlockSpec((DEC_SEQ, H_A * LANES), lambda b: (off + b, 0)),
                  pl.BlockSpec((None, PAST_LEN, KV_LORA), lambda b: (b, 0, 0)),
                  pl.BlockSpec((None, PAST_LEN, ROPE), lambda b: (b, 0, 0)),
                  pl.BlockSpec((DEC_SEQ, KV_LORA), lambda b: (off + b, 0)),
                  pl.BlockSpec((DEC_SEQ, ROPE), lambda b: (off + b, 0)),
                  _const(wukt.shape), _const(wuvh.shape)],
        out_specs=pl.BlockSpec((DEC_SEQ, H_A * V_A), lambda b: (b, 0)),
        compiler_params=_params(),
        name="mla_sample",
    )(q16, cache_lat, cache_kr, lat, kr, wukt, wuvh)


def _band_heads(q, blocks, bias_refs, valid):
    lane = lax.broadcasted_iota(jnp.int32, (q.shape[0], LANES), 1)
    scores = []
    for hd in range(H_B):
        sl = slice((hd // 2) * LANES, (hd // 2 + 1) * LANES)
        q2 = q[:, sl]
        own = (lane >= DH_B) if hd % 2 else (lane < DH_B)
        qm = jnp.where(own, q2, jnp.zeros_like(q2))
        ss = [lax.dot_general(qm, k[:, sl], NT, preferred_element_type=F32) * BAND_SCALE + b_ref[hd]
              for (k, _), b_ref in zip(blocks, bias_refs)]
        if valid is not None:
            ss = [jnp.where(valid, s, NEG) for s in ss]
        scores.append(ss)
    weights = [_softmax_rows(ss) for ss in scores]
    outs = []
    for hd in range(H_B):
        sl = slice((hd // 2) * LANES, (hd // 2 + 1) * LANES)
        outs.append(functools.reduce(jnp.add, [_dot(w, v[:, sl]) for w, (_, v) in zip(weights[hd], blocks)]))
    pairs = [_pair_select(outs[2 * pr], outs[2 * pr + 1]) for pr in range(H_B // 2)]
    return jnp.concatenate(pairs, axis=1).astype(BF16)


def _band_prompt_kernel(q_ref, k_ref, v_ref, bias_ref, o_ref):
    c = pl.program_id(1)
    start = pl.multiple_of(c * CHUNK, CHUNK)
    k = k_ref[pl.ds(start, BAND), :]
    v = v_ref[pl.ds(start, BAND), :]
    valid = (start - WINDOW_B + lax.broadcasted_iota(jnp.int32, (CHUNK, BAND), 1)) >= 0
    o_ref[...] = _band_heads(q_ref[...], [(k, v)], [bias_ref], valid)


def _band_prompt(qb16, kpad, vpad, bias):
    nc = SEQ // CHUNK
    return pl.pallas_call(
        _band_prompt_kernel,
        out_shape=jax.ShapeDtypeStruct((N_P, HB), BF16),
        grid=(BATCH, nc),
        in_specs=[pl.BlockSpec((CHUNK, HB), lambda b, c: (b * nc + c, 0)),
                  pl.BlockSpec((None, SEQ + WINDOW_B, HB), lambda b, c: (b, 0, 0)),
                  pl.BlockSpec((None, SEQ + WINDOW_B, HB), lambda b, c: (b, 0, 0)),
                  _const(bias.shape)],
        out_specs=pl.BlockSpec((CHUNK, HB), lambda b, c: (b * nc + c, 0)),
        compiler_params=_params(2),
        name="band_prompt",
    )(qb16, kpad, vpad, bias)


def _band_sample_kernel(q_ref, ck_ref, cv_ref, nk16_ref, nv16_ref, nk32_ref, nv32_ref, bias_c_ref, bias_n_ref,
                        o_ref, bk_ref, bv_ref):
    ck = ck_ref[...]
    cv = cv_ref[...]
    blocks = [(ck.astype(BF16), cv.astype(BF16)), (nk16_ref[...], nv16_ref[...])]
    o_ref[...] = _band_heads(q_ref[...], blocks, [bias_c_ref, bias_n_ref], None)
    keep = WINDOW_B - DEC_SEQ
    bk_ref[0:keep, :] = ck[DEC_SEQ:, :]
    bk_ref[keep:, :] = nk32_ref[...]
    bv_ref[0:keep, :] = cv[DEC_SEQ:, :]
    bv_ref[keep:, :] = nv32_ref[...]


def _band_sample(qb16, cache_k, cache_v, kb16, vb16, kb32, vb32, bias_c, bias_n):
    off = N_P // DEC_SEQ
    new = lambda: pl.BlockSpec((DEC_SEQ, HB), lambda b: (off + b, 0))
    cache = lambda: pl.BlockSpec((None, WINDOW_B, HB), lambda b: (b, 0, 0))
    return pl.pallas_call(
        _band_sample_kernel,
        out_shape=[jax.ShapeDtypeStruct((N_S, HB), BF16),
                   jax.ShapeDtypeStruct((DEC_BATCH, WINDOW_B, HB), F32),
                   jax.ShapeDtypeStruct((DEC_BATCH, WINDOW_B, HB), F32)],
        grid=(DEC_BATCH,),
        in_specs=[new(), cache(), cache(), new(), new(), new(), new(),
                  _const(bias_c.shape), _const(bias_n.shape)],
        out_specs=[pl.BlockSpec((DEC_SEQ, HB), lambda b: (b, 0)), cache(), cache()],
        compiler_params=_params(),
        name="band_sample",
    )(qb16, cache_k, cache_v, kb16, vb16, kb32, vb32, bias_c, bias_n)


def _mix_out_kernel(oa_ref, ob_ref, sga_ref, sgb_ref, x_ref, wa_ref, wb_ref, wout_ref, gffn_ref,
                    x1_ref, h2_ref, h16_ref):
    merged = sga_ref[...] * _dot(oa_ref[...], wa_ref[...]) + sgb_ref[...] * _dot(ob_ref[...], wb_ref[...])
    x1 = x_ref[...] + _dot(merged.astype(BF16), wout_ref[...])
    x1_ref[...] = x1
    h2 = _rms(x1, gffn_ref[...])
    h2_ref[...] = h2
    h16_ref[...] = h2.astype(BF16)


def _mix_out(oa, ob, sga, sgb, x, wa, wb, wout, gffn):
    n = x.shape[0]
    return pl.pallas_call(
        _mix_out_kernel,
        out_shape=[jax.ShapeDtypeStruct((n, D_MODEL), F32), jax.ShapeDtypeStruct((n, D_MODEL), F32),
                   jax.ShapeDtypeStruct((n, D_MODEL), BF16)],
        grid=(n // TM,),
        in_specs=[_rows(H_A * V_A), _rows(HB), _rows(D_MODEL), _rows(D_MODEL), _rows(D_MODEL),
                  _const(wa.shape), _const(wb.shape), _const(wout.shape), _const(gffn.shape)],
        out_specs=[_rows(D_MODEL)] * 3,
        compiler_params=_params(),
        name="mix_out",
    )(oa, ob, sga, sgb, x, wa, wb, wout, gffn)


def _top16(s, n_rows):
    row = lax.broadcasted_iota(jnp.int32, (n_rows, TQ), 0).astype(F32)
    vals, ids = [], []
    for _ in range(TOPK):
        m = jnp.max(s, axis=0, keepdims=True)
        i = jnp.min(jnp.where(s == m, row, float(n_rows)), axis=0, keepdims=True)
        vals.append(m)
        ids.append(i)
        s = jnp.where(row == i, -jnp.inf, s)
    return jnp.concatenate(vals, axis=0), jnp.concatenate(ids, axis=0)


def _pair_candidates(f1, f2):
    h = SUBLANES
    blocks = [f1(0, 1, 0, h), f1(0, 1, h, 2 * h)]
    blocks += [f1(a, a + 1, 0, h) for a in range(1, h)]
    blocks += [f2(h, 2 * h, 0, 1)]
    return jnp.concatenate(blocks, axis=0)


def _peer_topk_kernel(h_ref, wqt_ref, keys_ref, idx_ref, g_ref, qt_ref):
    qt_ref[...] = lax.dot_general(wqt_ref[...], h_ref[...], NT, preferred_element_type=F32).astype(BF16)
    r8 = lax.broadcasted_iota(jnp.int32, (SUBLANES, TQ), 0).astype(F32)
    flat = lambda a0, a1, b0, b1: (r8 + float(b0)) + float(TOPK * a0)
    flat_t = lambda a0, a1, b0, b1: (r8 + float(a0)) * float(TOPK) + float(b0)
    cflat = _pair_candidates(flat, flat_t)

    def one_head(hd):
        tops = []
        for p in range(2):
            hp = hd * 2 + p
            qs = qt_ref[pl.ds(pl.multiple_of(hp * N_KEYS, N_KEYS), N_KEYS), :]
            tops.append(_top16(_dot(keys_ref[hp], qs), N_KEYS))
        (s1, i1), (s2, i2) = tops
        add = lambda x, y: (lambda a0, a1, b0, b1: x[a0:a1, :] + y[b0:b1, :])
        cand = _pair_candidates(add(s1, s2), add(s1, s2))
        e1 = i1 * float(N_KEYS)
        cidx = _pair_candidates(add(e1, i2), add(e1, i2))
        vals, ids = [], []
        for _ in range(TOPK):
            m = jnp.max(cand, axis=0, keepdims=True)
            c = jnp.min(jnp.where(cand == m, cflat, float(TOPK * TOPK)), axis=0, keepdims=True)
            sel = cflat == c
            vals.append(m)
            ids.append(jnp.sum(jnp.where(sel, cidx, 0.0), axis=0, keepdims=True))
            cand = jnp.where(sel, -jnp.inf, cand)
        best = jnp.concatenate(vals, axis=0)
        e = jnp.exp(best - best[0:1, :])
        g = e / jnp.sum(e, axis=0, keepdims=True)
        base = pl.multiple_of(hd * TOPK, TOPK)
        g_ref[pl.ds(base, TOPK), :] = g
        idx_ref[pl.ds(base, TOPK), :] = (jnp.concatenate(ids, axis=0) * float(ROW_WORDS)).astype(jnp.int32)

    def head_group(grp, carry):
        for e in range(HEADS_PER_TRIP):
            one_head(grp * HEADS_PER_TRIP + e)
        return carry

    lax.fori_loop(0, PEER_HEADS // HEADS_PER_TRIP, head_group, 0)


def _peer_topk(h16, wqt, keys):
    n = h16.shape[0]
    nt = n // TQ
    return pl.pallas_call(
        _peer_topk_kernel,
        out_shape=[jax.ShapeDtypeStruct((nt, SLOTS, TQ), jnp.int32),
                   jax.ShapeDtypeStruct((nt, SLOTS, TQ), F32)],
        grid=(nt,),
        in_specs=[_rows(D_MODEL, TQ), _const(wqt.shape), _const(keys.shape)],
        out_specs=[pl.BlockSpec((None, SLOTS, TQ), lambda i: (i, 0, 0))] * 2,
        scratch_shapes=[pltpu.VMEM((2 * PEER_HEADS * N_KEYS, TQ), BF16)],
        compiler_params=_params(),
        name="peer_topk",
    )(h16, wqt, keys)


def _pack_table(t):
    b = lax.bitcast_convert_type(t.astype(BF16), jnp.uint16).astype(jnp.uint32)
    packed = (b[:, :HALF] << 16) | b[:, HALF:]
    return packed.reshape(N_EXPERTS * ROW_WORDS, LANES)


def _unpack(w):
    hi = lax.bitcast_convert_type(w & jnp.uint32(0xFFFF0000), F32)
    lo = lax.bitcast_convert_type(w << 16, F32)
    return hi, lo


def _slot_pairs(idx_ref, t):
    out = []
    for grp in range(PAIRS // IDX_GROUP):
        lo = idx_ref.at[0, pl.ds(t * SLOTS + grp * IDX_GROUP, IDX_GROUP)]
        hi = idx_ref.at[0, pl.ds(t * SLOTS + PAIRS + grp * IDX_GROUP, IDX_GROUP)]
        out.extend((lo[k], hi[k]) for k in range(IDX_GROUP))
    return out


def _load_pair(tab, ia, ib):
    wa = tab[pl.ds(pl.multiple_of(ia, ROW_WORDS), ROW_WORDS), :]
    wb = tab[pl.ds(pl.multiple_of(ib, ROW_WORDS), ROW_WORDS), :]
    return _unpack(jnp.concatenate([wa, wb], axis=0))


def _peer_u_kernel(idx_ref, x_ref, g_ref, tab, c_ref, prod_a, prod_b, rbuf, a_t):
    lane = lax.broadcasted_iota(jnp.int32, (SLOTS, LANES), 1)

    @pl.when(pl.program_id(0) == 0)
    def _():
        rbuf[...] = jnp.zeros_like(rbuf)
        a_t[...] = jnp.zeros_like(a_t)

    def reduce(slot, tok):
        col = jnp.sum(rbuf[slot], axis=1, keepdims=True)
        a_t[...] = jnp.where(lane == tok, col, a_t[...])

    def gather(t, slot, prod):
        xt = x_ref[t]
        x_hi = jnp.concatenate([xt[0:ROW_WORDS, :]] * 2, axis=0)
        x_lo = jnp.concatenate([xt[ROW_WORDS:, :]] * 2, axis=0)
        for j, (ia, ib) in enumerate(_slot_pairs(idx_ref, t)):
            hi, lo = _load_pair(tab, ia, ib)
            prod[pl.ds(j, SUBLANES, stride=STRIDE), :] = hi * x_hi + lo * x_lo
        halves = [functools.reduce(jnp.add, [prod[pl.ds((h * ROW_WORDS + s) * STRIDE, PAIRS), :]
                                             for s in range(ROW_WORDS)]) for h in range(2)]
        rbuf[slot] = jnp.concatenate(halves, axis=0)

    def two_tokens(p, carry):
        t0 = 2 * p
        reduce(0, t0 - 2)
        reduce(1, t0 - 1)
        gather(t0, 0, prod_a)
        gather(t0 + 1, 1, prod_b)
        return carry

    lax.fori_loop(0, TB // 2, two_tokens, 0)
    reduce(0, TB - 2)
    reduce(1, TB - 1)
    a = a_t[...].T[0:TB, :]
    gelu = 0.5 * a * (1.0 + lax.erf(a * (2.0 ** -0.5)))
    c_ref[...] = g_ref[...] * gelu


def _peer_u(idx, x, g, tab_u):
    n = x.shape[0]
    nb = n // TB
    return pl.pallas_call(
        _peer_u_kernel,
        out_shape=jax.ShapeDtypeStruct((n, SLOTS), F32),
        grid=(nb,),
        in_specs=[pl.BlockSpec((None, 1, TB * SLOTS), lambda i: (i, 0, 0), memory_space=pltpu.SMEM),
                  pl.BlockSpec((TB, SUBLANES, LANES), lambda i: (i, 0, 0)),
                  pl.BlockSpec((TB, SLOTS), lambda i: (i, 0)),
                  _const(tab_u.shape)],
        out_specs=pl.BlockSpec((TB, SLOTS), lambda i: (i, 0)),
        scratch_shapes=[pltpu.VMEM((SUBLANES * STRIDE, LANES), F32),
                        pltpu.VMEM((SUBLANES * STRIDE, LANES), F32),
                        pltpu.VMEM((2, SLOTS, LANES), F32),
                        pltpu.VMEM((SLOTS, LANES), F32)],
        compiler_params=_params(),
        name="peer_u",
    )(idx.reshape(nb, 1, TB * SLOTS), x.reshape(n, SUBLANES, LANES), g, tab_u)


def _peer_v_kernel(idx_ref, c_ref, tab, y_ref, *crep):
    row = lax.broadcasted_iota(jnp.int32, (SUBLANES, LANES), 0)
    eye = (lax.broadcasted_iota(jnp.int32, (SLOTS, LANES), 0)
           == lax.broadcasted_iota(jnp.int32, (SLOTS, LANES), 1))

    def spread(t, crep):
        col = jnp.sum(jnp.where(eye, c_ref[pl.ds(t, 1), :], 0.0), axis=1, keepdims=True)
        crep[...] = jnp.broadcast_to(col, (SLOTS, LANES))

    def token(t, crep):
        acc_hi = jnp.zeros((SUBLANES, LANES), F32)
        acc_lo = jnp.zeros((SUBLANES, LANES), F32)
        for j, (ia, ib) in enumerate(_slot_pairs(idx_ref, t)):
            hi, lo = _load_pair(tab, ia, ib)
            cm = jnp.where(row < ROW_WORDS, crep[pl.ds(j, 1), :], crep[pl.ds(PAIRS + j, 1), :])
            acc_hi = acc_hi + cm * hi
            acc_lo = acc_lo + cm * lo
        y_ref[t] = jnp.concatenate([acc_hi[0:ROW_WORDS, :] + acc_hi[ROW_WORDS:, :],
                                    acc_lo[0:ROW_WORDS, :] + acc_lo[ROW_WORDS:, :]], axis=0)

    ahead = CREP_RING // 2
    for k in range(ahead):
        spread(k, crep[k])

    def ring_trip(p, carry):
        t0 = CREP_RING * p
        for k in range(CREP_RING):
            spread(jnp.minimum(t0 + k + ahead, TB - 1), crep[(k + ahead) % CREP_RING])
            token(t0 + k, crep[k])
        return carry

    lax.fori_loop(0, TB // CREP_RING, ring_trip, 0)


def _peer_v(idx, c, tab_v):
    n = c.shape[0]
    nb = n // TB
    y = pl.pallas_call(
        _peer_v_kernel,
        out_shape=jax.ShapeDtypeStruct((n, SUBLANES, LANES), F32),
        grid=(nb,),
        in_specs=[pl.BlockSpec((None, 1, TB * SLOTS), lambda i: (i, 0, 0), memory_space=pltpu.SMEM),
                  pl.BlockSpec((TB, SLOTS), lambda i: (i, 0)),
                  _const(tab_v.shape)],
        out_specs=pl.BlockSpec((TB, SUBLANES, LANES), lambda i: (i, 0, 0)),
        scratch_shapes=[pltpu.VMEM((SLOTS, LANES), F32)] * CREP_RING,
        compiler_params=_params(),
        name="peer_v",
    )(idx.reshape(nb, 1, TB * SLOTS), c, tab_v)
    return y.reshape(n, D_MODEL)


def _final_kernel(x1_ref, y_ref, p_ref, gple_ref, wg_ref, wp_ref, gfin_ref, o_ref):
    x2 = x1_ref[...] + y_ref[...]
    gate = jax.nn.sigmoid(_dot(_rms(x2, gple_ref[...]).astype(BF16), wg_ref[...]))
    x3 = x2 + gate * _dot(p_ref[...].astype(BF16), wp_ref[...])
    o_ref[...] = _rms(x3, gfin_ref[...])


def _final(x1, y, p, gple, wg, wp, gfin):
    n = x1.shape[0]
    return pl.pallas_call(
        _final_kernel,
        out_shape=jax.ShapeDtypeStruct((n, D_MODEL), F32),
        grid=(n // TM,),
        in_specs=[_rows(D_MODEL), _rows(D_MODEL), _rows(PLE_DIM), _const(gple.shape), _const(wg.shape),
                  _const(wp.shape), _const(gfin.shape)],
        out_specs=_rows(D_MODEL),
        compiler_params=_params(),
        name="ple_final",
    )(x1, y, p, gple, wg, wp, gfin)


def _rope_tables():
    half = ROPE // 2
    freqs = ROPE_THETA ** (-np.arange(half, dtype=np.float32) / half)
    pos = np.concatenate([np.arange(SEQ), np.tile(PAST_LEN + np.arange(DEC_SEQ), TM // DEC_SEQ)])
    ang = jnp.asarray(pos, F32)[:, None] * jnp.asarray(freqs, F32)[None, :]
    cos, sin = jnp.cos(ang), jnp.sin(ang)
    n = pos.shape[0]
    pad = jnp.zeros((n, LANES - NOPE - ROPE), F32)
    cos_t = jnp.concatenate([jnp.ones((n, NOPE), F32), cos, cos, pad], axis=1)
    sin_t = jnp.concatenate([jnp.zeros((n, NOPE), F32), sin, sin, pad], axis=1)
    return cos_t, sin_t


def _rot_cols(w):
    half = ROPE // 2
    return jnp.concatenate([-w[..., half:], w[..., :half]], axis=-1)


def _rel_bias(table, n_q, n_k):
    diag = np.arange(-(n_q - 1), n_k)
    line = table[:, np.clip(WINDOW_B - diag, -REL_CLIP, REL_CLIP) + REL_CLIP]
    return jnp.stack([line[:, n_q - 1 - i:n_q - 1 - i + n_k] for i in range(n_q)], axis=1)


def kernel(x_prompt, x_sample, cache_latent, cache_krope, cache_band_k, cache_band_v, p_prompt, p_sample, g_mix, w_in, g_q_lora, g_kv_lora, w_uq, w_uk, w_uv, rel_bias, w_a_proj, w_b_proj, w_out, g_ffn, w_query, sub_keys, expert_u, expert_v, g_ple, w_ple_gate, w_ple_proj, g_final):
    w = w_in[0]
    o_kr = Q_LORA + KV_LORA
    o_qb = o_kr + ROPE
    o_ga = o_qb + 3 * HB
    w_kr = w[:, o_kr:o_qb]
    z64 = jnp.zeros((D_MODEL, NOPE), F32)
    z32 = jnp.zeros((D_MODEL, LANES - NOPE - ROPE), F32)
    w1 = jnp.concatenate([w[:, :o_kr], z64, w_kr, z32, z64, _rot_cols(w_kr), z32], axis=1).astype(BF16)
    w2 = w[:, o_qb:o_ga].astype(BF16)
    w3 = w[:, o_ga:].astype(BF16)
    wq = w_uq[0].reshape(Q_LORA, H_A, NOPE + ROPE)
    zq64 = jnp.zeros((Q_LORA, H_A, NOPE), F32)
    zq32 = jnp.zeros((Q_LORA, H_A, LANES - NOPE - ROPE), F32)
    wuq1 = jnp.concatenate([wq, zq32], axis=-1).reshape(Q_LORA, H_A * LANES).astype(BF16)
    wuq2 = jnp.concatenate([zq64, _rot_cols(wq[..., NOPE:]), zq32], axis=-1).reshape(Q_LORA, H_A * LANES).astype(BF16)
    wukp = jnp.concatenate([w_uk[0], jnp.zeros((KV_LORA, H_A, LANES - NOPE), F32)], axis=-1)
    wukp = wukp.reshape(KV_LORA, H_A * LANES).astype(BF16)
    wuv = w_uv[0].reshape(KV_LORA, H_A * V_A).astype(BF16)
    wukt = jnp.transpose(w_uk[0], (1, 2, 0)).astype(BF16)
    wuvh = jnp.transpose(w_uv[0], (1, 0, 2)).astype(BF16)
    cos_t, sin_t = _rope_tables()
    row = lambda g: g.reshape(1, -1)

    x = jnp.concatenate([x_prompt.reshape(N_P, D_MODEL), x_sample.reshape(N_S, D_MODEL)], axis=0)
    p = jnp.concatenate([p_prompt.reshape(N_P, PLE_DIM), p_sample.reshape(N_S, PLE_DIM)], axis=0)

    (lat, kr, q16, k16, v16, qb16, kb16, vb16, kb32, vb32, sga, sgb) = _inproj(
        x, row(g_mix[0]), w1, w2, w3, row(g_q_lora[0]), row(g_kv_lora[0]), wuq1, wuq2, wukp, wuv, cos_t, sin_t)

    oa_p = _mla_prompt(q16, k16, v16)
    oa_s = _mla_sample(q16, cache_latent[0], cache_krope[0], lat, kr, wukt, wuvh)

    pad = ((0, 0), (WINDOW_B, 0), (0, 0))
    kpad = jnp.pad(kb16[:N_P].reshape(BATCH, SEQ, HB), pad)
    vpad = jnp.pad(vb16[:N_P].reshape(BATCH, SEQ, HB), pad)
    tab = rel_bias[0]
    bias_p = _rel_bias(tab, CHUNK, BAND)
    bias_s = _rel_bias(tab, DEC_SEQ, WINDOW_B + DEC_SEQ)
    ob_p = _band_prompt(qb16, kpad, vpad, bias_p)
    ob_s, bk_s, bv_s = _band_sample(qb16, cache_band_k[0].reshape(DEC_BATCH, WINDOW_B, HB),
                                    cache_band_v[0].reshape(DEC_BATCH, WINDOW_B, HB),
                                    kb16, vb16, kb32, vb32, bias_s[:, :, :WINDOW_B], bias_s[:, :, WINDOW_B:])

    oa = jnp.concatenate([oa_p, oa_s], axis=0)
    ob = jnp.concatenate([ob_p, ob_s], axis=0)
    x1, h2, h16 = _mix_out(oa, ob, sga, sgb, x, w_a_proj[0].astype(BF16), w_b_proj[0].astype(BF16),
                           w_out[0].astype(BF16), row(g_ffn[0]))
    wqt = jnp.transpose(w_query[0]).astype(BF16)
    keys = sub_keys[0].reshape(2 * PEER_HEADS, N_KEYS, N_KEYS).astype(BF16)
    idx_t, g_t = _peer_topk(h16, wqt, keys)
    idx = jnp.swapaxes(idx_t, 1, 2).reshape(N_TOK, SLOTS)
    gw = jnp.swapaxes(g_t, 1, 2).reshape(N_TOK, SLOTS)

    c = _peer_u(idx, h2, gw, _pack_table(expert_u[0]))
    y = _peer_v(idx, c, _pack_table(expert_v[0]))

    out = _final(x1, y, p, row(g_ple[0]), w_ple_gate[0].astype(BF16), w_ple_proj[0].astype(BF16), row(g_final))

    y_prompt = out[:N_P].reshape(BATCH, SEQ, D_MODEL)
    y_sample = out[N_P:].reshape(DEC_BATCH, DEC_SEQ, D_MODEL)
    lat_p = lat[:N_P].reshape(1, BATCH, SEQ, KV_LORA)
    kr_p = kr[:N_P].reshape(1, BATCH, SEQ, ROPE)
    bk_p = kb32[:N_P].reshape(BATCH, SEQ, H_B, DH_B)[None, :, SEQ - WINDOW_B:]
    bv_p = vb32[:N_P].reshape(BATCH, SEQ, H_B, DH_B)[None, :, SEQ - WINDOW_B:]
    lat_s = lat[N_P:].reshape(1, DEC_BATCH, DEC_SEQ, KV_LORA)
    kr_s = kr[N_P:].reshape(1, DEC_BATCH, DEC_SEQ, ROPE)
    bk_s = bk_s.reshape(1, DEC_BATCH, WINDOW_B, H_B, DH_B)
    bv_s = bv_s.reshape(1, DEC_BATCH, WINDOW_B, H_B, DH_B)
    return (y_prompt, y_sample, lat_p, kr_p, bk_p, bv_p, lat_s, kr_s, bk_s, bv_s)
```

```python
import functools
import jax
import jax.numpy as jnp
import numpy as np
from jax import lax
from jax.experimental import pallas as pl
from jax.experimental.pallas import tpu as pltpu

D_MODEL = 1024
BATCH = 16
SEQ = 2048
DEC_BATCH = 32
DEC_SEQ = 32
PAST_LEN = 4096
CHUNK = 64
EPS = 1e-6
H_A = 8
Q_LORA = 384
KV_LORA = 256
NOPE = 64
ROPE = 32
V_A = 64
ROPE_THETA = 10000.0
MLA_SCALE = (NOPE + ROPE) ** -0.5
H_B = 8
DH_B = 64
BAND_PREV = 8
BAND = (BAND_PREV + 1) * CHUNK
WINDOW_B = BAND_PREV * CHUNK
REL_CLIP = 256
BAND_SCALE = DH_B ** -0.5
PEER_HEADS = 8
N_KEYS = 128
N_EXPERTS = N_KEYS * N_KEYS
TOPK = 16
PLE_DIM = 256

N_P = BATCH * SEQ
N_S = DEC_BATCH * DEC_SEQ
N_TOK = N_P + N_S
HB = H_B * DH_B
HALF = D_MODEL // 2
SLOTS = PEER_HEADS * TOPK
LANES = 128
SUBLANES = 8
TM = 256
TQ = 128
KV_STEP = 512
TB = 64
CREP_RING = 4
ROW_WORDS = HALF // LANES
PAIRS = SLOTS // 2
STRIDE = 72
HEADS_PER_TRIP = 2
IDX_GROUP = 16
VMEM_LIMIT = 56 * 1024 * 1024
NEG = -1e30
BF16 = jnp.bfloat16
F32 = jnp.float32
NT = (((1,), (1,)), ((), ()))


def _params(n_axes=1):
    return pltpu.CompilerParams(dimension_semantics=("arbitrary",) * n_axes,
                                vmem_limit_bytes=VMEM_LIMIT)


def _const(shape):
    nd = len(shape)
    return pl.BlockSpec(shape, lambda *_: (0,) * nd, pipeline_mode=pl.Buffered(1))


def _rows(width, tile=TM):
    return pl.BlockSpec((tile, width), lambda i: (i, 0))


P_TILES = N_P // TM


def _rows_prompt(width):
    return pl.BlockSpec((TM, width), lambda i: (jnp.minimum(i, P_TILES - 1), 0))


def _rows_sample(width):
    return pl.BlockSpec((TM, width), lambda i: (jnp.maximum(i - P_TILES, 0), 0))


def _group_rows(p_ref, s_ref):
    return jnp.where(pl.program_id(0) < P_TILES, p_ref[...], s_ref[...])


def _rms(x, g):
    return x * lax.rsqrt(jnp.mean(x * x, axis=-1, keepdims=True) + EPS) * g


def _dot(a, b):
    return jnp.dot(a, b, preferred_element_type=F32)


def _softmax_rows(s_list):
    m = functools.reduce(jnp.maximum, [jnp.max(s, axis=-1, keepdims=True) for s in s_list])
    p_list = [jnp.exp(s - m) for s in s_list]
    inv = 1.0 / functools.reduce(jnp.add, [jnp.sum(p, axis=-1, keepdims=True) for p in p_list])
    return [(p * inv).astype(BF16) for p in p_list]


def _inproj_kernel(xp_ref, xs_ref, gmix_ref, w1_ref, w2_ref, w3_ref, gq_ref, gkv_ref, wuq1_ref, wuq2_ref,
                   wukp_ref, wuv_ref, cos_ref, sin_ref,
                   lat_ref, kr_ref, q16_ref, k16_ref, v16_ref, qb16_ref, kb16_ref, vb16_ref,
                   kb32_ref, vb32_ref, sga_ref, sgb_ref):
    h = _rms(_group_rows(xp_ref, xs_ref), gmix_ref[...]).astype(BF16)
    cos = cos_ref[...]
    sin = sin_ref[...]
    z1 = _dot(h, w1_ref[...])
    k128 = z1[:, 640:768] * cos + z1[:, 768:896] * sin
    kr_ref[...] = k128[:, NOPE:NOPE + ROPE]
    cqn = _rms(z1[:, :Q_LORA], gq_ref[...]).astype(BF16)
    cos8 = jnp.concatenate([cos] * H_A, axis=1)
    sin8 = jnp.concatenate([sin] * H_A, axis=1)
    q = _dot(cqn, wuq1_ref[...]) * cos8 + _dot(cqn, wuq2_ref[...]) * sin8
    q16_ref[...] = q.astype(BF16)
    ckvn = _rms(z1[:, Q_LORA:Q_LORA + KV_LORA], gkv_ref[...])
    lat_ref[...] = ckvn
    ckvn16 = ckvn.astype(BF16)
    kk = _dot(ckvn16, wukp_ref[...]) + jnp.concatenate([k128] * H_A, axis=1)
    k16_ref[...] = kk.astype(BF16)
    v16_ref[...] = _dot(ckvn16, wuv_ref[...]).astype(BF16)
    z2 = _dot(h, w2_ref[...])
    qb16_ref[...] = z2[:, :HB].astype(BF16)
    kb = z2[:, HB:2 * HB]
    vb = z2[:, 2 * HB:]
    kb32_ref[...] = kb
    vb32_ref[...] = vb
    kb16_ref[...] = kb.astype(BF16)
    vb16_ref[...] = vb.astype(BF16)
    z3 = _dot(h, w3_ref[...])
    sga_ref[...] = jax.nn.sigmoid(z3[:, :D_MODEL])
    sgb_ref[...] = jax.nn.sigmoid(z3[:, D_MODEL:])


def _rope_rows():
    per_seq = SEQ // TM
    return pl.BlockSpec((TM, LANES), lambda i: (jnp.where(i < N_P // TM, i % per_seq, per_seq), 0))


def _inproj(xp, xs, gmix, w1, w2, w3, gq, gkv, wuq1, wuq2, wukp, wuv, cos, sin):
    n = N_TOK
    widths = [(KV_LORA, F32), (ROPE, F32), (H_A * LANES, BF16), (H_A * LANES, BF16), (H_A * V_A, BF16),
              (HB, BF16), (HB, BF16), (HB, BF16), (HB, F32), (HB, F32), (D_MODEL, F32), (D_MODEL, F32)]
    return pl.pallas_call(
        _inproj_kernel,
        out_shape=[jax.ShapeDtypeStruct((n, w), d) for w, d in widths],
        grid=(n // TM,),
        in_specs=[_rows_prompt(D_MODEL), _rows_sample(D_MODEL),
                  _const(gmix.shape), _const(w1.shape), _const(w2.shape), _const(w3.shape),
                  _const(gq.shape), _const(gkv.shape), _const(wuq1.shape), _const(wuq2.shape),
                  _const(wukp.shape), _const(wuv.shape), _rope_rows(), _rope_rows()],
        out_specs=[_rows(w) for w, _ in widths],
        compiler_params=_params(),
        name="inproj",
    )(xp, xs, gmix, w1, w2, w3, gq, gkv, wuq1, wuq2, wukp, wuv, cos, sin)


def _pair_select(o_even, o_odd):
    lane = lax.broadcasted_iota(jnp.int32, o_even.shape, 1)
    return jnp.where(lane < V_A, o_even, o_odd)


def _mla_prompt_tile(q_ref, k_ref, v_ref, o_ref, i, nk):
    q_chunk = (i * TQ + lax.broadcasted_iota(jnp.int32, (TQ, nk), 0)) // CHUNK
    k_chunk = lax.broadcasted_iota(jnp.int32, (TQ, nk), 1) // CHUNK
    mask = k_chunk <= q_chunk
    scores = []
    for hd in range(H_A):
        qh = q_ref[:, hd * LANES:(hd + 1) * LANES]
        kh = k_ref[0:nk, hd * LANES:(hd + 1) * LANES]
        s = lax.dot_general(qh, kh, NT, preferred_element_type=F32) * MLA_SCALE
        scores.append(jnp.where(mask, s, NEG))
    weights = [_softmax_rows([s])[0] for s in scores]
    outs = [_dot(weights[hd], v_ref[0:nk, (hd // 2) * LANES:(hd // 2 + 1) * LANES]) for hd in range(H_A)]
    pairs = [_pair_select(outs[2 * pr], outs[2 * pr + 1]) for pr in range(H_A // 2)]
    o_ref[...] = jnp.concatenate(pairs, axis=1).astype(BF16)


def _mla_prompt_kernel(q_ref, k_ref, v_ref, o_ref):
    i = pl.program_id(1)
    tiles_per_step = KV_STEP // TQ
    for grp in range(SEQ // KV_STEP):
        @pl.when(i // tiles_per_step == grp)
        def _():
            _mla_prompt_tile(q_ref, k_ref, v_ref, o_ref, i, (grp + 1) * KV_STEP)


def _mla_prompt(q16, k16, v16):
    nq = SEQ // TQ
    return pl.pallas_call(
        _mla_prompt_kernel,
        out_shape=jax.ShapeDtypeStruct((N_P, H_A * V_A), BF16),
        grid=(BATCH, nq),
        in_specs=[pl.BlockSpec((TQ, H_A * LANES), lambda b, i: (b * nq + i, 0)),
                  pl.BlockSpec((SEQ, H_A * LANES), lambda b, i: (b, 0)),
                  pl.BlockSpec((SEQ, H_A * V_A), lambda b, i: (b, 0))],
        out_specs=pl.BlockSpec((TQ, H_A * V_A), lambda b, i: (b * nq + i, 0)),
        compiler_params=_params(2),
        name="mla_prompt",
    )(q16, k16, v16)


def _mla_sample_kernel(q_ref, clat_ref, ckr_ref, nlat_ref, nkr_ref, wukt_ref, wuvh_ref, o_ref):
    q = q_ref[...].astype(F32)
    qlat, qrope = [], []
    for hd in range(H_A):
        qn = q[:, hd * LANES:hd * LANES + NOPE].astype(BF16)
        qlat.append(_dot(qn, wukt_ref[hd]).astype(BF16))
        qrope.append(q[:, hd * LANES + NOPE:hd * LANES + NOPE + ROPE].astype(BF16))
    qlat = jnp.concatenate(qlat, axis=0)
    qrope = jnp.concatenate(qrope, axis=0)
    clat = clat_ref[...].astype(BF16)
    ckr = ckr_ref[...].astype(BF16)
    nlat = nlat_ref[...].astype(BF16)
    nkr = nkr_ref[...].astype(BF16)
    s_c = (lax.dot_general(qlat, clat, NT, preferred_element_type=F32)
           + lax.dot_general(qrope, ckr, NT, preferred_element_type=F32)) * MLA_SCALE
    s_n = (lax.dot_general(qlat, nlat, NT, preferred_element_type=F32)
           + lax.dot_general(qrope, nkr, NT, preferred_element_type=F32)) * MLA_SCALE
    w_c, w_n = _softmax_rows([s_c, s_n])
    olat = (_dot(w_c, clat) + _dot(w_n, nlat)).astype(BF16)
    outs = [_dot(olat[hd * DEC_SEQ:(hd + 1) * DEC_SEQ, :], wuvh_ref[hd]) for hd in range(H_A)]
    o_ref[...] = jnp.concatenate(outs, axis=1).astype(BF16)


def _mla_sample(q16, cache_lat, cache_kr, lat, kr, wukt, wuvh):
    off = N_P // DEC_SEQ
    return pl.pallas_call(
        _mla_sample_kernel,
        out_shape=jax.ShapeDtypeStruct((N_S, H_A * V_A), BF16),
        grid=(DEC_BATCH,),
        in_specs=[pl.BlockSpec((DEC_SEQ, H_A * LANES), lambda b: (off + b, 0)),
                  pl.BlockSpec((None, PAST_LEN, KV_LORA), lambda b: (b, 0, 0)),
                  pl.BlockSpec((None, PAST_LEN, ROPE), lambda b: (b, 0, 0)),
                  pl.BlockSpec((DEC_SEQ, KV_LORA), lambda b: (off + b, 0)),
                  pl.BlockSpec((DEC_SEQ, ROPE), lambda b: (off + b, 0)),
                  _const(wukt.shape), _const(wuvh.shape)],
        out_specs=pl.BlockSpec((DEC_SEQ, H_A * V_A), lambda b: (b, 0)),
        compiler_params=_params(),
        name="mla_sample",
    )(q16, cache_lat, cache_kr, lat, kr, wukt, wuvh)


def _band_heads(q, blocks, bias_refs, valid):
    lane = lax.broadcasted_iota(jnp.int32, (q.shape[0], LANES), 1)
    scores = []
    for hd in range(H_B):
        sl = slice((hd // 2) * LANES, (hd // 2 + 1) * LANES)
        q2 = q[:, sl]
        own = (lane >= DH_B) if hd % 2 else (lane < DH_B)
        qm = jnp.where(own, q2, jnp.zeros_like(q2))
        ss = [lax.dot_general(qm, k[:, sl], NT, preferred_element_type=F32) * BAND_SCALE + b_ref[hd]
              for (k, _), b_ref in zip(blocks, bias_refs)]
        if valid is not None:
            ss = [jnp.where(valid, s, NEG) for s in ss]
        scores.append(ss)
    weights = [_softmax_rows(ss) for ss in scores]
    outs = []
    for hd in range(H_B):
        sl = slice((hd // 2) * LANES, (hd // 2 + 1) * LANES)
        outs.append(functools.reduce(jnp.add, [_dot(w, v[:, sl]) for w, (_, v) in zip(weights[hd], blocks)]))
    pairs = [_pair_select(outs[2 * pr], outs[2 * pr + 1]) for pr in range(H_B // 2)]
    return jnp.concatenate(pairs, axis=1).astype(BF16)


def _band_prompt_kernel(q_ref, k_ref, v_ref, bias_ref, o_ref):
    c = pl.program_id(1)
    start = pl.multiple_of(c * CHUNK, CHUNK)
    k = k_ref[pl.ds(start, BAND), :]
    v = v_ref[pl.ds(start, BAND), :]
    valid = (start - WINDOW_B + lax.broadcasted_iota(jnp.int32, (CHUNK, BAND), 1)) >= 0
    o_ref[...] = _band_heads(q_ref[...], [(k, v)], [bias_ref], valid)


def _band_prompt(qb16, kpad, vpad, bias):
    nc = SEQ // CHUNK
    return pl.pallas_call(
        _band_prompt_kernel,
        out_shape=jax.ShapeDtypeStruct((N_P, HB), BF16),
        grid=(BATCH, nc),
        in_specs=[pl.BlockSpec((CHUNK, HB), lambda b, c: (b * nc + c, 0)),
                  pl.BlockSpec((None, SEQ + WINDOW_B, HB), lambda b, c: (b, 0, 0)),
                  pl.BlockSpec((None, SEQ + WINDOW_B, HB), lambda b, c: (b, 0, 0)),
                  _const(bias.shape)],
        out_specs=pl.BlockSpec((CHUNK, HB), lambda b, c: (b * nc + c, 0)),
        compiler_params=_params(2),
        name="band_prompt",
    )(qb16, kpad, vpad, bias)


def _band_sample_kernel(q_ref, ck_ref, cv_ref, nk16_ref, nv16_ref, nk32_ref, nv32_ref, bias_c_ref, bias_n_ref,
                        o_ref, bk_ref, bv_ref):
    ck = ck_ref[...]
    cv = cv_ref[...]
    blocks = [(ck.astype(BF16), cv.astype(BF16)), (nk16_ref[...], nv16_ref[...])]
    o_ref[...] = _band_heads(q_ref[...], blocks, [bias_c_ref, bias_n_ref], None)
    keep = WINDOW_B - DEC_SEQ
    bk_ref[0:keep, :] = ck[DEC_SEQ:, :]
    bk_ref[keep:, :] = nk32_ref[...]
    bv_ref[0:keep, :] = cv[DEC_SEQ:, :]
    bv_ref[keep:, :] = nv32_ref[...]


def _band_sample(qb16, cache_k, cache_v, kb16, vb16, kb32, vb32, bias_c, bias_n):
    off = N_P // DEC_SEQ
    new = lambda: pl.BlockSpec((DEC_SEQ, HB), lambda b: (off + b, 0))
    cache = lambda: pl.BlockSpec((None, WINDOW_B, HB), lambda b: (b, 0, 0))
    return pl.pallas_call(
        _band_sample_kernel,
        out_shape=[jax.ShapeDtypeStruct((N_S, HB), BF16),
                   jax.ShapeDtypeStruct((DEC_BATCH, WINDOW_B, HB), F32),
                   jax.ShapeDtypeStruct((DEC_BATCH, WINDOW_B, HB), F32)],
        grid=(DEC_BATCH,),
        in_specs=[new(), cache(), cache(), new(), new(), new(), new(),
                  _const(bias_c.shape), _const(bias_n.shape)],
        out_specs=[pl.BlockSpec((DEC_SEQ, HB), lambda b: (b, 0)), cache(), cache()],
        compiler_params=_params(),
        name="band_sample",
    )(qb16, cache_k, cache_v, kb16, vb16, kb32, vb32, bias_c, bias_n)


def _mix_out_kernel(oa_ref, ob_ref, sga_ref, sgb_ref, xp_ref, xs_ref, wa_ref, wb_ref, wout_ref, gffn_ref,
                    x1_ref, h2_ref, h16_ref):
    merged = sga_ref[...] * _dot(oa_ref[...], wa_ref[...]) + sgb_ref[...] * _dot(ob_ref[...], wb_ref[...])
    x1 = _group_rows(xp_ref, xs_ref) + _dot(merged.astype(BF16), wout_ref[...])
    x1_ref[...] = x1
    h2 = _rms(x1, gffn_ref[...])
    h2_ref[...] = h2
    h16_ref[...] = h2.astype(BF16)


def _mix_out(oa, ob, sga, sgb, xp, xs, wa, wb, wout, gffn):
    n = N_TOK
    return pl.pallas_call(
        _mix_out_kernel,
        out_shape=[jax.ShapeDtypeStruct((n, D_MODEL), F32), jax.ShapeDtypeStruct((n, D_MODEL), F32),
                   jax.ShapeDtypeStruct((n, D_MODEL), BF16)],
        grid=(n // TM,),
        in_specs=[_rows(H_A * V_A), _rows(HB), _rows(D_MODEL), _rows(D_MODEL),
                  _rows_prompt(D_MODEL), _rows_sample(D_MODEL),
                  _const(wa.shape), _const(wb.shape), _const(wout.shape), _const(gffn.shape)],
        out_specs=[_rows(D_MODEL)] * 3,
        compiler_params=_params(),
        name="mix_out",
    )(oa, ob, sga, sgb, xp, xs, wa, wb, wout, gffn)


def _top16(s, n_rows):
    row = lax.broadcasted_iota(jnp.int32, (n_rows, TQ), 0).astype(F32)
    vals, ids = [], []
    for _ in range(TOPK):
        m = jnp.max(s, axis=0, keepdims=True)
        i = jnp.min(jnp.where(s == m, row, float(n_rows)), axis=0, keepdims=True)
        vals.append(m)
        ids.append(i)
        s = jnp.where(row == i, -jnp.inf, s)
    return jnp.concatenate(vals, axis=0), jnp.concatenate(ids, axis=0)


def _pair_candidates(f1, f2):
    h = SUBLANES
    blocks = [f1(0, 1, 0, h), f1(0, 1, h, 2 * h)]
    blocks += [f1(a, a + 1, 0, h) for a in range(1, h)]
    blocks += [f2(h, 2 * h, 0, 1)]
    return jnp.concatenate(blocks, axis=0)


def _peer_topk_kernel(h_ref, wqt_ref, keys_ref, idx_ref, g_ref, qt_ref, g_t, id_t):
    qt_ref[...] = lax.dot_general(wqt_ref[...], h_ref[...], NT, preferred_element_type=F32).astype(BF16)
    r8 = lax.broadcasted_iota(jnp.int32, (SUBLANES, TQ), 0).astype(F32)
    flat = lambda a0, a1, b0, b1: (r8 + float(b0)) + float(TOPK * a0)
    flat_t = lambda a0, a1, b0, b1: (r8 + float(a0)) * float(TOPK) + float(b0)
    cflat = _pair_candidates(flat, flat_t)

    def one_head(hd):
        tops = []
        for p in range(2):
            hp = hd * 2 + p
            qs = qt_ref[pl.ds(pl.multiple_of(hp * N_KEYS, N_KEYS), N_KEYS), :]
            tops.append(_top16(_dot(keys_ref[hp], qs), N_KEYS))
        (s1, i1), (s2, i2) = tops
        add = lambda x, y: (lambda a0, a1, b0, b1: x[a0:a1, :] + y[b0:b1, :])
        cand = _pair_candidates(add(s1, s2), add(s1, s2))
        e1 = i1 * float(N_KEYS)
        cidx = _pair_candidates(add(e1, i2), add(e1, i2))
        vals, ids = [], []
        for _ in range(TOPK):
            m = jnp.max(cand, axis=0, keepdims=True)
            c = jnp.min(jnp.where(cand == m, cflat, float(TOPK * TOPK)), axis=0, keepdims=True)
            sel = cflat == c
            vals.append(m)
            ids.append(jnp.sum(jnp.where(sel, cidx, 0.0), axis=0, keepdims=True))
            cand = jnp.where(sel, -jnp.inf, cand)
        best = jnp.concatenate(vals, axis=0)
        e = jnp.exp(best - best[0:1, :])
        g = e / jnp.sum(e, axis=0, keepdims=True)
        base = pl.multiple_of(hd * TOPK, TOPK)
        g_t[pl.ds(base, TOPK), :] = g
        id_t[pl.ds(base, TOPK), :] = jnp.concatenate(ids, axis=0)

    def head_group(grp, carry):
        for e in range(HEADS_PER_TRIP):
            one_head(grp * HEADS_PER_TRIP + e)
        return carry

    lax.fori_loop(0, PEER_HEADS // HEADS_PER_TRIP, head_group, 0)
    g_ref[...] = g_t[...].T
    idx_ref[...] = (id_t[...].T * float(ROW_WORDS)).astype(jnp.int32)


def _peer_topk(h16, wqt, keys):
    n = h16.shape[0]
    nt = n // TQ
    return pl.pallas_call(
        _peer_topk_kernel,
        out_shape=[jax.ShapeDtypeStruct((n, SLOTS), jnp.int32),
                   jax.ShapeDtypeStruct((n, SLOTS), F32)],
        grid=(nt,),
        in_specs=[_rows(D_MODEL, TQ), _const(wqt.shape), _const(keys.shape)],
        out_specs=[_rows(SLOTS, TQ)] * 2,
        scratch_shapes=[pltpu.VMEM((2 * PEER_HEADS * N_KEYS, TQ), BF16),
                        pltpu.VMEM((SLOTS, TQ), F32), pltpu.VMEM((SLOTS, TQ), F32)],
        compiler_params=_params(),
        name="peer_topk",
    )(h16, wqt, keys)


def _pack_table(t):
    b = lax.bitcast_convert_type(t.astype(BF16), jnp.uint16).astype(jnp.uint32)
    packed = (b[:, :HALF] << 16) | b[:, HALF:]
    return packed.reshape(N_EXPERTS * ROW_WORDS, LANES)


def _unpack(w):
    hi = lax.bitcast_convert_type(w & jnp.uint32(0xFFFF0000), F32)
    lo = lax.bitcast_convert_type(w << 16, F32)
    return hi, lo


def _slot_pairs(idx_ref, t):
    out = []
    for grp in range(PAIRS // IDX_GROUP):
        lo = idx_ref.at[0, pl.ds(t * SLOTS + grp * IDX_GROUP, IDX_GROUP)]
        hi = idx_ref.at[0, pl.ds(t * SLOTS + PAIRS + grp * IDX_GROUP, IDX_GROUP)]
        out.extend((lo[k], hi[k]) for k in range(IDX_GROUP))
    return out


def _load_pair(tab, ia, ib):
    wa = tab[pl.ds(pl.multiple_of(ia, ROW_WORDS), ROW_WORDS), :]
    wb = tab[pl.ds(pl.multiple_of(ib, ROW_WORDS), ROW_WORDS), :]
    return _unpack(jnp.concatenate([wa, wb], axis=0))


def _peer_u_kernel(idx_ref, x_ref, g_ref, tab, c_ref, prod_a, prod_b, rbuf, a_t):
    lane = lax.broadcasted_iota(jnp.int32, (SLOTS, LANES), 1)

    @pl.when(pl.program_id(0) == 0)
    def _():
        rbuf[...] = jnp.zeros_like(rbuf)
        a_t[...] = jnp.zeros_like(a_t)

    def reduce(slot, tok):
        col = jnp.sum(rbuf[slot], axis=1, keepdims=True)
        a_t[...] = jnp.where(lane == tok, col, a_t[...])

    def gather(t, slot, prod):
        xt = x_ref[t]
        x_hi = jnp.concatenate([xt[0:ROW_WORDS, :]] * 2, axis=0)
        x_lo = jnp.concatenate([xt[ROW_WORDS:, :]] * 2, axis=0)
        for j, (ia, ib) in enumerate(_slot_pairs(idx_ref, t)):
            hi, lo = _load_pair(tab, ia, ib)
            prod[pl.ds(j, SUBLANES, stride=STRIDE), :] = hi * x_hi + lo * x_lo
        halves = [functools.reduce(jnp.add, [prod[pl.ds((h * ROW_WORDS + s) * STRIDE, PAIRS), :]
                                             for s in range(ROW_WORDS)]) for h in range(2)]
        rbuf[slot] = jnp.concatenate(halves, axis=0)

    def two_tokens(p, carry):
        t0 = 2 * p
        reduce(0, t0 - 2)
        reduce(1, t0 - 1)
        gather(t0, 0, prod_a)
        gather(t0 + 1, 1, prod_b)
        return carry

    lax.fori_loop(0, TB // 2, two_tokens, 0)
    reduce(0, TB - 2)
    reduce(1, TB - 1)
    a = a_t[...].T[0:TB, :]
    gelu = 0.5 * a * (1.0 + lax.erf(a * (2.0 ** -0.5)))
    c_ref[...] = g_ref[...] * gelu


def _peer_u(idx, x, g, tab_u):
    n = x.shape[0]
    nb = n // TB
    return pl.pallas_call(
        _peer_u_kernel,
        out_shape=jax.ShapeDtypeStruct((n, SLOTS), F32),
        grid=(nb,),
        in_specs=[pl.BlockSpec((None, 1, TB * SLOTS), lambda i: (i, 0, 0), memory_space=pltpu.SMEM),
                  pl.BlockSpec((TB, SUBLANES, LANES), lambda i: (i, 0, 0)),
                  pl.BlockSpec((TB, SLOTS), lambda i: (i, 0)),
                  _const(tab_u.shape)],
        out_specs=pl.BlockSpec((TB, SLOTS), lambda i: (i, 0)),
        scratch_shapes=[pltpu.VMEM((SUBLANES * STRIDE, LANES), F32),
                        pltpu.VMEM((SUBLANES * STRIDE, LANES), F32),
                        pltpu.VMEM((2, SLOTS, LANES), F32),
                        pltpu.VMEM((SLOTS, LANES), F32)],
        compiler_params=_params(),
        name="peer_u",
    )(idx.reshape(nb, 1, TB * SLOTS), x.reshape(n, SUBLANES, LANES), g, tab_u)


def _peer_v_kernel(idx_ref, c_ref, tab, y_ref, *crep):
    row = lax.broadcasted_iota(jnp.int32, (SUBLANES, LANES), 0)
    eye = (lax.broadcasted_iota(jnp.int32, (SLOTS, LANES), 0)
           == lax.broadcasted_iota(jnp.int32, (SLOTS, LANES), 1))

    def spread(t, crep):
        col = jnp.sum(jnp.where(eye, c_ref[pl.ds(t, 1), :], 0.0), axis=1, keepdims=True)
        crep[...] = jnp.broadcast_to(col, (SLOTS, LANES))

    def token(t, crep):
        acc_hi = jnp.zeros((SUBLANES, LANES), F32)
        acc_lo = jnp.zeros((SUBLANES, LANES), F32)
        for j, (ia, ib) in enumerate(_slot_pairs(idx_ref, t)):
            hi, lo = _load_pair(tab, ia, ib)
            cm = jnp.where(row < ROW_WORDS, crep[pl.ds(j, 1), :], crep[pl.ds(PAIRS + j, 1), :])
            acc_hi = acc_hi + cm * hi
            acc_lo = acc_lo + cm * lo
        y_ref[t] = jnp.concatenate([acc_hi[0:ROW_WORDS, :] + acc_hi[ROW_WORDS:, :],
                                    acc_lo[0:ROW_WORDS, :] + acc_lo[ROW_WORDS:, :]], axis=0)

    ahead = CREP_RING // 2
    for k in range(ahead):
        spread(k, crep[k])

    def ring_trip(p, carry):
        t0 = CREP_RING * p
        for k in range(CREP_RING):
            spread(jnp.minimum(t0 + k + ahead, TB - 1), crep[(k + ahead) % CREP_RING])
            token(t0 + k, crep[k])
        return carry

    lax.fori_loop(0, TB // CREP_RING, ring_trip, 0)


def _peer_v(idx, c, tab_v):
    n = c.shape[0]
    nb = n // TB
    y = pl.pallas_call(
        _peer_v_kernel,
        out_shape=jax.ShapeDtypeStruct((n, SUBLANES, LANES), F32),
        grid=(nb,),
        in_specs=[pl.BlockSpec((None, 1, TB * SLOTS), lambda i: (i, 0, 0), memory_space=pltpu.SMEM),
                  pl.BlockSpec((TB, SLOTS), lambda i: (i, 0)),
                  _const(tab_v.shape)],
        out_specs=pl.BlockSpec((TB, SUBLANES, LANES), lambda i: (i, 0, 0)),
        scratch_shapes=[pltpu.VMEM((SLOTS, LANES), F32)] * CREP_RING,
        compiler_params=_params(),
        name="peer_v",
    )(idx.reshape(nb, 1, TB * SLOTS), c, tab_v)
    return y.reshape(n, D_MODEL)


def _final_kernel(x1_ref, y_ref, pp_ref, ps_ref, gple_ref, wg_ref, wp_ref, gfin_ref, op_ref, os_ref):
    x2 = x1_ref[...] + y_ref[...]
    gate = jax.nn.sigmoid(_dot(_rms(x2, gple_ref[...]).astype(BF16), wg_ref[...]))
    x3 = x2 + gate * _dot(_group_rows(pp_ref, ps_ref).astype(BF16), wp_ref[...])
    out = _rms(x3, gfin_ref[...])
    is_prompt = pl.program_id(0) < P_TILES

    @pl.when(is_prompt)
    def _():
        op_ref[...] = out

    @pl.when(jnp.logical_not(is_prompt))
    def _():
        os_ref[...] = out


def _final(x1, y, pp, ps, gple, wg, wp, gfin):
    return pl.pallas_call(
        _final_kernel,
        out_shape=[jax.ShapeDtypeStruct((N_P, D_MODEL), F32), jax.ShapeDtypeStruct((N_S, D_MODEL), F32)],
        grid=(N_TOK // TM,),
        in_specs=[_rows(D_MODEL), _rows(D_MODEL), _rows_prompt(PLE_DIM), _rows_sample(PLE_DIM),
                  _const(gple.shape), _const(wg.shape), _const(wp.shape), _const(gfin.shape)],
        out_specs=[_rows_prompt(D_MODEL), _rows_sample(D_MODEL)],
        compiler_params=_params(),
        name="ple_final",
    )(x1, y, pp, ps, gple, wg, wp, gfin)


def _rope_tables():
    half = ROPE // 2
    freqs = ROPE_THETA ** (-np.arange(half, dtype=np.float32) / half)
    pos = np.concatenate([np.arange(SEQ), np.tile(PAST_LEN + np.arange(DEC_SEQ), TM // DEC_SEQ)])
    ang = jnp.asarray(pos, F32)[:, None] * jnp.asarray(freqs, F32)[None, :]
    cos, sin = jnp.cos(ang), jnp.sin(ang)
    n = pos.shape[0]
    pad = jnp.zeros((n, LANES - NOPE - ROPE), F32)
    cos_t = jnp.concatenate([jnp.ones((n, NOPE), F32), cos, cos, pad], axis=1)
    sin_t = jnp.concatenate([jnp.zeros((n, NOPE), F32), sin, sin, pad], axis=1)
    return cos_t, sin_t


def _rot_cols(w):
    half = ROPE // 2
    return jnp.concatenate([-w[..., half:], w[..., :half]], axis=-1)


def _rel_bias(table, n_q, n_k):
    diag = np.arange(-(n_q - 1), n_k)
    line = table[:, np.clip(WINDOW_B - diag, -REL_CLIP, REL_CLIP) + REL_CLIP]
    return jnp.stack([line[:, n_q - 1 - i:n_q - 1 - i + n_k] for i in range(n_q)], axis=1)


def kernel(x_prompt, x_sample, cache_latent, cache_krope, cache_band_k, cache_band_v, p_prompt, p_sample, g_mix, w_in, g_q_lora, g_kv_lora, w_uq, w_uk, w_uv, rel_bias, w_a_proj, w_b_proj, w_out, g_ffn, w_query, sub_keys, expert_u, expert_v, g_ple, w_ple_gate, w_ple_proj, g_final):
    w = w_in[0]
    o_kr = Q_LORA + KV_LORA
    o_qb = o_kr + ROPE
    o_ga = o_qb + 3 * HB
    w_kr = w[:, o_kr:o_qb]
    z64 = jnp.zeros((D_MODEL, NOPE), F32)
    z32 = jnp.zeros((D_MODEL, LANES - NOPE - ROPE), F32)
    w1 = jnp.concatenate([w[:, :o_kr], z64, w_kr, z32, z64, _rot_cols(w_kr), z32], axis=1).astype(BF16)
    w2 = w[:, o_qb:o_ga].astype(BF16)
    w3 = w[:, o_ga:].astype(BF16)
    wq = w_uq[0].reshape(Q_LORA, H_A, NOPE + ROPE)
    zq64 = jnp.zeros((Q_LORA, H_A, NOPE), F32)
    zq32 = jnp.zeros((Q_LORA, H_A, LANES - NOPE - ROPE), F32)
    wuq1 = jnp.concatenate([wq, zq32], axis=-1).reshape(Q_LORA, H_A * LANES).astype(BF16)
    wuq2 = jnp.concatenate([zq64, _rot_cols(wq[..., NOPE:]), zq32], axis=-1).reshape(Q_LORA, H_A * LANES).astype(BF16)
    wukp = jnp.concatenate([w_uk[0], jnp.zeros((KV_LORA, H_A, LANES - NOPE), F32)], axis=-1)
    wukp = wukp.reshape(KV_LORA, H_A * LANES).astype(BF16)
    wuv = w_uv[0].reshape(KV_LORA, H_A * V_A).astype(BF16)
    wukt = jnp.transpose(w_uk[0], (1, 2, 0)).astype(BF16)
    wuvh = jnp.transpose(w_uv[0], (1, 0, 2)).astype(BF16)
    cos_t, sin_t = _rope_tables()
    row = lambda g: g.reshape(1, -1)

    xp = x_prompt.reshape(N_P, D_MODEL)
    xs = x_sample.reshape(N_S, D_MODEL)

    (lat, kr, q16, k16, v16, qb16, kb16, vb16, kb32, vb32, sga, sgb) = _inproj(
        xp, xs, row(g_mix[0]), w1, w2, w3, row(g_q_lora[0]), row(g_kv_lora[0]), wuq1, wuq2, wukp, wuv, cos_t, sin_t)

    oa_p = _mla_prompt(q16, k16, v16)
    oa_s = _mla_sample(q16, cache_latent[0], cache_krope[0], lat, kr, wukt, wuvh)

    pad = ((0, 0), (WINDOW_B, 0), (0, 0))
    kpad = jnp.pad(kb16[:N_P].reshape(BATCH, SEQ, HB), pad)
    vpad = jnp.pad(vb16[:N_P].reshape(BATCH, SEQ, HB), pad)
    tab = rel_bias[0]
    bias_p = _rel_bias(tab, CHUNK, BAND)
    bias_s = _rel_bias(tab, DEC_SEQ, WINDOW_B + DEC_SEQ)
    ob_p = _band_prompt(qb16, kpad, vpad, bias_p)
    ob_s, bk_s, bv_s = _band_sample(qb16, cache_band_k[0].reshape(DEC_BATCH, WINDOW_B, HB),
                                    cache_band_v[0].reshape(DEC_BATCH, WINDOW_B, HB),
                                    kb16, vb16, kb32, vb32, bias_s[:, :, :WINDOW_B], bias_s[:, :, WINDOW_B:])

    oa = jnp.concatenate([oa_p, oa_s], axis=0)
    ob = jnp.concatenate([ob_p, ob_s], axis=0)
    x1, h2, h16 = _mix_out(oa, ob, sga, sgb, xp, xs, w_a_proj[0].astype(BF16), w_b_proj[0].astype(BF16),
                           w_out[0].astype(BF16), row(g_ffn[0]))
    wqt = jnp.transpose(w_query[0]).astype(BF16)
    keys = sub_keys[0].reshape(2 * PEER_HEADS, N_KEYS, N_KEYS).astype(BF16)
    idx, gw = _peer_topk(h16, wqt, keys)

    c = _peer_u(idx, h2, gw, _pack_table(expert_u[0]))
    y = _peer_v(idx, c, _pack_table(expert_v[0]))

    out_p, out_s = _final(x1, y, p_prompt.reshape(N_P, PLE_DIM), p_sample.reshape(N_S, PLE_DIM), row(g_ple[0]),
                          w_ple_gate[0].astype(BF16), w_ple_proj[0].astype(BF16), row(g_final))

    y_prompt = out_p.reshape(BATCH, SEQ, D_MODEL)
    y_sample = out_s.reshape(DEC_BATCH, DEC_SEQ, D_MODEL)
    lat_p = lat[:N_P].reshape(1, BATCH, SEQ, KV_LORA)
    kr_p = kr[:N_P].reshape(1, BATCH, SEQ, ROPE)
    bk_p = kb32[:N_P].reshape(BATCH, SEQ, H_B, DH_B)[None, :, SEQ - WINDOW_B:]
    bv_p = vb32[:N_P].reshape(BATCH, SEQ, H_B, DH_B)[None, :, SEQ - WINDOW_B:]
    lat_s = lat[N_P:].reshape(1, DEC_BATCH, DEC_SEQ, KV_LORA)
    kr_s = kr[N_P:].reshape(1, DEC_BATCH, DEC_SEQ, ROPE)
    bk_s = bk_s.reshape(1, DEC_BATCH, WINDOW_B, H_B, DH_B)
    bv_s = bv_s.reshape(1, DEC_BATCH, WINDOW_B, H_B, DH_B)
    return (y_prompt, y_sample, lat_p, kr_p, bk_p, bv_p, lat_s, kr_s, bk_s, bv_s)
```

```python
import functools
import jax
import jax.numpy as jnp
import numpy as np
from jax import lax
from jax.experimental import pallas as pl
from jax.experimental.pallas import tpu as pltpu

D_MODEL = 1024
BATCH = 16
SEQ = 2048
DEC_BATCH = 32
DEC_SEQ = 32
PAST_LEN = 4096
CHUNK = 64
EPS = 1e-6
H_A = 8
Q_LORA = 384
KV_LORA = 256
NOPE = 64
ROPE = 32
V_A = 64
ROPE_THETA = 10000.0
MLA_SCALE = (NOPE + ROPE) ** -0.5
H_B = 8
DH_B = 64
BAND_PREV = 8
BAND = (BAND_PREV + 1) * CHUNK
WINDOW_B = BAND_PREV * CHUNK
REL_CLIP = 256
BAND_SCALE = DH_B ** -0.5
PEER_HEADS = 8
N_KEYS = 128
N_EXPERTS = N_KEYS * N_KEYS
TOPK = 16
PLE_DIM = 256

N_P = BATCH * SEQ
N_S = DEC_BATCH * DEC_SEQ
N_TOK = N_P + N_S
HB = H_B * DH_B
HALF = D_MODEL // 2
SLOTS = PEER_HEADS * TOPK
LANES = 128
SUBLANES = 8
TM = 256
TQ = 128
KV_STEP = 512
TB = 64
CREP_RING = 4
ROW_WORDS = HALF // LANES
PAIRS = SLOTS // 2
STRIDE = 72
HEADS_PER_TRIP = 8
IDX_GROUP = 8
VMEM_LIMIT = 56 * 1024 * 1024
NEG = -1e30
BF16 = jnp.bfloat16
F32 = jnp.float32
NT = (((1,), (1,)), ((), ()))


def _params(n_axes=1):
    return pltpu.CompilerParams(dimension_semantics=("arbitrary",) * n_axes,
                                vmem_limit_bytes=VMEM_LIMIT)


def _const(shape):
    nd = len(shape)
    return pl.BlockSpec(shape, lambda *_: (0,) * nd, pipeline_mode=pl.Buffered(1))


def _rows(width, tile=TM):
    return pl.BlockSpec((tile, width), lambda i: (i, 0))


P_TILES = N_P // TM


def _rows_prompt(width):
    return pl.BlockSpec((TM, width), lambda i: (jnp.minimum(i, P_TILES - 1), 0))


def _rows_sample(width):
    return pl.BlockSpec((TM, width), lambda i: (jnp.maximum(i - P_TILES, 0), 0))


def _group_rows(p_ref, s_ref):
    return jnp.where(pl.program_id(0) < P_TILES, p_ref[...], s_ref[...])


def _rms(x, g):
    return x * lax.rsqrt(jnp.mean(x * x, axis=-1, keepdims=True) + EPS) * g


def _dot(a, b):
    return jnp.dot(a, b, preferred_element_type=F32)


def _softmax_rows(s_list):
    m = functools.reduce(jnp.maximum, [jnp.max(s, axis=-1, keepdims=True) for s in s_list])
    p_list = [jnp.exp(s - m) for s in s_list]
    inv = 1.0 / functools.reduce(jnp.add, [jnp.sum(p, axis=-1, keepdims=True) for p in p_list])
    return [(p * inv).astype(BF16) for p in p_list]


def _inproj_kernel(xp_ref, xs_ref, gmix_ref, w1_ref, w2_ref, w3_ref, gq_ref, gkv_ref, wuq1_ref, wuq2_ref,
                   wukp_ref, wuv_ref, cos_ref, sin_ref,
                   lat_ref, kr_ref, q16_ref, k16_ref, v16_ref, qb16_ref, kb16_ref, vb16_ref,
                   kb32_ref, vb32_ref, sga_ref, sgb_ref):
    h = _rms(_group_rows(xp_ref, xs_ref), gmix_ref[...]).astype(BF16)
    cos = cos_ref[...]
    sin = sin_ref[...]
    z1 = _dot(h, w1_ref[...])
    k128 = z1[:, 640:768] * cos + z1[:, 768:896] * sin
    kr_ref[...] = k128[:, NOPE:NOPE + ROPE]
    cqn = _rms(z1[:, :Q_LORA], gq_ref[...]).astype(BF16)
    cos8 = jnp.concatenate([cos] * H_A, axis=1)
    sin8 = jnp.concatenate([sin] * H_A, axis=1)
    q = _dot(cqn, wuq1_ref[...]) * cos8 + _dot(cqn, wuq2_ref[...]) * sin8
    q16_ref[...] = q.astype(BF16)
    ckvn = _rms(z1[:, Q_LORA:Q_LORA + KV_LORA], gkv_ref[...])
    lat_ref[...] = ckvn
    ckvn16 = ckvn.astype(BF16)
    kk = _dot(ckvn16, wukp_ref[...]) + jnp.concatenate([k128] * H_A, axis=1)
    k16_ref[...] = kk.astype(BF16)
    v16_ref[...] = _dot(ckvn16, wuv_ref[...]).astype(BF16)
    z2 = _dot(h, w2_ref[...])
    qb16_ref[...] = z2[:, :HB].astype(BF16)
    kb = z2[:, HB:2 * HB]
    vb = z2[:, 2 * HB:]
    kb32_ref[...] = kb
    vb32_ref[...] = vb
    kb16_ref[...] = kb.astype(BF16)
    vb16_ref[...] = vb.astype(BF16)
    z3 = _dot(h, w3_ref[...])
    sga_ref[...] = jax.nn.sigmoid(z3[:, :D_MODEL])
    sgb_ref[...] = jax.nn.sigmoid(z3[:, D_MODEL:])


def _rope_rows():
    per_seq = SEQ // TM
    return pl.BlockSpec((TM, LANES), lambda i: (jnp.where(i < N_P // TM, i % per_seq, per_seq), 0))


def _inproj(xp, xs, gmix, w1, w2, w3, gq, gkv, wuq1, wuq2, wukp, wuv, cos, sin):
    n = N_TOK
    widths = [(KV_LORA, F32), (ROPE, F32), (H_A * LANES, BF16), (H_A * LANES, BF16), (H_A * V_A, BF16),
              (HB, BF16), (HB, BF16), (HB, BF16), (HB, F32), (HB, F32), (D_MODEL, F32), (D_MODEL, F32)]
    return pl.pallas_call(
        _inproj_kernel,
        out_shape=[jax.ShapeDtypeStruct((n, w), d) for w, d in widths],
        grid=(n // TM,),
        in_specs=[_rows_prompt(D_MODEL), _rows_sample(D_MODEL),
                  _const(gmix.shape), _const(w1.shape), _const(w2.shape), _const(w3.shape),
                  _const(gq.shape), _const(gkv.shape), _const(wuq1.shape), _const(wuq2.shape),
                  _const(wukp.shape), _const(wuv.shape), _rope_rows(), _rope_rows()],
        out_specs=[_rows(w) for w, _ in widths],
        compiler_params=_params(),
        name="inproj",
    )(xp, xs, gmix, w1, w2, w3, gq, gkv, wuq1, wuq2, wukp, wuv, cos, sin)


def _pair_select(o_even, o_odd):
    lane = lax.broadcasted_iota(jnp.int32, o_even.shape, 1)
    return jnp.where(lane < V_A, o_even, o_odd)


def _mla_prompt_tile(q_ref, k_ref, v_ref, o_ref, i, nk):
    q_chunk = (i * TQ + lax.broadcasted_iota(jnp.int32, (TQ, nk), 0)) // CHUNK
    k_chunk = lax.broadcasted_iota(jnp.int32, (TQ, nk), 1) // CHUNK
    mask = k_chunk <= q_chunk
    scores = []
    for hd in range(H_A):
        qh = q_ref[:, hd * LANES:(hd + 1) * LANES]
        kh = k_ref[0:nk, hd * LANES:(hd + 1) * LANES]
        s = lax.dot_general(qh, kh, NT, preferred_element_type=F32) * MLA_SCALE
        scores.append(jnp.where(mask, s, NEG))
    weights = [_softmax_rows([s])[0] for s in scores]
    outs = [_dot(weights[hd], v_ref[0:nk, (hd // 2) * LANES:(hd // 2 + 1) * LANES]) for hd in range(H_A)]
    pairs = [_pair_select(outs[2 * pr], outs[2 * pr + 1]) for pr in range(H_A // 2)]
    o_ref[...] = jnp.concatenate(pairs, axis=1).astype(BF16)


def _mla_prompt_kernel(q_ref, k_ref, v_ref, o_ref):
    i = pl.program_id(1)
    tiles_per_step = KV_STEP // TQ
    for grp in range(SEQ // KV_STEP):
        @pl.when(i // tiles_per_step == grp)
        def _():
            _mla_prompt_tile(q_ref, k_ref, v_ref, o_ref, i, (grp + 1) * KV_STEP)


def _mla_prompt(q16, k16, v16):
    nq = SEQ // TQ
    return pl.pallas_call(
        _mla_prompt_kernel,
        out_shape=jax.ShapeDtypeStruct((N_P, H_A * V_A), BF16),
        grid=(BATCH, nq),
        in_specs=[pl.BlockSpec((TQ, H_A * LANES), lambda b, i: (b * nq + i, 0)),
                  pl.BlockSpec((SEQ, H_A * LANES), lambda b, i: (b, 0)),
                  pl.BlockSpec((SEQ, H_A * V_A), lambda b, i: (b, 0))],
        out_specs=pl.BlockSpec((TQ, H_A * V_A), lambda b, i: (b * nq + i, 0)),
        compiler_params=_params(2),
        name="mla_prompt",
    )(q16, k16, v16)


def _mla_sample_kernel(q_ref, clat_ref, ckr_ref, nlat_ref, nkr_ref, wukt_ref, wuvh_ref, o_ref):
    q = q_ref[...].astype(F32)
    qlat, qrope = [], []
    for hd in range(H_A):
        qn = q[:, hd * LANES:hd * LANES + NOPE].astype(BF16)
        qlat.append(_dot(qn, wukt_ref[hd]).astype(BF16))
        qrope.append(q[:, hd * LANES + NOPE:hd * LANES + NOPE + ROPE].astype(BF16))
    qlat = jnp.concatenate(qlat, axis=0)
    qrope = jnp.concatenate(qrope, axis=0)
    clat = clat_ref[...].astype(BF16)
    ckr = ckr_ref[...].astype(BF16)
    nlat = nlat_ref[...].astype(BF16)
    nkr = nkr_ref[...].astype(BF16)
    s_c = (lax.dot_general(qlat, clat, NT, preferred_element_type=F32)
           + lax.dot_general(qrope, ckr, NT, preferred_element_type=F32)) * MLA_SCALE
    s_n = (lax.dot_general(qlat, nlat, NT, preferred_element_type=F32)
           + lax.dot_general(qrope, nkr, NT, preferred_element_type=F32)) * MLA_SCALE
    w_c, w_n = _softmax_rows([s_c, s_n])
    olat = (_dot(w_c, clat) + _dot(w_n, nlat)).astype(BF16)
    outs = [_dot(olat[hd * DEC_SEQ:(hd + 1) * DEC_SEQ, :], wuvh_ref[hd]) for hd in range(H_A)]
    o_ref[...] = jnp.concatenate(outs, axis=1).astype(BF16)


def _mla_sample(q16, cache_lat, cache_kr, lat, kr, wukt, wuvh):
    off = N_P // DEC_SEQ
    return pl.pallas_call(
        _mla_sample_kernel,
        out_shape=jax.ShapeDtypeStruct((N_S, H_A * V_A), BF16),
        grid=(DEC_BATCH,),
        in_specs=[pl.BlockSpec((DEC_SEQ, H_A * LANES), lambda b: (off + b, 0)),
                  pl.BlockSpec((None, PAST_LEN, KV_LORA), lambda b: (b, 0, 0)),
                  pl.BlockSpec((None, PAST_LEN, ROPE), lambda b: (b, 0, 0)),
                  pl.BlockSpec((DEC_SEQ, KV_LORA), lambda b: (off + b, 0)),
                  pl.BlockSpec((DEC_SEQ, ROPE), lambda b: (off + b, 0)),
                  _const(wukt.shape), _const(wuvh.shape)],
        out_specs=pl.BlockSpec((DEC_SEQ, H_A * V_A), lambda b: (b, 0)),
        compiler_params=_params(),
        name="mla_sample",
    )(q16, cache_lat, cache_kr, lat, kr, wukt, wuvh)


def _band_heads(q, blocks, bias_refs, valid):
    lane = lax.broadcasted_iota(jnp.int32, (q.shape[0], LANES), 1)
    scores = []
    for hd in range(H_B):
        sl = slice((hd // 2) * LANES, (hd // 2 + 1) * LANES)
        q2 = q[:, sl]
        own = (lane >= DH_B) if hd % 2 else (lane < DH_B)
        qm = jnp.where(own, q2, jnp.zeros_like(q2))
        ss = [lax.dot_general(qm, k[:, sl], NT, preferred_element_type=F32) * BAND_SCALE + b_ref[hd]
              for (k, _), b_ref in zip(blocks, bias_refs)]
        if valid is not None:
            ss = [jnp.where(valid, s, NEG) for s in ss]
        scores.append(ss)
    weights = [_softmax_rows(ss) for ss in scores]
    outs = []
    for hd in range(H_B):
        sl = slice((hd // 2) * LANES, (hd // 2 + 1) * LANES)
        outs.append(functools.reduce(jnp.add, [_dot(w, v[:, sl]) for w, (_, v) in zip(weights[hd], blocks)]))
    pairs = [_pair_select(outs[2 * pr], outs[2 * pr + 1]) for pr in range(H_B // 2)]
    return jnp.concatenate(pairs, axis=1).astype(BF16)


def _band_prompt_kernel(q_ref, k_ref, v_ref, bias_ref, o_ref):
    c = pl.program_id(1)
    start = pl.multiple_of(c * CHUNK, CHUNK)
    k = k_ref[pl.ds(start, BAND), :]
    v = v_ref[pl.ds(start, BAND), :]
    valid = (start - WINDOW_B + lax.broadcasted_iota(jnp.int32, (CHUNK, BAND), 1)) >= 0
    o_ref[...] = _band_heads(q_ref[...], [(k, v)], [bias_ref], valid)


def _band_prompt(qb16, kpad, vpad, bias):
    nc = SEQ // CHUNK
    return pl.pallas_call(
        _band_prompt_kernel,
        out_shape=jax.ShapeDtypeStruct((N_P, HB), BF16),
        grid=(BATCH, nc),
        in_specs=[pl.BlockSpec((CHUNK, HB), lambda b, c: (b * nc + c, 0)),
                  pl.BlockSpec((None, SEQ + WINDOW_B, HB), lambda b, c: (b, 0, 0)),
                  pl.BlockSpec((None, SEQ + WINDOW_B, HB), lambda b, c: (b, 0, 0)),
                  _const(bias.shape)],
        out_specs=pl.BlockSpec((CHUNK, HB), lambda b, c: (b * nc + c, 0)),
        compiler_params=_params(2),
        name="band_prompt",
    )(qb16, kpad, vpad, bias)


def _band_sample_kernel(q_ref, ck_ref, cv_ref, nk16_ref, nv16_ref, nk32_ref, nv32_ref, bias_c_ref, bias_n_ref,
                        o_ref, bk_ref, bv_ref):
    ck = ck_ref[...]
    cv = cv_ref[...]
    blocks = [(ck.astype(BF16), cv.astype(BF16)), (nk16_ref[...], nv16_ref[...])]
    o_ref[...] = _band_heads(q_ref[...], blocks, [bias_c_ref, bias_n_ref], None)
    keep = WINDOW_B - DEC_SEQ
    bk_ref[0:keep, :] = ck[DEC_SEQ:, :]
    bk_ref[keep:, :] = nk32_ref[...]
    bv_ref[0:keep, :] = cv[DEC_SEQ:, :]
    bv_ref[keep:, :] = nv32_ref[...]


def _band_sample(qb16, cache_k, cache_v, kb16, vb16, kb32, vb32, bias_c, bias_n):
    off = N_P // DEC_SEQ
    new = lambda: pl.BlockSpec((DEC_SEQ, HB), lambda b: (off + b, 0))
    cache = lambda: pl.BlockSpec((None, WINDOW_B, HB), lambda b: (b, 0, 0))
    return pl.pallas_call(
        _band_sample_kernel,
        out_shape=[jax.ShapeDtypeStruct((N_S, HB), BF16),
                   jax.ShapeDtypeStruct((DEC_BATCH, WINDOW_B, HB), F32),
                   jax.ShapeDtypeStruct((DEC_BATCH, WINDOW_B, HB), F32)],
        grid=(DEC_BATCH,),
        in_specs=[new(), cache(), cache(), new(), new(), new(), new(),
                  _const(bias_c.shape), _const(bias_n.shape)],
        out_specs=[pl.BlockSpec((DEC_SEQ, HB), lambda b: (b, 0)), cache(), cache()],
        compiler_params=_params(),
        name="band_sample",
    )(qb16, cache_k, cache_v, kb16, vb16, kb32, vb32, bias_c, bias_n)


def _mix_out_kernel(oa_ref, ob_ref, sga_ref, sgb_ref, xp_ref, xs_ref, wa_ref, wb_ref, wout_ref, gffn_ref,
                    x1_ref, h2_ref, h16_ref):
    merged = sga_ref[...] * _dot(oa_ref[...], wa_ref[...]) + sgb_ref[...] * _dot(ob_ref[...], wb_ref[...])
    x1 = _group_rows(xp_ref, xs_ref) + _dot(merged.astype(BF16), wout_ref[...])
    x1_ref[...] = x1
    h2 = _rms(x1, gffn_ref[...])
    h2_ref[...] = h2
    h16_ref[...] = h2.astype(BF16)


def _mix_out(oa, ob, sga, sgb, xp, xs, wa, wb, wout, gffn):
    n = N_TOK
    return pl.pallas_call(
        _mix_out_kernel,
        out_shape=[jax.ShapeDtypeStruct((n, D_MODEL), F32), jax.ShapeDtypeStruct((n, D_MODEL), F32),
                   jax.ShapeDtypeStruct((n, D_MODEL), BF16)],
        grid=(n // TM,),
        in_specs=[_rows(H_A * V_A), _rows(HB), _rows(D_MODEL), _rows(D_MODEL),
                  _rows_prompt(D_MODEL), _rows_sample(D_MODEL),
                  _const(wa.shape), _const(wb.shape), _const(wout.shape), _const(gffn.shape)],
        out_specs=[_rows(D_MODEL)] * 3,
        compiler_params=_params(),
        name="mix_out",
    )(oa, ob, sga, sgb, xp, xs, wa, wb, wout, gffn)


def _top16(s, n_rows):
    row = lax.broadcasted_iota(jnp.int32, (n_rows, TQ), 0).astype(F32)
    vals, ids = [], []
    for _ in range(TOPK):
        m = jnp.max(s, axis=0, keepdims=True)
        i = jnp.min(jnp.where(s == m, row, float(n_rows)), axis=0, keepdims=True)
        vals.append(m)
        ids.append(i)
        s = jnp.where(row == i, -jnp.inf, s)
    return jnp.concatenate(vals, axis=0), jnp.concatenate(ids, axis=0)


def _pair_candidates(f1, f2):
    h = SUBLANES
    blocks = [f1(0, 1, 0, h), f1(0, 1, h, 2 * h)]
    blocks += [f1(a, a + 1, 0, h) for a in range(1, h)]
    blocks += [f2(h, 2 * h, 0, 1)]
    return jnp.concatenate(blocks, axis=0)


def _peer_topk_kernel(h_ref, wqt_ref, keys_ref, idx_ref, g_ref, qt_ref, g_t, id_t):
    qt_ref[...] = lax.dot_general(wqt_ref[...], h_ref[...], NT, preferred_element_type=F32).astype(BF16)
    r8 = lax.broadcasted_iota(jnp.int32, (SUBLANES, TQ), 0).astype(F32)
    flat = lambda a0, a1, b0, b1: (r8 + float(b0)) + float(TOPK * a0)
    flat_t = lambda a0, a1, b0, b1: (r8 + float(a0)) * float(TOPK) + float(b0)
    cflat = _pair_candidates(flat, flat_t)

    def one_head(hd):
        tops = []
        for p in range(2):
            hp = hd * 2 + p
            qs = qt_ref[pl.ds(pl.multiple_of(hp * N_KEYS, N_KEYS), N_KEYS), :]
            tops.append(_top16(_dot(keys_ref[hp], qs), N_KEYS))
        (s1, i1), (s2, i2) = tops
        add = lambda x, y: (lambda a0, a1, b0, b1: x[a0:a1, :] + y[b0:b1, :])
        cand = _pair_candidates(add(s1, s2), add(s1, s2))
        e1 = i1 * float(N_KEYS)
        cidx = _pair_candidates(add(e1, i2), add(e1, i2))
        vals, ids = [], []
        for _ in range(TOPK):
            m = jnp.max(cand, axis=0, keepdims=True)
            c = jnp.min(jnp.where(cand == m, cflat, float(TOPK * TOPK)), axis=0, keepdims=True)
            sel = cflat == c
            vals.append(m)
            ids.append(jnp.sum(jnp.where(sel, cidx, 0.0), axis=0, keepdims=True))
            cand = jnp.where(sel, -jnp.inf, cand)
        best = jnp.concatenate(vals, axis=0)
        e = jnp.exp(best - best[0:1, :])
        g = e / jnp.sum(e, axis=0, keepdims=True)
        base = pl.multiple_of(hd * TOPK, TOPK)
        g_t[pl.ds(base, TOPK), :] = g
        id_t[pl.ds(base, TOPK), :] = jnp.concatenate(ids, axis=0)

    def head_group(grp, carry):
        for e in range(HEADS_PER_TRIP):
            one_head(grp * HEADS_PER_TRIP + e)
        return carry

    lax.fori_loop(0, PEER_HEADS // HEADS_PER_TRIP, head_group, 0)
    g_ref[...] = g_t[...].T
    idx_ref[...] = (id_t[...].T * float(ROW_WORDS)).astype(jnp.int32)


def _peer_topk(h16, wqt, keys):
    n = h16.shape[0]
    nt = n // TQ
    return pl.pallas_call(
        _peer_topk_kernel,
        out_shape=[jax.ShapeDtypeStruct((n, SLOTS), jnp.int32),
                   jax.ShapeDtypeStruct((n, SLOTS), F32)],
        grid=(nt,),
        in_specs=[_rows(D_MODEL, TQ), _const(wqt.shape), _const(keys.shape)],
        out_specs=[_rows(SLOTS, TQ)] * 2,
        scratch_shapes=[pltpu.VMEM((2 * PEER_HEADS * N_KEYS, TQ), BF16),
                        pltpu.VMEM((SLOTS, TQ), F32), pltpu.VMEM((SLOTS, TQ), F32)],
        compiler_params=_params(),
        name="peer_topk",
    )(h16, wqt, keys)


def _pack_table(t):
    b = lax.bitcast_convert_type(t.astype(BF16), jnp.uint16).astype(jnp.uint32)
    packed = (b[:, :HALF] << 16) | b[:, HALF:]
    return packed.reshape(N_EXPERTS * ROW_WORDS, LANES)


def _unpack(w):
    hi = lax.bitcast_convert_type(w & jnp.uint32(0xFFFF0000), F32)
    lo = lax.bitcast_convert_type(w << 16, F32)
    return hi, lo


def _slot_pairs(idx_ref, t):
    out = []
    for grp in range(PAIRS // IDX_GROUP):
        lo = idx_ref.at[0, pl.ds(t * SLOTS + grp * IDX_GROUP, IDX_GROUP)]
        hi = idx_ref.at[0, pl.ds(t * SLOTS + PAIRS + grp * IDX_GROUP, IDX_GROUP)]
        out.extend((lo[k], hi[k]) for k in range(IDX_GROUP))
    return out


def _load_pair(tab, ia, ib):
    wa = tab[pl.ds(pl.multiple_of(ia, ROW_WORDS), ROW_WORDS), :]
    wb = tab[pl.ds(pl.multiple_of(ib, ROW_WORDS), ROW_WORDS), :]
    return _unpack(jnp.concatenate([wa, wb], axis=0))


def _peer_u_kernel(idx_ref, x_ref, g_ref, tab, c_ref, prod_a, prod_b, rbuf, a_t):
    lane = lax.broadcasted_iota(jnp.int32, (SLOTS, LANES), 1)

    @pl.when(pl.program_id(0) == 0)
    def _():
        rbuf[...] = jnp.zeros_like(rbuf)
        a_t[...] = jnp.zeros_like(a_t)

    def reduce(slot, tok):
        col = jnp.sum(rbuf[slot], axis=1, keepdims=True)
        a_t[...] = jnp.where(lane == tok, col, a_t[...])

    def gather(t, slot, prod):
        xt = x_ref[t]
        x_hi = jnp.concatenate([xt[0:ROW_WORDS, :]] * 2, axis=0)
        x_lo = jnp.concatenate([xt[ROW_WORDS:, :]] * 2, axis=0)
        for j, (ia, ib) in enumerate(_slot_pairs(idx_ref, t)):
            hi, lo = _load_pair(tab, ia, ib)
            prod[pl.ds(j, SUBLANES, stride=STRIDE), :] = hi * x_hi + lo * x_lo
        halves = [functools.reduce(jnp.add, [prod[pl.ds((h * ROW_WORDS + s) * STRIDE, PAIRS), :]
                                             for s in range(ROW_WORDS)]) for h in range(2)]
        rbuf[slot] = jnp.concatenate(halves, axis=0)

    def two_tokens(p, carry):
        t0 = 2 * p
        reduce(0, t0 - 2)
        reduce(1, t0 - 1)
        gather(t0, 0, prod_a)
        gather(t0 + 1, 1, prod_b)
        return carry

    lax.fori_loop(0, TB // 2, two_tokens, 0)
    reduce(0, TB - 2)
    reduce(1, TB - 1)
    a = a_t[...].T[0:TB, :]
    gelu = 0.5 * a * (1.0 + lax.erf(a * (2.0 ** -0.5)))
    c_ref[...] = g_ref[...] * gelu


def _peer_u(idx, x, g, tab_u):
    n = x.shape[0]
    nb = n // TB
    return pl.pallas_call(
        _peer_u_kernel,
        out_shape=jax.ShapeDtypeStruct((n, SLOTS), F32),
        grid=(nb,),
        in_specs=[pl.BlockSpec((None, 1, TB * SLOTS), lambda i: (i, 0, 0), memory_space=pltpu.SMEM),
                  pl.BlockSpec((TB, SUBLANES, LANES), lambda i: (i, 0, 0)),
                  pl.BlockSpec((TB, SLOTS), lambda i: (i, 0)),
                  _const(tab_u.shape)],
        out_specs=pl.BlockSpec((TB, SLOTS), lambda i: (i, 0)),
        scratch_shapes=[pltpu.VMEM((SUBLANES * STRIDE, LANES), F32),
                        pltpu.VMEM((SUBLANES * STRIDE, LANES), F32),
                        pltpu.VMEM((2, SLOTS, LANES), F32),
                        pltpu.VMEM((SLOTS, LANES), F32)],
        compiler_params=_params(),
        name="peer_u",
    )(idx.reshape(nb, 1, TB * SLOTS), x.reshape(n, SUBLANES, LANES), g, tab_u)


def _peer_v_kernel(idx_ref, c_ref, tab, y_ref, *crep):
    row = lax.broadcasted_iota(jnp.int32, (SUBLANES, LANES), 0)
    eye = (lax.broadcasted_iota(jnp.int32, (SLOTS, LANES), 0)
           == lax.broadcasted_iota(jnp.int32, (SLOTS, LANES), 1))

    def spread(t, crep):
        col = jnp.sum(jnp.where(eye, c_ref[pl.ds(t, 1), :], 0.0), axis=1, keepdims=True)
        crep[...] = jnp.broadcast_to(col, (SLOTS, LANES))

    def token(t, crep):
        acc_hi = jnp.zeros((SUBLANES, LANES), F32)
        acc_lo = jnp.zeros((SUBLANES, LANES), F32)
        for j, (ia, ib) in enumerate(_slot_pairs(idx_ref, t)):
            hi, lo = _load_pair(tab, ia, ib)
            cm = jnp.where(row < ROW_WORDS, crep[pl.ds(j, 1), :], crep[pl.ds(PAIRS + j, 1), :])
            acc_hi = acc_hi + cm * hi
            acc_lo = acc_lo + cm * lo
        y_ref[t] = jnp.concatenate([acc_hi[0:ROW_WORDS, :] + acc_hi[ROW_WORDS:, :],
                                    acc_lo[0:ROW_WORDS, :] + acc_lo[ROW_WORDS:, :]], axis=0)

    ahead = CREP_RING // 2
    for k in range(ahead):
        spread(k, crep[k])

    def ring_trip(p, carry):
        t0 = CREP_RING * p
        for k in range(CREP_RING):
            spread(jnp.minimum(t0 + k + ahead, TB - 1), crep[(k + ahead) % CREP_RING])
            token(t0 + k, crep[k])
        return carry

    lax.fori_loop(0, TB // CREP_RING, ring_trip, 0)


def _peer_v(idx, c, tab_v):
    n = c.shape[0]
    nb = n // TB
    y = pl.pallas_call(
        _peer_v_kernel,
        out_shape=jax.ShapeDtypeStruct((n, SUBLANES, LANES), F32),
        grid=(nb,),
        in_specs=[pl.BlockSpec((None, 1, TB * SLOTS), lambda i: (i, 0, 0), memory_space=pltpu.SMEM),
                  pl.BlockSpec((TB, SLOTS), lambda i: (i, 0)),
                  _const(tab_v.shape)],
        out_specs=pl.BlockSpec((TB, SUBLANES, LANES), lambda i: (i, 0, 0)),
        scratch_shapes=[pltpu.VMEM((SLOTS, LANES), F32)] * CREP_RING,
        compiler_params=_params(),
        name="peer_v",
    )(idx.reshape(nb, 1, TB * SLOTS), c, tab_v)
    return y.reshape(n, D_MODEL)


def _final_kernel(x1_ref, y_ref, pp_ref, ps_ref, gple_ref, wg_ref, wp_ref, gfin_ref, op_ref, os_ref):
    x2 = x1_ref[...] + y_ref[...]
    gate = jax.nn.sigmoid(_dot(_rms(x2, gple_ref[...]).astype(BF16), wg_ref[...]))
    x3 = x2 + gate * _dot(_group_rows(pp_ref, ps_ref).astype(BF16), wp_ref[...])
    out = _rms(x3, gfin_ref[...])
    is_prompt = pl.program_id(0) < P_TILES

    @pl.when(is_prompt)
    def _():
        op_ref[...] = out

    @pl.when(jnp.logical_not(is_prompt))
    def _():
        os_ref[...] = out


def _final(x1, y, pp, ps, gple, wg, wp, gfin):
    return pl.pallas_call(
        _final_kernel,
        out_shape=[jax.ShapeDtypeStruct((N_P, D_MODEL), F32), jax.ShapeDtypeStruct((N_S, D_MODEL), F32)],
        grid=(N_TOK // TM,),
        in_specs=[_rows(D_MODEL), _rows(D_MODEL), _rows_prompt(PLE_DIM), _rows_sample(PLE_DIM),
                  _const(gple.shape), _const(wg.shape), _const(wp.shape), _const(gfin.shape)],
        out_specs=[_rows_prompt(D_MODEL), _rows_sample(D_MODEL)],
        compiler_params=_params(),
        name="ple_final",
    )(x1, y, pp, ps, gple, wg, wp, gfin)


def _rope_tables():
    half = ROPE // 2
    freqs = ROPE_THETA ** (-np.arange(half, dtype=np.float32) / half)
    pos = np.concatenate([np.arange(SEQ), np.tile(PAST_LEN + np.arange(DEC_SEQ), TM // DEC_SEQ)])
    ang = jnp.asarray(pos, F32)[:, None] * jnp.asarray(freqs, F32)[None, :]
    cos, sin = jnp.cos(ang), jnp.sin(ang)
    n = pos.shape[0]
    pad = jnp.zeros((n, LANES - NOPE - ROPE), F32)
    cos_t = jnp.concatenate([jnp.ones((n, NOPE), F32), cos, cos, pad], axis=1)
    sin_t = jnp.concatenate([jnp.zeros((n, NOPE), F32), sin, sin, pad], axis=1)
    return cos_t, sin_t


def _rot_cols(w):
    half = ROPE // 2
    return jnp.concatenate([-w[..., half:], w[..., :half]], axis=-1)


def _rel_bias(table, n_q, n_k):
    diag = np.arange(-(n_q - 1), n_k)
    line = table[:, np.clip(WINDOW_B - diag, -REL_CLIP, REL_CLIP) + REL_CLIP]
    return jnp.stack([line[:, n_q - 1 - i:n_q - 1 - i + n_k] for i in range(n_q)], axis=1)


def kernel(x_prompt, x_sample, cache_latent, cache_krope, cache_band_k, cache_band_v, p_prompt, p_sample, g_mix, w_in, g_q_lora, g_kv_lora, w_uq, w_uk, w_uv, rel_bias, w_a_proj, w_b_proj, w_out, g_ffn, w_query, sub_keys, expert_u, expert_v, g_ple, w_ple_gate, w_ple_proj, g_final):
    w = w_in[0]
    o_kr = Q_LORA + KV_LORA
    o_qb = o_kr + ROPE
    o_ga = o_qb + 3 * HB
    w_kr = w[:, o_kr:o_qb]
    z64 = jnp.zeros((D_MODEL, NOPE), F32)
    z32 = jnp.zeros((D_MODEL, LANES - NOPE - ROPE), F32)
    w1 = jnp.concatenate([w[:, :o_kr], z64, w_kr, z32, z64, _rot_cols(w_kr), z32], axis=1).astype(BF16)
    w2 = w[:, o_qb:o_ga].astype(BF16)
    w3 = w[:, o_ga:].astype(BF16)
    wq = w_uq[0].reshape(Q_LORA, H_A, NOPE + ROPE)
    zq64 = jnp.zeros((Q_LORA, H_A, NOPE), F32)
    zq32 = jnp.zeros((Q_LORA, H_A, LANES - NOPE - ROPE), F32)
    wuq1 = jnp.concatenate([wq, zq32], axis=-1).reshape(Q_LORA, H_A * LANES).astype(BF16)
    wuq2 = jnp.concatenate([zq64, _rot_cols(wq[..., NOPE:]), zq32], axis=-1).reshape(Q_LORA, H_A * LANES).astype(BF16)
    wukp = jnp.concatenate([w_uk[0], jnp.zeros((KV_LORA, H_A, LANES - NOPE), F32)], axis=-1)
    wukp = wukp.reshape(KV_LORA, H_A * LANES).astype(BF16)
    wuv = w_uv[0].reshape(KV_LORA, H_A * V_A).astype(BF16)
    wukt = jnp.transpose(w_uk[0], (1, 2, 0)).astype(BF16)
    wuvh = jnp.transpose(w_uv[0], (1, 0, 2)).astype(BF16)
    cos_t, sin_t = _rope_tables()
    row = lambda g: g.reshape(1, -1)

    xp = x_prompt.reshape(N_P, D_MODEL)
    xs = x_sample.reshape(N_S, D_MODEL)

    (lat, kr, q16, k16, v16, qb16, kb16, vb16, kb32, vb32, sga, sgb) = _inproj(
        xp, xs, row(g_mix[0]), w1, w2, w3, row(g_q_lora[0]), row(g_kv_lora[0]), wuq1, wuq2, wukp, wuv, cos_t, sin_t)

    oa_p = _mla_prompt(q16, k16, v16)
    oa_s = _mla_sample(q16, cache_latent[0], cache_krope[0], lat, kr, wukt, wuvh)

    pad = ((0, 0), (WINDOW_B, 0), (0, 0))
    kpad = jnp.pad(kb16[:N_P].reshape(BATCH, SEQ, HB), pad)
    vpad = jnp.pad(vb16[:N_P].reshape(BATCH, SEQ, HB), pad)
    tab = rel_bias[0]
    bias_p = _rel_bias(tab, CHUNK, BAND)
    bias_s = _rel_bias(tab, DEC_SEQ, WINDOW_B + DEC_SEQ)
    ob_p = _band_prompt(qb16, kpad, vpad, bias_p)
    ob_s, bk_s, bv_s = _band_sample(qb16, cache_band_k[0].reshape(DEC_BATCH, WINDOW_B, HB),
                                    cache_band_v[0].reshape(DEC_BATCH, WINDOW_B, HB),
                                    kb16, vb16, kb32, vb32, bias_s[:, :, :WINDOW_B], bias_s[:, :, WINDOW_B:])

    oa = jnp.concatenate([oa_p, oa_s], axis=0)
    ob = jnp.concatenate([ob_p, ob_s], axis=0)
    x1, h2, h16 = _mix_out(oa, ob, sga, sgb, xp, xs, w_a_proj[0].astype(BF16), w_b_proj[0].astype(BF16),
                           w_out[0].astype(BF16), row(g_ffn[0]))
    wqt = jnp.transpose(w_query[0]).astype(BF16)
    keys = sub_keys[0].reshape(2 * PEER_HEADS, N_KEYS, N_KEYS).astype(BF16)
    idx, gw = _peer_topk(h16, wqt, keys)

    c = _peer_u(idx, h2, gw, _pack_table(expert_u[0]))
    y = _peer_v(idx, c, _pack_table(expert_v[0]))

    out_p, out_s = _final(x1, y, p_prompt.reshape(N_P, PLE_DIM), p_sample.reshape(N_S, PLE_DIM), row(g_ple[0]),
                          w_ple_gate[0].astype(BF16), w_ple_proj[0].astype(BF16), row(g_final))

    y_prompt = out_p.reshape(BATCH, SEQ, D_MODEL)
    y_sample = out_s.reshape(DEC_BATCH, DEC_SEQ, D_MODEL)
    lat_p = lat[:N_P].reshape(1, BATCH, SEQ, KV_LORA)
    kr_p = kr[:N_P].reshape(1, BATCH, SEQ, ROPE)
    bk_p = kb32[:N_P].reshape(BATCH, SEQ, H_B, DH_B)[None, :, SEQ - WINDOW_B:]
    bv_p = vb32[:N_P].reshape(BATCH, SEQ, H_B, DH_B)[None, :, SEQ - WINDOW_B:]
    lat_s = lat[N_P:].reshape(1, DEC_BATCH, DEC_SEQ, KV_LORA)
    kr_s = kr[N_P:].reshape(1, DEC_BATCH, DEC_SEQ, ROPE)
    bk_s = bk_s.reshape(1, DEC_BATCH, WINDOW_B, H_B, DH_B)
    bv_s = bv_s.reshape(1, DEC_BATCH, WINDOW_B, H_B, DH_B)
    return (y_prompt, y_sample, lat_p, kr_p, bk_p, bv_p, lat_s, kr_s, bk_s, bv_s)
```

```python
import functools
import jax
import jax.numpy as jnp
import numpy as np
from jax import lax
from jax.experimental import pallas as pl
from jax.experimental.pallas import tpu as pltpu
from jax.experimental.pallas import tpu_sc as plsc

D_MODEL = 1024
BATCH = 16
SEQ = 2048
DEC_BATCH = 32
DEC_SEQ = 32
PAST_LEN = 4096
CHUNK = 64
EPS = 1e-6
H_A = 8
Q_LORA = 384
KV_LORA = 256
NOPE = 64
ROPE = 32
V_A = 64
ROPE_THETA = 10000.0
MLA_SCALE = (NOPE + ROPE) ** -0.5
H_B = 8
DH_B = 64
BAND_PREV = 8
BAND = (BAND_PREV + 1) * CHUNK
WINDOW_B = BAND_PREV * CHUNK
REL_CLIP = 256
BAND_SCALE = DH_B ** -0.5
PEER_HEADS = 8
N_KEYS = 128
N_EXPERTS = N_KEYS * N_KEYS
TOPK = 16
PLE_DIM = 256

N_P = BATCH * SEQ
N_S = DEC_BATCH * DEC_SEQ
N_TOK = N_P + N_S
HB = H_B * DH_B
HALF = D_MODEL // 2
SLOTS = PEER_HEADS * TOPK
LANES = 128
SUBLANES = 8
TM = 256
TQ = 128
KV_STEP = 512
TB = 64
CREP_RING = 4
ROW_WORDS = HALF // LANES
PAIRS = SLOTS // 2
STRIDE = 72
HEADS_PER_TRIP = 8
IDX_GROUP = 8
VMEM_LIMIT = 56 * 1024 * 1024
SC_WORKERS = 32
SC_LANES = 16
SC_TOKENS = 8
SC_ROWS = 32
N_SC = 16384
NEG = -1e30
BF16 = jnp.bfloat16
F32 = jnp.float32
NT = (((1,), (1,)), ((), ()))


def _params(n_axes=1):
    return pltpu.CompilerParams(dimension_semantics=("arbitrary",) * n_axes,
                                vmem_limit_bytes=VMEM_LIMIT)


def _const(shape):
    nd = len(shape)
    return pl.BlockSpec(shape, lambda *_: (0,) * nd, pipeline_mode=pl.Buffered(1))


def _rows(width, tile=TM):
    return pl.BlockSpec((tile, width), lambda i: (i, 0))


P_TILES = N_P // TM


def _rows_prompt(width):
    return pl.BlockSpec((TM, width), lambda i: (jnp.minimum(i, P_TILES - 1), 0))


def _rows_sample(width):
    return pl.BlockSpec((TM, width), lambda i: (jnp.maximum(i - P_TILES, 0), 0))


def _group_rows(p_ref, s_ref):
    return jnp.where(pl.program_id(0) < P_TILES, p_ref[...], s_ref[...])


def _rms(x, g):
    return x * lax.rsqrt(jnp.mean(x * x, axis=-1, keepdims=True) + EPS) * g


def _dot(a, b):
    return jnp.dot(a, b, preferred_element_type=F32)


def _softmax_rows(s_list):
    m = functools.reduce(jnp.maximum, [jnp.max(s, axis=-1, keepdims=True) for s in s_list])
    p_list = [jnp.exp(s - m) for s in s_list]
    inv = 1.0 / functools.reduce(jnp.add, [jnp.sum(p, axis=-1, keepdims=True) for p in p_list])
    return [(p * inv).astype(BF16) for p in p_list]


def _inproj_kernel(xp_ref, xs_ref, gmix_ref, w1_ref, w2_ref, w3_ref, gq_ref, gkv_ref, wuq1_ref, wuq2_ref,
                   wukp_ref, wuv_ref, cos_ref, sin_ref,
                   lat_ref, kr_ref, q16_ref, k16_ref, v16_ref, qb16_ref, kb16_ref, vb16_ref,
                   kb32_ref, vb32_ref, sga_ref, sgb_ref):
    h = _rms(_group_rows(xp_ref, xs_ref), gmix_ref[...]).astype(BF16)
    cos = cos_ref[...]
    sin = sin_ref[...]
    z1 = _dot(h, w1_ref[...])
    k128 = z1[:, 640:768] * cos + z1[:, 768:896] * sin
    kr_ref[...] = k128[:, NOPE:NOPE + ROPE]
    cqn = _rms(z1[:, :Q_LORA], gq_ref[...]).astype(BF16)
    cos8 = jnp.concatenate([cos] * H_A, axis=1)
    sin8 = jnp.concatenate([sin] * H_A, axis=1)
    q = _dot(cqn, wuq1_ref[...]) * cos8 + _dot(cqn, wuq2_ref[...]) * sin8
    q16_ref[...] = q.astype(BF16)
    ckvn = _rms(z1[:, Q_LORA:Q_LORA + KV_LORA], gkv_ref[...])
    lat_ref[...] = ckvn
    ckvn16 = ckvn.astype(BF16)
    kk = _dot(ckvn16, wukp_ref[...]) + jnp.concatenate([k128] * H_A, axis=1)
    k16_ref[...] = kk.astype(BF16)
    v16_ref[...] = _dot(ckvn16, wuv_ref[...]).astype(BF16)
    z2 = _dot(h, w2_ref[...])
    qb16_ref[...] = z2[:, :HB].astype(BF16)
    kb = z2[:, HB:2 * HB]
    vb = z2[:, 2 * HB:]
    kb32_ref[...] = kb
    vb32_ref[...] = vb
    kb16_ref[...] = kb.astype(BF16)
    vb16_ref[...] = vb.astype(BF16)
    z3 = _dot(h, w3_ref[...])
    sga_ref[...] = jax.nn.sigmoid(z3[:, :D_MODEL])
    sgb_ref[...] = jax.nn.sigmoid(z3[:, D_MODEL:])


def _rope_rows():
    per_seq = SEQ // TM
    return pl.BlockSpec((TM, LANES), lambda i: (jnp.where(i < N_P // TM, i % per_seq, per_seq), 0))


def _inproj(xp, xs, gmix, w1, w2, w3, gq, gkv, wuq1, wuq2, wukp, wuv, cos, sin):
    n = N_TOK
    widths = [(KV_LORA, F32), (ROPE, F32), (H_A * LANES, BF16), (H_A * LANES, BF16), (H_A * V_A, BF16),
              (HB, BF16), (HB, BF16), (HB, BF16), (HB, F32), (HB, F32), (D_MODEL, F32), (D_MODEL, F32)]
    return pl.pallas_call(
        _inproj_kernel,
        out_shape=[jax.ShapeDtypeStruct((n, w), d) for w, d in widths],
        grid=(n // TM,),
        in_specs=[_rows_prompt(D_MODEL), _rows_sample(D_MODEL),
                  _const(gmix.shape), _const(w1.shape), _const(w2.shape), _const(w3.shape),
                  _const(gq.shape), _const(gkv.shape), _const(wuq1.shape), _const(wuq2.shape),
                  _const(wukp.shape), _const(wuv.shape), _rope_rows(), _rope_rows()],
        out_specs=[_rows(w) for w, _ in widths],
        compiler_params=_params(),
        name="inproj",
    )(xp, xs, gmix, w1, w2, w3, gq, gkv, wuq1, wuq2, wukp, wuv, cos, sin)


def _pair_select(o_even, o_odd):
    lane = lax.broadcasted_iota(jnp.int32, o_even.shape, 1)
    return jnp.where(lane < V_A, o_even, o_odd)


def _mla_prompt_tile(q_ref, k_ref, v_ref, o_ref, i, nk):
    q_chunk = (i * TQ + lax.broadcasted_iota(jnp.int32, (TQ, nk), 0)) // CHUNK
    k_chunk = lax.broadcasted_iota(jnp.int32, (TQ, nk), 1) // CHUNK
    mask = k_chunk <= q_chunk
    scores = []
    for hd in range(H_A):
        qh = q_ref[:, hd * LANES:(hd + 1) * LANES]
        kh = k_ref[0:nk, hd * LANES:(hd + 1) * LANES]
        s = lax.dot_general(qh, kh, NT, preferred_element_type=F32) * MLA_SCALE
        scores.append(jnp.where(mask, s, NEG))
    weights = [_softmax_rows([s])[0] for s in scores]
    outs = [_dot(weights[hd], v_ref[0:nk, (hd // 2) * LANES:(hd // 2 + 1) * LANES]) for hd in range(H_A)]
    pairs = [_pair_select(outs[2 * pr], outs[2 * pr + 1]) for pr in range(H_A // 2)]
    o_ref[...] = jnp.concatenate(pairs, axis=1).astype(BF16)


def _mla_prompt_kernel(q_ref, k_ref, v_ref, o_ref):
    i = pl.program_id(1)
    tiles_per_step = KV_STEP // TQ
    for grp in range(SEQ // KV_STEP):
        @pl.when(i // tiles_per_step == grp)
        def _():
            _mla_prompt_tile(q_ref, k_ref, v_ref, o_ref, i, (grp + 1) * KV_STEP)


def _mla_prompt(q16, k16, v16):
    nq = SEQ // TQ
    return pl.pallas_call(
        _mla_prompt_kernel,
        out_shape=jax.ShapeDtypeStruct((N_P, H_A * V_A), BF16),
        grid=(BATCH, nq),
        in_specs=[pl.BlockSpec((TQ, H_A * LANES), lambda b, i: (b * nq + i, 0)),
                  pl.BlockSpec((SEQ, H_A * LANES), lambda b, i: (b, 0)),
                  pl.BlockSpec((SEQ, H_A * V_A), lambda b, i: (b, 0))],
        out_specs=pl.BlockSpec((TQ, H_A * V_A), lambda b, i: (b * nq + i, 0)),
        compiler_params=_params(2),
        name="mla_prompt",
    )(q16, k16, v16)


def _mla_sample_kernel(q_ref, clat_ref, ckr_ref, nlat_ref, nkr_ref, wukt_ref, wuvh_ref, o_ref):
    q = q_ref[...].astype(F32)
    qlat, qrope = [], []
    for hd in range(H_A):
        qn = q[:, hd * LANES:hd * LANES + NOPE].astype(BF16)
        qlat.append(_dot(qn, wukt_ref[hd]).astype(BF16))
        qrope.append(q[:, hd * LANES + NOPE:hd * LANES + NOPE + ROPE].astype(BF16))
    qlat = jnp.concatenate(qlat, axis=0)
    qrope = jnp.concatenate(qrope, axis=0)
    clat = clat_ref[...].astype(BF16)
    ckr = ckr_ref[...].astype(BF16)
    nlat = nlat_ref[...].astype(BF16)
    nkr = nkr_ref[...].astype(BF16)
    s_c = (lax.dot_general(qlat, clat, NT, preferred_element_type=F32)
           + lax.dot_general(qrope, ckr, NT, preferred_element_type=F32)) * MLA_SCALE
    s_n = (lax.dot_general(qlat, nlat, NT, preferred_element_type=F32)
           + lax.dot_general(qrope, nkr, NT, preferred_element_type=F32)) * MLA_SCALE
    w_c, w_n = _softmax_rows([s_c, s_n])
    olat = (_dot(w_c, clat) + _dot(w_n, nlat)).astype(BF16)
    outs = [_dot(olat[hd * DEC_SEQ:(hd + 1) * DEC_SEQ, :], wuvh_ref[hd]) for hd in range(H_A)]
    o_ref[...] = jnp.concatenate(outs, axis=1).astype(BF16)


def _mla_sample(q16, cache_lat, cache_kr, lat, kr, wukt, wuvh):
    off = N_P // DEC_SEQ
    return pl.pallas_call(
        _mla_sample_kernel,
        out_shape=jax.ShapeDtypeStruct((N_S, H_A * V_A), BF16),
        grid=(DEC_BATCH,),
        in_specs=[pl.BlockSpec((DEC_SEQ, H_A * LANES), lambda b: (off + b, 0)),
                  pl.BlockSpec((None, PAST_LEN, KV_LORA), lambda b: (b, 0, 0)),
                  pl.BlockSpec((None, PAST_LEN, ROPE), lambda b: (b, 0, 0)),
                  pl.BlockSpec((DEC_SEQ, KV_LORA), lambda b: (off + b, 0)),
                  pl.BlockSpec((DEC_SEQ, ROPE), lambda b: (off + b, 0)),
                  _const(wukt.shape), _const(wuvh.shape)],
        out_specs=pl.BlockSpec((DEC_SEQ, H_A * V_A), lambda b: (b, 0)),
        compiler_params=_params(),
        name="mla_sample",
    )(q16, cache_lat, cache_kr, lat, kr, wukt, wuvh)


def _band_heads(q, blocks, bias_refs, valid):
    lane = lax.broadcasted_iota(jnp.int32, (q.shape[0], LANES), 1)
    scores = []
    for hd in range(H_B):
        sl = slice((hd // 2) * LANES, (hd // 2 + 1) * LANES)
        q2 = q[:, sl]
        own = (lane >= DH_B) if hd % 2 else (lane < DH_B)
        qm = jnp.where(own, q2, jnp.zeros_like(q2))
        ss = [lax.dot_general(qm, k[:, sl], NT, preferred_element_type=F32) * BAND_SCALE + b_ref[hd]
              for (k, _), b_ref in zip(blocks, bias_refs)]
        if valid is not None:
            ss = [jnp.where(valid, s, NEG) for s in ss]
        scores.append(ss)
    weights = [_softmax_rows(ss) for ss in scores]
    outs = []
    for hd in range(H_B):
        sl = slice((hd // 2) * LANES, (hd // 2 + 1) * LANES)
        outs.append(functools.reduce(jnp.add, [_dot(w, v[:, sl]) for w, (_, v) in zip(weights[hd], blocks)]))
    pairs = [_pair_select(outs[2 * pr], outs[2 * pr + 1]) for pr in range(H_B // 2)]
    return jnp.concatenate(pairs, axis=1).astype(BF16)


def _band_prompt_kernel(q_ref, k_ref, v_ref, bias_ref, o_ref):
    c = pl.program_id(1)
    start = pl.multiple_of(c * CHUNK, CHUNK)
    k = k_ref[pl.ds(start, BAND), :]
    v = v_ref[pl.ds(start, BAND), :]
    valid = (start - WINDOW_B + lax.broadcasted_iota(jnp.int32, (CHUNK, BAND), 1)) >= 0
    o_ref[...] = _band_heads(q_ref[...], [(k, v)], [bias_ref], valid)


def _band_prompt(qb16, kpad, vpad, bias):
    nc = SEQ // CHUNK
    return pl.pallas_call(
        _band_prompt_kernel,
        out_shape=jax.ShapeDtypeStruct((N_P, HB), BF16),
        grid=(BATCH, nc),
        in_specs=[pl.BlockSpec((CHUNK, HB), lambda b, c: (b * nc + c, 0)),
                  pl.BlockSpec((None, SEQ + WINDOW_B, HB), lambda b, c: (b, 0, 0)),
                  pl.BlockSpec((None, SEQ + WINDOW_B, HB), lambda b, c: (b, 0, 0)),
                  _const(bias.shape)],
        out_specs=pl.BlockSpec((CHUNK, HB), lambda b, c: (b * nc + c, 0)),
        compiler_params=_params(2),
        name="band_prompt",
    )(qb16, kpad, vpad, bias)


def _band_sample_kernel(q_ref, ck_ref, cv_ref, nk16_ref, nv16_ref, nk32_ref, nv32_ref, bias_c_ref, bias_n_ref,
                        o_ref, bk_ref, bv_ref):
    ck = ck_ref[...]
    cv = cv_ref[...]
    blocks = [(ck.astype(BF16), cv.astype(BF16)), (nk16_ref[...], nv16_ref[...])]
    o_ref[...] = _band_heads(q_ref[...], blocks, [bias_c_ref, bias_n_ref], None)
    keep = WINDOW_B - DEC_SEQ
    bk_ref[0:keep, :] = ck[DEC_SEQ:, :]
    bk_ref[keep:, :] = nk32_ref[...]
    bv_ref[0:keep, :] = cv[DEC_SEQ:, :]
    bv_ref[keep:, :] = nv32_ref[...]


def _band_sample(qb16, cache_k, cache_v, kb16, vb16, kb32, vb32, bias_c, bias_n):
    off = N_P // DEC_SEQ
    new = lambda: pl.BlockSpec((DEC_SEQ, HB), lambda b: (off + b, 0))
    cache = lambda: pl.BlockSpec((None, WINDOW_B, HB), lambda b: (b, 0, 0))
    return pl.pallas_call(
        _band_sample_kernel,
        out_shape=[jax.ShapeDtypeStruct((N_S, HB), BF16),
                   jax.ShapeDtypeStruct((DEC_BATCH, WINDOW_B, HB), F32),
                   jax.ShapeDtypeStruct((DEC_BATCH, WINDOW_B, HB), F32)],
        grid=(DEC_BATCH,),
        in_specs=[new(), cache(), cache(), new(), new(), new(), new(),
                  _const(bias_c.shape), _const(bias_n.shape)],
        out_specs=[pl.BlockSpec((DEC_SEQ, HB), lambda b: (b, 0)), cache(), cache()],
        compiler_params=_params(),
        name="band_sample",
    )(qb16, cache_k, cache_v, kb16, vb16, kb32, vb32, bias_c, bias_n)


def _mix_out_kernel(oa_ref, ob_ref, sga_ref, sgb_ref, xp_ref, xs_ref, wa_ref, wb_ref, wout_ref, gffn_ref,
                    x1_ref, h2_ref, h16_ref):
    merged = sga_ref[...] * _dot(oa_ref[...], wa_ref[...]) + sgb_ref[...] * _dot(ob_ref[...], wb_ref[...])
    x1 = _group_rows(xp_ref, xs_ref) + _dot(merged.astype(BF16), wout_ref[...])
    x1_ref[...] = x1
    h2 = _rms(x1, gffn_ref[...])
    h2_ref[...] = h2
    h16_ref[...] = h2.astype(BF16)


def _mix_out(oa, ob, sga, sgb, xp, xs, wa, wb, wout, gffn):
    n = N_TOK
    return pl.pallas_call(
        _mix_out_kernel,
        out_shape=[jax.ShapeDtypeStruct((n, D_MODEL), F32), jax.ShapeDtypeStruct((n, D_MODEL), F32),
                   jax.ShapeDtypeStruct((n, D_MODEL), BF16)],
        grid=(n // TM,),
        in_specs=[_rows(H_A * V_A), _rows(HB), _rows(D_MODEL), _rows(D_MODEL),
                  _rows_prompt(D_MODEL), _rows_sample(D_MODEL),
                  _const(wa.shape), _const(wb.shape), _const(wout.shape), _const(gffn.shape)],
        out_specs=[_rows(D_MODEL)] * 3,
        compiler_params=_params(),
        name="mix_out",
    )(oa, ob, sga, sgb, xp, xs, wa, wb, wout, gffn)


def _top16(s, n_rows):
    row = lax.broadcasted_iota(jnp.int32, (n_rows, TQ), 0).astype(F32)
    vals, ids = [], []
    for _ in range(TOPK):
        m = jnp.max(s, axis=0, keepdims=True)
        i = jnp.min(jnp.where(s == m, row, float(n_rows)), axis=0, keepdims=True)
        vals.append(m)
        ids.append(i)
        s = jnp.where(row == i, -jnp.inf, s)
    return jnp.concatenate(vals, axis=0), jnp.concatenate(ids, axis=0)


def _pair_candidates(f1, f2):
    h = SUBLANES
    blocks = [f1(0, 1, 0, h), f1(0, 1, h, 2 * h)]
    blocks += [f1(a, a + 1, 0, h) for a in range(1, h)]
    blocks += [f2(h, 2 * h, 0, 1)]
    return jnp.concatenate(blocks, axis=0)


def _peer_topk_kernel(h_ref, wqt_ref, keys_ref, idx_ref, g_ref, qt_ref, g_t, id_t):
    qt_ref[...] = lax.dot_general(wqt_ref[...], h_ref[...], NT, preferred_element_type=F32).astype(BF16)
    r8 = lax.broadcasted_iota(jnp.int32, (SUBLANES, TQ), 0).astype(F32)
    flat = lambda a0, a1, b0, b1: (r8 + float(b0)) + float(TOPK * a0)
    flat_t = lambda a0, a1, b0, b1: (r8 + float(a0)) * float(TOPK) + float(b0)
    cflat = _pair_candidates(flat, flat_t)

    def one_head(hd):
        tops = []
        for p in range(2):
            hp = hd * 2 + p
            qs = qt_ref[pl.ds(pl.multiple_of(hp * N_KEYS, N_KEYS), N_KEYS), :]
            tops.append(_top16(_dot(keys_ref[hp], qs), N_KEYS))
        (s1, i1), (s2, i2) = tops
        add = lambda x, y: (lambda a0, a1, b0, b1: x[a0:a1, :] + y[b0:b1, :])
        cand = _pair_candidates(add(s1, s2), add(s1, s2))
        e1 = i1 * float(N_KEYS)
        cidx = _pair_candidates(add(e1, i2), add(e1, i2))
        vals, ids = [], []
        for _ in range(TOPK):
            m = jnp.max(cand, axis=0, keepdims=True)
            c = jnp.min(jnp.where(cand == m, cflat, float(TOPK * TOPK)), axis=0, keepdims=True)
            sel = cflat == c
            vals.append(m)
            ids.append(jnp.sum(jnp.where(sel, cidx, 0.0), axis=0, keepdims=True))
            cand = jnp.where(sel, -jnp.inf, cand)
        best = jnp.concatenate(vals, axis=0)
        e = jnp.exp(best - best[0:1, :])
        g = e / jnp.sum(e, axis=0, keepdims=True)
        base = pl.multiple_of(hd * TOPK, TOPK)
        g_t[pl.ds(base, TOPK), :] = g
        id_t[pl.ds(base, TOPK), :] = jnp.concatenate(ids, axis=0)

    def head_group(grp, carry):
        for e in range(HEADS_PER_TRIP):
            one_head(grp * HEADS_PER_TRIP + e)
        return carry

    lax.fori_loop(0, PEER_HEADS // HEADS_PER_TRIP, head_group, 0)
    g_ref[...] = g_t[...].T
    idx_ref[...] = (id_t[...].T * float(ROW_WORDS)).astype(jnp.int32)


def _peer_topk(h16, wqt, keys):
    n = h16.shape[0]
    nt = n // TQ
    return pl.pallas_call(
        _peer_topk_kernel,
        out_shape=[jax.ShapeDtypeStruct((n, SLOTS), jnp.int32),
                   jax.ShapeDtypeStruct((n, SLOTS), F32)],
        grid=(nt,),
        in_specs=[_rows(D_MODEL, TQ), _const(wqt.shape), _const(keys.shape)],
        out_specs=[_rows(SLOTS, TQ)] * 2,
        scratch_shapes=[pltpu.VMEM((2 * PEER_HEADS * N_KEYS, TQ), BF16),
                        pltpu.VMEM((SLOTS, TQ), F32), pltpu.VMEM((SLOTS, TQ), F32)],
        compiler_params=_params(),
        name="peer_topk",
    )(h16, wqt, keys)


def _pack_table(t):
    b = lax.bitcast_convert_type(t.astype(BF16), jnp.uint16).astype(jnp.uint32)
    packed = (b[:, :HALF] << 16) | b[:, HALF:]
    return packed.reshape(N_EXPERTS * ROW_WORDS, LANES)


def _unpack(w):
    hi = lax.bitcast_convert_type(w & jnp.uint32(0xFFFF0000), F32)
    lo = lax.bitcast_convert_type(w << 16, F32)
    return hi, lo


def _slot_pairs(idx_ref, t):
    out = []
    for grp in range(PAIRS // IDX_GROUP):
        lo = idx_ref.at[0, pl.ds(t * SLOTS + grp * IDX_GROUP, IDX_GROUP)]
        hi = idx_ref.at[0, pl.ds(t * SLOTS + PAIRS + grp * IDX_GROUP, IDX_GROUP)]
        out.extend((lo[k], hi[k]) for k in range(IDX_GROUP))
    return out


def _load_pair(tab, ia, ib):
    wa = tab[pl.ds(pl.multiple_of(ia, ROW_WORDS), ROW_WORDS), :]
    wb = tab[pl.ds(pl.multiple_of(ib, ROW_WORDS), ROW_WORDS), :]
    return _unpack(jnp.concatenate([wa, wb], axis=0))


def _peer_u_kernel(idx_ref, x_ref, g_ref, tab, c_ref, prod_a, prod_b, rbuf, a_t):
    lane = lax.broadcasted_iota(jnp.int32, (SLOTS, LANES), 1)

    @pl.when(pl.program_id(0) == 0)
    def _():
        rbuf[...] = jnp.zeros_like(rbuf)
        a_t[...] = jnp.zeros_like(a_t)

    def reduce(slot, tok):
        col = jnp.sum(rbuf[slot], axis=1, keepdims=True)
        a_t[...] = jnp.where(lane == tok, col, a_t[...])

    def gather(t, slot, prod):
        xt = x_ref[t]
        x_hi = jnp.concatenate([xt[0:ROW_WORDS, :]] * 2, axis=0)
        x_lo = jnp.concatenate([xt[ROW_WORDS:, :]] * 2, axis=0)
        for j, (ia, ib) in enumerate(_slot_pairs(idx_ref, t)):
            hi, lo = _load_pair(tab, ia, ib)
            prod[pl.ds(j, SUBLANES, stride=STRIDE), :] = hi * x_hi + lo * x_lo
        halves = [functools.reduce(jnp.add, [prod[pl.ds((h * ROW_WORDS + s) * STRIDE, PAIRS), :]
                                             for s in range(ROW_WORDS)]) for h in range(2)]
        rbuf[slot] = jnp.concatenate(halves, axis=0)

    def two_tokens(p, carry):
        t0 = 2 * p
        reduce(0, t0 - 2)
        reduce(1, t0 - 1)
        gather(t0, 0, prod_a)
        gather(t0 + 1, 1, prod_b)
        return carry

    lax.fori_loop(0, TB // 2, two_tokens, 0)
    reduce(0, TB - 2)
    reduce(1, TB - 1)
    a = a_t[...].T[0:TB, :]
    gelu = 0.5 * a * (1.0 + lax.erf(a * (2.0 ** -0.5)))
    c_ref[...] = g_ref[...] * gelu


def _peer_u(idx, x, g, tab_u):
    n = x.shape[0]
    nb = n // TB
    return pl.pallas_call(
        _peer_u_kernel,
        out_shape=jax.ShapeDtypeStruct((n, SLOTS), F32),
        grid=(nb,),
        in_specs=[pl.BlockSpec((None, 1, TB * SLOTS), lambda i: (i, 0, 0), memory_space=pltpu.SMEM),
                  pl.BlockSpec((TB, SUBLANES, LANES), lambda i: (i, 0, 0)),
                  pl.BlockSpec((TB, SLOTS), lambda i: (i, 0)),
                  _const(tab_u.shape)],
        out_specs=pl.BlockSpec((TB, SLOTS), lambda i: (i, 0)),
        scratch_shapes=[pltpu.VMEM((SUBLANES * STRIDE, LANES), F32),
                        pltpu.VMEM((SUBLANES * STRIDE, LANES), F32),
                        pltpu.VMEM((2, SLOTS, LANES), F32),
                        pltpu.VMEM((SLOTS, LANES), F32)],
        compiler_params=_params(),
        name="peer_u",
    )(idx.reshape(nb, 1, TB * SLOTS), x.reshape(n, SUBLANES, LANES), g, tab_u)


def _peer_v_kernel(idx_ref, c_ref, tab, y_ref, *crep):
    row = lax.broadcasted_iota(jnp.int32, (SUBLANES, LANES), 0)
    eye = (lax.broadcasted_iota(jnp.int32, (SLOTS, LANES), 0)
           == lax.broadcasted_iota(jnp.int32, (SLOTS, LANES), 1))

    def spread(t, crep):
        col = jnp.sum(jnp.where(eye, c_ref[pl.ds(t, 1), :], 0.0), axis=1, keepdims=True)
        crep[...] = jnp.broadcast_to(col, (SLOTS, LANES))

    def token(t, crep):
        acc_hi = jnp.zeros((SUBLANES, LANES), F32)
        acc_lo = jnp.zeros((SUBLANES, LANES), F32)
        for j, (ia, ib) in enumerate(_slot_pairs(idx_ref, t)):
            hi, lo = _load_pair(tab, ia, ib)
            cm = jnp.where(row < ROW_WORDS, crep[pl.ds(j, 1), :], crep[pl.ds(PAIRS + j, 1), :])
            acc_hi = acc_hi + cm * hi
            acc_lo = acc_lo + cm * lo
        y_ref[t] = jnp.concatenate([acc_hi[0:ROW_WORDS, :] + acc_hi[ROW_WORDS:, :],
                                    acc_lo[0:ROW_WORDS, :] + acc_lo[ROW_WORDS:, :]], axis=0)

    ahead = CREP_RING // 2
    for k in range(ahead):
        spread(k, crep[k])

    def ring_trip(p, carry):
        t0 = CREP_RING * p
        for k in range(CREP_RING):
            spread(jnp.minimum(t0 + k + ahead, TB - 1), crep[(k + ahead) % CREP_RING])
            token(t0 + k, crep[k])
        return carry

    lax.fori_loop(0, TB // CREP_RING, ring_trip, 0)


def _peer_v(idx, c, tab_v):
    n = c.shape[0]
    nb = n // TB
    y = pl.pallas_call(
        _peer_v_kernel,
        out_shape=jax.ShapeDtypeStruct((n, SUBLANES, LANES), F32),
        grid=(nb,),
        in_specs=[pl.BlockSpec((None, 1, TB * SLOTS), lambda i: (i, 0, 0), memory_space=pltpu.SMEM),
                  pl.BlockSpec((TB, SLOTS), lambda i: (i, 0)),
                  _const(tab_v.shape)],
        out_specs=pl.BlockSpec((TB, SUBLANES, LANES), lambda i: (i, 0, 0)),
        scratch_shapes=[pltpu.VMEM((SLOTS, LANES), F32)] * CREP_RING,
        compiler_params=_params(),
        name="peer_v",
    )(idx.reshape(nb, 1, TB * SLOTS), c, tab_v)
    return y.reshape(n, D_MODEL)


def _sc_peer_v(idx, c, table):
    n = idx.shape[0]
    per_worker = n // SC_WORKERS
    batches = per_worker // SC_TOKENS
    groups = SC_TOKENS * SLOTS // SC_ROWS
    chunks = D_MODEL // SC_LANES
    mesh = plsc.VectorSubcoreMesh(core_axis_name="c", subcore_axis_name="s")

    def body(idx_hbm, c_hbm, tab_hbm, y_hbm, idx_v, c_v, rows_a, rows_b, y_v, sem_a, sem_b):
        wid = lax.axis_index("s") * 2 + lax.axis_index("c")
        zero = jnp.zeros((SC_LANES,), F32)

        def gather(g, rows, sem):
            return pltpu.make_async_copy(tab_hbm.at[idx_v.at[pl.ds(g * SC_ROWS, SC_ROWS)]], rows, sem)

        def accumulate(g, rows):
            tok = g // (SLOTS // SC_ROWS)

            @pl.loop(0, SC_ROWS)
            def _(r):
                w = plsc.load_gather(c_v, [jnp.full((SC_LANES,), g * SC_ROWS + r, jnp.int32)])
                for ch in range(chunks):
                    sl = pl.ds(ch * SC_LANES, SC_LANES)
                    plsc.addupdate(y_v.at[tok, sl], w * rows[r, sl])

        @pl.loop(0, batches)
        def _(b):
            t0 = wid * per_worker + b * SC_TOKENS
            pltpu.sync_copy(idx_hbm.at[pl.ds(t0 * SLOTS, SC_TOKENS * SLOTS)], idx_v)
            pltpu.sync_copy(c_hbm.at[pl.ds(t0 * SLOTS, SC_TOKENS * SLOTS)], c_v)
            for t in range(SC_TOKENS):
                for ch in range(chunks):
                    y_v[t, pl.ds(ch * SC_LANES, SC_LANES)] = zero
            gather(0, rows_a, sem_a).start()

            @pl.loop(0, groups // 2)
            def _(h):
                g = 2 * h
                gather(g + 1, rows_b, sem_b).start()
                gather(g, rows_a, sem_a).wait()
                accumulate(g, rows_a)

                @pl.when(h + 1 < groups // 2)
                def _():
                    gather(g + 2, rows_a, sem_a).start()
                gather(g + 1, rows_b, sem_b).wait()
                accumulate(g + 1, rows_b)

            pltpu.sync_copy(y_v, y_hbm.at[pl.ds(t0, SC_TOKENS)])

    return pl.kernel(
        body,
        out_type=jax.ShapeDtypeStruct((n, D_MODEL), F32),
        mesh=mesh,
        scratch_types=[pltpu.VMEM((SC_TOKENS * SLOTS,), jnp.int32),
                       pltpu.VMEM((SC_TOKENS * SLOTS,), F32),
                       pltpu.VMEM((SC_ROWS, D_MODEL), F32),
                       pltpu.VMEM((SC_ROWS, D_MODEL), F32),
                       pltpu.VMEM((SC_TOKENS, D_MODEL), F32),
                       pltpu.SemaphoreType.DMA,
                       pltpu.SemaphoreType.DMA],
        compiler_params=pltpu.CompilerParams(needs_layout_passes=False),
        name="sc_peer_v",
    )(idx.reshape(n * SLOTS), c.reshape(n * SLOTS), table)


def _final_kernel(x1_ref, y_ref, pp_ref, ps_ref, gple_ref, wg_ref, wp_ref, gfin_ref, op_ref, os_ref):
    x2 = x1_ref[...] + y_ref[...]
    gate = jax.nn.sigmoid(_dot(_rms(x2, gple_ref[...]).astype(BF16), wg_ref[...]))
    x3 = x2 + gate * _dot(_group_rows(pp_ref, ps_ref).astype(BF16), wp_ref[...])
    out = _rms(x3, gfin_ref[...])
    is_prompt = pl.program_id(0) < P_TILES

    @pl.when(is_prompt)
    def _():
        op_ref[...] = out

    @pl.when(jnp.logical_not(is_prompt))
    def _():
        os_ref[...] = out


def _final(x1, y, pp, ps, gple, wg, wp, gfin):
    return pl.pallas_call(
        _final_kernel,
        out_shape=[jax.ShapeDtypeStruct((N_P, D_MODEL), F32), jax.ShapeDtypeStruct((N_S, D_MODEL), F32)],
        grid=(N_TOK // TM,),
        in_specs=[_rows(D_MODEL), _rows(D_MODEL), _rows_prompt(PLE_DIM), _rows_sample(PLE_DIM),
                  _const(gple.shape), _const(wg.shape), _const(wp.shape), _const(gfin.shape)],
        out_specs=[_rows_prompt(D_MODEL), _rows_sample(D_MODEL)],
        compiler_params=_params(),
        name="ple_final",
    )(x1, y, pp, ps, gple, wg, wp, gfin)


def _rope_tables():
    half = ROPE // 2
    freqs = ROPE_THETA ** (-np.arange(half, dtype=np.float32) / half)
    pos = np.concatenate([np.arange(SEQ), np.tile(PAST_LEN + np.arange(DEC_SEQ), TM // DEC_SEQ)])
    ang = jnp.asarray(pos, F32)[:, None] * jnp.asarray(freqs, F32)[None, :]
    cos, sin = jnp.cos(ang), jnp.sin(ang)
    n = pos.shape[0]
    pad = jnp.zeros((n, LANES - NOPE - ROPE), F32)
    cos_t = jnp.concatenate([jnp.ones((n, NOPE), F32), cos, cos, pad], axis=1)
    sin_t = jnp.concatenate([jnp.zeros((n, NOPE), F32), sin, sin, pad], axis=1)
    return cos_t, sin_t


def _rot_cols(w):
    half = ROPE // 2
    return jnp.concatenate([-w[..., half:], w[..., :half]], axis=-1)


def _rel_bias(table, n_q, n_k):
    diag = np.arange(-(n_q - 1), n_k)
    line = table[:, np.clip(WINDOW_B - diag, -REL_CLIP, REL_CLIP) + REL_CLIP]
    return jnp.stack([line[:, n_q - 1 - i:n_q - 1 - i + n_k] for i in range(n_q)], axis=1)


def kernel(x_prompt, x_sample, cache_latent, cache_krope, cache_band_k, cache_band_v, p_prompt, p_sample, g_mix, w_in, g_q_lora, g_kv_lora, w_uq, w_uk, w_uv, rel_bias, w_a_proj, w_b_proj, w_out, g_ffn, w_query, sub_keys, expert_u, expert_v, g_ple, w_ple_gate, w_ple_proj, g_final):
    w = w_in[0]
    o_kr = Q_LORA + KV_LORA
    o_qb = o_kr + ROPE
    o_ga = o_qb + 3 * HB
    w_kr = w[:, o_kr:o_qb]
    z64 = jnp.zeros((D_MODEL, NOPE), F32)
    z32 = jnp.zeros((D_MODEL, LANES - NOPE - ROPE), F32)
    w1 = jnp.concatenate([w[:, :o_kr], z64, w_kr, z32, z64, _rot_cols(w_kr), z32], axis=1).astype(BF16)
    w2 = w[:, o_qb:o_ga].astype(BF16)
    w3 = w[:, o_ga:].astype(BF16)
    wq = w_uq[0].reshape(Q_LORA, H_A, NOPE + ROPE)
    zq64 = jnp.zeros((Q_LORA, H_A, NOPE), F32)
    zq32 = jnp.zeros((Q_LORA, H_A, LANES - NOPE - ROPE), F32)
    wuq1 = jnp.concatenate([wq, zq32], axis=-1).reshape(Q_LORA, H_A * LANES).astype(BF16)
    wuq2 = jnp.concatenate([zq64, _rot_cols(wq[..., NOPE:]), zq32], axis=-1).reshape(Q_LORA, H_A * LANES).astype(BF16)
    wukp = jnp.concatenate([w_uk[0], jnp.zeros((KV_LORA, H_A, LANES - NOPE), F32)], axis=-1)
    wukp = wukp.reshape(KV_LORA, H_A * LANES).astype(BF16)
    wuv = w_uv[0].reshape(KV_LORA, H_A * V_A).astype(BF16)
    wukt = jnp.transpose(w_uk[0], (1, 2, 0)).astype(BF16)
    wuvh = jnp.transpose(w_uv[0], (1, 0, 2)).astype(BF16)
    cos_t, sin_t = _rope_tables()
    row = lambda g: g.reshape(1, -1)

    xp = x_prompt.reshape(N_P, D_MODEL)
    xs = x_sample.reshape(N_S, D_MODEL)

    (lat, kr, q16, k16, v16, qb16, kb16, vb16, kb32, vb32, sga, sgb) = _inproj(
        xp, xs, row(g_mix[0]), w1, w2, w3, row(g_q_lora[0]), row(g_kv_lora[0]), wuq1, wuq2, wukp, wuv, cos_t, sin_t)

    oa_p = _mla_prompt(q16, k16, v16)
    oa_s = _mla_sample(q16, cache_latent[0], cache_krope[0], lat, kr, wukt, wuvh)

    pad = ((0, 0), (WINDOW_B, 0), (0, 0))
    kpad = jnp.pad(kb16[:N_P].reshape(BATCH, SEQ, HB), pad)
    vpad = jnp.pad(vb16[:N_P].reshape(BATCH, SEQ, HB), pad)
    tab = rel_bias[0]
    bias_p = _rel_bias(tab, CHUNK, BAND)
    bias_s = _rel_bias(tab, DEC_SEQ, WINDOW_B + DEC_SEQ)
    ob_p = _band_prompt(qb16, kpad, vpad, bias_p)
    ob_s, bk_s, bv_s = _band_sample(qb16, cache_band_k[0].reshape(DEC_BATCH, WINDOW_B, HB),
                                    cache_band_v[0].reshape(DEC_BATCH, WINDOW_B, HB),
                                    kb16, vb16, kb32, vb32, bias_s[:, :, :WINDOW_B], bias_s[:, :, WINDOW_B:])

    oa = jnp.concatenate([oa_p, oa_s], axis=0)
    ob = jnp.concatenate([ob_p, ob_s], axis=0)
    x1, h2, h16 = _mix_out(oa, ob, sga, sgb, xp, xs, w_a_proj[0].astype(BF16), w_b_proj[0].astype(BF16),
                           w_out[0].astype(BF16), row(g_ffn[0]))
    wqt = jnp.transpose(w_query[0]).astype(BF16)
    keys = sub_keys[0].reshape(2 * PEER_HEADS, N_KEYS, N_KEYS).astype(BF16)
    idx, gw = _peer_topk(h16, wqt, keys)

    tab_u = _pack_table(expert_u[0])
    tab_v = _pack_table(expert_v[0])
    if N_SC:
        c_sc = _peer_u(idx[:N_SC], h2[:N_SC], gw[:N_SC], tab_u)
        y_sc = _sc_peer_v(idx[:N_SC] // ROW_WORDS, c_sc, expert_v[0])
        c_tc = _peer_u(idx[N_SC:], h2[N_SC:], gw[N_SC:], tab_u)
        y = jnp.concatenate([y_sc, _peer_v(idx[N_SC:], c_tc, tab_v)], axis=0)
    else:
        y = _peer_v(idx, _peer_u(idx, h2, gw, tab_u), tab_v)

    out_p, out_s = _final(x1, y, p_prompt.reshape(N_P, PLE_DIM), p_sample.reshape(N_S, PLE_DIM), row(g_ple[0]),
                          w_ple_gate[0].astype(BF16), w_ple_proj[0].astype(BF16), row(g_final))

    y_prompt = out_p.reshape(BATCH, SEQ, D_MODEL)
    y_sample = out_s.reshape(DEC_BATCH, DEC_SEQ, D_MODEL)
    lat_p = lat[:N_P].reshape(1, BATCH, SEQ, KV_LORA)
    kr_p = kr[:N_P].reshape(1, BATCH, SEQ, ROPE)
    bk_p = kb32[:N_P].reshape(BATCH, SEQ, H_B, DH_B)[None, :, SEQ - WINDOW_B:]
    bv_p = vb32[:N_P].reshape(BATCH, SEQ, H_B, DH_B)[None, :, SEQ - WINDOW_B:]
    lat_s = lat[N_P:].reshape(1, DEC_BATCH, DEC_SEQ, KV_LORA)
    kr_s = kr[N_P:].reshape(1, DEC_BATCH, DEC_SEQ, ROPE)
    bk_s = bk_s.reshape(1, DEC_BATCH, WINDOW_B, H_B, DH_B)
    bv_s = bv_s.reshape(1, DEC_BATCH, WINDOW_B, H_B, DH_B)
    return (y_prompt, y_sample, lat_p, kr_p, bk_p, bv_p, lat_s, kr_s, bk_s, bv_s)
```

```python
import functools
import jax
import jax.numpy as jnp
import numpy as np
from jax import lax
from jax.experimental import pallas as pl
from jax.experimental.pallas import tpu as pltpu
from jax.experimental.pallas import tpu_sc as plsc

D_MODEL = 1024
BATCH = 16
SEQ = 2048
DEC_BATCH = 32
DEC_SEQ = 32
PAST_LEN = 4096
CHUNK = 64
EPS = 1e-6
H_A = 8
Q_LORA = 384
KV_LORA = 256
NOPE = 64
ROPE = 32
V_A = 64
ROPE_THETA = 10000.0
MLA_SCALE = (NOPE + ROPE) ** -0.5
H_B = 8
DH_B = 64
BAND_PREV = 8
BAND = (BAND_PREV + 1) * CHUNK
WINDOW_B = BAND_PREV * CHUNK
REL_CLIP = 256
BAND_SCALE = DH_B ** -0.5
PEER_HEADS = 8
N_KEYS = 128
N_EXPERTS = N_KEYS * N_KEYS
TOPK = 16
PLE_DIM = 256

N_P = BATCH * SEQ
N_S = DEC_BATCH * DEC_SEQ
N_TOK = N_P + N_S
HB = H_B * DH_B
HALF = D_MODEL // 2
SLOTS = PEER_HEADS * TOPK
LANES = 128
SUBLANES = 8
TM = 256
TQ = 128
KV_STEP = 512
TB = 64
CREP_RING = 4
ROW_WORDS = HALF // LANES
PAIRS = SLOTS // 2
STRIDE = 72
HEADS_PER_TRIP = 8
IDX_GROUP = 8
VMEM_LIMIT = 56 * 1024 * 1024
SC_WORKERS = 32
SC_LANES = 16
SC_TOKENS = 8
SC_ROWS = 32
SC_ACC = 8
N_SC = 16384
NEG = -1e30
BF16 = jnp.bfloat16
F32 = jnp.float32
NT = (((1,), (1,)), ((), ()))


def _params(n_axes=1):
    return pltpu.CompilerParams(dimension_semantics=("arbitrary",) * n_axes,
                                vmem_limit_bytes=VMEM_LIMIT)


def _const(shape):
    nd = len(shape)
    return pl.BlockSpec(shape, lambda *_: (0,) * nd, pipeline_mode=pl.Buffered(1))


def _rows(width, tile=TM):
    return pl.BlockSpec((tile, width), lambda i: (i, 0))


P_TILES = N_P // TM


def _rows_prompt(width):
    return pl.BlockSpec((TM, width), lambda i: (jnp.minimum(i, P_TILES - 1), 0))


def _rows_sample(width):
    return pl.BlockSpec((TM, width), lambda i: (jnp.maximum(i - P_TILES, 0), 0))


def _group_rows(p_ref, s_ref):
    return jnp.where(pl.program_id(0) < P_TILES, p_ref[...], s_ref[...])


def _rms(x, g):
    return x * lax.rsqrt(jnp.mean(x * x, axis=-1, keepdims=True) + EPS) * g


def _dot(a, b):
    return jnp.dot(a, b, preferred_element_type=F32)


def _softmax_rows(s_list):
    m = functools.reduce(jnp.maximum, [jnp.max(s, axis=-1, keepdims=True) for s in s_list])
    p_list = [jnp.exp(s - m) for s in s_list]
    inv = 1.0 / functools.reduce(jnp.add, [jnp.sum(p, axis=-1, keepdims=True) for p in p_list])
    return [(p * inv).astype(BF16) for p in p_list]


def _inproj_kernel(xp_ref, xs_ref, gmix_ref, w1_ref, w2_ref, w3_ref, gq_ref, gkv_ref, wuq1_ref, wuq2_ref,
                   wukp_ref, wuv_ref, cos_ref, sin_ref,
                   lat_ref, kr_ref, q16_ref, k16_ref, v16_ref, qb16_ref, kb16_ref, vb16_ref,
                   kb32_ref, vb32_ref, sga_ref, sgb_ref):
    h = _rms(_group_rows(xp_ref, xs_ref), gmix_ref[...]).astype(BF16)
    cos = cos_ref[...]
    sin = sin_ref[...]
    z1 = _dot(h, w1_ref[...])
    k128 = z1[:, 640:768] * cos + z1[:, 768:896] * sin
    kr_ref[...] = k128[:, NOPE:NOPE + ROPE]
    cqn = _rms(z1[:, :Q_LORA], gq_ref[...]).astype(BF16)
    cos8 = jnp.concatenate([cos] * H_A, axis=1)
    sin8 = jnp.concatenate([sin] * H_A, axis=1)
    q = _dot(cqn, wuq1_ref[...]) * cos8 + _dot(cqn, wuq2_ref[...]) * sin8
    q16_ref[...] = q.astype(BF16)
    ckvn = _rms(z1[:, Q_LORA:Q_LORA + KV_LORA], gkv_ref[...])
    lat_ref[...] = ckvn
    ckvn16 = ckvn.astype(BF16)
    kk = _dot(ckvn16, wukp_ref[...]) + jnp.concatenate([k128] * H_A, axis=1)
    k16_ref[...] = kk.astype(BF16)
    v16_ref[...] = _dot(ckvn16, wuv_ref[...]).astype(BF16)
    z2 = _dot(h, w2_ref[...])
    qb16_ref[...] = z2[:, :HB].astype(BF16)
    kb = z2[:, HB:2 * HB]
    vb = z2[:, 2 * HB:]
    kb32_ref[...] = kb
    vb32_ref[...] = vb
    kb16_ref[...] = kb.astype(BF16)
    vb16_ref[...] = vb.astype(BF16)
    z3 = _dot(h, w3_ref[...])
    sga_ref[...] = jax.nn.sigmoid(z3[:, :D_MODEL])
    sgb_ref[...] = jax.nn.sigmoid(z3[:, D_MODEL:])


def _rope_rows():
    per_seq = SEQ // TM
    return pl.BlockSpec((TM, LANES), lambda i: (jnp.where(i < N_P // TM, i % per_seq, per_seq), 0))


def _inproj(xp, xs, gmix, w1, w2, w3, gq, gkv, wuq1, wuq2, wukp, wuv, cos, sin):
    n = N_TOK
    widths = [(KV_LORA, F32), (ROPE, F32), (H_A * LANES, BF16), (H_A * LANES, BF16), (H_A * V_A, BF16),
              (HB, BF16), (HB, BF16), (HB, BF16), (HB, F32), (HB, F32), (D_MODEL, F32), (D_MODEL, F32)]
    return pl.pallas_call(
        _inproj_kernel,
        out_shape=[jax.ShapeDtypeStruct((n, w), d) for w, d in widths],
        grid=(n // TM,),
        in_specs=[_rows_prompt(D_MODEL), _rows_sample(D_MODEL),
                  _const(gmix.shape), _const(w1.shape), _const(w2.shape), _const(w3.shape),
                  _const(gq.shape), _const(gkv.shape), _const(wuq1.shape), _const(wuq2.shape),
                  _const(wukp.shape), _const(wuv.shape), _rope_rows(), _rope_rows()],
        out_specs=[_rows(w) for w, _ in widths],
        compiler_params=_params(),
        name="inproj",
    )(xp, xs, gmix, w1, w2, w3, gq, gkv, wuq1, wuq2, wukp, wuv, cos, sin)


def _pair_select(o_even, o_odd):
    lane = lax.broadcasted_iota(jnp.int32, o_even.shape, 1)
    return jnp.where(lane < V_A, o_even, o_odd)


def _mla_prompt_tile(q_ref, k_ref, v_ref, o_ref, i, nk):
    q_chunk = (i * TQ + lax.broadcasted_iota(jnp.int32, (TQ, nk), 0)) // CHUNK
    k_chunk = lax.broadcasted_iota(jnp.int32, (TQ, nk), 1) // CHUNK
    mask = k_chunk <= q_chunk
    scores = []
    for hd in range(H_A):
        qh = q_ref[:, hd * LANES:(hd + 1) * LANES]
        kh = k_ref[0:nk, hd * LANES:(hd + 1) * LANES]
        s = lax.dot_general(qh, kh, NT, preferred_element_type=F32) * MLA_SCALE
        scores.append(jnp.where(mask, s, NEG))
    weights = [_softmax_rows([s])[0] for s in scores]
    outs = [_dot(weights[hd], v_ref[0:nk, (hd // 2) * LANES:(hd // 2 + 1) * LANES]) for hd in range(H_A)]
    pairs = [_pair_select(outs[2 * pr], outs[2 * pr + 1]) for pr in range(H_A // 2)]
    o_ref[...] = jnp.concatenate(pairs, axis=1).astype(BF16)


def _mla_prompt_kernel(q_ref, k_ref, v_ref, o_ref):
    i = pl.program_id(1)
    tiles_per_step = KV_STEP // TQ
    for grp in range(SEQ // KV_STEP):
        @pl.when(i // tiles_per_step == grp)
        def _():
            _mla_prompt_tile(q_ref, k_ref, v_ref, o_ref, i, (grp + 1) * KV_STEP)


def _mla_prompt(q16, k16, v16):
    nq = SEQ // TQ
    return pl.pallas_call(
        _mla_prompt_kernel,
        out_shape=jax.ShapeDtypeStruct((N_P, H_A * V_A), BF16),
        grid=(BATCH, nq),
        in_specs=[pl.BlockSpec((TQ, H_A * LANES), lambda b, i: (b * nq + i, 0)),
                  pl.BlockSpec((SEQ, H_A * LANES), lambda b, i: (b, 0)),
                  pl.BlockSpec((SEQ, H_A * V_A), lambda b, i: (b, 0))],
        out_specs=pl.BlockSpec((TQ, H_A * V_A), lambda b, i: (b * nq + i, 0)),
        compiler_params=_params(2),
        name="mla_prompt",
    )(q16, k16, v16)


def _mla_sample_kernel(q_ref, clat_ref, ckr_ref, nlat_ref, nkr_ref, wukt_ref, wuvh_ref, o_ref):
    q = q_ref[...].astype(F32)
    qlat, qrope = [], []
    for hd in range(H_A):
        qn = q[:, hd * LANES:hd * LANES + NOPE].astype(BF16)
        qlat.append(_dot(qn, wukt_ref[hd]).astype(BF16))
        qrope.append(q[:, hd * LANES + NOPE:hd * LANES + NOPE + ROPE].astype(BF16))
    qlat = jnp.concatenate(qlat, axis=0)
    qrope = jnp.concatenate(qrope, axis=0)
    clat = clat_ref[...].astype(BF16)
    ckr = ckr_ref[...].astype(BF16)
    nlat = nlat_ref[...].astype(BF16)
    nkr = nkr_ref[...].astype(BF16)
    s_c = (lax.dot_general(qlat, clat, NT, preferred_element_type=F32)
           + lax.dot_general(qrope, ckr, NT, preferred_element_type=F32)) * MLA_SCALE
    s_n = (lax.dot_general(qlat, nlat, NT, preferred_element_type=F32)
           + lax.dot_general(qrope, nkr, NT, preferred_element_type=F32)) * MLA_SCALE
    w_c, w_n = _softmax_rows([s_c, s_n])
    olat = (_dot(w_c, clat) + _dot(w_n, nlat)).astype(BF16)
    outs = [_dot(olat[hd * DEC_SEQ:(hd + 1) * DEC_SEQ, :], wuvh_ref[hd]) for hd in range(H_A)]
    o_ref[...] = jnp.concatenate(outs, axis=1).astype(BF16)


def _mla_sample(q16, cache_lat, cache_kr, lat, kr, wukt, wuvh):
    off = N_P // DEC_SEQ
    return pl.pallas_call(
        _mla_sample_kernel,
        out_shape=jax.ShapeDtypeStruct((N_S, H_A * V_A), BF16),
        grid=(DEC_BATCH,),
        in_specs=[pl.BlockSpec((DEC_SEQ, H_A * LANES), lambda b: (off + b, 0)),
                  pl.BlockSpec((None, PAST_LEN, KV_LORA), lambda b: (b, 0, 0)),
                  pl.BlockSpec((None, PAST_LEN, ROPE), lambda b: (b, 0, 0)),
                  pl.BlockSpec((DEC_SEQ, KV_LORA), lambda b: (off + b, 0)),
                  pl.BlockSpec((DEC_SEQ, ROPE), lambda b: (off + b, 0)),
                  _const(wukt.shape), _const(wuvh.shape)],
        out_specs=pl.BlockSpec((DEC_SEQ, H_A * V_A), lambda b: (b, 0)),
        compiler_params=_params(),
        name="mla_sample",
    )(q16, cache_lat, cache_kr, lat, kr, wukt, wuvh)


def _band_heads(q, blocks, bias_refs, valid):
    lane = lax.broadcasted_iota(jnp.int32, (q.shape[0], LANES), 1)
    scores = []
    for hd in range(H_B):
        sl = slice((hd // 2) * LANES, (hd // 2 + 1) * LANES)
        q2 = q[:, sl]
        own = (lane >= DH_B) if hd % 2 else (lane < DH_B)
        qm = jnp.where(own, q2, jnp.zeros_like(q2))
        ss = [lax.dot_general(qm, k[:, sl], NT, preferred_element_type=F32) * BAND_SCALE + b_ref[hd]
              for (k, _), b_ref in zip(blocks, bias_refs)]
        if valid is not None:
            ss = [jnp.where(valid, s, NEG) for s in ss]
        scores.append(ss)
    weights = [_softmax_rows(ss) for ss in scores]
    outs = []
    for hd in range(H_B):
        sl = slice((hd // 2) * LANES, (hd // 2 + 1) * LANES)
        outs.append(functools.reduce(jnp.add, [_dot(w, v[:, sl]) for w, (_, v) in zip(weights[hd], blocks)]))
    pairs = [_pair_select(outs[2 * pr], outs[2 * pr + 1]) for pr in range(H_B // 2)]
    return jnp.concatenate(pairs, axis=1).astype(BF16)


def _band_prompt_kernel(q_ref, k_ref, v_ref, bias_ref, o_ref):
    c = pl.program_id(1)
    start = pl.multiple_of(c * CHUNK, CHUNK)
    k = k_ref[pl.ds(start, BAND), :]
    v = v_ref[pl.ds(start, BAND), :]
    valid = (start - WINDOW_B + lax.broadcasted_iota(jnp.int32, (CHUNK, BAND), 1)) >= 0
    o_ref[...] = _band_heads(q_ref[...], [(k, v)], [bias_ref], valid)


def _band_prompt(qb16, kpad, vpad, bias):
    nc = SEQ // CHUNK
    return pl.pallas_call(
        _band_prompt_kernel,
        out_shape=jax.ShapeDtypeStruct((N_P, HB), BF16),
        grid=(BATCH, nc),
        in_specs=[pl.BlockSpec((CHUNK, HB), lambda b, c: (b * nc + c, 0)),
                  pl.BlockSpec((None, SEQ + WINDOW_B, HB), lambda b, c: (b, 0, 0)),
                  pl.BlockSpec((None, SEQ + WINDOW_B, HB), lambda b, c: (b, 0, 0)),
                  _const(bias.shape)],
        out_specs=pl.BlockSpec((CHUNK, HB), lambda b, c: (b * nc + c, 0)),
        compiler_params=_params(2),
        name="band_prompt",
    )(qb16, kpad, vpad, bias)


def _band_sample_kernel(q_ref, ck_ref, cv_ref, nk16_ref, nv16_ref, nk32_ref, nv32_ref, bias_c_ref, bias_n_ref,
                        o_ref, bk_ref, bv_ref):
    ck = ck_ref[...]
    cv = cv_ref[...]
    blocks = [(ck.astype(BF16), cv.astype(BF16)), (nk16_ref[...], nv16_ref[...])]
    o_ref[...] = _band_heads(q_ref[...], blocks, [bias_c_ref, bias_n_ref], None)
    keep = WINDOW_B - DEC_SEQ
    bk_ref[0:keep, :] = ck[DEC_SEQ:, :]
    bk_ref[keep:, :] = nk32_ref[...]
    bv_ref[0:keep, :] = cv[DEC_SEQ:, :]
    bv_ref[keep:, :] = nv32_ref[...]


def _band_sample(qb16, cache_k, cache_v, kb16, vb16, kb32, vb32, bias_c, bias_n):
    off = N_P // DEC_SEQ
    new = lambda: pl.BlockSpec((DEC_SEQ, HB), lambda b: (off + b, 0))
    cache = lambda: pl.BlockSpec((None, WINDOW_B, HB), lambda b: (b, 0, 0))
    return pl.pallas_call(
        _band_sample_kernel,
        out_shape=[jax.ShapeDtypeStruct((N_S, HB), BF16),
                   jax.ShapeDtypeStruct((DEC_BATCH, WINDOW_B, HB), F32),
                   jax.ShapeDtypeStruct((DEC_BATCH, WINDOW_B, HB), F32)],
        grid=(DEC_BATCH,),
        in_specs=[new(), cache(), cache(), new(), new(), new(), new(),
                  _const(bias_c.shape), _const(bias_n.shape)],
        out_specs=[pl.BlockSpec((DEC_SEQ, HB), lambda b: (b, 0)), cache(), cache()],
        compiler_params=_params(),
        name="band_sample",
    )(qb16, cache_k, cache_v, kb16, vb16, kb32, vb32, bias_c, bias_n)


def _mix_out_kernel(oa_ref, ob_ref, sga_ref, sgb_ref, xp_ref, xs_ref, wa_ref, wb_ref, wout_ref, gffn_ref,
                    x1_ref, h2_ref, h16_ref):
    merged = sga_ref[...] * _dot(oa_ref[...], wa_ref[...]) + sgb_ref[...] * _dot(ob_ref[...], wb_ref[...])
    x1 = _group_rows(xp_ref, xs_ref) + _dot(merged.astype(BF16), wout_ref[...])
    x1_ref[...] = x1
    h2 = _rms(x1, gffn_ref[...])
    h2_ref[...] = h2
    h16_ref[...] = h2.astype(BF16)


def _mix_out(oa, ob, sga, sgb, xp, xs, wa, wb, wout, gffn):
    n = N_TOK
    return pl.pallas_call(
        _mix_out_kernel,
        out_shape=[jax.ShapeDtypeStruct((n, D_MODEL), F32), jax.ShapeDtypeStruct((n, D_MODEL), F32),
                   jax.ShapeDtypeStruct((n, D_MODEL), BF16)],
        grid=(n // TM,),
        in_specs=[_rows(H_A * V_A), _rows(HB), _rows(D_MODEL), _rows(D_MODEL),
                  _rows_prompt(D_MODEL), _rows_sample(D_MODEL),
                  _const(wa.shape), _const(wb.shape), _const(wout.shape), _const(gffn.shape)],
        out_specs=[_rows(D_MODEL)] * 3,
        compiler_params=_params(),
        name="mix_out",
    )(oa, ob, sga, sgb, xp, xs, wa, wb, wout, gffn)


def _top16(s, n_rows):
    row = lax.broadcasted_iota(jnp.int32, (n_rows, TQ), 0).astype(F32)
    vals, ids = [], []
    for _ in range(TOPK):
        m = jnp.max(s, axis=0, keepdims=True)
        i = jnp.min(jnp.where(s == m, row, float(n_rows)), axis=0, keepdims=True)
        vals.append(m)
        ids.append(i)
        s = jnp.where(row == i, -jnp.inf, s)
    return jnp.concatenate(vals, axis=0), jnp.concatenate(ids, axis=0)


def _pair_candidates(f1, f2):
    h = SUBLANES
    blocks = [f1(0, 1, 0, h), f1(0, 1, h, 2 * h)]
    blocks += [f1(a, a + 1, 0, h) for a in range(1, h)]
    blocks += [f2(h, 2 * h, 0, 1)]
    return jnp.concatenate(blocks, axis=0)


def _peer_topk_kernel(h_ref, wqt_ref, keys_ref, idx_ref, g_ref, qt_ref, g_t, id_t):
    qt_ref[...] = lax.dot_general(wqt_ref[...], h_ref[...], NT, preferred_element_type=F32).astype(BF16)
    r8 = lax.broadcasted_iota(jnp.int32, (SUBLANES, TQ), 0).astype(F32)
    flat = lambda a0, a1, b0, b1: (r8 + float(b0)) + float(TOPK * a0)
    flat_t = lambda a0, a1, b0, b1: (r8 + float(a0)) * float(TOPK) + float(b0)
    cflat = _pair_candidates(flat, flat_t)

    def one_head(hd):
        tops = []
        for p in range(2):
            hp = hd * 2 + p
            qs = qt_ref[pl.ds(pl.multiple_of(hp * N_KEYS, N_KEYS), N_KEYS), :]
            tops.append(_top16(_dot(keys_ref[hp], qs), N_KEYS))
        (s1, i1), (s2, i2) = tops
        add = lambda x, y: (lambda a0, a1, b0, b1: x[a0:a1, :] + y[b0:b1, :])
        cand = _pair_candidates(add(s1, s2), add(s1, s2))
        e1 = i1 * float(N_KEYS)
        cidx = _pair_candidates(add(e1, i2), add(e1, i2))
        vals, ids = [], []
        for _ in range(TOPK):
            m = jnp.max(cand, axis=0, keepdims=True)
            c = jnp.min(jnp.where(cand == m, cflat, float(TOPK * TOPK)), axis=0, keepdims=True)
            sel = cflat == c
            vals.append(m)
            ids.append(jnp.sum(jnp.where(sel, cidx, 0.0), axis=0, keepdims=True))
            cand = jnp.where(sel, -jnp.inf, cand)
        best = jnp.concatenate(vals, axis=0)
        e = jnp.exp(best - best[0:1, :])
        g = e / jnp.sum(e, axis=0, keepdims=True)
        base = pl.multiple_of(hd * TOPK, TOPK)
        g_t[pl.ds(base, TOPK), :] = g
        id_t[pl.ds(base, TOPK), :] = jnp.concatenate(ids, axis=0)

    def head_group(grp, carry):
        for e in range(HEADS_PER_TRIP):
            one_head(grp * HEADS_PER_TRIP + e)
        return carry

    lax.fori_loop(0, PEER_HEADS // HEADS_PER_TRIP, head_group, 0)
    g_ref[...] = g_t[...].T
    idx_ref[...] = (id_t[...].T * float(ROW_WORDS)).astype(jnp.int32)


def _peer_topk(h16, wqt, keys):
    n = h16.shape[0]
    nt = n // TQ
    return pl.pallas_call(
        _peer_topk_kernel,
        out_shape=[jax.ShapeDtypeStruct((n, SLOTS), jnp.int32),
                   jax.ShapeDtypeStruct((n, SLOTS), F32)],
        grid=(nt,),
        in_specs=[_rows(D_MODEL, TQ), _const(wqt.shape), _const(keys.shape)],
        out_specs=[_rows(SLOTS, TQ)] * 2,
        scratch_shapes=[pltpu.VMEM((2 * PEER_HEADS * N_KEYS, TQ), BF16),
                        pltpu.VMEM((SLOTS, TQ), F32), pltpu.VMEM((SLOTS, TQ), F32)],
        compiler_params=_params(),
        name="peer_topk",
    )(h16, wqt, keys)


def _pack_table(t):
    b = lax.bitcast_convert_type(t.astype(BF16), jnp.uint16).astype(jnp.uint32)
    packed = (b[:, :HALF] << 16) | b[:, HALF:]
    return packed.reshape(N_EXPERTS * ROW_WORDS, LANES)


def _unpack(w):
    hi = lax.bitcast_convert_type(w & jnp.uint32(0xFFFF0000), F32)
    lo = lax.bitcast_convert_type(w << 16, F32)
    return hi, lo


def _slot_pairs(idx_ref, t):
    out = []
    for grp in range(PAIRS // IDX_GROUP):
        lo = idx_ref.at[0, pl.ds(t * SLOTS + grp * IDX_GROUP, IDX_GROUP)]
        hi = idx_ref.at[0, pl.ds(t * SLOTS + PAIRS + grp * IDX_GROUP, IDX_GROUP)]
        out.extend((lo[k], hi[k]) for k in range(IDX_GROUP))
    return out


def _load_pair(tab, ia, ib):
    wa = tab[pl.ds(pl.multiple_of(ia, ROW_WORDS), ROW_WORDS), :]
    wb = tab[pl.ds(pl.multiple_of(ib, ROW_WORDS), ROW_WORDS), :]
    return _unpack(jnp.concatenate([wa, wb], axis=0))


def _peer_u_kernel(idx_ref, x_ref, g_ref, tab, c_ref, prod_a, prod_b, rbuf, a_t):
    lane = lax.broadcasted_iota(jnp.int32, (SLOTS, LANES), 1)

    @pl.when(pl.program_id(0) == 0)
    def _():
        rbuf[...] = jnp.zeros_like(rbuf)
        a_t[...] = jnp.zeros_like(a_t)

    def reduce(slot, tok):
        col = jnp.sum(rbuf[slot], axis=1, keepdims=True)
        a_t[...] = jnp.where(lane == tok, col, a_t[...])

    def gather(t, slot, prod):
        xt = x_ref[t]
        x_hi = jnp.concatenate([xt[0:ROW_WORDS, :]] * 2, axis=0)
        x_lo = jnp.concatenate([xt[ROW_WORDS:, :]] * 2, axis=0)
        for j, (ia, ib) in enumerate(_slot_pairs(idx_ref, t)):
            hi, lo = _load_pair(tab, ia, ib)
            prod[pl.ds(j, SUBLANES, stride=STRIDE), :] = hi * x_hi + lo * x_lo
        halves = [functools.reduce(jnp.add, [prod[pl.ds((h * ROW_WORDS + s) * STRIDE, PAIRS), :]
                                             for s in range(ROW_WORDS)]) for h in range(2)]
        rbuf[slot] = jnp.concatenate(halves, axis=0)

    def two_tokens(p, carry):
        t0 = 2 * p
        reduce(0, t0 - 2)
        reduce(1, t0 - 1)
        gather(t0, 0, prod_a)
        gather(t0 + 1, 1, prod_b)
        return carry

    lax.fori_loop(0, TB // 2, two_tokens, 0)
    reduce(0, TB - 2)
    reduce(1, TB - 1)
    a = a_t[...].T[0:TB, :]
    gelu = 0.5 * a * (1.0 + lax.erf(a * (2.0 ** -0.5)))
    c_ref[...] = g_ref[...] * gelu


def _peer_u(idx, x, g, tab_u):
    n = x.shape[0]
    nb = n // TB
    return pl.pallas_call(
        _peer_u_kernel,
        out_shape=jax.ShapeDtypeStruct((n, SLOTS), F32),
        grid=(nb,),
        in_specs=[pl.BlockSpec((None, 1, TB * SLOTS), lambda i: (i, 0, 0), memory_space=pltpu.SMEM),
                  pl.BlockSpec((TB, SUBLANES, LANES), lambda i: (i, 0, 0)),
                  pl.BlockSpec((TB, SLOTS), lambda i: (i, 0)),
                  _const(tab_u.shape)],
        out_specs=pl.BlockSpec((TB, SLOTS), lambda i: (i, 0)),
        scratch_shapes=[pltpu.VMEM((SUBLANES * STRIDE, LANES), F32),
                        pltpu.VMEM((SUBLANES * STRIDE, LANES), F32),
                        pltpu.VMEM((2, SLOTS, LANES), F32),
                        pltpu.VMEM((SLOTS, LANES), F32)],
        compiler_params=_params(),
        name="peer_u",
    )(idx.reshape(nb, 1, TB * SLOTS), x.reshape(n, SUBLANES, LANES), g, tab_u)


def _peer_v_kernel(idx_ref, c_ref, tab, y_ref, *crep):
    row = lax.broadcasted_iota(jnp.int32, (SUBLANES, LANES), 0)
    eye = (lax.broadcasted_iota(jnp.int32, (SLOTS, LANES), 0)
           == lax.broadcasted_iota(jnp.int32, (SLOTS, LANES), 1))

    def spread(t, crep):
        col = jnp.sum(jnp.where(eye, c_ref[pl.ds(t, 1), :], 0.0), axis=1, keepdims=True)
        crep[...] = jnp.broadcast_to(col, (SLOTS, LANES))

    def token(t, crep):
        acc_hi = jnp.zeros((SUBLANES, LANES), F32)
        acc_lo = jnp.zeros((SUBLANES, LANES), F32)
        for j, (ia, ib) in enumerate(_slot_pairs(idx_ref, t)):
            hi, lo = _load_pair(tab, ia, ib)
            cm = jnp.where(row < ROW_WORDS, crep[pl.ds(j, 1), :], crep[pl.ds(PAIRS + j, 1), :])
            acc_hi = acc_hi + cm * hi
            acc_lo = acc_lo + cm * lo
        y_ref[t] = jnp.concatenate([acc_hi[0:ROW_WORDS, :] + acc_hi[ROW_WORDS:, :],
                                    acc_lo[0:ROW_WORDS, :] + acc_lo[ROW_WORDS:, :]], axis=0)

    ahead = CREP_RING // 2
    for k in range(ahead):
        spread(k, crep[k])

    def ring_trip(p, carry):
        t0 = CREP_RING * p
        for k in range(CREP_RING):
            spread(jnp.minimum(t0 + k + ahead, TB - 1), crep[(k + ahead) % CREP_RING])
            token(t0 + k, crep[k])
        return carry

    lax.fori_loop(0, TB // CREP_RING, ring_trip, 0)


def _peer_v(idx, c, tab_v):
    n = c.shape[0]
    nb = n // TB
    y = pl.pallas_call(
        _peer_v_kernel,
        out_shape=jax.ShapeDtypeStruct((n, SUBLANES, LANES), F32),
        grid=(nb,),
        in_specs=[pl.BlockSpec((None, 1, TB * SLOTS), lambda i: (i, 0, 0), memory_space=pltpu.SMEM),
                  pl.BlockSpec((TB, SLOTS), lambda i: (i, 0)),
                  _const(tab_v.shape)],
        out_specs=pl.BlockSpec((TB, SUBLANES, LANES), lambda i: (i, 0, 0)),
        scratch_shapes=[pltpu.VMEM((SLOTS, LANES), F32)] * CREP_RING,
        compiler_params=_params(),
        name="peer_v",
    )(idx.reshape(nb, 1, TB * SLOTS), c, tab_v)
    return y.reshape(n, D_MODEL)


def _sc_peer_v(idx, c, table):
    n = idx.shape[0]
    per_worker = n // SC_WORKERS
    batches = per_worker // SC_TOKENS
    groups = SC_TOKENS * SLOTS // SC_ROWS
    chunks = D_MODEL // SC_LANES
    mesh = plsc.VectorSubcoreMesh(core_axis_name="c", subcore_axis_name="s")

    def body(idx_hbm, c_hbm, tab_hbm, y_hbm, idx_v, c_v, rows_a, rows_b, y_v, sem_a, sem_b):
        wid = lax.axis_index("s") * 2 + lax.axis_index("c")
        zero = jnp.zeros((SC_LANES,), F32)

        def gather(g, rows, sem):
            return pltpu.make_async_copy(tab_hbm.at[idx_v.at[pl.ds(g * SC_ROWS, SC_ROWS)]], rows, sem)

        def accumulate(g, rows):
            tok = g // (SLOTS // SC_ROWS)
            ws = [plsc.load_gather(c_v, [jnp.full((SC_LANES,), g * SC_ROWS + r, jnp.int32)])
                  for r in range(SC_ROWS)]

            @pl.loop(0, chunks // SC_ACC)
            def _(cb):
                base = pl.multiple_of(cb * (SC_ACC * SC_LANES), SC_ACC * SC_LANES)
                acc = [zero] * SC_ACC
                for r in range(SC_ROWS):
                    for k in range(SC_ACC):
                        acc[k] = acc[k] + ws[r] * rows[r, pl.ds(base + k * SC_LANES, SC_LANES)]
                for k in range(SC_ACC):
                    plsc.addupdate(y_v.at[tok, pl.ds(base + k * SC_LANES, SC_LANES)], acc[k])

        @pl.loop(0, batches)
        def _(b):
            t0 = wid * per_worker + b * SC_TOKENS
            pltpu.sync_copy(idx_hbm.at[pl.ds(t0 * SLOTS, SC_TOKENS * SLOTS)], idx_v)
            pltpu.sync_copy(c_hbm.at[pl.ds(t0 * SLOTS, SC_TOKENS * SLOTS)], c_v)
            for t in range(SC_TOKENS):
                for ch in range(chunks):
                    y_v[t, pl.ds(ch * SC_LANES, SC_LANES)] = zero
            gather(0, rows_a, sem_a).start()

            @pl.loop(0, groups // 2)
            def _(h):
                g = 2 * h
                gather(g + 1, rows_b, sem_b).start()
                gather(g, rows_a, sem_a).wait()
                accumulate(g, rows_a)

                @pl.when(h + 1 < groups // 2)
                def _():
                    gather(g + 2, rows_a, sem_a).start()
                gather(g + 1, rows_b, sem_b).wait()
                accumulate(g + 1, rows_b)

            pltpu.sync_copy(y_v, y_hbm.at[pl.ds(t0, SC_TOKENS)])

    return pl.kernel(
        body,
        out_type=jax.ShapeDtypeStruct((n, D_MODEL), F32),
        mesh=mesh,
        scratch_types=[pltpu.VMEM((SC_TOKENS * SLOTS,), jnp.int32),
                       pltpu.VMEM((SC_TOKENS * SLOTS,), F32),
                       pltpu.VMEM((SC_ROWS, D_MODEL), F32),
                       pltpu.VMEM((SC_ROWS, D_MODEL), F32),
                       pltpu.VMEM((SC_TOKENS, D_MODEL), F32),
                       pltpu.SemaphoreType.DMA,
                       pltpu.SemaphoreType.DMA],
        compiler_params=pltpu.CompilerParams(needs_layout_passes=False),
        name="sc_peer_v",
    )(idx.reshape(n * SLOTS), c.reshape(n * SLOTS), table)


def _final_kernel(x1_ref, y_ref, pp_ref, ps_ref, gple_ref, wg_ref, wp_ref, gfin_ref, op_ref, os_ref):
    x2 = x1_ref[...] + y_ref[...]
    gate = jax.nn.sigmoid(_dot(_rms(x2, gple_ref[...]).astype(BF16), wg_ref[...]))
    x3 = x2 + gate * _dot(_group_rows(pp_ref, ps_ref).astype(BF16), wp_ref[...])
    out = _rms(x3, gfin_ref[...])
    is_prompt = pl.program_id(0) < P_TILES

    @pl.when(is_prompt)
    def _():
        op_ref[...] = out

    @pl.when(jnp.logical_not(is_prompt))
    def _():
        os_ref[...] = out


def _final(x1, y, pp, ps, gple, wg, wp, gfin):
    return pl.pallas_call(
        _final_kernel,
        out_shape=[jax.ShapeDtypeStruct((N_P, D_MODEL), F32), jax.ShapeDtypeStruct((N_S, D_MODEL), F32)],
        grid=(N_TOK // TM,),
        in_specs=[_rows(D_MODEL), _rows(D_MODEL), _rows_prompt(PLE_DIM), _rows_sample(PLE_DIM),
                  _const(gple.shape), _const(wg.shape), _const(wp.shape), _const(gfin.shape)],
        out_specs=[_rows_prompt(D_MODEL), _rows_sample(D_MODEL)],
        compiler_params=_params(),
        name="ple_final",
    )(x1, y, pp, ps, gple, wg, wp, gfin)


def _rope_tables():
    half = ROPE // 2
    freqs = ROPE_THETA ** (-np.arange(half, dtype=np.float32) / half)
    pos = np.concatenate([np.arange(SEQ), np.tile(PAST_LEN + np.arange(DEC_SEQ), TM // DEC_SEQ)])
    ang = jnp.asarray(pos, F32)[:, None] * jnp.asarray(freqs, F32)[None, :]
    cos, sin = jnp.cos(ang), jnp.sin(ang)
    n = pos.shape[0]
    pad = jnp.zeros((n, LANES - NOPE - ROPE), F32)
    cos_t = jnp.concatenate([jnp.ones((n, NOPE), F32), cos, cos, pad], axis=1)
    sin_t = jnp.concatenate([jnp.zeros((n, NOPE), F32), sin, sin, pad], axis=1)
    return cos_t, sin_t


def _rot_cols(w):
    half = ROPE // 2
    return jnp.concatenate([-w[..., half:], w[..., :half]], axis=-1)


def _rel_bias(table, n_q, n_k):
    diag = np.arange(-(n_q - 1), n_k)
    line = table[:, np.clip(WINDOW_B - diag, -REL_CLIP, REL_CLIP) + REL_CLIP]
    return jnp.stack([line[:, n_q - 1 - i:n_q - 1 - i + n_k] for i in range(n_q)], axis=1)


def kernel(x_prompt, x_sample, cache_latent, cache_krope, cache_band_k, cache_band_v, p_prompt, p_sample, g_mix, w_in, g_q_lora, g_kv_lora, w_uq, w_uk, w_uv, rel_bias, w_a_proj, w_b_proj, w_out, g_ffn, w_query, sub_keys, expert_u, expert_v, g_ple, w_ple_gate, w_ple_proj, g_final):
    w = w_in[0]
    o_kr = Q_LORA + KV_LORA
    o_qb = o_kr + ROPE
    o_ga = o_qb + 3 * HB
    w_kr = w[:, o_kr:o_qb]
    z64 = jnp.zeros((D_MODEL, NOPE), F32)
    z32 = jnp.zeros((D_MODEL, LANES - NOPE - ROPE), F32)
    w1 = jnp.concatenate([w[:, :o_kr], z64, w_kr, z32, z64, _rot_cols(w_kr), z32], axis=1).astype(BF16)
    w2 = w[:, o_qb:o_ga].astype(BF16)
    w3 = w[:, o_ga:].astype(BF16)
    wq = w_uq[0].reshape(Q_LORA, H_A, NOPE + ROPE)
    zq64 = jnp.zeros((Q_LORA, H_A, NOPE), F32)
    zq32 = jnp.zeros((Q_LORA, H_A, LANES - NOPE - ROPE), F32)
    wuq1 = jnp.concatenate([wq, zq32], axis=-1).reshape(Q_LORA, H_A * LANES).astype(BF16)
    wuq2 = jnp.concatenate([zq64, _rot_cols(wq[..., NOPE:]), zq32], axis=-1).reshape(Q_LORA, H_A * LANES).astype(BF16)
    wukp = jnp.concatenate([w_uk[0], jnp.zeros((KV_LORA, H_A, LANES - NOPE), F32)], axis=-1)
    wukp = wukp.reshape(KV_LORA, H_A * LANES).astype(BF16)
    wuv = w_uv[0].reshape(KV_LORA, H_A * V_A).astype(BF16)
    wukt = jnp.transpose(w_uk[0], (1, 2, 0)).astype(BF16)
    wuvh = jnp.transpose(w_uv[0], (1, 0, 2)).astype(BF16)
    cos_t, sin_t = _rope_tables()
    row = lambda g: g.reshape(1, -1)

    xp = x_prompt.reshape(N_P, D_MODEL)
    xs = x_sample.reshape(N_S, D_MODEL)

    (lat, kr, q16, k16, v16, qb16, kb16, vb16, kb32, vb32, sga, sgb) = _inproj(
        xp, xs, row(g_mix[0]), w1, w2, w3, row(g_q_lora[0]), row(g_kv_lora[0]), wuq1, wuq2, wukp, wuv, cos_t, sin_t)

    oa_p = _mla_prompt(q16, k16, v16)
    oa_s = _mla_sample(q16, cache_latent[0], cache_krope[0], lat, kr, wukt, wuvh)

    pad = ((0, 0), (WINDOW_B, 0), (0, 0))
    kpad = jnp.pad(kb16[:N_P].reshape(BATCH, SEQ, HB), pad)
    vpad = jnp.pad(vb16[:N_P].reshape(BATCH, SEQ, HB), pad)
    tab = rel_bias[0]
    bias_p = _rel_bias(tab, CHUNK, BAND)
    bias_s = _rel_bias(tab, DEC_SEQ, WINDOW_B + DEC_SEQ)
    ob_p = _band_prompt(qb16, kpad, vpad, bias_p)
    ob_s, bk_s, bv_s = _band_sample(qb16, cache_band_k[0].reshape(DEC_BATCH, WINDOW_B, HB),
                                    cache_band_v[0].reshape(DEC_BATCH, WINDOW_B, HB),
                                    kb16, vb16, kb32, vb32, bias_s[:, :, :WINDOW_B], bias_s[:, :, WINDOW_B:])

    oa = jnp.concatenate([oa_p, oa_s], axis=0)
    ob = jnp.concatenate([ob_p, ob_s], axis=0)
    x1, h2, h16 = _mix_out(oa, ob, sga, sgb, xp, xs, w_a_proj[0].astype(BF16), w_b_proj[0].astype(BF16),
                           w_out[0].astype(BF16), row(g_ffn[0]))
    wqt = jnp.transpose(w_query[0]).astype(BF16)
    keys = sub_keys[0].reshape(2 * PEER_HEADS, N_KEYS, N_KEYS).astype(BF16)
    idx, gw = _peer_topk(h16, wqt, keys)

    tab_u = _pack_table(expert_u[0])
    tab_v = _pack_table(expert_v[0])
    if N_SC:
        c_sc = _peer_u(idx[:N_SC], h2[:N_SC], gw[:N_SC], tab_u)
        y_sc = _sc_peer_v(idx[:N_SC] // ROW_WORDS, c_sc, expert_v[0])
        c_tc = _peer_u(idx[N_SC:], h2[N_SC:], gw[N_SC:], tab_u)
        y = jnp.concatenate([y_sc, _peer_v(idx[N_SC:], c_tc, tab_v)], axis=0)
    else:
        y = _peer_v(idx, _peer_u(idx, h2, gw, tab_u), tab_v)

    out_p, out_s = _final(x1, y, p_prompt.reshape(N_P, PLE_DIM), p_sample.reshape(N_S, PLE_DIM), row(g_ple[0]),
                          w_ple_gate[0].astype(BF16), w_ple_proj[0].astype(BF16), row(g_final))

    y_prompt = out_p.reshape(BATCH, SEQ, D_MODEL)
    y_sample = out_s.reshape(DEC_BATCH, DEC_SEQ, D_MODEL)
    lat_p = lat[:N_P].reshape(1, BATCH, SEQ, KV_LORA)
    kr_p = kr[:N_P].reshape(1, BATCH, SEQ, ROPE)
    bk_p = kb32[:N_P].reshape(BATCH, SEQ, H_B, DH_B)[None, :, SEQ - WINDOW_B:]
    bv_p = vb32[:N_P].reshape(BATCH, SEQ, H_B, DH_B)[None, :, SEQ - WINDOW_B:]
    lat_s = lat[N_P:].reshape(1, DEC_BATCH, DEC_SEQ, KV_LORA)
    kr_s = kr[N_P:].reshape(1, DEC_BATCH, DEC_SEQ, ROPE)
    bk_s = bk_s.reshape(1, DEC_BATCH, WINDOW_B, H_B, DH_B)
    bv_s = bv_s.reshape(1, DEC_BATCH, WINDOW_B, H_B, DH_B)
    return (y_prompt, y_sample, lat_p, kr_p, bk_p, bv_p, lat_s, kr_s, bk_s, bv_s)
```

```python
import functools
import jax
import jax.numpy as jnp
import numpy as np
from jax import lax
from jax.experimental import pallas as pl
from jax.experimental.pallas import tpu as pltpu
from jax.experimental.pallas import tpu_sc as plsc

D_MODEL = 1024
BATCH = 16
SEQ = 2048
DEC_BATCH = 32
DEC_SEQ = 32
PAST_LEN = 4096
CHUNK = 64
EPS = 1e-6
H_A = 8
Q_LORA = 384
KV_LORA = 256
NOPE = 64
ROPE = 32
V_A = 64
ROPE_THETA = 10000.0
MLA_SCALE = (NOPE + ROPE) ** -0.5
H_B = 8
DH_B = 64
BAND_PREV = 8
BAND = (BAND_PREV + 1) * CHUNK
WINDOW_B = BAND_PREV * CHUNK
REL_CLIP = 256
BAND_SCALE = DH_B ** -0.5
PEER_HEADS = 8
N_KEYS = 128
N_EXPERTS = N_KEYS * N_KEYS
TOPK = 16
PLE_DIM = 256

N_P = BATCH * SEQ
N_S = DEC_BATCH * DEC_SEQ
N_TOK = N_P + N_S
HB = H_B * DH_B
HALF = D_MODEL // 2
SLOTS = PEER_HEADS * TOPK
LANES = 128
SUBLANES = 8
TM = 256
TQ = 128
KV_STEP = 512
TB = 64
CREP_RING = 4
ROW_WORDS = HALF // LANES
PAIRS = SLOTS // 2
STRIDE = 72
HEADS_PER_TRIP = 8
IDX_GROUP = 8
VMEM_LIMIT = 56 * 1024 * 1024
SC_WORKERS = 32
SC_LANES = 16
SC_TOKENS = 8
SC_ROWS = 32
SC_ACC = 8
N_SC = 20480
NEG = -1e30
BF16 = jnp.bfloat16
F32 = jnp.float32
NT = (((1,), (1,)), ((), ()))


def _params(n_axes=1):
    return pltpu.CompilerParams(dimension_semantics=("arbitrary",) * n_axes,
                                vmem_limit_bytes=VMEM_LIMIT)


def _const(shape):
    nd = len(shape)
    return pl.BlockSpec(shape, lambda *_: (0,) * nd, pipeline_mode=pl.Buffered(1))


def _rows(width, tile=TM):
    return pl.BlockSpec((tile, width), lambda i: (i, 0))


P_TILES = N_P // TM


def _rows_head(width, tiles):
    return pl.BlockSpec((TM, width), lambda i: (jnp.minimum(i, tiles - 1), 0))


def _rows_tail(width, tiles):
    return pl.BlockSpec((TM, width), lambda i: (jnp.maximum(i - tiles, 0), 0))


def _rows_prompt(width):
    return _rows_head(width, P_TILES)


def _rows_sample(width):
    return _rows_tail(width, P_TILES)


def _split_rows(head_ref, tail_ref, tiles):
    return jnp.where(pl.program_id(0) < tiles, head_ref[...], tail_ref[...])


def _group_rows(p_ref, s_ref):
    return _split_rows(p_ref, s_ref, P_TILES)


def _rms(x, g):
    return x * lax.rsqrt(jnp.mean(x * x, axis=-1, keepdims=True) + EPS) * g


def _dot(a, b):
    return jnp.dot(a, b, preferred_element_type=F32)


def _softmax_rows(s_list):
    m = functools.reduce(jnp.maximum, [jnp.max(s, axis=-1, keepdims=True) for s in s_list])
    p_list = [jnp.exp(s - m) for s in s_list]
    inv = 1.0 / functools.reduce(jnp.add, [jnp.sum(p, axis=-1, keepdims=True) for p in p_list])
    return [(p * inv).astype(BF16) for p in p_list]


def _inproj_kernel(xp_ref, xs_ref, gmix_ref, w1_ref, w2_ref, w3_ref, gq_ref, gkv_ref, wuq1_ref, wuq2_ref,
                   wukp_ref, wuv_ref, cos_ref, sin_ref,
                   lat_ref, kr_ref, q16_ref, k16_ref, v16_ref, qb16_ref, kb16_ref, vb16_ref,
                   kb32_ref, vb32_ref, sga_ref, sgb_ref):
    h = _rms(_group_rows(xp_ref, xs_ref), gmix_ref[...]).astype(BF16)
    cos = cos_ref[...]
    sin = sin_ref[...]
    z1 = _dot(h, w1_ref[...])
    k128 = z1[:, 640:768] * cos + z1[:, 768:896] * sin
    kr_ref[...] = k128[:, NOPE:NOPE + ROPE]
    cqn = _rms(z1[:, :Q_LORA], gq_ref[...]).astype(BF16)
    cos8 = jnp.concatenate([cos] * H_A, axis=1)
    sin8 = jnp.concatenate([sin] * H_A, axis=1)
    q = _dot(cqn, wuq1_ref[...]) * cos8 + _dot(cqn, wuq2_ref[...]) * sin8
    q16_ref[...] = q.astype(BF16)
    ckvn = _rms(z1[:, Q_LORA:Q_LORA + KV_LORA], gkv_ref[...])
    lat_ref[...] = ckvn
    ckvn16 = ckvn.astype(BF16)
    kk = _dot(ckvn16, wukp_ref[...]) + jnp.concatenate([k128] * H_A, axis=1)
    k16_ref[...] = kk.astype(BF16)
    v16_ref[...] = _dot(ckvn16, wuv_ref[...]).astype(BF16)
    z2 = _dot(h, w2_ref[...])
    qb16_ref[...] = z2[:, :HB].astype(BF16)
    kb = z2[:, HB:2 * HB]
    vb = z2[:, 2 * HB:]
    kb32_ref[...] = kb
    vb32_ref[...] = vb
    kb16_ref[...] = kb.astype(BF16)
    vb16_ref[...] = vb.astype(BF16)
    z3 = _dot(h, w3_ref[...])
    sga_ref[...] = jax.nn.sigmoid(z3[:, :D_MODEL])
    sgb_ref[...] = jax.nn.sigmoid(z3[:, D_MODEL:])


def _rope_rows():
    per_seq = SEQ // TM
    return pl.BlockSpec((TM, LANES), lambda i: (jnp.where(i < N_P // TM, i % per_seq, per_seq), 0))


def _inproj(xp, xs, gmix, w1, w2, w3, gq, gkv, wuq1, wuq2, wukp, wuv, cos, sin):
    n = N_TOK
    widths = [(KV_LORA, F32), (ROPE, F32), (H_A * LANES, BF16), (H_A * LANES, BF16), (H_A * V_A, BF16),
              (HB, BF16), (HB, BF16), (HB, BF16), (HB, F32), (HB, F32), (D_MODEL, F32), (D_MODEL, F32)]
    return pl.pallas_call(
        _inproj_kernel,
        out_shape=[jax.ShapeDtypeStruct((n, w), d) for w, d in widths],
        grid=(n // TM,),
        in_specs=[_rows_prompt(D_MODEL), _rows_sample(D_MODEL),
                  _const(gmix.shape), _const(w1.shape), _const(w2.shape), _const(w3.shape),
                  _const(gq.shape), _const(gkv.shape), _const(wuq1.shape), _const(wuq2.shape),
                  _const(wukp.shape), _const(wuv.shape), _rope_rows(), _rope_rows()],
        out_specs=[_rows(w) for w, _ in widths],
        compiler_params=_params(),
        name="inproj",
    )(xp, xs, gmix, w1, w2, w3, gq, gkv, wuq1, wuq2, wukp, wuv, cos, sin)


def _pair_select(o_even, o_odd):
    lane = lax.broadcasted_iota(jnp.int32, o_even.shape, 1)
    return jnp.where(lane < V_A, o_even, o_odd)


def _mla_prompt_tile(q_ref, k_ref, v_ref, o_ref, i, nk):
    q_chunk = (i * TQ + lax.broadcasted_iota(jnp.int32, (TQ, nk), 0)) // CHUNK
    k_chunk = lax.broadcasted_iota(jnp.int32, (TQ, nk), 1) // CHUNK
    mask = k_chunk <= q_chunk
    scores = []
    for hd in range(H_A):
        qh = q_ref[:, hd * LANES:(hd + 1) * LANES]
        kh = k_ref[0:nk, hd * LANES:(hd + 1) * LANES]
        s = lax.dot_general(qh, kh, NT, preferred_element_type=F32) * MLA_SCALE
        scores.append(jnp.where(mask, s, NEG))
    weights = [_softmax_rows([s])[0] for s in scores]
    outs = [_dot(weights[hd], v_ref[0:nk, (hd // 2) * LANES:(hd // 2 + 1) * LANES]) for hd in range(H_A)]
    pairs = [_pair_select(outs[2 * pr], outs[2 * pr + 1]) for pr in range(H_A // 2)]
    o_ref[...] = jnp.concatenate(pairs, axis=1).astype(BF16)


def _mla_prompt_kernel(q_ref, k_ref, v_ref, o_ref):
    i = pl.program_id(1)
    tiles_per_step = KV_STEP // TQ
    for grp in range(SEQ // KV_STEP):
        @pl.when(i // tiles_per_step == grp)
        def _():
            _mla_prompt_tile(q_ref, k_ref, v_ref, o_ref, i, (grp + 1) * KV_STEP)


def _mla_prompt(q16, k16, v16):
    nq = SEQ // TQ
    return pl.pallas_call(
        _mla_prompt_kernel,
        out_shape=jax.ShapeDtypeStruct((N_P, H_A * V_A), BF16),
        grid=(BATCH, nq),
        in_specs=[pl.BlockSpec((TQ, H_A * LANES), lambda b, i: (b * nq + i, 0)),
                  pl.BlockSpec((SEQ, H_A * LANES), lambda b, i: (b, 0)),
                  pl.BlockSpec((SEQ, H_A * V_A), lambda b, i: (b, 0))],
        out_specs=pl.BlockSpec((TQ, H_A * V_A), lambda b, i: (b * nq + i, 0)),
        compiler_params=_params(2),
        name="mla_prompt",
    )(q16, k16, v16)


def _mla_sample_kernel(q_ref, clat_ref, ckr_ref, nlat_ref, nkr_ref, wukt_ref, wuvh_ref, o_ref):
    q = q_ref[...].astype(F32)
    qlat, qrope = [], []
    for hd in range(H_A):
        qn = q[:, hd * LANES:hd * LANES + NOPE].astype(BF16)
        qlat.append(_dot(qn, wukt_ref[hd]).astype(BF16))
        qrope.append(q[:, hd * LANES + NOPE:hd * LANES + NOPE + ROPE].astype(BF16))
    qlat = jnp.concatenate(qlat, axis=0)
    qrope = jnp.concatenate(qrope, axis=0)
    clat = clat_ref[...].astype(BF16)
    ckr = ckr_ref[...].astype(BF16)
    nlat = nlat_ref[...].astype(BF16)
    nkr = nkr_ref[...].astype(BF16)
    s_c = (lax.dot_general(qlat, clat, NT, preferred_element_type=F32)
           + lax.dot_general(qrope, ckr, NT, preferred_element_type=F32)) * MLA_SCALE
    s_n = (lax.dot_general(qlat, nlat, NT, preferred_element_type=F32)
           + lax.dot_general(qrope, nkr, NT, preferred_element_type=F32)) * MLA_SCALE
    w_c, w_n = _softmax_rows([s_c, s_n])
    olat = (_dot(w_c, clat) + _dot(w_n, nlat)).astype(BF16)
    outs = [_dot(olat[hd * DEC_SEQ:(hd + 1) * DEC_SEQ, :], wuvh_ref[hd]) for hd in range(H_A)]
    o_ref[...] = jnp.concatenate(outs, axis=1).astype(BF16)


def _mla_sample(q16, cache_lat, cache_kr, lat, kr, wukt, wuvh):
    off = N_P // DEC_SEQ
    return pl.pallas_call(
        _mla_sample_kernel,
        out_shape=jax.ShapeDtypeStruct((N_S, H_A * V_A), BF16),
        grid=(DEC_BATCH,),
        in_specs=[pl.BlockSpec((DEC_SEQ, H_A * LANES), lambda b: (off + b, 0)),
                  pl.BlockSpec((None, PAST_LEN, KV_LORA), lambda b: (b, 0, 0)),
                  pl.BlockSpec((None, PAST_LEN, ROPE), lambda b: (b, 0, 0)),
                  pl.BlockSpec((DEC_SEQ, KV_LORA), lambda b: (off + b, 0)),
                  pl.BlockSpec((DEC_SEQ, ROPE), lambda b: (off + b, 0)),
                  _const(wukt.shape), _const(wuvh.shape)],
        out_specs=pl.BlockSpec((DEC_SEQ, H_A * V_A), lambda b: (b, 0)),
        compiler_params=_params(),
        name="mla_sample",
    )(q16, cache_lat, cache_kr, lat, kr, wukt, wuvh)


def _band_heads(q, blocks, bias_refs, valid):
    lane = lax.broadcasted_iota(jnp.int32, (q.shape[0], LANES), 1)
    scores = []
    for hd in range(H_B):
        sl = slice((hd // 2) * LANES, (hd // 2 + 1) * LANES)
        q2 = q[:, sl]
        own = (lane >= DH_B) if hd % 2 else (lane < DH_B)
        qm = jnp.where(own, q2, jnp.zeros_like(q2))
        ss = [lax.dot_general(qm, k[:, sl], NT, preferred_element_type=F32) * BAND_SCALE + b_ref[hd]
              for (k, _), b_ref in zip(blocks, bias_refs)]
        if valid is not None:
            ss = [jnp.where(valid, s, NEG) for s in ss]
        scores.append(ss)
    weights = [_softmax_rows(ss) for ss in scores]
    outs = []
    for hd in range(H_B):
        sl = slice((hd // 2) * LANES, (hd // 2 + 1) * LANES)
        outs.append(functools.reduce(jnp.add, [_dot(w, v[:, sl]) for w, (_, v) in zip(weights[hd], blocks)]))
    pairs = [_pair_select(outs[2 * pr], outs[2 * pr + 1]) for pr in range(H_B // 2)]
    return jnp.concatenate(pairs, axis=1).astype(BF16)


def _band_prompt_kernel(q_ref, k_ref, v_ref, bias_ref, o_ref):
    c = pl.program_id(1)
    start = pl.multiple_of(c * CHUNK, CHUNK)
    k = k_ref[pl.ds(start, BAND), :]
    v = v_ref[pl.ds(start, BAND), :]
    valid = (start - WINDOW_B + lax.broadcasted_iota(jnp.int32, (CHUNK, BAND), 1)) >= 0
    o_ref[...] = _band_heads(q_ref[...], [(k, v)], [bias_ref], valid)


def _band_prompt(qb16, kpad, vpad, bias):
    nc = SEQ // CHUNK
    return pl.pallas_call(
        _band_prompt_kernel,
        out_shape=jax.ShapeDtypeStruct((N_P, HB), BF16),
        grid=(BATCH, nc),
        in_specs=[pl.BlockSpec((CHUNK, HB), lambda b, c: (b * nc + c, 0)),
                  pl.BlockSpec((None, SEQ + WINDOW_B, HB), lambda b, c: (b, 0, 0)),
                  pl.BlockSpec((None, SEQ + WINDOW_B, HB), lambda b, c: (b, 0, 0)),
                  _const(bias.shape)],
        out_specs=pl.BlockSpec((CHUNK, HB), lambda b, c: (b * nc + c, 0)),
        compiler_params=_params(2),
        name="band_prompt",
    )(qb16, kpad, vpad, bias)


def _band_sample_kernel(q_ref, ck_ref, cv_ref, nk16_ref, nv16_ref, nk32_ref, nv32_ref, bias_c_ref, bias_n_ref,
                        o_ref, bk_ref, bv_ref):
    ck = ck_ref[...]
    cv = cv_ref[...]
    blocks = [(ck.astype(BF16), cv.astype(BF16)), (nk16_ref[...], nv16_ref[...])]
    o_ref[...] = _band_heads(q_ref[...], blocks, [bias_c_ref, bias_n_ref], None)
    keep = WINDOW_B - DEC_SEQ
    bk_ref[0:keep, :] = ck[DEC_SEQ:, :]
    bk_ref[keep:, :] = nk32_ref[...]
    bv_ref[0:keep, :] = cv[DEC_SEQ:, :]
    bv_ref[keep:, :] = nv32_ref[...]


def _band_sample(qb16, cache_k, cache_v, kb16, vb16, kb32, vb32, bias_c, bias_n):
    off = N_P // DEC_SEQ
    new = lambda: pl.BlockSpec((DEC_SEQ, HB), lambda b: (off + b, 0))
    cache = lambda: pl.BlockSpec((None, WINDOW_B, HB), lambda b: (b, 0, 0))
    return pl.pallas_call(
        _band_sample_kernel,
        out_shape=[jax.ShapeDtypeStruct((N_S, HB), BF16),
                   jax.ShapeDtypeStruct((DEC_BATCH, WINDOW_B, HB), F32),
                   jax.ShapeDtypeStruct((DEC_BATCH, WINDOW_B, HB), F32)],
        grid=(DEC_BATCH,),
        in_specs=[new(), cache(), cache(), new(), new(), new(), new(),
                  _const(bias_c.shape), _const(bias_n.shape)],
        out_specs=[pl.BlockSpec((DEC_SEQ, HB), lambda b: (b, 0)), cache(), cache()],
        compiler_params=_params(),
        name="band_sample",
    )(qb16, cache_k, cache_v, kb16, vb16, kb32, vb32, bias_c, bias_n)


def _mix_out_kernel(oa_ref, ob_ref, sga_ref, sgb_ref, xp_ref, xs_ref, wa_ref, wb_ref, wout_ref, gffn_ref,
                    x1_ref, h2_ref, h16_ref):
    merged = sga_ref[...] * _dot(oa_ref[...], wa_ref[...]) + sgb_ref[...] * _dot(ob_ref[...], wb_ref[...])
    x1 = _group_rows(xp_ref, xs_ref) + _dot(merged.astype(BF16), wout_ref[...])
    x1_ref[...] = x1
    h2 = _rms(x1, gffn_ref[...])
    h2_ref[...] = h2
    h16_ref[...] = h2.astype(BF16)


def _mix_out(oa, ob, sga, sgb, xp, xs, wa, wb, wout, gffn):
    n = N_TOK
    return pl.pallas_call(
        _mix_out_kernel,
        out_shape=[jax.ShapeDtypeStruct((n, D_MODEL), F32), jax.ShapeDtypeStruct((n, D_MODEL), F32),
                   jax.ShapeDtypeStruct((n, D_MODEL), BF16)],
        grid=(n // TM,),
        in_specs=[_rows(H_A * V_A), _rows(HB), _rows(D_MODEL), _rows(D_MODEL),
                  _rows_prompt(D_MODEL), _rows_sample(D_MODEL),
                  _const(wa.shape), _const(wb.shape), _const(wout.shape), _const(gffn.shape)],
        out_specs=[_rows(D_MODEL)] * 3,
        compiler_params=_params(),
        name="mix_out",
    )(oa, ob, sga, sgb, xp, xs, wa, wb, wout, gffn)


def _top16(s, n_rows):
    row = lax.broadcasted_iota(jnp.int32, (n_rows, TQ), 0).astype(F32)
    vals, ids = [], []
    for _ in range(TOPK):
        m = jnp.max(s, axis=0, keepdims=True)
        i = jnp.min(jnp.where(s == m, row, float(n_rows)), axis=0, keepdims=True)
        vals.append(m)
        ids.append(i)
        s = jnp.where(row == i, -jnp.inf, s)
    return jnp.concatenate(vals, axis=0), jnp.concatenate(ids, axis=0)


def _pair_candidates(f1, f2):
    h = SUBLANES
    blocks = [f1(0, 1, 0, h), f1(0, 1, h, 2 * h)]
    blocks += [f1(a, a + 1, 0, h) for a in range(1, h)]
    blocks += [f2(h, 2 * h, 0, 1)]
    return jnp.concatenate(blocks, axis=0)


def _peer_topk_kernel(h_ref, wqt_ref, keys_ref, idx_ref, g_ref, qt_ref, g_t, id_t):
    qt_ref[...] = lax.dot_general(wqt_ref[...], h_ref[...], NT, preferred_element_type=F32).astype(BF16)
    r8 = lax.broadcasted_iota(jnp.int32, (SUBLANES, TQ), 0).astype(F32)
    flat = lambda a0, a1, b0, b1: (r8 + float(b0)) + float(TOPK * a0)
    flat_t = lambda a0, a1, b0, b1: (r8 + float(a0)) * float(TOPK) + float(b0)
    cflat = _pair_candidates(flat, flat_t)

    def one_head(hd):
        tops = []
        for p in range(2):
            hp = hd * 2 + p
            qs = qt_ref[pl.ds(pl.multiple_of(hp * N_KEYS, N_KEYS), N_KEYS), :]
            tops.append(_top16(_dot(keys_ref[hp], qs), N_KEYS))
        (s1, i1), (s2, i2) = tops
        add = lambda x, y: (lambda a0, a1, b0, b1: x[a0:a1, :] + y[b0:b1, :])
        cand = _pair_candidates(add(s1, s2), add(s1, s2))
        e1 = i1 * float(N_KEYS)
        cidx = _pair_candidates(add(e1, i2), add(e1, i2))
        vals, ids = [], []
        for _ in range(TOPK):
            m = jnp.max(cand, axis=0, keepdims=True)
            c = jnp.min(jnp.where(cand == m, cflat, float(TOPK * TOPK)), axis=0, keepdims=True)
            sel = cflat == c
            vals.append(m)
            ids.append(jnp.sum(jnp.where(sel, cidx, 0.0), axis=0, keepdims=True))
            cand = jnp.where(sel, -jnp.inf, cand)
        best = jnp.concatenate(vals, axis=0)
        e = jnp.exp(best - best[0:1, :])
        g = e / jnp.sum(e, axis=0, keepdims=True)
        base = pl.multiple_of(hd * TOPK, TOPK)
        g_t[pl.ds(base, TOPK), :] = g
        id_t[pl.ds(base, TOPK), :] = jnp.concatenate(ids, axis=0)

    def head_group(grp, carry):
        for e in range(HEADS_PER_TRIP):
            one_head(grp * HEADS_PER_TRIP + e)
        return carry

    lax.fori_loop(0, PEER_HEADS // HEADS_PER_TRIP, head_group, 0)
    g_ref[...] = g_t[...].T
    idx_ref[...] = (id_t[...].T * float(ROW_WORDS)).astype(jnp.int32)


def _peer_topk(h16, wqt, keys):
    n = h16.shape[0]
    nt = n // TQ
    return pl.pallas_call(
        _peer_topk_kernel,
        out_shape=[jax.ShapeDtypeStruct((n, SLOTS), jnp.int32),
                   jax.ShapeDtypeStruct((n, SLOTS), F32)],
        grid=(nt,),
        in_specs=[_rows(D_MODEL, TQ), _const(wqt.shape), _const(keys.shape)],
        out_specs=[_rows(SLOTS, TQ)] * 2,
        scratch_shapes=[pltpu.VMEM((2 * PEER_HEADS * N_KEYS, TQ), BF16),
                        pltpu.VMEM((SLOTS, TQ), F32), pltpu.VMEM((SLOTS, TQ), F32)],
        compiler_params=_params(),
        name="peer_topk",
    )(h16, wqt, keys)


def _pack_table(t):
    b = lax.bitcast_convert_type(t.astype(BF16), jnp.uint16).astype(jnp.uint32)
    packed = (b[:, :HALF] << 16) | b[:, HALF:]
    return packed.reshape(N_EXPERTS * ROW_WORDS, LANES)


def _unpack(w):
    hi = lax.bitcast_convert_type(w & jnp.uint32(0xFFFF0000), F32)
    lo = lax.bitcast_convert_type(w << 16, F32)
    return hi, lo


def _slot_pairs(idx_ref, t):
    out = []
    for grp in range(PAIRS // IDX_GROUP):
        lo = idx_ref.at[0, pl.ds(t * SLOTS + grp * IDX_GROUP, IDX_GROUP)]
        hi = idx_ref.at[0, pl.ds(t * SLOTS + PAIRS + grp * IDX_GROUP, IDX_GROUP)]
        out.extend((lo[k], hi[k]) for k in range(IDX_GROUP))
    return out


def _load_pair(tab, ia, ib):
    wa = tab[pl.ds(pl.multiple_of(ia, ROW_WORDS), ROW_WORDS), :]
    wb = tab[pl.ds(pl.multiple_of(ib, ROW_WORDS), ROW_WORDS), :]
    return _unpack(jnp.concatenate([wa, wb], axis=0))


def _peer_u_kernel(idx_ref, x_ref, g_ref, tab, c_ref, prod_a, prod_b, rbuf, a_t):
    lane = lax.broadcasted_iota(jnp.int32, (SLOTS, LANES), 1)

    @pl.when(pl.program_id(0) == 0)
    def _():
        rbuf[...] = jnp.zeros_like(rbuf)
        a_t[...] = jnp.zeros_like(a_t)

    def reduce(slot, tok):
        col = jnp.sum(rbuf[slot], axis=1, keepdims=True)
        a_t[...] = jnp.where(lane == tok, col, a_t[...])

    def gather(t, slot, prod):
        xt = x_ref[t]
        x_hi = jnp.concatenate([xt[0:ROW_WORDS, :]] * 2, axis=0)
        x_lo = jnp.concatenate([xt[ROW_WORDS:, :]] * 2, axis=0)
        for j, (ia, ib) in enumerate(_slot_pairs(idx_ref, t)):
            hi, lo = _load_pair(tab, ia, ib)
            prod[pl.ds(j, SUBLANES, stride=STRIDE), :] = hi * x_hi + lo * x_lo
        halves = [functools.reduce(jnp.add, [prod[pl.ds((h * ROW_WORDS + s) * STRIDE, PAIRS), :]
                                             for s in range(ROW_WORDS)]) for h in range(2)]
        rbuf[slot] = jnp.concatenate(halves, axis=0)

    def two_tokens(p, carry):
        t0 = 2 * p
        reduce(0, t0 - 2)
        reduce(1, t0 - 1)
        gather(t0, 0, prod_a)
        gather(t0 + 1, 1, prod_b)
        return carry

    lax.fori_loop(0, TB // 2, two_tokens, 0)
    reduce(0, TB - 2)
    reduce(1, TB - 1)
    a = a_t[...].T[0:TB, :]
    gelu = 0.5 * a * (1.0 + lax.erf(a * (2.0 ** -0.5)))
    c_ref[...] = g_ref[...] * gelu


def _peer_u(idx, x, g, tab_u, first, count):
    n = x.shape[0]
    nb = count // TB
    fb = first // TB
    return pl.pallas_call(
        _peer_u_kernel,
        out_shape=jax.ShapeDtypeStruct((count, SLOTS), F32),
        grid=(nb,),
        in_specs=[pl.BlockSpec((None, 1, TB * SLOTS), lambda i: (fb + i, 0, 0), memory_space=pltpu.SMEM),
                  pl.BlockSpec((TB, SUBLANES, LANES), lambda i: (fb + i, 0, 0)),
                  pl.BlockSpec((TB, SLOTS), lambda i: (fb + i, 0)),
                  _const(tab_u.shape)],
        out_specs=pl.BlockSpec((TB, SLOTS), lambda i: (i, 0)),
        scratch_shapes=[pltpu.VMEM((SUBLANES * STRIDE, LANES), F32),
                        pltpu.VMEM((SUBLANES * STRIDE, LANES), F32),
                        pltpu.VMEM((2, SLOTS, LANES), F32),
                        pltpu.VMEM((SLOTS, LANES), F32)],
        compiler_params=_params(),
        name="peer_u",
    )(idx.reshape(n // TB, 1, TB * SLOTS), x.reshape(n, SUBLANES, LANES), g, tab_u)


def _peer_v_kernel(idx_ref, c_ref, tab, y_ref, *crep):
    row = lax.broadcasted_iota(jnp.int32, (SUBLANES, LANES), 0)
    eye = (lax.broadcasted_iota(jnp.int32, (SLOTS, LANES), 0)
           == lax.broadcasted_iota(jnp.int32, (SLOTS, LANES), 1))

    def spread(t, crep):
        col = jnp.sum(jnp.where(eye, c_ref[pl.ds(t, 1), :], 0.0), axis=1, keepdims=True)
        crep[...] = jnp.broadcast_to(col, (SLOTS, LANES))

    def token(t, crep):
        acc_hi = jnp.zeros((SUBLANES, LANES), F32)
        acc_lo = jnp.zeros((SUBLANES, LANES), F32)
        for j, (ia, ib) in enumerate(_slot_pairs(idx_ref, t)):
            hi, lo = _load_pair(tab, ia, ib)
            cm = jnp.where(row < ROW_WORDS, crep[pl.ds(j, 1), :], crep[pl.ds(PAIRS + j, 1), :])
            acc_hi = acc_hi + cm * hi
            acc_lo = acc_lo + cm * lo
        y_ref[t] = jnp.concatenate([acc_hi[0:ROW_WORDS, :] + acc_hi[ROW_WORDS:, :],
                                    acc_lo[0:ROW_WORDS, :] + acc_lo[ROW_WORDS:, :]], axis=0)

    ahead = CREP_RING // 2
    for k in range(ahead):
        spread(k, crep[k])

    def ring_trip(p, carry):
        t0 = CREP_RING * p
        for k in range(CREP_RING):
            spread(jnp.minimum(t0 + k + ahead, TB - 1), crep[(k + ahead) % CREP_RING])
            token(t0 + k, crep[k])
        return carry

    lax.fori_loop(0, TB // CREP_RING, ring_trip, 0)


def _peer_v(idx, c, tab_v, first):
    n = c.shape[0]
    nb = n // TB
    fb = first // TB
    y = pl.pallas_call(
        _peer_v_kernel,
        out_shape=jax.ShapeDtypeStruct((n, SUBLANES, LANES), F32),
        grid=(nb,),
        in_specs=[pl.BlockSpec((None, 1, TB * SLOTS), lambda i: (fb + i, 0, 0), memory_space=pltpu.SMEM),
                  pl.BlockSpec((TB, SLOTS), lambda i: (i, 0)),
                  _const(tab_v.shape)],
        out_specs=pl.BlockSpec((TB, SUBLANES, LANES), lambda i: (i, 0, 0)),
        scratch_shapes=[pltpu.VMEM((SLOTS, LANES), F32)] * CREP_RING,
        compiler_params=_params(),
        name="peer_v",
    )(idx.reshape(idx.shape[0] // TB, 1, TB * SLOTS), c, tab_v)
    return y.reshape(n, D_MODEL)


def _sc_peer_v(idx, c, table):
    n = idx.shape[0]
    per_worker = n // SC_WORKERS
    batches = per_worker // SC_TOKENS
    groups = SC_TOKENS * SLOTS // SC_ROWS
    chunks = D_MODEL // SC_LANES
    mesh = plsc.VectorSubcoreMesh(core_axis_name="c", subcore_axis_name="s")

    def body(idx_hbm, c_hbm, tab_hbm, y_hbm, idx_v, c_v, rows_a, rows_b, y_v, sem_a, sem_b):
        wid = lax.axis_index("s") * 2 + lax.axis_index("c")
        zero = jnp.zeros((SC_LANES,), F32)

        def gather(g, rows, sem):
            return pltpu.make_async_copy(tab_hbm.at[idx_v.at[pl.ds(g * SC_ROWS, SC_ROWS)]], rows, sem)

        def accumulate(g, rows):
            tok = g // (SLOTS // SC_ROWS)
            ws = [plsc.load_gather(c_v, [jnp.full((SC_LANES,), g * SC_ROWS + r, jnp.int32)])
                  for r in range(SC_ROWS)]

            @pl.loop(0, chunks // SC_ACC)
            def _(cb):
                base = pl.multiple_of(cb * (SC_ACC * SC_LANES), SC_ACC * SC_LANES)
                acc = [zero] * SC_ACC
                for r in range(SC_ROWS):
                    for k in range(SC_ACC):
                        acc[k] = acc[k] + ws[r] * rows[r, pl.ds(base + k * SC_LANES, SC_LANES)]
                for k in range(SC_ACC):
                    plsc.addupdate(y_v.at[tok, pl.ds(base + k * SC_LANES, SC_LANES)], acc[k])

        @pl.loop(0, batches)
        def _(b):
            t0 = wid * per_worker + b * SC_TOKENS
            pltpu.sync_copy(idx_hbm.at[pl.ds(t0 * SLOTS, SC_TOKENS * SLOTS)], idx_v)
            pltpu.sync_copy(c_hbm.at[pl.ds(t0 * SLOTS, SC_TOKENS * SLOTS)], c_v)
            for t in range(SC_TOKENS):
                for ch in range(chunks):
                    y_v[t, pl.ds(ch * SC_LANES, SC_LANES)] = zero
            gather(0, rows_a, sem_a).start()

            @pl.loop(0, groups // 2)
            def _(h):
                g = 2 * h
                gather(g + 1, rows_b, sem_b).start()
                gather(g, rows_a, sem_a).wait()
                accumulate(g, rows_a)

                @pl.when(h + 1 < groups // 2)
                def _():
                    gather(g + 2, rows_a, sem_a).start()
                gather(g + 1, rows_b, sem_b).wait()
                accumulate(g + 1, rows_b)

            pltpu.sync_copy(y_v, y_hbm.at[pl.ds(t0, SC_TOKENS)])

    return pl.kernel(
        body,
        out_type=jax.ShapeDtypeStruct((n, D_MODEL), F32),
        mesh=mesh,
        scratch_types=[pltpu.VMEM((SC_TOKENS * SLOTS,), jnp.int32),
                       pltpu.VMEM((SC_TOKENS * SLOTS,), F32),
                       pltpu.VMEM((SC_ROWS, D_MODEL), F32),
                       pltpu.VMEM((SC_ROWS, D_MODEL), F32),
                       pltpu.VMEM((SC_TOKENS, D_MODEL), F32),
                       pltpu.SemaphoreType.DMA,
                       pltpu.SemaphoreType.DMA],
        compiler_params=pltpu.CompilerParams(needs_layout_passes=False),
        name="sc_peer_v",
    )(idx.reshape(n * SLOTS), c.reshape(n * SLOTS), table)


def _final_kernel(x1_ref, yh_ref, yt_ref, pp_ref, ps_ref, gple_ref, wg_ref, wp_ref, gfin_ref, op_ref, os_ref):
    x2 = x1_ref[...] + _split_rows(yh_ref, yt_ref, N_SC // TM)
    gate = jax.nn.sigmoid(_dot(_rms(x2, gple_ref[...]).astype(BF16), wg_ref[...]))
    x3 = x2 + gate * _dot(_group_rows(pp_ref, ps_ref).astype(BF16), wp_ref[...])
    out = _rms(x3, gfin_ref[...])
    is_prompt = pl.program_id(0) < P_TILES

    @pl.when(is_prompt)
    def _():
        op_ref[...] = out

    @pl.when(jnp.logical_not(is_prompt))
    def _():
        os_ref[...] = out


def _final(x1, y_head, y_tail, pp, ps, gple, wg, wp, gfin):
    return pl.pallas_call(
        _final_kernel,
        out_shape=[jax.ShapeDtypeStruct((N_P, D_MODEL), F32), jax.ShapeDtypeStruct((N_S, D_MODEL), F32)],
        grid=(N_TOK // TM,),
        in_specs=[_rows(D_MODEL), _rows_head(D_MODEL, N_SC // TM), _rows_tail(D_MODEL, N_SC // TM),
                  _rows_prompt(PLE_DIM), _rows_sample(PLE_DIM),
                  _const(gple.shape), _const(wg.shape), _const(wp.shape), _const(gfin.shape)],
        out_specs=[_rows_prompt(D_MODEL), _rows_sample(D_MODEL)],
        compiler_params=_params(),
        name="ple_final",
    )(x1, y_head, y_tail, pp, ps, gple, wg, wp, gfin)


def _rope_tables():
    half = ROPE // 2
    freqs = ROPE_THETA ** (-np.arange(half, dtype=np.float32) / half)
    pos = np.concatenate([np.arange(SEQ), np.tile(PAST_LEN + np.arange(DEC_SEQ), TM // DEC_SEQ)])
    ang = jnp.asarray(pos, F32)[:, None] * jnp.asarray(freqs, F32)[None, :]
    cos, sin = jnp.cos(ang), jnp.sin(ang)
    n = pos.shape[0]
    pad = jnp.zeros((n, LANES - NOPE - ROPE), F32)
    cos_t = jnp.concatenate([jnp.ones((n, NOPE), F32), cos, cos, pad], axis=1)
    sin_t = jnp.concatenate([jnp.zeros((n, NOPE), F32), sin, sin, pad], axis=1)
    return cos_t, sin_t


def _rot_cols(w):
    half = ROPE // 2
    return jnp.concatenate([-w[..., half:], w[..., :half]], axis=-1)


def _rel_bias(table, n_q, n_k):
    diag = np.arange(-(n_q - 1), n_k)
    line = table[:, np.clip(WINDOW_B - diag, -REL_CLIP, REL_CLIP) + REL_CLIP]
    return jnp.stack([line[:, n_q - 1 - i:n_q - 1 - i + n_k] for i in range(n_q)], axis=1)


def kernel(x_prompt, x_sample, cache_latent, cache_krope, cache_band_k, cache_band_v, p_prompt, p_sample, g_mix, w_in, g_q_lora, g_kv_lora, w_uq, w_uk, w_uv, rel_bias, w_a_proj, w_b_proj, w_out, g_ffn, w_query, sub_keys, expert_u, expert_v, g_ple, w_ple_gate, w_ple_proj, g_final):
    w = w_in[0]
    o_kr = Q_LORA + KV_LORA
    o_qb = o_kr + ROPE
    o_ga = o_qb + 3 * HB
    w_kr = w[:, o_kr:o_qb]
    z64 = jnp.zeros((D_MODEL, NOPE), F32)
    z32 = jnp.zeros((D_MODEL, LANES - NOPE - ROPE), F32)
    w1 = jnp.concatenate([w[:, :o_kr], z64, w_kr, z32, z64, _rot_cols(w_kr), z32], axis=1).astype(BF16)
    w2 = w[:, o_qb:o_ga].astype(BF16)
    w3 = w[:, o_ga:].astype(BF16)
    wq = w_uq[0].reshape(Q_LORA, H_A, NOPE + ROPE)
    zq64 = jnp.zeros((Q_LORA, H_A, NOPE), F32)
    zq32 = jnp.zeros((Q_LORA, H_A, LANES - NOPE - ROPE), F32)
    wuq1 = jnp.concatenate([wq, zq32], axis=-1).reshape(Q_LORA, H_A * LANES).astype(BF16)
    wuq2 = jnp.concatenate([zq64, _rot_cols(wq[..., NOPE:]), zq32], axis=-1).reshape(Q_LORA, H_A * LANES).astype(BF16)
    wukp = jnp.concatenate([w_uk[0], jnp.zeros((KV_LORA, H_A, LANES - NOPE), F32)], axis=-1)
    wukp = wukp.reshape(KV_LORA, H_A * LANES).astype(BF16)
    wuv = w_uv[0].reshape(KV_LORA, H_A * V_A).astype(BF16)
    wukt = jnp.transpose(w_uk[0], (1, 2, 0)).astype(BF16)
    wuvh = jnp.transpose(w_uv[0], (1, 0, 2)).astype(BF16)
    cos_t, sin_t = _rope_tables()
    row = lambda g: g.reshape(1, -1)

    xp = x_prompt.reshape(N_P, D_MODEL)
    xs = x_sample.reshape(N_S, D_MODEL)

    (lat, kr, q16, k16, v16, qb16, kb16, vb16, kb32, vb32, sga, sgb) = _inproj(
        xp, xs, row(g_mix[0]), w1, w2, w3, row(g_q_lora[0]), row(g_kv_lora[0]), wuq1, wuq2, wukp, wuv, cos_t, sin_t)

    oa_p = _mla_prompt(q16, k16, v16)
    oa_s = _mla_sample(q16, cache_latent[0], cache_krope[0], lat, kr, wukt, wuvh)

    pad = ((0, 0), (WINDOW_B, 0), (0, 0))
    kpad = jnp.pad(kb16[:N_P].reshape(BATCH, SEQ, HB), pad)
    vpad = jnp.pad(vb16[:N_P].reshape(BATCH, SEQ, HB), pad)
    tab = rel_bias[0]
    bias_p = _rel_bias(tab, CHUNK, BAND)
    bias_s = _rel_bias(tab, DEC_SEQ, WINDOW_B + DEC_SEQ)
    ob_p = _band_prompt(qb16, kpad, vpad, bias_p)
    ob_s, bk_s, bv_s = _band_sample(qb16, cache_band_k[0].reshape(DEC_BATCH, WINDOW_B, HB),
                                    cache_band_v[0].reshape(DEC_BATCH, WINDOW_B, HB),
                                    kb16, vb16, kb32, vb32, bias_s[:, :, :WINDOW_B], bias_s[:, :, WINDOW_B:])

    oa = jnp.concatenate([oa_p, oa_s], axis=0)
    ob = jnp.concatenate([ob_p, ob_s], axis=0)
    x1, h2, h16 = _mix_out(oa, ob, sga, sgb, xp, xs, w_a_proj[0].astype(BF16), w_b_proj[0].astype(BF16),
                           w_out[0].astype(BF16), row(g_ffn[0]))
    wqt = jnp.transpose(w_query[0]).astype(BF16)
    keys = sub_keys[0].reshape(2 * PEER_HEADS, N_KEYS, N_KEYS).astype(BF16)
    idx, gw = _peer_topk(h16, wqt, keys)

    tab_u = _pack_table(expert_u[0])
    tab_v = _pack_table(expert_v[0])
    c_sc = _peer_u(idx, h2, gw, tab_u, 0, N_SC)
    y_sc = _sc_peer_v(idx[:N_SC] // ROW_WORDS, c_sc, expert_v[0])
    c_tc = _peer_u(idx, h2, gw, tab_u, N_SC, N_TOK - N_SC)
    y_tc = _peer_v(idx, c_tc, tab_v, N_SC)

    out_p, out_s = _final(x1, y_sc, y_tc, p_prompt.reshape(N_P, PLE_DIM), p_sample.reshape(N_S, PLE_DIM), row(g_ple[0]),
                          w_ple_gate[0].astype(BF16), w_ple_proj[0].astype(BF16), row(g_final))

    y_prompt = out_p.reshape(BATCH, SEQ, D_MODEL)
    y_sample = out_s.reshape(DEC_BATCH, DEC_SEQ, D_MODEL)
    lat_p = lat[:N_P].reshape(1, BATCH, SEQ, KV_LORA)
    kr_p = kr[:N_P].reshape(1, BATCH, SEQ, ROPE)
    bk_p = kb32[:N_P].reshape(BATCH, SEQ, H_B, DH_B)[None, :, SEQ - WINDOW_B:]
    bv_p = vb32[:N_P].reshape(BATCH, SEQ, H_B, DH_B)[None, :, SEQ - WINDOW_B:]
    lat_s = lat[N_P:].reshape(1, DEC_BATCH, DEC_SEQ, KV_LORA)
    kr_s = kr[N_P:].reshape(1, DEC_BATCH, DEC_SEQ, ROPE)
    bk_s = bk_s.reshape(1, DEC_BATCH, WINDOW_B, H_B, DH_B)
    bv_s = bv_s.reshape(1, DEC_BATCH, WINDOW_B, H_B, DH_B)
    return (y_prompt, y_sample, lat_p, kr_p, bk_p, bv_p, lat_s, kr_s, bk_s, bv_s)
```

```python
import functools
import jax
import jax.numpy as jnp
import numpy as np
from jax import lax
from jax.experimental import pallas as pl
from jax.experimental.pallas import tpu as pltpu
from jax.experimental.pallas import tpu_sc as plsc

D_MODEL = 1024
BATCH = 16
SEQ = 2048
DEC_BATCH = 32
DEC_SEQ = 32
PAST_LEN = 4096
CHUNK = 64
EPS = 1e-6
H_A = 8
Q_LORA = 384
KV_LORA = 256
NOPE = 64
ROPE = 32
V_A = 64
ROPE_THETA = 10000.0
MLA_SCALE = (NOPE + ROPE) ** -0.5
H_B = 8
DH_B = 64
BAND_PREV = 8
BAND = (BAND_PREV + 1) * CHUNK
WINDOW_B = BAND_PREV * CHUNK
REL_CLIP = 256
BAND_SCALE = DH_B ** -0.5
PEER_HEADS = 8
N_KEYS = 128
N_EXPERTS = N_KEYS * N_KEYS
TOPK = 16
PLE_DIM = 256

N_P = BATCH * SEQ
N_S = DEC_BATCH * DEC_SEQ
N_TOK = N_P + N_S
HB = H_B * DH_B
HALF = D_MODEL // 2
SLOTS = PEER_HEADS * TOPK
LANES = 128
SUBLANES = 8
TM = 256
TQ = 128
KV_STEP = 512
TB = 64
CREP_RING = 4
ROW_WORDS = HALF // LANES
PAIRS = SLOTS // 2
STRIDE = 72
HEADS_PER_TRIP = 8
IDX_GROUP = 8
VMEM_LIMIT = 56 * 1024 * 1024
SC_WORKERS = 32
SC_LANES = 16
SC_TOKENS = 8
SC_ROWS = 32
SC_ACC = 8
N_SC = 16384
NEG = -1e30
BF16 = jnp.bfloat16
F32 = jnp.float32
NT = (((1,), (1,)), ((), ()))


def _params(n_axes=1):
    return pltpu.CompilerParams(dimension_semantics=("arbitrary",) * n_axes,
                                vmem_limit_bytes=VMEM_LIMIT)


def _const(shape):
    nd = len(shape)
    return pl.BlockSpec(shape, lambda *_: (0,) * nd, pipeline_mode=pl.Buffered(1))


def _rows(width, tile=TM):
    return pl.BlockSpec((tile, width), lambda i: (i, 0))


P_TILES = N_P // TM


def _rows_head(width, tiles):
    return pl.BlockSpec((TM, width), lambda i: (jnp.minimum(i, tiles - 1), 0))


def _rows_tail(width, tiles):
    return pl.BlockSpec((TM, width), lambda i: (jnp.maximum(i - tiles, 0), 0))


def _rows_prompt(width):
    return _rows_head(width, P_TILES)


def _rows_sample(width):
    return _rows_tail(width, P_TILES)


def _split_rows(head_ref, tail_ref, tiles):
    return jnp.where(pl.program_id(0) < tiles, head_ref[...], tail_ref[...])


def _group_rows(p_ref, s_ref):
    return _split_rows(p_ref, s_ref, P_TILES)


def _rms(x, g):
    return x * lax.rsqrt(jnp.mean(x * x, axis=-1, keepdims=True) + EPS) * g


def _dot(a, b):
    return jnp.dot(a, b, preferred_element_type=F32)


def _softmax_rows(s_list):
    m = functools.reduce(jnp.maximum, [jnp.max(s, axis=-1, keepdims=True) for s in s_list])
    p_list = [jnp.exp(s - m) for s in s_list]
    inv = 1.0 / functools.reduce(jnp.add, [jnp.sum(p, axis=-1, keepdims=True) for p in p_list])
    return [(p * inv).astype(BF16) for p in p_list]


def _inproj_kernel(xp_ref, xs_ref, gmix_ref, w1_ref, w2_ref, w3_ref, gq_ref, gkv_ref, wuq1_ref, wuq2_ref,
                   wukp_ref, wuv_ref, cos_ref, sin_ref,
                   lat_ref, kr_ref, q16_ref, k16_ref, v16_ref, qb16_ref, kb16_ref, vb16_ref,
                   kb32_ref, vb32_ref, sga_ref, sgb_ref):
    h = _rms(_group_rows(xp_ref, xs_ref), gmix_ref[...]).astype(BF16)
    cos = cos_ref[...]
    sin = sin_ref[...]
    z1 = _dot(h, w1_ref[...])
    k128 = z1[:, 640:768] * cos + z1[:, 768:896] * sin
    kr_ref[...] = k128[:, NOPE:NOPE + ROPE]
    cqn = _rms(z1[:, :Q_LORA], gq_ref[...]).astype(BF16)
    cos8 = jnp.concatenate([cos] * H_A, axis=1)
    sin8 = jnp.concatenate([sin] * H_A, axis=1)
    q = _dot(cqn, wuq1_ref[...]) * cos8 + _dot(cqn, wuq2_ref[...]) * sin8
    q16_ref[...] = q.astype(BF16)
    ckvn = _rms(z1[:, Q_LORA:Q_LORA + KV_LORA], gkv_ref[...])
    lat_ref[...] = ckvn
    ckvn16 = ckvn.astype(BF16)
    kk = _dot(ckvn16, wukp_ref[...]) + jnp.concatenate([k128] * H_A, axis=1)
    k16_ref[...] = kk.astype(BF16)
    v16_ref[...] = _dot(ckvn16, wuv_ref[...]).astype(BF16)
    z2 = _dot(h, w2_ref[...])
    qb16_ref[...] = z2[:, :HB].astype(BF16)
    kb = z2[:, HB:2 * HB]
    vb = z2[:, 2 * HB:]
    kb32_ref[...] = kb
    vb32_ref[...] = vb
    kb16_ref[...] = kb.astype(BF16)
    vb16_ref[...] = vb.astype(BF16)
    z3 = _dot(h, w3_ref[...])
    sga_ref[...] = jax.nn.sigmoid(z3[:, :D_MODEL])
    sgb_ref[...] = jax.nn.sigmoid(z3[:, D_MODEL:])


def _rope_rows():
    per_seq = SEQ // TM
    return pl.BlockSpec((TM, LANES), lambda i: (jnp.where(i < N_P // TM, i % per_seq, per_seq), 0))


def _inproj(xp, xs, gmix, w1, w2, w3, gq, gkv, wuq1, wuq2, wukp, wuv, cos, sin):
    n = N_TOK
    widths = [(KV_LORA, F32), (ROPE, F32), (H_A * LANES, BF16), (H_A * LANES, BF16), (H_A * V_A, BF16),
              (HB, BF16), (HB, BF16), (HB, BF16), (HB, F32), (HB, F32), (D_MODEL, F32), (D_MODEL, F32)]
    return pl.pallas_call(
        _inproj_kernel,
        out_shape=[jax.ShapeDtypeStruct((n, w), d) for w, d in widths],
        grid=(n // TM,),
        in_specs=[_rows_prompt(D_MODEL), _rows_sample(D_MODEL),
                  _const(gmix.shape), _const(w1.shape), _const(w2.shape), _const(w3.shape),
                  _const(gq.shape), _const(gkv.shape), _const(wuq1.shape), _const(wuq2.shape),
                  _const(wukp.shape), _const(wuv.shape), _rope_rows(), _rope_rows()],
        out_specs=[_rows(w) for w, _ in widths],
        compiler_params=_params(),
        name="inproj",
    )(xp, xs, gmix, w1, w2, w3, gq, gkv, wuq1, wuq2, wukp, wuv, cos, sin)


def _pair_select(o_even, o_odd):
    lane = lax.broadcasted_iota(jnp.int32, o_even.shape, 1)
    return jnp.where(lane < V_A, o_even, o_odd)


def _mla_prompt_tile(q_ref, k_ref, v_ref, o_ref, i, nk):
    q_chunk = (i * TQ + lax.broadcasted_iota(jnp.int32, (TQ, nk), 0)) // CHUNK
    k_chunk = lax.broadcasted_iota(jnp.int32, (TQ, nk), 1) // CHUNK
    mask = k_chunk <= q_chunk
    scores = []
    for hd in range(H_A):
        qh = q_ref[:, hd * LANES:(hd + 1) * LANES]
        kh = k_ref[0:nk, hd * LANES:(hd + 1) * LANES]
        s = lax.dot_general(qh, kh, NT, preferred_element_type=F32) * MLA_SCALE
        scores.append(jnp.where(mask, s, NEG))
    weights = [_softmax_rows([s])[0] for s in scores]
    outs = [_dot(weights[hd], v_ref[0:nk, (hd // 2) * LANES:(hd // 2 + 1) * LANES]) for hd in range(H_A)]
    pairs = [_pair_select(outs[2 * pr], outs[2 * pr + 1]) for pr in range(H_A // 2)]
    o_ref[...] = jnp.concatenate(pairs, axis=1).astype(BF16)


def _mla_prompt_kernel(q_ref, k_ref, v_ref, o_ref):
    i = pl.program_id(1)
    tiles_per_step = KV_STEP // TQ
    for grp in range(SEQ // KV_STEP):
        @pl.when(i // tiles_per_step == grp)
        def _():
            _mla_prompt_tile(q_ref, k_ref, v_ref, o_ref, i, (grp + 1) * KV_STEP)


def _mla_prompt(q16, k16, v16):
    nq = SEQ // TQ
    return pl.pallas_call(
        _mla_prompt_kernel,
        out_shape=jax.ShapeDtypeStruct((N_P, H_A * V_A), BF16),
        grid=(BATCH, nq),
        in_specs=[pl.BlockSpec((TQ, H_A * LANES), lambda b, i: (b * nq + i, 0)),
                  pl.BlockSpec((SEQ, H_A * LANES), lambda b, i: (b, 0)),
                  pl.BlockSpec((SEQ, H_A * V_A), lambda b, i: (b, 0))],
        out_specs=pl.BlockSpec((TQ, H_A * V_A), lambda b, i: (b * nq + i, 0)),
        compiler_params=_params(2),
        name="mla_prompt",
    )(q16, k16, v16)


def _mla_sample_kernel(q_ref, clat_ref, ckr_ref, nlat_ref, nkr_ref, wukt_ref, wuvh_ref, o_ref):
    q = q_ref[...].astype(F32)
    qlat, qrope = [], []
    for hd in range(H_A):
        qn = q[:, hd * LANES:hd * LANES + NOPE].astype(BF16)
        qlat.append(_dot(qn, wukt_ref[hd]).astype(BF16))
        qrope.append(q[:, hd * LANES + NOPE:hd * LANES + NOPE + ROPE].astype(BF16))
    qlat = jnp.concatenate(qlat, axis=0)
    qrope = jnp.concatenate(qrope, axis=0)
    clat = clat_ref[...].astype(BF16)
    ckr = ckr_ref[...].astype(BF16)
    nlat = nlat_ref[...].astype(BF16)
    nkr = nkr_ref[...].astype(BF16)
    s_c = (lax.dot_general(qlat, clat, NT, preferred_element_type=F32)
           + lax.dot_general(qrope, ckr, NT, preferred_element_type=F32)) * MLA_SCALE
    s_n = (lax.dot_general(qlat, nlat, NT, preferred_element_type=F32)
           + lax.dot_general(qrope, nkr, NT, preferred_element_type=F32)) * MLA_SCALE
    w_c, w_n = _softmax_rows([s_c, s_n])
    olat = (_dot(w_c, clat) + _dot(w_n, nlat)).astype(BF16)
    outs = [_dot(olat[hd * DEC_SEQ:(hd + 1) * DEC_SEQ, :], wuvh_ref[hd]) for hd in range(H_A)]
    o_ref[...] = jnp.concatenate(outs, axis=1).astype(BF16)


def _mla_sample(q16, cache_lat, cache_kr, lat, kr, wukt, wuvh):
    off = N_P // DEC_SEQ
    return pl.pallas_call(
        _mla_sample_kernel,
        out_shape=jax.ShapeDtypeStruct((N_S, H_A * V_A), BF16),
        grid=(DEC_BATCH,),
        in_specs=[pl.BlockSpec((DEC_SEQ, H_A * LANES), lambda b: (off + b, 0)),
                  pl.BlockSpec((None, PAST_LEN, KV_LORA), lambda b: (b, 0, 0)),
                  pl.BlockSpec((None, PAST_LEN, ROPE), lambda b: (b, 0, 0)),
                  pl.BlockSpec((DEC_SEQ, KV_LORA), lambda b: (off + b, 0)),
                  pl.BlockSpec((DEC_SEQ, ROPE), lambda b: (off + b, 0)),
                  _const(wukt.shape), _const(wuvh.shape)],
        out_specs=pl.BlockSpec((DEC_SEQ, H_A * V_A), lambda b: (b, 0)),
        compiler_params=_params(),
        name="mla_sample",
    )(q16, cache_lat, cache_kr, lat, kr, wukt, wuvh)


def _band_heads(q, blocks, bias_refs, valid):
    lane = lax.broadcasted_iota(jnp.int32, (q.shape[0], LANES), 1)
    scores = []
    for hd in range(H_B):
        sl = slice((hd // 2) * LANES, (hd // 2 + 1) * LANES)
        q2 = q[:, sl]
        own = (lane >= DH_B) if hd % 2 else (lane < DH_B)
        qm = jnp.where(own, q2, jnp.zeros_like(q2))
        ss = [lax.dot_general(qm, k[:, sl], NT, preferred_element_type=F32) * BAND_SCALE + b_ref[hd]
              for (k, _), b_ref in zip(blocks, bias_refs)]
        if valid is not None:
            ss = [jnp.where(valid, s, NEG) for s in ss]
        scores.append(ss)
    weights = [_softmax_rows(ss) for ss in scores]
    outs = []
    for hd in range(H_B):
        sl = slice((hd // 2) * LANES, (hd // 2 + 1) * LANES)
        outs.append(functools.reduce(jnp.add, [_dot(w, v[:, sl]) for w, (_, v) in zip(weights[hd], blocks)]))
    pairs = [_pair_select(outs[2 * pr], outs[2 * pr + 1]) for pr in range(H_B // 2)]
    return jnp.concatenate(pairs, axis=1).astype(BF16)


def _band_prompt_kernel(q_ref, k_ref, v_ref, bias_ref, o_ref):
    c = pl.program_id(1)
    start = pl.multiple_of(c * CHUNK, CHUNK)
    k = k_ref[pl.ds(start, BAND), :]
    v = v_ref[pl.ds(start, BAND), :]
    valid = (start - WINDOW_B + lax.broadcasted_iota(jnp.int32, (CHUNK, BAND), 1)) >= 0
    o_ref[...] = _band_heads(q_ref[...], [(k, v)], [bias_ref], valid)


def _band_prompt(qb16, kpad, vpad, bias):
    nc = SEQ // CHUNK
    return pl.pallas_call(
        _band_prompt_kernel,
        out_shape=jax.ShapeDtypeStruct((N_P, HB), BF16),
        grid=(BATCH, nc),
        in_specs=[pl.BlockSpec((CHUNK, HB), lambda b, c: (b * nc + c, 0)),
                  pl.BlockSpec((None, SEQ + WINDOW_B, HB), lambda b, c: (b, 0, 0)),
                  pl.BlockSpec((None, SEQ + WINDOW_B, HB), lambda b, c: (b, 0, 0)),
                  _const(bias.shape)],
        out_specs=pl.BlockSpec((CHUNK, HB), lambda b, c: (b * nc + c, 0)),
        compiler_params=_params(2),
        name="band_prompt",
    )(qb16, kpad, vpad, bias)


def _band_sample_kernel(q_ref, ck_ref, cv_ref, nk16_ref, nv16_ref, nk32_ref, nv32_ref, bias_c_ref, bias_n_ref,
                        o_ref, bk_ref, bv_ref):
    ck = ck_ref[...]
    cv = cv_ref[...]
    blocks = [(ck.astype(BF16), cv.astype(BF16)), (nk16_ref[...], nv16_ref[...])]
    o_ref[...] = _band_heads(q_ref[...], blocks, [bias_c_ref, bias_n_ref], None)
    keep = WINDOW_B - DEC_SEQ
    bk_ref[0:keep, :] = ck[DEC_SEQ:, :]
    bk_ref[keep:, :] = nk32_ref[...]
    bv_ref[0:keep, :] = cv[DEC_SEQ:, :]
    bv_ref[keep:, :] = nv32_ref[...]


def _band_sample(qb16, cache_k, cache_v, kb16, vb16, kb32, vb32, bias_c, bias_n):
    off = N_P // DEC_SEQ
    new = lambda: pl.BlockSpec((DEC_SEQ, HB), lambda b: (off + b, 0))
    cache = lambda: pl.BlockSpec((None, WINDOW_B, HB), lambda b: (b, 0, 0))
    return pl.pallas_call(
        _band_sample_kernel,
        out_shape=[jax.ShapeDtypeStruct((N_S, HB), BF16),
                   jax.ShapeDtypeStruct((DEC_BATCH, WINDOW_B, HB), F32),
                   jax.ShapeDtypeStruct((DEC_BATCH, WINDOW_B, HB), F32)],
        grid=(DEC_BATCH,),
        in_specs=[new(), cache(), cache(), new(), new(), new(), new(),
                  _const(bias_c.shape), _const(bias_n.shape)],
        out_specs=[pl.BlockSpec((DEC_SEQ, HB), lambda b: (b, 0)), cache(), cache()],
        compiler_params=_params(),
        name="band_sample",
    )(qb16, cache_k, cache_v, kb16, vb16, kb32, vb32, bias_c, bias_n)


def _mix_out_kernel(oa_ref, ob_ref, sga_ref, sgb_ref, xp_ref, xs_ref, wa_ref, wb_ref, wout_ref, gffn_ref,
                    x1_ref, h2_ref, h16_ref):
    merged = sga_ref[...] * _dot(oa_ref[...], wa_ref[...]) + sgb_ref[...] * _dot(ob_ref[...], wb_ref[...])
    x1 = _group_rows(xp_ref, xs_ref) + _dot(merged.astype(BF16), wout_ref[...])
    x1_ref[...] = x1
    h2 = _rms(x1, gffn_ref[...])
    h2_ref[...] = h2
    h16_ref[...] = h2.astype(BF16)


def _mix_out(oa, ob, sga, sgb, xp, xs, wa, wb, wout, gffn):
    n = N_TOK
    return pl.pallas_call(
        _mix_out_kernel,
        out_shape=[jax.ShapeDtypeStruct((n, D_MODEL), F32), jax.ShapeDtypeStruct((n, D_MODEL), F32),
                   jax.ShapeDtypeStruct((n, D_MODEL), BF16)],
        grid=(n // TM,),
        in_specs=[_rows(H_A * V_A), _rows(HB), _rows(D_MODEL), _rows(D_MODEL),
                  _rows_prompt(D_MODEL), _rows_sample(D_MODEL),
                  _const(wa.shape), _const(wb.shape), _const(wout.shape), _const(gffn.shape)],
        out_specs=[_rows(D_MODEL)] * 3,
        compiler_params=_params(),
        name="mix_out",
    )(oa, ob, sga, sgb, xp, xs, wa, wb, wout, gffn)


def _top16(s, n_rows):
    row = lax.broadcasted_iota(jnp.int32, (n_rows, TQ), 0).astype(F32)
    vals, ids = [], []
    for _ in range(TOPK):
        m = jnp.max(s, axis=0, keepdims=True)
        i = jnp.min(jnp.where(s == m, row, float(n_rows)), axis=0, keepdims=True)
        vals.append(m)
        ids.append(i)
        s = jnp.where(row == i, -jnp.inf, s)
    return jnp.concatenate(vals, axis=0), jnp.concatenate(ids, axis=0)


def _pair_candidates(f1, f2):
    h = SUBLANES
    blocks = [f1(0, 1, 0, h), f1(0, 1, h, 2 * h)]
    blocks += [f1(a, a + 1, 0, h) for a in range(1, h)]
    blocks += [f2(h, 2 * h, 0, 1)]
    return jnp.concatenate(blocks, axis=0)


def _peer_topk_kernel(h_ref, wqt_ref, keys_ref, idx_ref, g_ref, qt_ref, g_t, id_t):
    qt_ref[...] = lax.dot_general(wqt_ref[...], h_ref[...], NT, preferred_element_type=F32).astype(BF16)
    r8 = lax.broadcasted_iota(jnp.int32, (SUBLANES, TQ), 0).astype(F32)
    flat = lambda a0, a1, b0, b1: (r8 + float(b0)) + float(TOPK * a0)
    flat_t = lambda a0, a1, b0, b1: (r8 + float(a0)) * float(TOPK) + float(b0)
    cflat = _pair_candidates(flat, flat_t)

    def one_head(hd):
        tops = []
        for p in range(2):
            hp = hd * 2 + p
            qs = qt_ref[pl.ds(pl.multiple_of(hp * N_KEYS, N_KEYS), N_KEYS), :]
            tops.append(_top16(_dot(keys_ref[hp], qs), N_KEYS))
        (s1, i1), (s2, i2) = tops
        add = lambda x, y: (lambda a0, a1, b0, b1: x[a0:a1, :] + y[b0:b1, :])
        cand = _pair_candidates(add(s1, s2), add(s1, s2))
        e1 = i1 * float(N_KEYS)
        cidx = _pair_candidates(add(e1, i2), add(e1, i2))
        vals, ids = [], []
        for _ in range(TOPK):
            m = jnp.max(cand, axis=0, keepdims=True)
            c = jnp.min(jnp.where(cand == m, cflat, float(TOPK * TOPK)), axis=0, keepdims=True)
            sel = cflat == c
            vals.append(m)
            ids.append(jnp.sum(jnp.where(sel, cidx, 0.0), axis=0, keepdims=True))
            cand = jnp.where(sel, -jnp.inf, cand)
        best = jnp.concatenate(vals, axis=0)
        e = jnp.exp(best - best[0:1, :])
        g = e / jnp.sum(e, axis=0, keepdims=True)
        base = pl.multiple_of(hd * TOPK, TOPK)
        g_t[pl.ds(base, TOPK), :] = g
        id_t[pl.ds(base, TOPK), :] = jnp.concatenate(ids, axis=0)

    def head_group(grp, carry):
        for e in range(HEADS_PER_TRIP):
            one_head(grp * HEADS_PER_TRIP + e)
        return carry

    lax.fori_loop(0, PEER_HEADS // HEADS_PER_TRIP, head_group, 0)
    g_ref[...] = g_t[...].T
    idx_ref[...] = (id_t[...].T * float(ROW_WORDS)).astype(jnp.int32)


def _peer_topk(h16, wqt, keys):
    n = h16.shape[0]
    nt = n // TQ
    return pl.pallas_call(
        _peer_topk_kernel,
        out_shape=[jax.ShapeDtypeStruct((n, SLOTS), jnp.int32),
                   jax.ShapeDtypeStruct((n, SLOTS), F32)],
        grid=(nt,),
        in_specs=[_rows(D_MODEL, TQ), _const(wqt.shape), _const(keys.shape)],
        out_specs=[_rows(SLOTS, TQ)] * 2,
        scratch_shapes=[pltpu.VMEM((2 * PEER_HEADS * N_KEYS, TQ), BF16),
                        pltpu.VMEM((SLOTS, TQ), F32), pltpu.VMEM((SLOTS, TQ), F32)],
        compiler_params=_params(),
        name="peer_topk",
    )(h16, wqt, keys)


def _pack_table(t):
    b = lax.bitcast_convert_type(t.astype(BF16), jnp.uint16).astype(jnp.uint32)
    packed = (b[:, :HALF] << 16) | b[:, HALF:]
    return packed.reshape(N_EXPERTS * ROW_WORDS, LANES)


def _unpack(w):
    hi = lax.bitcast_convert_type(w & jnp.uint32(0xFFFF0000), F32)
    lo = lax.bitcast_convert_type(w << 16, F32)
    return hi, lo


def _slot_pairs(idx_ref, t):
    out = []
    for grp in range(PAIRS // IDX_GROUP):
        lo = idx_ref.at[0, pl.ds(t * SLOTS + grp * IDX_GROUP, IDX_GROUP)]
        hi = idx_ref.at[0, pl.ds(t * SLOTS + PAIRS + grp * IDX_GROUP, IDX_GROUP)]
        out.extend((lo[k], hi[k]) for k in range(IDX_GROUP))
    return out


def _load_pair(tab, ia, ib):
    wa = tab[pl.ds(pl.multiple_of(ia, ROW_WORDS), ROW_WORDS), :]
    wb = tab[pl.ds(pl.multiple_of(ib, ROW_WORDS), ROW_WORDS), :]
    return _unpack(jnp.concatenate([wa, wb], axis=0))


def _peer_u_kernel(idx_ref, x_ref, g_ref, tab, c_ref, prod_a, prod_b, rbuf, a_t):
    lane = lax.broadcasted_iota(jnp.int32, (SLOTS, LANES), 1)

    @pl.when(pl.program_id(0) == 0)
    def _():
        rbuf[...] = jnp.zeros_like(rbuf)
        a_t[...] = jnp.zeros_like(a_t)

    def reduce(slot, tok):
        col = jnp.sum(rbuf[slot], axis=1, keepdims=True)
        a_t[...] = jnp.where(lane == tok, col, a_t[...])

    def gather(t, slot, prod):
        xt = x_ref[t]
        x_hi = jnp.concatenate([xt[0:ROW_WORDS, :]] * 2, axis=0)
        x_lo = jnp.concatenate([xt[ROW_WORDS:, :]] * 2, axis=0)
        for j, (ia, ib) in enumerate(_slot_pairs(idx_ref, t)):
            hi, lo = _load_pair(tab, ia, ib)
            prod[pl.ds(j, SUBLANES, stride=STRIDE), :] = hi * x_hi + lo * x_lo
        halves = [functools.reduce(jnp.add, [prod[pl.ds((h * ROW_WORDS + s) * STRIDE, PAIRS), :]
                                             for s in range(ROW_WORDS)]) for h in range(2)]
        rbuf[slot] = jnp.concatenate(halves, axis=0)

    def two_tokens(p, carry):
        t0 = 2 * p
        reduce(0, t0 - 2)
        reduce(1, t0 - 1)
        gather(t0, 0, prod_a)
        gather(t0 + 1, 1, prod_b)
        return carry

    lax.fori_loop(0, TB // 2, two_tokens, 0)
    reduce(0, TB - 2)
    reduce(1, TB - 1)
    a = a_t[...].T[0:TB, :]
    gelu = 0.5 * a * (1.0 + lax.erf(a * (2.0 ** -0.5)))
    c_ref[...] = g_ref[...] * gelu


def _peer_u(idx, x, g, tab_u, first, count):
    n = x.shape[0]
    nb = count // TB
    fb = first // TB
    return pl.pallas_call(
        _peer_u_kernel,
        out_shape=jax.ShapeDtypeStruct((count, SLOTS), F32),
        grid=(nb,),
        in_specs=[pl.BlockSpec((None, 1, TB * SLOTS), lambda i: (fb + i, 0, 0), memory_space=pltpu.SMEM),
                  pl.BlockSpec((TB, SUBLANES, LANES), lambda i: (fb + i, 0, 0)),
                  pl.BlockSpec((TB, SLOTS), lambda i: (fb + i, 0)),
                  _const(tab_u.shape)],
        out_specs=pl.BlockSpec((TB, SLOTS), lambda i: (i, 0)),
        scratch_shapes=[pltpu.VMEM((SUBLANES * STRIDE, LANES), F32),
                        pltpu.VMEM((SUBLANES * STRIDE, LANES), F32),
                        pltpu.VMEM((2, SLOTS, LANES), F32),
                        pltpu.VMEM((SLOTS, LANES), F32)],
        compiler_params=_params(),
        name="peer_u",
    )(idx.reshape(n // TB, 1, TB * SLOTS), x.reshape(n, SUBLANES, LANES), g, tab_u)


def _peer_v_kernel(idx_ref, c_ref, tab, y_ref, *crep):
    row = lax.broadcasted_iota(jnp.int32, (SUBLANES, LANES), 0)
    eye = (lax.broadcasted_iota(jnp.int32, (SLOTS, LANES), 0)
           == lax.broadcasted_iota(jnp.int32, (SLOTS, LANES), 1))

    def spread(t, crep):
        col = jnp.sum(jnp.where(eye, c_ref[pl.ds(t, 1), :], 0.0), axis=1, keepdims=True)
        crep[...] = jnp.broadcast_to(col, (SLOTS, LANES))

    def token(t, crep):
        acc_hi = jnp.zeros((SUBLANES, LANES), F32)
        acc_lo = jnp.zeros((SUBLANES, LANES), F32)
        for j, (ia, ib) in enumerate(_slot_pairs(idx_ref, t)):
            hi, lo = _load_pair(tab, ia, ib)
            cm = jnp.where(row < ROW_WORDS, crep[pl.ds(j, 1), :], crep[pl.ds(PAIRS + j, 1), :])
            acc_hi = acc_hi + cm * hi
            acc_lo = acc_lo + cm * lo
        y_ref[t] = jnp.concatenate([acc_hi[0:ROW_WORDS, :] + acc_hi[ROW_WORDS:, :],
                                    acc_lo[0:ROW_WORDS, :] + acc_lo[ROW_WORDS:, :]], axis=0)

    ahead = CREP_RING // 2
    for k in range(ahead):
        spread(k, crep[k])

    def ring_trip(p, carry):
        t0 = CREP_RING * p
        for k in range(CREP_RING):
            spread(jnp.minimum(t0 + k + ahead, TB - 1), crep[(k + ahead) % CREP_RING])
            token(t0 + k, crep[k])
        return carry

    lax.fori_loop(0, TB // CREP_RING, ring_trip, 0)


def _peer_v(idx, c, tab_v, first):
    n = c.shape[0]
    nb = n // TB
    fb = first // TB
    y = pl.pallas_call(
        _peer_v_kernel,
        out_shape=jax.ShapeDtypeStruct((n, SUBLANES, LANES), F32),
        grid=(nb,),
        in_specs=[pl.BlockSpec((None, 1, TB * SLOTS), lambda i: (fb + i, 0, 0), memory_space=pltpu.SMEM),
                  pl.BlockSpec((TB, SLOTS), lambda i: (i, 0)),
                  _const(tab_v.shape)],
        out_specs=pl.BlockSpec((TB, SUBLANES, LANES), lambda i: (i, 0, 0)),
        scratch_shapes=[pltpu.VMEM((SLOTS, LANES), F32)] * CREP_RING,
        compiler_params=_params(),
        name="peer_v",
    )(idx.reshape(idx.shape[0] // TB, 1, TB * SLOTS), c, tab_v)
    return y.reshape(n, D_MODEL)


def _sc_peer_v(idx, c, table):
    n = idx.shape[0]
    per_worker = n // SC_WORKERS
    batches = per_worker // SC_TOKENS
    groups = SC_TOKENS * SLOTS // SC_ROWS
    chunks = D_MODEL // SC_LANES
    mesh = plsc.VectorSubcoreMesh(core_axis_name="c", subcore_axis_name="s")

    def body(idx_hbm, c_hbm, tab_hbm, y_hbm, idx_v, c_v, rows_a, rows_b, y_v, sem_a, sem_b):
        wid = lax.axis_index("s") * 2 + lax.axis_index("c")
        zero = jnp.zeros((SC_LANES,), F32)

        def gather(g, rows, sem):
            return pltpu.make_async_copy(tab_hbm.at[idx_v.at[pl.ds(g * SC_ROWS, SC_ROWS)]], rows, sem)

        def accumulate(g, rows):
            tok = g // (SLOTS // SC_ROWS)
            ws = [plsc.load_gather(c_v, [jnp.full((SC_LANES,), g * SC_ROWS + r, jnp.int32)])
                  for r in range(SC_ROWS)]

            @pl.loop(0, chunks // SC_ACC)
            def _(cb):
                base = pl.multiple_of(cb * (SC_ACC * SC_LANES), SC_ACC * SC_LANES)
                acc = [zero] * SC_ACC
                for r in range(SC_ROWS):
                    for k in range(SC_ACC):
                        acc[k] = acc[k] + ws[r] * rows[r, pl.ds(base + k * SC_LANES, SC_LANES)]
                for k in range(SC_ACC):
                    plsc.addupdate(y_v.at[tok, pl.ds(base + k * SC_LANES, SC_LANES)], acc[k])

        @pl.loop(0, batches)
        def _(b):
            t0 = wid * per_worker + b * SC_TOKENS
            pltpu.sync_copy(idx_hbm.at[pl.ds(t0 * SLOTS, SC_TOKENS * SLOTS)], idx_v)
            pltpu.sync_copy(c_hbm.at[pl.ds(t0 * SLOTS, SC_TOKENS * SLOTS)], c_v)
            for t in range(SC_TOKENS):
                for ch in range(chunks):
                    y_v[t, pl.ds(ch * SC_LANES, SC_LANES)] = zero
            gather(0, rows_a, sem_a).start()

            @pl.loop(0, groups // 2)
            def _(h):
                g = 2 * h
                gather(g + 1, rows_b, sem_b).start()
                gather(g, rows_a, sem_a).wait()
                accumulate(g, rows_a)

                @pl.when(h + 1 < groups // 2)
                def _():
                    gather(g + 2, rows_a, sem_a).start()
                gather(g + 1, rows_b, sem_b).wait()
                accumulate(g + 1, rows_b)

            pltpu.sync_copy(y_v, y_hbm.at[pl.ds(t0, SC_TOKENS)])

    return pl.kernel(
        body,
        out_type=jax.ShapeDtypeStruct((n, D_MODEL), F32),
        mesh=mesh,
        scratch_types=[pltpu.VMEM((SC_TOKENS * SLOTS,), jnp.int32),
                       pltpu.VMEM((SC_TOKENS * SLOTS,), F32),
                       pltpu.VMEM((SC_ROWS, D_MODEL), F32),
                       pltpu.VMEM((SC_ROWS, D_MODEL), F32),
                       pltpu.VMEM((SC_TOKENS, D_MODEL), F32),
                       pltpu.SemaphoreType.DMA,
                       pltpu.SemaphoreType.DMA],
        compiler_params=pltpu.CompilerParams(needs_layout_passes=False),
        name="sc_peer_v",
    )(idx.reshape(n * SLOTS), c.reshape(n * SLOTS), table)


def _final_kernel(x1_ref, yh_ref, yt_ref, pp_ref, ps_ref, gple_ref, wg_ref, wp_ref, gfin_ref, op_ref, os_ref):
    x2 = x1_ref[...] + _split_rows(yh_ref, yt_ref, N_SC // TM)
    gate = jax.nn.sigmoid(_dot(_rms(x2, gple_ref[...]).astype(BF16), wg_ref[...]))
    x3 = x2 + gate * _dot(_group_rows(pp_ref, ps_ref).astype(BF16), wp_ref[...])
    out = _rms(x3, gfin_ref[...])
    is_prompt = pl.program_id(0) < P_TILES

    @pl.when(is_prompt)
    def _():
        op_ref[...] = out

    @pl.when(jnp.logical_not(is_prompt))
    def _():
        os_ref[...] = out


def _final(x1, y_head, y_tail, pp, ps, gple, wg, wp, gfin):
    return pl.pallas_call(
        _final_kernel,
        out_shape=[jax.ShapeDtypeStruct((N_P, D_MODEL), F32), jax.ShapeDtypeStruct((N_S, D_MODEL), F32)],
        grid=(N_TOK // TM,),
        in_specs=[_rows(D_MODEL), _rows_head(D_MODEL, N_SC // TM), _rows_tail(D_MODEL, N_SC // TM),
                  _rows_prompt(PLE_DIM), _rows_sample(PLE_DIM),
                  _const(gple.shape), _const(wg.shape), _const(wp.shape), _const(gfin.shape)],
        out_specs=[_rows_prompt(D_MODEL), _rows_sample(D_MODEL)],
        compiler_params=_params(),
        name="ple_final",
    )(x1, y_head, y_tail, pp, ps, gple, wg, wp, gfin)


def _rope_tables():
    half = ROPE // 2
    freqs = ROPE_THETA ** (-np.arange(half, dtype=np.float32) / half)
    pos = np.concatenate([np.arange(SEQ), np.tile(PAST_LEN + np.arange(DEC_SEQ), TM // DEC_SEQ)])
    ang = jnp.asarray(pos, F32)[:, None] * jnp.asarray(freqs, F32)[None, :]
    cos, sin = jnp.cos(ang), jnp.sin(ang)
    n = pos.shape[0]
    pad = jnp.zeros((n, LANES - NOPE - ROPE), F32)
    cos_t = jnp.concatenate([jnp.ones((n, NOPE), F32), cos, cos, pad], axis=1)
    sin_t = jnp.concatenate([jnp.zeros((n, NOPE), F32), sin, sin, pad], axis=1)
    return cos_t, sin_t


def _rot_cols(w):
    half = ROPE // 2
    return jnp.concatenate([-w[..., half:], w[..., :half]], axis=-1)


def _rel_bias(table, n_q, n_k):
    diag = np.arange(-(n_q - 1), n_k)
    line = table[:, np.clip(WINDOW_B - diag, -REL_CLIP, REL_CLIP) + REL_CLIP]
    return jnp.stack([line[:, n_q - 1 - i:n_q - 1 - i + n_k] for i in range(n_q)], axis=1)


def kernel(x_prompt, x_sample, cache_latent, cache_krope, cache_band_k, cache_band_v, p_prompt, p_sample, g_mix, w_in, g_q_lora, g_kv_lora, w_uq, w_uk, w_uv, rel_bias, w_a_proj, w_b_proj, w_out, g_ffn, w_query, sub_keys, expert_u, expert_v, g_ple, w_ple_gate, w_ple_proj, g_final):
    w = w_in[0]
    o_kr = Q_LORA + KV_LORA
    o_qb = o_kr + ROPE
    o_ga = o_qb + 3 * HB
    w_kr = w[:, o_kr:o_qb]
    z64 = jnp.zeros((D_MODEL, NOPE), F32)
    z32 = jnp.zeros((D_MODEL, LANES - NOPE - ROPE), F32)
    w1 = jnp.concatenate([w[:, :o_kr], z64, w_kr, z32, z64, _rot_cols(w_kr), z32], axis=1).astype(BF16)
    w2 = w[:, o_qb:o_ga].astype(BF16)
    w3 = w[:, o_ga:].astype(BF16)
    wq = w_uq[0].reshape(Q_LORA, H_A, NOPE + ROPE)
    zq64 = jnp.zeros((Q_LORA, H_A, NOPE), F32)
    zq32 = jnp.zeros((Q_LORA, H_A, LANES - NOPE - ROPE), F32)
    wuq1 = jnp.concatenate([wq, zq32], axis=-1).reshape(Q_LORA, H_A * LANES).astype(BF16)
    wuq2 = jnp.concatenate([zq64, _rot_cols(wq[..., NOPE:]), zq32], axis=-1).reshape(Q_LORA, H_A * LANES).astype(BF16)
    wukp = jnp.concatenate([w_uk[0], jnp.zeros((KV_LORA, H_A, LANES - NOPE), F32)], axis=-1)
    wukp = wukp.reshape(KV_LORA, H_A * LANES).astype(BF16)
    wuv = w_uv[0].reshape(KV_LORA, H_A * V_A).astype(BF16)
    wukt = jnp.transpose(w_uk[0], (1, 2, 0)).astype(BF16)
    wuvh = jnp.transpose(w_uv[0], (1, 0, 2)).astype(BF16)
    cos_t, sin_t = _rope_tables()
    row = lambda g: g.reshape(1, -1)

    xp = x_prompt.reshape(N_P, D_MODEL)
    xs = x_sample.reshape(N_S, D_MODEL)

    (lat, kr, q16, k16, v16, qb16, kb16, vb16, kb32, vb32, sga, sgb) = _inproj(
        xp, xs, row(g_mix[0]), w1, w2, w3, row(g_q_lora[0]), row(g_kv_lora[0]), wuq1, wuq2, wukp, wuv, cos_t, sin_t)

    oa_p = _mla_prompt(q16, k16, v16)
    oa_s = _mla_sample(q16, cache_latent[0], cache_krope[0], lat, kr, wukt, wuvh)

    pad = ((0, 0), (WINDOW_B, 0), (0, 0))
    kpad = jnp.pad(kb16[:N_P].reshape(BATCH, SEQ, HB), pad)
    vpad = jnp.pad(vb16[:N_P].reshape(BATCH, SEQ, HB), pad)
    tab = rel_bias[0]
    bias_p = _rel_bias(tab, CHUNK, BAND)
    bias_s = _rel_bias(tab, DEC_SEQ, WINDOW_B + DEC_SEQ)
    ob_p = _band_prompt(qb16, kpad, vpad, bias_p)
    ob_s, bk_s, bv_s = _band_sample(qb16, cache_band_k[0].reshape(DEC_BATCH, WINDOW_B, HB),
                                    cache_band_v[0].reshape(DEC_BATCH, WINDOW_B, HB),
                                    kb16, vb16, kb32, vb32, bias_s[:, :, :WINDOW_B], bias_s[:, :, WINDOW_B:])

    oa = jnp.concatenate([oa_p, oa_s], axis=0)
    ob = jnp.concatenate([ob_p, ob_s], axis=0)
    x1, h2, h16 = _mix_out(oa, ob, sga, sgb, xp, xs, w_a_proj[0].astype(BF16), w_b_proj[0].astype(BF16),
                           w_out[0].astype(BF16), row(g_ffn[0]))
    wqt = jnp.transpose(w_query[0]).astype(BF16)
    keys = sub_keys[0].reshape(2 * PEER_HEADS, N_KEYS, N_KEYS).astype(BF16)
    idx, gw = _peer_topk(h16, wqt, keys)

    tab_u = _pack_table(expert_u[0])
    tab_v = _pack_table(expert_v[0])
    c_sc = _peer_u(idx, h2, gw, tab_u, 0, N_SC)
    y_sc = _sc_peer_v(idx[:N_SC] // ROW_WORDS, c_sc, expert_v[0])
    c_tc = _peer_u(idx, h2, gw, tab_u, N_SC, N_TOK - N_SC)
    y_tc = _peer_v(idx, c_tc, tab_v, N_SC)

    out_p, out_s = _final(x1, y_sc, y_tc, p_prompt.reshape(N_P, PLE_DIM), p_sample.reshape(N_S, PLE_DIM), row(g_ple[0]),
                          w_ple_gate[0].astype(BF16), w_ple_proj[0].astype(BF16), row(g_final))

    y_prompt = out_p.reshape(BATCH, SEQ, D_MODEL)
    y_sample = out_s.reshape(DEC_BATCH, DEC_SEQ, D_MODEL)
    lat_p = lat[:N_P].reshape(1, BATCH, SEQ, KV_LORA)
    kr_p = kr[:N_P].reshape(1, BATCH, SEQ, ROPE)
    bk_p = kb32[:N_P].reshape(BATCH, SEQ, H_B, DH_B)[None, :, SEQ - WINDOW_B:]
    bv_p = vb32[:N_P].reshape(BATCH, SEQ, H_B, DH_B)[None, :, SEQ - WINDOW_B:]
    lat_s = lat[N_P:].reshape(1, DEC_BATCH, DEC_SEQ, KV_LORA)
    kr_s = kr[N_P:].reshape(1, DEC_BATCH, DEC_SEQ, ROPE)
    bk_s = bk_s.reshape(1, DEC_BATCH, WINDOW_B, H_B, DH_B)
    bv_s = bv_s.reshape(1, DEC_BATCH, WINDOW_B, H_B, DH_B)
    return (y_prompt, y_sample, lat_p, kr_p, bk_p, bv_p, lat_s, kr_s, bk_s, bv_s)
```

```python
import functools
import jax
import jax.numpy as jnp
import numpy as np
from jax import lax
from jax.experimental import pallas as pl
from jax.experimental.pallas import tpu as pltpu
from jax.experimental.pallas import tpu_sc as plsc

D_MODEL = 1024
BATCH = 16
SEQ = 2048
DEC_BATCH = 32
DEC_SEQ = 32
PAST_LEN = 4096
CHUNK = 64
EPS = 1e-6
H_A = 8
Q_LORA = 384
KV_LORA = 256
NOPE = 64
ROPE = 32
V_A = 64
ROPE_THETA = 10000.0
MLA_SCALE = (NOPE + ROPE) ** -0.5
H_B = 8
DH_B = 64
BAND_PREV = 8
BAND = (BAND_PREV + 1) * CHUNK
WINDOW_B = BAND_PREV * CHUNK
REL_CLIP = 256
BAND_SCALE = DH_B ** -0.5
PEER_HEADS = 8
N_KEYS = 128
N_EXPERTS = N_KEYS * N_KEYS
TOPK = 16
PLE_DIM = 256

N_P = BATCH * SEQ
N_S = DEC_BATCH * DEC_SEQ
N_TOK = N_P + N_S
HB = H_B * DH_B
HALF = D_MODEL // 2
SLOTS = PEER_HEADS * TOPK
LANES = 128
SUBLANES = 8
TM = 256
TQ = 128
KV_STEP = 512
TB = 64
CREP_RING = 4
ROW_WORDS = HALF // LANES
PAIRS = SLOTS // 2
STRIDE = 72
HEADS_PER_TRIP = 8
IDX_GROUP = 8
VMEM_LIMIT = 56 * 1024 * 1024
SC_WORKERS = 32
SC_LANES = 16
SC_TOKENS = 8
SC_ROWS = 32
SC_ACC = 8
SC_WAVES = (4096, 17408)
NEG = -1e30
BF16 = jnp.bfloat16
F32 = jnp.float32
NT = (((1,), (1,)), ((), ()))


def _params(n_axes=1):
    return pltpu.CompilerParams(dimension_semantics=("arbitrary",) * n_axes,
                                vmem_limit_bytes=VMEM_LIMIT)


def _const(shape):
    nd = len(shape)
    return pl.BlockSpec(shape, lambda *_: (0,) * nd, pipeline_mode=pl.Buffered(1))


def _rows(width, tile=TM):
    return pl.BlockSpec((tile, width), lambda i: (i, 0))


P_TILES = N_P // TM


def _rows_head(width, tiles):
    return pl.BlockSpec((TM, width), lambda i: (jnp.minimum(i, tiles - 1), 0))


def _rows_tail(width, tiles):
    return pl.BlockSpec((TM, width), lambda i: (jnp.maximum(i - tiles, 0), 0))


def _rows_prompt(width):
    return _rows_head(width, P_TILES)


def _rows_sample(width):
    return _rows_tail(width, P_TILES)


def _split_rows(head_ref, tail_ref, tiles):
    return jnp.where(pl.program_id(0) < tiles, head_ref[...], tail_ref[...])


def _group_rows(p_ref, s_ref):
    return _split_rows(p_ref, s_ref, P_TILES)


def _rms(x, g):
    return x * lax.rsqrt(jnp.mean(x * x, axis=-1, keepdims=True) + EPS) * g


def _dot(a, b):
    return jnp.dot(a, b, preferred_element_type=F32)


def _softmax_rows(s_list):
    m = functools.reduce(jnp.maximum, [jnp.max(s, axis=-1, keepdims=True) for s in s_list])
    p_list = [jnp.exp(s - m) for s in s_list]
    inv = 1.0 / functools.reduce(jnp.add, [jnp.sum(p, axis=-1, keepdims=True) for p in p_list])
    return [(p * inv).astype(BF16) for p in p_list]


def _inproj_kernel(xp_ref, xs_ref, gmix_ref, w1_ref, w2_ref, w3_ref, gq_ref, gkv_ref, wuq1_ref, wuq2_ref,
                   wukp_ref, wuv_ref, cos_ref, sin_ref,
                   lat_ref, kr_ref, q16_ref, k16_ref, v16_ref, qb16_ref, kb16_ref, vb16_ref,
                   kb32_ref, vb32_ref, sga_ref, sgb_ref):
    h = _rms(_group_rows(xp_ref, xs_ref), gmix_ref[...]).astype(BF16)
    cos = cos_ref[...]
    sin = sin_ref[...]
    z1 = _dot(h, w1_ref[...])
    k128 = z1[:, 640:768] * cos + z1[:, 768:896] * sin
    kr_ref[...] = k128[:, NOPE:NOPE + ROPE]
    cqn = _rms(z1[:, :Q_LORA], gq_ref[...]).astype(BF16)
    cos8 = jnp.concatenate([cos] * H_A, axis=1)
    sin8 = jnp.concatenate([sin] * H_A, axis=1)
    q = _dot(cqn, wuq1_ref[...]) * cos8 + _dot(cqn, wuq2_ref[...]) * sin8
    q16_ref[...] = q.astype(BF16)
    ckvn = _rms(z1[:, Q_LORA:Q_LORA + KV_LORA], gkv_ref[...])
    lat_ref[...] = ckvn
    ckvn16 = ckvn.astype(BF16)
    kk = _dot(ckvn16, wukp_ref[...]) + jnp.concatenate([k128] * H_A, axis=1)
    k16_ref[...] = kk.astype(BF16)
    v16_ref[...] = _dot(ckvn16, wuv_ref[...]).astype(BF16)
    z2 = _dot(h, w2_ref[...])
    qb16_ref[...] = z2[:, :HB].astype(BF16)
    kb = z2[:, HB:2 * HB]
    vb = z2[:, 2 * HB:]
    kb32_ref[...] = kb
    vb32_ref[...] = vb
    kb16_ref[...] = kb.astype(BF16)
    vb16_ref[...] = vb.astype(BF16)
    z3 = _dot(h, w3_ref[...])
    sga_ref[...] = jax.nn.sigmoid(z3[:, :D_MODEL])
    sgb_ref[...] = jax.nn.sigmoid(z3[:, D_MODEL:])


def _rope_rows():
    per_seq = SEQ // TM
    return pl.BlockSpec((TM, LANES), lambda i: (jnp.where(i < N_P // TM, i % per_seq, per_seq), 0))


def _inproj(xp, xs, gmix, w1, w2, w3, gq, gkv, wuq1, wuq2, wukp, wuv, cos, sin):
    n = N_TOK
    widths = [(KV_LORA, F32), (ROPE, F32), (H_A * LANES, BF16), (H_A * LANES, BF16), (H_A * V_A, BF16),
              (HB, BF16), (HB, BF16), (HB, BF16), (HB, F32), (HB, F32), (D_MODEL, F32), (D_MODEL, F32)]
    return pl.pallas_call(
        _inproj_kernel,
        out_shape=[jax.ShapeDtypeStruct((n, w), d) for w, d in widths],
        grid=(n // TM,),
        in_specs=[_rows_prompt(D_MODEL), _rows_sample(D_MODEL),
                  _const(gmix.shape), _const(w1.shape), _const(w2.shape), _const(w3.shape),
                  _const(gq.shape), _const(gkv.shape), _const(wuq1.shape), _const(wuq2.shape),
                  _const(wukp.shape), _const(wuv.shape), _rope_rows(), _rope_rows()],
        out_specs=[_rows(w) for w, _ in widths],
        compiler_params=_params(),
        name="inproj",
    )(xp, xs, gmix, w1, w2, w3, gq, gkv, wuq1, wuq2, wukp, wuv, cos, sin)


def _pair_select(o_even, o_odd):
    lane = lax.broadcasted_iota(jnp.int32, o_even.shape, 1)
    return jnp.where(lane < V_A, o_even, o_odd)


def _mla_prompt_tile(q_ref, k_ref, v_ref, o_ref, i, nk):
    q_chunk = (i * TQ + lax.broadcasted_iota(jnp.int32, (TQ, nk), 0)) // CHUNK
    k_chunk = lax.broadcasted_iota(jnp.int32, (TQ, nk), 1) // CHUNK
    mask = k_chunk <= q_chunk
    scores = []
    for hd in range(H_A):
        qh = q_ref[:, hd * LANES:(hd + 1) * LANES]
        kh = k_ref[0:nk, hd * LANES:(hd + 1) * LANES]
        s = lax.dot_general(qh, kh, NT, preferred_element_type=F32) * MLA_SCALE
        scores.append(jnp.where(mask, s, NEG))
    weights = [_softmax_rows([s])[0] for s in scores]
    outs = [_dot(weights[hd], v_ref[0:nk, (hd // 2) * LANES:(hd // 2 + 1) * LANES]) for hd in range(H_A)]
    pairs = [_pair_select(outs[2 * pr], outs[2 * pr + 1]) for pr in range(H_A // 2)]
    o_ref[...] = jnp.concatenate(pairs, axis=1).astype(BF16)


def _mla_prompt_kernel(q_ref, k_ref, v_ref, o_ref):
    i = pl.program_id(1)
    tiles_per_step = KV_STEP // TQ
    for grp in range(SEQ // KV_STEP):
        @pl.when(i // tiles_per_step == grp)
        def _():
            _mla_prompt_tile(q_ref, k_ref, v_ref, o_ref, i, (grp + 1) * KV_STEP)


def _mla_prompt(q16, k16, v16):
    nq = SEQ // TQ
    return pl.pallas_call(
        _mla_prompt_kernel,
        out_shape=jax.ShapeDtypeStruct((N_P, H_A * V_A), BF16),
        grid=(BATCH, nq),
        in_specs=[pl.BlockSpec((TQ, H_A * LANES), lambda b, i: (b * nq + i, 0)),
                  pl.BlockSpec((SEQ, H_A * LANES), lambda b, i: (b, 0)),
                  pl.BlockSpec((SEQ, H_A * V_A), lambda b, i: (b, 0))],
        out_specs=pl.BlockSpec((TQ, H_A * V_A), lambda b, i: (b * nq + i, 0)),
        compiler_params=_params(2),
        name="mla_prompt",
    )(q16, k16, v16)


def _mla_sample_kernel(q_ref, clat_ref, ckr_ref, nlat_ref, nkr_ref, wukt_ref, wuvh_ref, o_ref):
    q = q_ref[...].astype(F32)
    qlat, qrope = [], []
    for hd in range(H_A):
        qn = q[:, hd * LANES:hd * LANES + NOPE].astype(BF16)
        qlat.append(_dot(qn, wukt_ref[hd]).astype(BF16))
        qrope.append(q[:, hd * LANES + NOPE:hd * LANES + NOPE + ROPE].astype(BF16))
    qlat = jnp.concatenate(qlat, axis=0)
    qrope = jnp.concatenate(qrope, axis=0)
    clat = clat_ref[...].astype(BF16)
    ckr = ckr_ref[...].astype(BF16)
    nlat = nlat_ref[...].astype(BF16)
    nkr = nkr_ref[...].astype(BF16)
    s_c = (lax.dot_general(qlat, clat, NT, preferred_element_type=F32)
           + lax.dot_general(qrope, ckr, NT, preferred_element_type=F32)) * MLA_SCALE
    s_n = (lax.dot_general(qlat, nlat, NT, preferred_element_type=F32)
           + lax.dot_general(qrope, nkr, NT, preferred_element_type=F32)) * MLA_SCALE
    w_c, w_n = _softmax_rows([s_c, s_n])
    olat = (_dot(w_c, clat) + _dot(w_n, nlat)).astype(BF16)
    outs = [_dot(olat[hd * DEC_SEQ:(hd + 1) * DEC_SEQ, :], wuvh_ref[hd]) for hd in range(H_A)]
    o_ref[...] = jnp.concatenate(outs, axis=1).astype(BF16)


def _mla_sample(q16, cache_lat, cache_kr, lat, kr, wukt, wuvh):
    off = N_P // DEC_SEQ
    return pl.pallas_call(
        _mla_sample_kernel,
        out_shape=jax.ShapeDtypeStruct((N_S, H_A * V_A), BF16),
        grid=(DEC_BATCH,),
        in_specs=[pl.BlockSpec((DEC_SEQ, H_A * LANES), lambda b: (off + b, 0)),
                  pl.BlockSpec((None, PAST_LEN, KV_LORA), lambda b: (b, 0, 0)),
                  pl.BlockSpec((None, PAST_LEN, ROPE), lambda b: (b, 0, 0)),
                  pl.BlockSpec((DEC_SEQ, KV_LORA), lambda b: (off + b, 0)),
                  pl.BlockSpec((DEC_SEQ, ROPE), lambda b: (off + b, 0)),
                  _const(wukt.shape), _const(wuvh.shape)],
        out_specs=pl.BlockSpec((DEC_SEQ, H_A * V_A), lambda b: (b, 0)),
        compiler_params=_params(),
        name="mla_sample",
    )(q16, cache_lat, cache_kr, lat, kr, wukt, wuvh)


def _band_heads(q, blocks, bias_refs, valid):
    lane = lax.broadcasted_iota(jnp.int32, (q.shape[0], LANES), 1)
    scores = []
    for hd in range(H_B):
        sl = slice((hd // 2) * LANES, (hd // 2 + 1) * LANES)
        q2 = q[:, sl]
        own = (lane >= DH_B) if hd % 2 else (lane < DH_B)
        qm = jnp.where(own, q2, jnp.zeros_like(q2))
        ss = [lax.dot_general(qm, k[:, sl], NT, preferred_element_type=F32) * BAND_SCALE + b_ref[hd]
              for (k, _), b_ref in zip(blocks, bias_refs)]
        if valid is not None:
            ss = [jnp.where(valid, s, NEG) for s in ss]
        scores.append(ss)
    weights = [_softmax_rows(ss) for ss in scores]
    outs = []
    for hd in range(H_B):
        sl = slice((hd // 2) * LANES, (hd // 2 + 1) * LANES)
        outs.append(functools.reduce(jnp.add, [_dot(w, v[:, sl]) for w, (_, v) in zip(weights[hd], blocks)]))
    pairs = [_pair_select(outs[2 * pr], outs[2 * pr + 1]) for pr in range(H_B // 2)]
    return jnp.concatenate(pairs, axis=1).astype(BF16)


def _band_prompt_kernel(q_ref, k_ref, v_ref, bias_ref, o_ref):
    c = pl.program_id(1)
    start = pl.multiple_of(c * CHUNK, CHUNK)
    k = k_ref[pl.ds(start, BAND), :]
    v = v_ref[pl.ds(start, BAND), :]
    valid = (start - WINDOW_B + lax.broadcasted_iota(jnp.int32, (CHUNK, BAND), 1)) >= 0
    o_ref[...] = _band_heads(q_ref[...], [(k, v)], [bias_ref], valid)


def _band_prompt(qb16, kpad, vpad, bias):
    nc = SEQ // CHUNK
    return pl.pallas_call(
        _band_prompt_kernel,
        out_shape=jax.ShapeDtypeStruct((N_P, HB), BF16),
        grid=(BATCH, nc),
        in_specs=[pl.BlockSpec((CHUNK, HB), lambda b, c: (b * nc + c, 0)),
                  pl.BlockSpec((None, SEQ + WINDOW_B, HB), lambda b, c: (b, 0, 0)),
                  pl.BlockSpec((None, SEQ + WINDOW_B, HB), lambda b, c: (b, 0, 0)),
                  _const(bias.shape)],
        out_specs=pl.BlockSpec((CHUNK, HB), lambda b, c: (b * nc + c, 0)),
        compiler_params=_params(2),
        name="band_prompt",
    )(qb16, kpad, vpad, bias)


def _band_sample_kernel(q_ref, ck_ref, cv_ref, nk16_ref, nv16_ref, nk32_ref, nv32_ref, bias_c_ref, bias_n_ref,
                        o_ref, bk_ref, bv_ref):
    ck = ck_ref[...]
    cv = cv_ref[...]
    blocks = [(ck.astype(BF16), cv.astype(BF16)), (nk16_ref[...], nv16_ref[...])]
    o_ref[...] = _band_heads(q_ref[...], blocks, [bias_c_ref, bias_n_ref], None)
    keep = WINDOW_B - DEC_SEQ
    bk_ref[0:keep, :] = ck[DEC_SEQ:, :]
    bk_ref[keep:, :] = nk32_ref[...]
    bv_ref[0:keep, :] = cv[DEC_SEQ:, :]
    bv_ref[keep:, :] = nv32_ref[...]


def _band_sample(qb16, cache_k, cache_v, kb16, vb16, kb32, vb32, bias_c, bias_n):
    off = N_P // DEC_SEQ
    new = lambda: pl.BlockSpec((DEC_SEQ, HB), lambda b: (off + b, 0))
    cache = lambda: pl.BlockSpec((None, WINDOW_B, HB), lambda b: (b, 0, 0))
    return pl.pallas_call(
        _band_sample_kernel,
        out_shape=[jax.ShapeDtypeStruct((N_S, HB), BF16),
                   jax.ShapeDtypeStruct((DEC_BATCH, WINDOW_B, HB), F32),
                   jax.ShapeDtypeStruct((DEC_BATCH, WINDOW_B, HB), F32)],
        grid=(DEC_BATCH,),
        in_specs=[new(), cache(), cache(), new(), new(), new(), new(),
                  _const(bias_c.shape), _const(bias_n.shape)],
        out_specs=[pl.BlockSpec((DEC_SEQ, HB), lambda b: (b, 0)), cache(), cache()],
        compiler_params=_params(),
        name="band_sample",
    )(qb16, cache_k, cache_v, kb16, vb16, kb32, vb32, bias_c, bias_n)


def _mix_out_kernel(oa_ref, ob_ref, sga_ref, sgb_ref, xp_ref, xs_ref, wa_ref, wb_ref, wout_ref, gffn_ref,
                    x1_ref, h2_ref, h16_ref):
    merged = sga_ref[...] * _dot(oa_ref[...], wa_ref[...]) + sgb_ref[...] * _dot(ob_ref[...], wb_ref[...])
    x1 = _group_rows(xp_ref, xs_ref) + _dot(merged.astype(BF16), wout_ref[...])
    x1_ref[...] = x1
    h2 = _rms(x1, gffn_ref[...])
    h2_ref[...] = h2
    h16_ref[...] = h2.astype(BF16)


def _mix_out(oa, ob, sga, sgb, xp, xs, wa, wb, wout, gffn):
    n = N_TOK
    return pl.pallas_call(
        _mix_out_kernel,
        out_shape=[jax.ShapeDtypeStruct((n, D_MODEL), F32), jax.ShapeDtypeStruct((n, D_MODEL), F32),
                   jax.ShapeDtypeStruct((n, D_MODEL), BF16)],
        grid=(n // TM,),
        in_specs=[_rows(H_A * V_A), _rows(HB), _rows(D_MODEL), _rows(D_MODEL),
                  _rows_prompt(D_MODEL), _rows_sample(D_MODEL),
                  _const(wa.shape), _const(wb.shape), _const(wout.shape), _const(gffn.shape)],
        out_specs=[_rows(D_MODEL)] * 3,
        compiler_params=_params(),
        name="mix_out",
    )(oa, ob, sga, sgb, xp, xs, wa, wb, wout, gffn)


def _top16(s, n_rows):
    row = lax.broadcasted_iota(jnp.int32, (n_rows, TQ), 0).astype(F32)
    vals, ids = [], []
    for _ in range(TOPK):
        m = jnp.max(s, axis=0, keepdims=True)
        i = jnp.min(jnp.where(s == m, row, float(n_rows)), axis=0, keepdims=True)
        vals.append(m)
        ids.append(i)
        s = jnp.where(row == i, -jnp.inf, s)
    return jnp.concatenate(vals, axis=0), jnp.concatenate(ids, axis=0)


def _pair_candidates(f1, f2):
    h = SUBLANES
    blocks = [f1(0, 1, 0, h), f1(0, 1, h, 2 * h)]
    blocks += [f1(a, a + 1, 0, h) for a in range(1, h)]
    blocks += [f2(h, 2 * h, 0, 1)]
    return jnp.concatenate(blocks, axis=0)


def _peer_topk_kernel(h_ref, wqt_ref, keys_ref, idx_ref, g_ref, qt_ref, g_t, id_t):
    qt_ref[...] = lax.dot_general(wqt_ref[...], h_ref[...], NT, preferred_element_type=F32).astype(BF16)
    r8 = lax.broadcasted_iota(jnp.int32, (SUBLANES, TQ), 0).astype(F32)
    flat = lambda a0, a1, b0, b1: (r8 + float(b0)) + float(TOPK * a0)
    flat_t = lambda a0, a1, b0, b1: (r8 + float(a0)) * float(TOPK) + float(b0)
    cflat = _pair_candidates(flat, flat_t)

    def one_head(hd):
        tops = []
        for p in range(2):
            hp = hd * 2 + p
            qs = qt_ref[pl.ds(pl.multiple_of(hp * N_KEYS, N_KEYS), N_KEYS), :]
            tops.append(_top16(_dot(keys_ref[hp], qs), N_KEYS))
        (s1, i1), (s2, i2) = tops
        add = lambda x, y: (lambda a0, a1, b0, b1: x[a0:a1, :] + y[b0:b1, :])
        cand = _pair_candidates(add(s1, s2), add(s1, s2))
        e1 = i1 * float(N_KEYS)
        cidx = _pair_candidates(add(e1, i2), add(e1, i2))
        vals, ids = [], []
        for _ in range(TOPK):
            m = jnp.max(cand, axis=0, keepdims=True)
            c = jnp.min(jnp.where(cand == m, cflat, float(TOPK * TOPK)), axis=0, keepdims=True)
            sel = cflat == c
            vals.append(m)
            ids.append(jnp.sum(jnp.where(sel, cidx, 0.0), axis=0, keepdims=True))
            cand = jnp.where(sel, -jnp.inf, cand)
        best = jnp.concatenate(vals, axis=0)
        e = jnp.exp(best - best[0:1, :])
        g = e / jnp.sum(e, axis=0, keepdims=True)
        base = pl.multiple_of(hd * TOPK, TOPK)
        g_t[pl.ds(base, TOPK), :] = g
        id_t[pl.ds(base, TOPK), :] = jnp.concatenate(ids, axis=0)

    def head_group(grp, carry):
        for e in range(HEADS_PER_TRIP):
            one_head(grp * HEADS_PER_TRIP + e)
        return carry

    lax.fori_loop(0, PEER_HEADS // HEADS_PER_TRIP, head_group, 0)
    g_ref[...] = g_t[...].T
    idx_ref[...] = (id_t[...].T * float(ROW_WORDS)).astype(jnp.int32)


def _peer_topk(h16, wqt, keys):
    n = h16.shape[0]
    nt = n // TQ
    return pl.pallas_call(
        _peer_topk_kernel,
        out_shape=[jax.ShapeDtypeStruct((n, SLOTS), jnp.int32),
                   jax.ShapeDtypeStruct((n, SLOTS), F32)],
        grid=(nt,),
        in_specs=[_rows(D_MODEL, TQ), _const(wqt.shape), _const(keys.shape)],
        out_specs=[_rows(SLOTS, TQ)] * 2,
        scratch_shapes=[pltpu.VMEM((2 * PEER_HEADS * N_KEYS, TQ), BF16),
                        pltpu.VMEM((SLOTS, TQ), F32), pltpu.VMEM((SLOTS, TQ), F32)],
        compiler_params=_params(),
        name="peer_topk",
    )(h16, wqt, keys)


def _pack_table(t):
    b = lax.bitcast_convert_type(t.astype(BF16), jnp.uint16).astype(jnp.uint32)
    packed = (b[:, :HALF] << 16) | b[:, HALF:]
    return packed.reshape(N_EXPERTS * ROW_WORDS, LANES)


def _unpack(w):
    hi = lax.bitcast_convert_type(w & jnp.uint32(0xFFFF0000), F32)
    lo = lax.bitcast_convert_type(w << 16, F32)
    return hi, lo


def _slot_pairs(idx_ref, t):
    out = []
    for grp in range(PAIRS // IDX_GROUP):
        lo = idx_ref.at[0, pl.ds(t * SLOTS + grp * IDX_GROUP, IDX_GROUP)]
        hi = idx_ref.at[0, pl.ds(t * SLOTS + PAIRS + grp * IDX_GROUP, IDX_GROUP)]
        out.extend((lo[k], hi[k]) for k in range(IDX_GROUP))
    return out


def _load_pair(tab, ia, ib):
    wa = tab[pl.ds(pl.multiple_of(ia, ROW_WORDS), ROW_WORDS), :]
    wb = tab[pl.ds(pl.multiple_of(ib, ROW_WORDS), ROW_WORDS), :]
    return _unpack(jnp.concatenate([wa, wb], axis=0))


def _peer_u_kernel(idx_ref, x_ref, g_ref, tab, c_ref, prod_a, prod_b, rbuf, a_t):
    lane = lax.broadcasted_iota(jnp.int32, (SLOTS, LANES), 1)

    @pl.when(pl.program_id(0) == 0)
    def _():
        rbuf[...] = jnp.zeros_like(rbuf)
        a_t[...] = jnp.zeros_like(a_t)

    def reduce(slot, tok):
        col = jnp.sum(rbuf[slot], axis=1, keepdims=True)
        a_t[...] = jnp.where(lane == tok, col, a_t[...])

    def gather(t, slot, prod):
        xt = x_ref[t]
        x_hi = jnp.concatenate([xt[0:ROW_WORDS, :]] * 2, axis=0)
        x_lo = jnp.concatenate([xt[ROW_WORDS:, :]] * 2, axis=0)
        for j, (ia, ib) in enumerate(_slot_pairs(idx_ref, t)):
            hi, lo = _load_pair(tab, ia, ib)
            prod[pl.ds(j, SUBLANES, stride=STRIDE), :] = hi * x_hi + lo * x_lo
        halves = [functools.reduce(jnp.add, [prod[pl.ds((h * ROW_WORDS + s) * STRIDE, PAIRS), :]
                                             for s in range(ROW_WORDS)]) for h in range(2)]
        rbuf[slot] = jnp.concatenate(halves, axis=0)

    def two_tokens(p, carry):
        t0 = 2 * p
        reduce(0, t0 - 2)
        reduce(1, t0 - 1)
        gather(t0, 0, prod_a)
        gather(t0 + 1, 1, prod_b)
        return carry

    lax.fori_loop(0, TB // 2, two_tokens, 0)
    reduce(0, TB - 2)
    reduce(1, TB - 1)
    a = a_t[...].T[0:TB, :]
    gelu = 0.5 * a * (1.0 + lax.erf(a * (2.0 ** -0.5)))
    c_ref[...] = g_ref[...] * gelu


def _peer_u(idx, x, g, tab_u, first, count):
    n = x.shape[0]
    nb = count // TB
    fb = first // TB
    return pl.pallas_call(
        _peer_u_kernel,
        out_shape=jax.ShapeDtypeStruct((count, SLOTS), F32),
        grid=(nb,),
        in_specs=[pl.BlockSpec((None, 1, TB * SLOTS), lambda i: (fb + i, 0, 0), memory_space=pltpu.SMEM),
                  pl.BlockSpec((TB, SUBLANES, LANES), lambda i: (fb + i, 0, 0)),
                  pl.BlockSpec((TB, SLOTS), lambda i: (fb + i, 0)),
                  _const(tab_u.shape)],
        out_specs=pl.BlockSpec((TB, SLOTS), lambda i: (i, 0)),
        scratch_shapes=[pltpu.VMEM((SUBLANES * STRIDE, LANES), F32),
                        pltpu.VMEM((SUBLANES * STRIDE, LANES), F32),
                        pltpu.VMEM((2, SLOTS, LANES), F32),
                        pltpu.VMEM((SLOTS, LANES), F32)],
        compiler_params=_params(),
        name="peer_u",
    )(idx.reshape(n // TB, 1, TB * SLOTS), x.reshape(n, SUBLANES, LANES), g, tab_u)


def _peer_v_kernel(idx_ref, c_ref, tab, y_ref, *crep):
    row = lax.broadcasted_iota(jnp.int32, (SUBLANES, LANES), 0)
    eye = (lax.broadcasted_iota(jnp.int32, (SLOTS, LANES), 0)
           == lax.broadcasted_iota(jnp.int32, (SLOTS, LANES), 1))

    def spread(t, crep):
        col = jnp.sum(jnp.where(eye, c_ref[pl.ds(t, 1), :], 0.0), axis=1, keepdims=True)
        crep[...] = jnp.broadcast_to(col, (SLOTS, LANES))

    def token(t, crep):
        acc_hi = jnp.zeros((SUBLANES, LANES), F32)
        acc_lo = jnp.zeros((SUBLANES, LANES), F32)
        for j, (ia, ib) in enumerate(_slot_pairs(idx_ref, t)):
            hi, lo = _load_pair(tab, ia, ib)
            cm = jnp.where(row < ROW_WORDS, crep[pl.ds(j, 1), :], crep[pl.ds(PAIRS + j, 1), :])
            acc_hi = acc_hi + cm * hi
            acc_lo = acc_lo + cm * lo
        y_ref[t] = jnp.concatenate([acc_hi[0:ROW_WORDS, :] + acc_hi[ROW_WORDS:, :],
                                    acc_lo[0:ROW_WORDS, :] + acc_lo[ROW_WORDS:, :]], axis=0)

    ahead = CREP_RING // 2
    for k in range(ahead):
        spread(k, crep[k])

    def ring_trip(p, carry):
        t0 = CREP_RING * p
        for k in range(CREP_RING):
            spread(jnp.minimum(t0 + k + ahead, TB - 1), crep[(k + ahead) % CREP_RING])
            token(t0 + k, crep[k])
        return carry

    lax.fori_loop(0, TB // CREP_RING, ring_trip, 0)


def _peer_v(idx, c, tab_v, first):
    n = c.shape[0]
    nb = n // TB
    fb = first // TB
    y = pl.pallas_call(
        _peer_v_kernel,
        out_shape=jax.ShapeDtypeStruct((n, SUBLANES, LANES), F32),
        grid=(nb,),
        in_specs=[pl.BlockSpec((None, 1, TB * SLOTS), lambda i: (fb + i, 0, 0), memory_space=pltpu.SMEM),
                  pl.BlockSpec((TB, SLOTS), lambda i: (i, 0)),
                  _const(tab_v.shape)],
        out_specs=pl.BlockSpec((TB, SUBLANES, LANES), lambda i: (i, 0, 0)),
        scratch_shapes=[pltpu.VMEM((SLOTS, LANES), F32)] * CREP_RING,
        compiler_params=_params(),
        name="peer_v",
    )(idx.reshape(idx.shape[0] // TB, 1, TB * SLOTS), c, tab_v)
    return y.reshape(n, D_MODEL)


def _sc_peer_v(idx, c, table):
    n = idx.shape[0]
    per_worker = n // SC_WORKERS
    batches = per_worker // SC_TOKENS
    groups = SC_TOKENS * SLOTS // SC_ROWS
    chunks = D_MODEL // SC_LANES
    mesh = plsc.VectorSubcoreMesh(core_axis_name="c", subcore_axis_name="s")

    def body(idx_hbm, c_hbm, tab_hbm, y_hbm, idx_v, c_v, rows_a, rows_b, y_v, sem_a, sem_b):
        wid = lax.axis_index("s") * 2 + lax.axis_index("c")
        zero = jnp.zeros((SC_LANES,), F32)

        def gather(g, rows, sem):
            return pltpu.make_async_copy(tab_hbm.at[idx_v.at[pl.ds(g * SC_ROWS, SC_ROWS)]], rows, sem)

        def accumulate(g, rows):
            tok = g // (SLOTS // SC_ROWS)
            ws = [plsc.load_gather(c_v, [jnp.full((SC_LANES,), g * SC_ROWS + r, jnp.int32)])
                  for r in range(SC_ROWS)]

            @pl.loop(0, chunks // SC_ACC)
            def _(cb):
                base = pl.multiple_of(cb * (SC_ACC * SC_LANES), SC_ACC * SC_LANES)
                acc = [zero] * SC_ACC
                for r in range(SC_ROWS):
                    for k in range(SC_ACC):
                        acc[k] = acc[k] + ws[r] * rows[r, pl.ds(base + k * SC_LANES, SC_LANES)]
                for k in range(SC_ACC):
                    plsc.addupdate(y_v.at[tok, pl.ds(base + k * SC_LANES, SC_LANES)], acc[k])

        @pl.loop(0, batches)
        def _(b):
            t0 = wid * per_worker + b * SC_TOKENS
            pltpu.sync_copy(idx_hbm.at[pl.ds(t0 * SLOTS, SC_TOKENS * SLOTS)], idx_v)
            pltpu.sync_copy(c_hbm.at[pl.ds(t0 * SLOTS, SC_TOKENS * SLOTS)], c_v)
            for t in range(SC_TOKENS):
                for ch in range(chunks):
                    y_v[t, pl.ds(ch * SC_LANES, SC_LANES)] = zero
            gather(0, rows_a, sem_a).start()

            @pl.loop(0, groups // 2)
            def _(h):
                g = 2 * h
                gather(g + 1, rows_b, sem_b).start()
                gather(g, rows_a, sem_a).wait()
                accumulate(g, rows_a)

                @pl.when(h + 1 < groups // 2)
                def _():
                    gather(g + 2, rows_a, sem_a).start()
                gather(g + 1, rows_b, sem_b).wait()
                accumulate(g + 1, rows_b)

            pltpu.sync_copy(y_v, y_hbm.at[pl.ds(t0, SC_TOKENS)])

    return pl.kernel(
        body,
        out_type=jax.ShapeDtypeStruct((n, D_MODEL), F32),
        mesh=mesh,
        scratch_types=[pltpu.VMEM((SC_TOKENS * SLOTS,), jnp.int32),
                       pltpu.VMEM((SC_TOKENS * SLOTS,), F32),
                       pltpu.VMEM((SC_ROWS, D_MODEL), F32),
                       pltpu.VMEM((SC_ROWS, D_MODEL), F32),
                       pltpu.VMEM((SC_TOKENS, D_MODEL), F32),
                       pltpu.SemaphoreType.DMA,
                       pltpu.SemaphoreType.DMA],
        compiler_params=pltpu.CompilerParams(needs_layout_passes=False),
        name="sc_peer_v",
    )(idx.reshape(n * SLOTS), c.reshape(n * SLOTS), table)


def _rows_segment(width, first, tiles):
    return pl.BlockSpec((TM, width), lambda i: (jnp.clip(i - first, 0, tiles - 1), 0))


def _final_kernel(ends, x1_ref, *refs):
    y_refs = refs[:len(ends)]
    pp_ref, ps_ref, gple_ref, wg_ref, wp_ref, gfin_ref, op_ref, os_ref = refs[len(ends):]
    y = y_refs[-1][...]
    for end, y_ref in reversed(list(zip(ends[:-1], y_refs[:-1]))):
        y = jnp.where(pl.program_id(0) < end, y_ref[...], y)
    x2 = x1_ref[...] + y
    gate = jax.nn.sigmoid(_dot(_rms(x2, gple_ref[...]).astype(BF16), wg_ref[...]))
    x3 = x2 + gate * _dot(_group_rows(pp_ref, ps_ref).astype(BF16), wp_ref[...])
    out = _rms(x3, gfin_ref[...])
    is_prompt = pl.program_id(0) < P_TILES

    @pl.when(is_prompt)
    def _():
        op_ref[...] = out

    @pl.when(jnp.logical_not(is_prompt))
    def _():
        os_ref[...] = out


def _final(x1, y_parts, pp, ps, gple, wg, wp, gfin):
    tiles = [yp.shape[0] // TM for yp in y_parts]
    ends = tuple(int(e) for e in np.cumsum(tiles))
    y_specs = [_rows_segment(D_MODEL, end - n, n) for end, n in zip(ends, tiles)]
    return pl.pallas_call(
        functools.partial(_final_kernel, ends),
        out_shape=[jax.ShapeDtypeStruct((N_P, D_MODEL), F32), jax.ShapeDtypeStruct((N_S, D_MODEL), F32)],
        grid=(N_TOK // TM,),
        in_specs=[_rows(D_MODEL), *y_specs,
                  _rows_prompt(PLE_DIM), _rows_sample(PLE_DIM),
                  _const(gple.shape), _const(wg.shape), _const(wp.shape), _const(gfin.shape)],
        out_specs=[_rows_prompt(D_MODEL), _rows_sample(D_MODEL)],
        compiler_params=_params(),
        name="ple_final",
    )(x1, *y_parts, pp, ps, gple, wg, wp, gfin)


def _rope_tables():
    half = ROPE // 2
    freqs = ROPE_THETA ** (-np.arange(half, dtype=np.float32) / half)
    pos = np.concatenate([np.arange(SEQ), np.tile(PAST_LEN + np.arange(DEC_SEQ), TM // DEC_SEQ)])
    ang = jnp.asarray(pos, F32)[:, None] * jnp.asarray(freqs, F32)[None, :]
    cos, sin = jnp.cos(ang), jnp.sin(ang)
    n = pos.shape[0]
    pad = jnp.zeros((n, LANES - NOPE - ROPE), F32)
    cos_t = jnp.concatenate([jnp.ones((n, NOPE), F32), cos, cos, pad], axis=1)
    sin_t = jnp.concatenate([jnp.zeros((n, NOPE), F32), sin, sin, pad], axis=1)
    return cos_t, sin_t


def _rot_cols(w):
    half = ROPE // 2
    return jnp.concatenate([-w[..., half:], w[..., :half]], axis=-1)


def _rel_bias(table, n_q, n_k):
    diag = np.arange(-(n_q - 1), n_k)
    line = table[:, np.clip(WINDOW_B - diag, -REL_CLIP, REL_CLIP) + REL_CLIP]
    return jnp.stack([line[:, n_q - 1 - i:n_q - 1 - i + n_k] for i in range(n_q)], axis=1)


def kernel(x_prompt, x_sample, cache_latent, cache_krope, cache_band_k, cache_band_v, p_prompt, p_sample, g_mix, w_in, g_q_lora, g_kv_lora, w_uq, w_uk, w_uv, rel_bias, w_a_proj, w_b_proj, w_out, g_ffn, w_query, sub_keys, expert_u, expert_v, g_ple, w_ple_gate, w_ple_proj, g_final):
    w = w_in[0]
    o_kr = Q_LORA + KV_LORA
    o_qb = o_kr + ROPE
    o_ga = o_qb + 3 * HB
    w_kr = w[:, o_kr:o_qb]
    z64 = jnp.zeros((D_MODEL, NOPE), F32)
    z32 = jnp.zeros((D_MODEL, LANES - NOPE - ROPE), F32)
    w1 = jnp.concatenate([w[:, :o_kr], z64, w_kr, z32, z64, _rot_cols(w_kr), z32], axis=1).astype(BF16)
    w2 = w[:, o_qb:o_ga].astype(BF16)
    w3 = w[:, o_ga:].astype(BF16)
    wq = w_uq[0].reshape(Q_LORA, H_A, NOPE + ROPE)
    zq64 = jnp.zeros((Q_LORA, H_A, NOPE), F32)
    zq32 = jnp.zeros((Q_LORA, H_A, LANES - NOPE - ROPE), F32)
    wuq1 = jnp.concatenate([wq, zq32], axis=-1).reshape(Q_LORA, H_A * LANES).astype(BF16)
    wuq2 = jnp.concatenate([zq64, _rot_cols(wq[..., NOPE:]), zq32], axis=-1).reshape(Q_LORA, H_A * LANES).astype(BF16)
    wukp = jnp.concatenate([w_uk[0], jnp.zeros((KV_LORA, H_A, LANES - NOPE), F32)], axis=-1)
    wukp = wukp.reshape(KV_LORA, H_A * LANES).astype(BF16)
    wuv = w_uv[0].reshape(KV_LORA, H_A * V_A).astype(BF16)
    wukt = jnp.transpose(w_uk[0], (1, 2, 0)).astype(BF16)
    wuvh = jnp.transpose(w_uv[0], (1, 0, 2)).astype(BF16)
    cos_t, sin_t = _rope_tables()
    row = lambda g: g.reshape(1, -1)

    xp = x_prompt.reshape(N_P, D_MODEL)
    xs = x_sample.reshape(N_S, D_MODEL)

    (lat, kr, q16, k16, v16, qb16, kb16, vb16, kb32, vb32, sga, sgb) = _inproj(
        xp, xs, row(g_mix[0]), w1, w2, w3, row(g_q_lora[0]), row(g_kv_lora[0]), wuq1, wuq2, wukp, wuv, cos_t, sin_t)

    oa_p = _mla_prompt(q16, k16, v16)
    oa_s = _mla_sample(q16, cache_latent[0], cache_krope[0], lat, kr, wukt, wuvh)

    pad = ((0, 0), (WINDOW_B, 0), (0, 0))
    kpad = jnp.pad(kb16[:N_P].reshape(BATCH, SEQ, HB), pad)
    vpad = jnp.pad(vb16[:N_P].reshape(BATCH, SEQ, HB), pad)
    tab = rel_bias[0]
    bias_p = _rel_bias(tab, CHUNK, BAND)
    bias_s = _rel_bias(tab, DEC_SEQ, WINDOW_B + DEC_SEQ)
    ob_p = _band_prompt(qb16, kpad, vpad, bias_p)
    ob_s, bk_s, bv_s = _band_sample(qb16, cache_band_k[0].reshape(DEC_BATCH, WINDOW_B, HB),
                                    cache_band_v[0].reshape(DEC_BATCH, WINDOW_B, HB),
                                    kb16, vb16, kb32, vb32, bias_s[:, :, :WINDOW_B], bias_s[:, :, WINDOW_B:])

    oa = jnp.concatenate([oa_p, oa_s], axis=0)
    ob = jnp.concatenate([ob_p, ob_s], axis=0)
    x1, h2, h16 = _mix_out(oa, ob, sga, sgb, xp, xs, w_a_proj[0].astype(BF16), w_b_proj[0].astype(BF16),
                           w_out[0].astype(BF16), row(g_ffn[0]))
    wqt = jnp.transpose(w_query[0]).astype(BF16)
    keys = sub_keys[0].reshape(2 * PEER_HEADS, N_KEYS, N_KEYS).astype(BF16)
    idx, gw = _peer_topk(h16, wqt, keys)

    tab_u = _pack_table(expert_u[0])
    tab_v = _pack_table(expert_v[0])
    y_parts = []
    first = 0
    for count in SC_WAVES:
        c_sc = _peer_u(idx, h2, gw, tab_u, first, count)
        y_parts.append(_sc_peer_v(idx[first:first + count] // ROW_WORDS, c_sc, expert_v[0]))
        first += count
    c_tc = _peer_u(idx, h2, gw, tab_u, first, N_TOK - first)
    y_parts.append(_peer_v(idx, c_tc, tab_v, first))

    out_p, out_s = _final(x1, y_parts, p_prompt.reshape(N_P, PLE_DIM), p_sample.reshape(N_S, PLE_DIM), row(g_ple[0]),
                          w_ple_gate[0].astype(BF16), w_ple_proj[0].astype(BF16), row(g_final))

    y_prompt = out_p.reshape(BATCH, SEQ, D_MODEL)
    y_sample = out_s.reshape(DEC_BATCH, DEC_SEQ, D_MODEL)
    lat_p = lat[:N_P].reshape(1, BATCH, SEQ, KV_LORA)
    kr_p = kr[:N_P].reshape(1, BATCH, SEQ, ROPE)
    bk_p = kb32[:N_P].reshape(BATCH, SEQ, H_B, DH_B)[None, :, SEQ - WINDOW_B:]
    bv_p = vb32[:N_P].reshape(BATCH, SEQ, H_B, DH_B)[None, :, SEQ - WINDOW_B:]
    lat_s = lat[N_P:].reshape(1, DEC_BATCH, DEC_SEQ, KV_LORA)
    kr_s = kr[N_P:].reshape(1, DEC_BATCH, DEC_SEQ, ROPE)
    bk_s = bk_s.reshape(1, DEC_BATCH, WINDOW_B, H_B, DH_B)
    bv_s = bv_s.reshape(1, DEC_BATCH, WINDOW_B, H_B, DH_B)
    return (y_prompt, y_sample, lat_p, kr_p, bk_p, bv_p, lat_s, kr_s, bk_s, bv_s)
```

```python
import functools
import jax
import jax.numpy as jnp
import numpy as np
from jax import lax
from jax.experimental import pallas as pl
from jax.experimental.pallas import tpu as pltpu
from jax.experimental.pallas import tpu_sc as plsc

D_MODEL = 1024
BATCH = 16
SEQ = 2048
DEC_BATCH = 32
DEC_SEQ = 32
PAST_LEN = 4096
CHUNK = 64
EPS = 1e-6
H_A = 8
Q_LORA = 384
KV_LORA = 256
NOPE = 64
ROPE = 32
V_A = 64
ROPE_THETA = 10000.0
MLA_SCALE = (NOPE + ROPE) ** -0.5
H_B = 8
DH_B = 64
BAND_PREV = 8
BAND = (BAND_PREV + 1) * CHUNK
WINDOW_B = BAND_PREV * CHUNK
REL_CLIP = 256
BAND_SCALE = DH_B ** -0.5
PEER_HEADS = 8
N_KEYS = 128
N_EXPERTS = N_KEYS * N_KEYS
TOPK = 16
PLE_DIM = 256

N_P = BATCH * SEQ
N_S = DEC_BATCH * DEC_SEQ
N_TOK = N_P + N_S
HB = H_B * DH_B
HALF = D_MODEL // 2
SLOTS = PEER_HEADS * TOPK
LANES = 128
SUBLANES = 8
TM = 256
TQ = 128
KV_STEP = 512
TB = 128
CREP_RING = 4
ROW_WORDS = HALF // LANES
PAIRS = SLOTS // 2
STRIDE = 72
HEADS_PER_TRIP = 8
IDX_GROUP = 8
VMEM_LIMIT = 56 * 1024 * 1024
SC_WORKERS = 32
SC_LANES = 16
SC_TOKENS = 8
SC_ROWS = 32
SC_ACC = 8
N_SC = 16384
NEG = -1e30
BF16 = jnp.bfloat16
F32 = jnp.float32
NT = (((1,), (1,)), ((), ()))


def _params(n_axes=1):
    return pltpu.CompilerParams(dimension_semantics=("arbitrary",) * n_axes,
                                vmem_limit_bytes=VMEM_LIMIT)


def _const(shape):
    nd = len(shape)
    return pl.BlockSpec(shape, lambda *_: (0,) * nd, pipeline_mode=pl.Buffered(1))


def _rows(width, tile=TM):
    return pl.BlockSpec((tile, width), lambda i: (i, 0))


P_TILES = N_P // TM


def _rows_head(width, tiles):
    return pl.BlockSpec((TM, width), lambda i: (jnp.minimum(i, tiles - 1), 0))


def _rows_tail(width, tiles):
    return pl.BlockSpec((TM, width), lambda i: (jnp.maximum(i - tiles, 0), 0))


def _rows_prompt(width):
    return _rows_head(width, P_TILES)


def _rows_sample(width):
    return _rows_tail(width, P_TILES)


def _split_rows(head_ref, tail_ref, tiles):
    return jnp.where(pl.program_id(0) < tiles, head_ref[...], tail_ref[...])


def _group_rows(p_ref, s_ref):
    return _split_rows(p_ref, s_ref, P_TILES)


def _rms(x, g):
    return x * lax.rsqrt(jnp.mean(x * x, axis=-1, keepdims=True) + EPS) * g


def _dot(a, b):
    return jnp.dot(a, b, preferred_element_type=F32)


def _softmax_rows(s_list):
    m = functools.reduce(jnp.maximum, [jnp.max(s, axis=-1, keepdims=True) for s in s_list])
    p_list = [jnp.exp(s - m) for s in s_list]
    inv = 1.0 / functools.reduce(jnp.add, [jnp.sum(p, axis=-1, keepdims=True) for p in p_list])
    return [(p * inv).astype(BF16) for p in p_list]


def _inproj_kernel(xp_ref, xs_ref, gmix_ref, w1_ref, w2_ref, w3_ref, gq_ref, gkv_ref, wuq1_ref, wuq2_ref,
                   wukp_ref, wuv_ref, cos_ref, sin_ref,
                   lat_ref, kr_ref, q16_ref, k16_ref, v16_ref, qb16_ref, kb16_ref, vb16_ref,
                   kb32_ref, vb32_ref, sga_ref, sgb_ref):
    h = _rms(_group_rows(xp_ref, xs_ref), gmix_ref[...]).astype(BF16)
    cos = cos_ref[...]
    sin = sin_ref[...]
    z1 = _dot(h, w1_ref[...])
    k128 = z1[:, 640:768] * cos + z1[:, 768:896] * sin
    kr_ref[...] = k128[:, NOPE:NOPE + ROPE]
    cqn = _rms(z1[:, :Q_LORA], gq_ref[...]).astype(BF16)
    cos8 = jnp.concatenate([cos] * H_A, axis=1)
    sin8 = jnp.concatenate([sin] * H_A, axis=1)
    q = _dot(cqn, wuq1_ref[...]) * cos8 + _dot(cqn, wuq2_ref[...]) * sin8
    q16_ref[...] = q.astype(BF16)
    ckvn = _rms(z1[:, Q_LORA:Q_LORA + KV_LORA], gkv_ref[...])
    lat_ref[...] = ckvn
    ckvn16 = ckvn.astype(BF16)
    kk = _dot(ckvn16, wukp_ref[...]) + jnp.concatenate([k128] * H_A, axis=1)
    k16_ref[...] = kk.astype(BF16)
    v16_ref[...] = _dot(ckvn16, wuv_ref[...]).astype(BF16)
    z2 = _dot(h, w2_ref[...])
    qb16_ref[...] = z2[:, :HB].astype(BF16)
    kb = z2[:, HB:2 * HB]
    vb = z2[:, 2 * HB:]
    kb32_ref[...] = kb
    vb32_ref[...] = vb
    kb16_ref[...] = kb.astype(BF16)
    vb16_ref[...] = vb.astype(BF16)
    z3 = _dot(h, w3_ref[...])
    sga_ref[...] = jax.nn.sigmoid(z3[:, :D_MODEL])
    sgb_ref[...] = jax.nn.sigmoid(z3[:, D_MODEL:])


def _rope_rows():
    per_seq = SEQ // TM
    return pl.BlockSpec((TM, LANES), lambda i: (jnp.where(i < N_P // TM, i % per_seq, per_seq), 0))


def _inproj(xp, xs, gmix, w1, w2, w3, gq, gkv, wuq1, wuq2, wukp, wuv, cos, sin):
    n = N_TOK
    widths = [(KV_LORA, F32), (ROPE, F32), (H_A * LANES, BF16), (H_A * LANES, BF16), (H_A * V_A, BF16),
              (HB, BF16), (HB, BF16), (HB, BF16), (HB, F32), (HB, F32), (D_MODEL, F32), (D_MODEL, F32)]
    return pl.pallas_call(
        _inproj_kernel,
        out_shape=[jax.ShapeDtypeStruct((n, w), d) for w, d in widths],
        grid=(n // TM,),
        in_specs=[_rows_prompt(D_MODEL), _rows_sample(D_MODEL),
                  _const(gmix.shape), _const(w1.shape), _const(w2.shape), _const(w3.shape),
                  _const(gq.shape), _const(gkv.shape), _const(wuq1.shape), _const(wuq2.shape),
                  _const(wukp.shape), _const(wuv.shape), _rope_rows(), _rope_rows()],
        out_specs=[_rows(w) for w, _ in widths],
        compiler_params=_params(),
        name="inproj",
    )(xp, xs, gmix, w1, w2, w3, gq, gkv, wuq1, wuq2, wukp, wuv, cos, sin)


def _pair_select(o_even, o_odd):
    lane = lax.broadcasted_iota(jnp.int32, o_even.shape, 1)
    return jnp.where(lane < V_A, o_even, o_odd)


def _mla_prompt_tile(q_ref, k_ref, v_ref, o_ref, i, nk):
    q_chunk = (i * TQ + lax.broadcasted_iota(jnp.int32, (TQ, nk), 0)) // CHUNK
    k_chunk = lax.broadcasted_iota(jnp.int32, (TQ, nk), 1) // CHUNK
    mask = k_chunk <= q_chunk
    scores = []
    for hd in range(H_A):
        qh = q_ref[:, hd * LANES:(hd + 1) * LANES]
        kh = k_ref[0:nk, hd * LANES:(hd + 1) * LANES]
        s = lax.dot_general(qh, kh, NT, preferred_element_type=F32) * MLA_SCALE
        scores.append(jnp.where(mask, s, NEG))
    weights = [_softmax_rows([s])[0] for s in scores]
    outs = [_dot(weights[hd], v_ref[0:nk, (hd // 2) * LANES:(hd // 2 + 1) * LANES]) for hd in range(H_A)]
    pairs = [_pair_select(outs[2 * pr], outs[2 * pr + 1]) for pr in range(H_A // 2)]
    o_ref[...] = jnp.concatenate(pairs, axis=1).astype(BF16)


def _mla_prompt_kernel(q_ref, k_ref, v_ref, o_ref):
    i = pl.program_id(1)
    tiles_per_step = KV_STEP // TQ
    for grp in range(SEQ // KV_STEP):
        @pl.when(i // tiles_per_step == grp)
        def _():
            _mla_prompt_tile(q_ref, k_ref, v_ref, o_ref, i, (grp + 1) * KV_STEP)


def _mla_prompt(q16, k16, v16):
    nq = SEQ // TQ
    return pl.pallas_call(
        _mla_prompt_kernel,
        out_shape=jax.ShapeDtypeStruct((N_P, H_A * V_A), BF16),
        grid=(BATCH, nq),
        in_specs=[pl.BlockSpec((TQ, H_A * LANES), lambda b, i: (b * nq + i, 0)),
                  pl.BlockSpec((SEQ, H_A * LANES), lambda b, i: (b, 0)),
                  pl.BlockSpec((SEQ, H_A * V_A), lambda b, i: (b, 0))],
        out_specs=pl.BlockSpec((TQ, H_A * V_A), lambda b, i: (b * nq + i, 0)),
        compiler_params=_params(2),
        name="mla_prompt",
    )(q16, k16, v16)


def _mla_sample_kernel(q_ref, clat_ref, ckr_ref, nlat_ref, nkr_ref, wukt_ref, wuvh_ref, o_ref):
    q = q_ref[...].astype(F32)
    qlat, qrope = [], []
    for hd in range(H_A):
        qn = q[:, hd * LANES:hd * LANES + NOPE].astype(BF16)
        qlat.append(_dot(qn, wukt_ref[hd]).astype(BF16))
        qrope.append(q[:, hd * LANES + NOPE:hd * LANES + NOPE + ROPE].astype(BF16))
    qlat = jnp.concatenate(qlat, axis=0)
    qrope = jnp.concatenate(qrope, axis=0)
    clat = clat_ref[...].astype(BF16)
    ckr = ckr_ref[...].astype(BF16)
    nlat = nlat_ref[...].astype(BF16)
    nkr = nkr_ref[...].astype(BF16)
    s_c = (lax.dot_general(qlat, clat, NT, preferred_element_type=F32)
           + lax.dot_general(qrope, ckr, NT, preferred_element_type=F32)) * MLA_SCALE
    s_n = (lax.dot_general(qlat, nlat, NT, preferred_element_type=F32)
           + lax.dot_general(qrope, nkr, NT, preferred_element_type=F32)) * MLA_SCALE
    w_c, w_n = _softmax_rows([s_c, s_n])
    olat = (_dot(w_c, clat) + _dot(w_n, nlat)).astype(BF16)
    outs = [_dot(olat[hd * DEC_SEQ:(hd + 1) * DEC_SEQ, :], wuvh_ref[hd]) for hd in range(H_A)]
    o_ref[...] = jnp.concatenate(outs, axis=1).astype(BF16)


def _mla_sample(q16, cache_lat, cache_kr, lat, kr, wukt, wuvh):
    off = N_P // DEC_SEQ
    return pl.pallas_call(
        _mla_sample_kernel,
        out_shape=jax.ShapeDtypeStruct((N_S, H_A * V_A), BF16),
        grid=(DEC_BATCH,),
        in_specs=[pl.BlockSpec((DEC_SEQ, H_A * LANES), lambda b: (off + b, 0)),
                  pl.BlockSpec((None, PAST_LEN, KV_LORA), lambda b: (b, 0, 0)),
                  pl.BlockSpec((None, PAST_LEN, ROPE), lambda b: (b, 0, 0)),
                  pl.BlockSpec((DEC_SEQ, KV_LORA), lambda b: (off + b, 0)),
                  pl.BlockSpec((DEC_SEQ, ROPE), lambda b: (off + b, 0)),
                  _const(wukt.shape), _const(wuvh.shape)],
        out_specs=pl.BlockSpec((DEC_SEQ, H_A * V_A), lambda b: (b, 0)),
        compiler_params=_params(),
        name="mla_sample",
    )(q16, cache_lat, cache_kr, lat, kr, wukt, wuvh)


def _band_heads(q, blocks, bias_refs, valid):
    lane = lax.broadcasted_iota(jnp.int32, (q.shape[0], LANES), 1)
    scores = []
    for hd in range(H_B):
        sl = slice((hd // 2) * LANES, (hd // 2 + 1) * LANES)
        q2 = q[:, sl]
        own = (lane >= DH_B) if hd % 2 else (lane < DH_B)
        qm = jnp.where(own, q2, jnp.zeros_like(q2))
        ss = [lax.dot_general(qm, k[:, sl], NT, preferred_element_type=F32) * BAND_SCALE + b_ref[hd]
              for (k, _), b_ref in zip(blocks, bias_refs)]
        if valid is not None:
            ss = [jnp.where(valid, s, NEG) for s in ss]
        scores.append(ss)
    weights = [_softmax_rows(ss) for ss in scores]
    outs = []
    for hd in range(H_B):
        sl = slice((hd // 2) * LANES, (hd // 2 + 1) * LANES)
        outs.append(functools.reduce(jnp.add, [_dot(w, v[:, sl]) for w, (_, v) in zip(weights[hd], blocks)]))
    pairs = [_pair_select(outs[2 * pr], outs[2 * pr + 1]) for pr in range(H_B // 2)]
    return jnp.concatenate(pairs, axis=1).astype(BF16)


def _band_prompt_kernel(q_ref, k_ref, v_ref, bias_ref, o_ref):
    c = pl.program_id(1)
    start = pl.multiple_of(c * CHUNK, CHUNK)
    k = k_ref[pl.ds(start, BAND), :]
    v = v_ref[pl.ds(start, BAND), :]
    valid = (start - WINDOW_B + lax.broadcasted_iota(jnp.int32, (CHUNK, BAND), 1)) >= 0
    o_ref[...] = _band_heads(q_ref[...], [(k, v)], [bias_ref], valid)


def _band_prompt(qb16, kpad, vpad, bias):
    nc = SEQ // CHUNK
    return pl.pallas_call(
        _band_prompt_kernel,
        out_shape=jax.ShapeDtypeStruct((N_P, HB), BF16),
        grid=(BATCH, nc),
        in_specs=[pl.BlockSpec((CHUNK, HB), lambda b, c: (b * nc + c, 0)),
                  pl.BlockSpec((None, SEQ + WINDOW_B, HB), lambda b, c: (b, 0, 0)),
                  pl.BlockSpec((None, SEQ + WINDOW_B, HB), lambda b, c: (b, 0, 0)),
                  _const(bias.shape)],
        out_specs=pl.BlockSpec((CHUNK, HB), lambda b, c: (b * nc + c, 0)),
        compiler_params=_params(2),
        name="band_prompt",
    )(qb16, kpad, vpad, bias)


def _band_sample_kernel(q_ref, ck_ref, cv_ref, nk16_ref, nv16_ref, nk32_ref, nv32_ref, bias_c_ref, bias_n_ref,
                        o_ref, bk_ref, bv_ref):
    ck = ck_ref[...]
    cv = cv_ref[...]
    blocks = [(ck.astype(BF16), cv.astype(BF16)), (nk16_ref[...], nv16_ref[...])]
    o_ref[...] = _band_heads(q_ref[...], blocks, [bias_c_ref, bias_n_ref], None)
    keep = WINDOW_B - DEC_SEQ
    bk_ref[0:keep, :] = ck[DEC_SEQ:, :]
    bk_ref[keep:, :] = nk32_ref[...]
    bv_ref[0:keep, :] = cv[DEC_SEQ:, :]
    bv_ref[keep:, :] = nv32_ref[...]


def _band_sample(qb16, cache_k, cache_v, kb16, vb16, kb32, vb32, bias_c, bias_n):
    off = N_P // DEC_SEQ
    new = lambda: pl.BlockSpec((DEC_SEQ, HB), lambda b: (off + b, 0))
    cache = lambda: pl.BlockSpec((None, WINDOW_B, HB), lambda b: (b, 0, 0))
    return pl.pallas_call(
        _band_sample_kernel,
        out_shape=[jax.ShapeDtypeStruct((N_S, HB), BF16),
                   jax.ShapeDtypeStruct((DEC_BATCH, WINDOW_B, HB), F32),
                   jax.ShapeDtypeStruct((DEC_BATCH, WINDOW_B, HB), F32)],
        grid=(DEC_BATCH,),
        in_specs=[new(), cache(), cache(), new(), new(), new(), new(),
                  _const(bias_c.shape), _const(bias_n.shape)],
        out_specs=[pl.BlockSpec((DEC_SEQ, HB), lambda b: (b, 0)), cache(), cache()],
        compiler_params=_params(),
        name="band_sample",
    )(qb16, cache_k, cache_v, kb16, vb16, kb32, vb32, bias_c, bias_n)


def _mix_out_kernel(oa_ref, ob_ref, sga_ref, sgb_ref, xp_ref, xs_ref, wa_ref, wb_ref, wout_ref, gffn_ref,
                    x1_ref, h2_ref, h16_ref):
    merged = sga_ref[...] * _dot(oa_ref[...], wa_ref[...]) + sgb_ref[...] * _dot(ob_ref[...], wb_ref[...])
    x1 = _group_rows(xp_ref, xs_ref) + _dot(merged.astype(BF16), wout_ref[...])
    x1_ref[...] = x1
    h2 = _rms(x1, gffn_ref[...])
    h2_ref[...] = h2
    h16_ref[...] = h2.astype(BF16)


def _mix_out(oa, ob, sga, sgb, xp, xs, wa, wb, wout, gffn):
    n = N_TOK
    return pl.pallas_call(
        _mix_out_kernel,
        out_shape=[jax.ShapeDtypeStruct((n, D_MODEL), F32), jax.ShapeDtypeStruct((n, D_MODEL), F32),
                   jax.ShapeDtypeStruct((n, D_MODEL), BF16)],
        grid=(n // TM,),
        in_specs=[_rows(H_A * V_A), _rows(HB), _rows(D_MODEL), _rows(D_MODEL),
                  _rows_prompt(D_MODEL), _rows_sample(D_MODEL),
                  _const(wa.shape), _const(wb.shape), _const(wout.shape), _const(gffn.shape)],
        out_specs=[_rows(D_MODEL)] * 3,
        compiler_params=_params(),
        name="mix_out",
    )(oa, ob, sga, sgb, xp, xs, wa, wb, wout, gffn)


def _top16(s, n_rows):
    row = lax.broadcasted_iota(jnp.int32, (n_rows, TQ), 0).astype(F32)
    vals, ids = [], []
    for _ in range(TOPK):
        m = jnp.max(s, axis=0, keepdims=True)
        i = jnp.min(jnp.where(s == m, row, float(n_rows)), axis=0, keepdims=True)
        vals.append(m)
        ids.append(i)
        s = jnp.where(row == i, -jnp.inf, s)
    return jnp.concatenate(vals, axis=0), jnp.concatenate(ids, axis=0)


def _pair_candidates(f1, f2):
    h = SUBLANES
    blocks = [f1(0, 1, 0, h), f1(0, 1, h, 2 * h)]
    blocks += [f1(a, a + 1, 0, h) for a in range(1, h)]
    blocks += [f2(h, 2 * h, 0, 1)]
    return jnp.concatenate(blocks, axis=0)


def _peer_topk_kernel(h_ref, wqt_ref, keys_ref, idx_ref, g_ref, qt_ref, g_t, id_t):
    qt_ref[...] = lax.dot_general(wqt_ref[...], h_ref[...], NT, preferred_element_type=F32).astype(BF16)
    r8 = lax.broadcasted_iota(jnp.int32, (SUBLANES, TQ), 0).astype(F32)
    flat = lambda a0, a1, b0, b1: (r8 + float(b0)) + float(TOPK * a0)
    flat_t = lambda a0, a1, b0, b1: (r8 + float(a0)) * float(TOPK) + float(b0)
    cflat = _pair_candidates(flat, flat_t)

    def one_head(hd):
        tops = []
        for p in range(2):
            hp = hd * 2 + p
            qs = qt_ref[pl.ds(pl.multiple_of(hp * N_KEYS, N_KEYS), N_KEYS), :]
            tops.append(_top16(_dot(keys_ref[hp], qs), N_KEYS))
        (s1, i1), (s2, i2) = tops
        add = lambda x, y: (lambda a0, a1, b0, b1: x[a0:a1, :] + y[b0:b1, :])
        cand = _pair_candidates(add(s1, s2), add(s1, s2))
        e1 = i1 * float(N_KEYS)
        cidx = _pair_candidates(add(e1, i2), add(e1, i2))
        vals, ids = [], []
        for _ in range(TOPK):
            m = jnp.max(cand, axis=0, keepdims=True)
            c = jnp.min(jnp.where(cand == m, cflat, float(TOPK * TOPK)), axis=0, keepdims=True)
            sel = cflat == c
            vals.append(m)
            ids.append(jnp.sum(jnp.where(sel, cidx, 0.0), axis=0, keepdims=True))
            cand = jnp.where(sel, -jnp.inf, cand)
        best = jnp.concatenate(vals, axis=0)
        e = jnp.exp(best - best[0:1, :])
        g = e / jnp.sum(e, axis=0, keepdims=True)
        base = pl.multiple_of(hd * TOPK, TOPK)
        g_t[pl.ds(base, TOPK), :] = g
        id_t[pl.ds(base, TOPK), :] = jnp.concatenate(ids, axis=0)

    def head_group(grp, carry):
        for e in range(HEADS_PER_TRIP):
            one_head(grp * HEADS_PER_TRIP + e)
        return carry

    lax.fori_loop(0, PEER_HEADS // HEADS_PER_TRIP, head_group, 0)
    g_ref[...] = g_t[...].T
    idx_ref[...] = (id_t[...].T * float(ROW_WORDS)).astype(jnp.int32)


def _peer_topk(h16, wqt, keys):
    n = h16.shape[0]
    nt = n // TQ
    return pl.pallas_call(
        _peer_topk_kernel,
        out_shape=[jax.ShapeDtypeStruct((n, SLOTS), jnp.int32),
                   jax.ShapeDtypeStruct((n, SLOTS), F32)],
        grid=(nt,),
        in_specs=[_rows(D_MODEL, TQ), _const(wqt.shape), _const(keys.shape)],
        out_specs=[_rows(SLOTS, TQ)] * 2,
        scratch_shapes=[pltpu.VMEM((2 * PEER_HEADS * N_KEYS, TQ), BF16),
                        pltpu.VMEM((SLOTS, TQ), F32), pltpu.VMEM((SLOTS, TQ), F32)],
        compiler_params=_params(),
        name="peer_topk",
    )(h16, wqt, keys)


def _pack_table(t):
    b = lax.bitcast_convert_type(t.astype(BF16), jnp.uint16).astype(jnp.uint32)
    packed = (b[:, :HALF] << 16) | b[:, HALF:]
    return packed.reshape(N_EXPERTS * ROW_WORDS, LANES)


def _unpack(w):
    hi = lax.bitcast_convert_type(w & jnp.uint32(0xFFFF0000), F32)
    lo = lax.bitcast_convert_type(w << 16, F32)
    return hi, lo


def _slot_pairs(idx_ref, t):
    out = []
    for grp in range(PAIRS // IDX_GROUP):
        lo = idx_ref.at[0, pl.ds(t * SLOTS + grp * IDX_GROUP, IDX_GROUP)]
        hi = idx_ref.at[0, pl.ds(t * SLOTS + PAIRS + grp * IDX_GROUP, IDX_GROUP)]
        out.extend((lo[k], hi[k]) for k in range(IDX_GROUP))
    return out


def _load_pair(tab, ia, ib):
    wa = tab[pl.ds(pl.multiple_of(ia, ROW_WORDS), ROW_WORDS), :]
    wb = tab[pl.ds(pl.multiple_of(ib, ROW_WORDS), ROW_WORDS), :]
    return _unpack(jnp.concatenate([wa, wb], axis=0))


def _peer_u_kernel(idx_ref, x_ref, g_ref, tab, c_ref, prod_a, prod_b, rbuf, a_t):
    lane = lax.broadcasted_iota(jnp.int32, (SLOTS, LANES), 1)

    @pl.when(pl.program_id(0) == 0)
    def _():
        rbuf[...] = jnp.zeros_like(rbuf)
        a_t[...] = jnp.zeros_like(a_t)

    def reduce(slot, tok):
        col = jnp.sum(rbuf[slot], axis=1, keepdims=True)
        a_t[...] = jnp.where(lane == tok, col, a_t[...])

    def gather(t, slot, prod):
        xt = x_ref[t]
        x_hi = jnp.concatenate([xt[0:ROW_WORDS, :]] * 2, axis=0)
        x_lo = jnp.concatenate([xt[ROW_WORDS:, :]] * 2, axis=0)
        for j, (ia, ib) in enumerate(_slot_pairs(idx_ref, t)):
            hi, lo = _load_pair(tab, ia, ib)
            prod[pl.ds(j, SUBLANES, stride=STRIDE), :] = hi * x_hi + lo * x_lo
        halves = [functools.reduce(jnp.add, [prod[pl.ds((h * ROW_WORDS + s) * STRIDE, PAIRS), :]
                                             for s in range(ROW_WORDS)]) for h in range(2)]
        rbuf[slot] = jnp.concatenate(halves, axis=0)

    def two_tokens(p, carry):
        t0 = 2 * p
        reduce(0, t0 - 2)
        reduce(1, t0 - 1)
        gather(t0, 0, prod_a)
        gather(t0 + 1, 1, prod_b)
        return carry

    lax.fori_loop(0, TB // 2, two_tokens, 0)
    reduce(0, TB - 2)
    reduce(1, TB - 1)
    a = a_t[...].T[0:TB, :]
    gelu = 0.5 * a * (1.0 + lax.erf(a * (2.0 ** -0.5)))
    c_ref[...] = g_ref[...] * gelu


def _peer_u(idx, x, g, tab_u, first, count):
    n = x.shape[0]
    nb = count // TB
    fb = first // TB
    return pl.pallas_call(
        _peer_u_kernel,
        out_shape=jax.ShapeDtypeStruct((count, SLOTS), F32),
        grid=(nb,),
        in_specs=[pl.BlockSpec((None, 1, TB * SLOTS), lambda i: (fb + i, 0, 0), memory_space=pltpu.SMEM),
                  pl.BlockSpec((TB, SUBLANES, LANES), lambda i: (fb + i, 0, 0)),
                  pl.BlockSpec((TB, SLOTS), lambda i: (fb + i, 0)),
                  _const(tab_u.shape)],
        out_specs=pl.BlockSpec((TB, SLOTS), lambda i: (i, 0)),
        scratch_shapes=[pltpu.VMEM((SUBLANES * STRIDE, LANES), F32),
                        pltpu.VMEM((SUBLANES * STRIDE, LANES), F32),
                        pltpu.VMEM((2, SLOTS, LANES), F32),
                        pltpu.VMEM((SLOTS, LANES), F32)],
        compiler_params=_params(),
        name="peer_u",
    )(idx.reshape(n // TB, 1, TB * SLOTS), x.reshape(n, SUBLANES, LANES), g, tab_u)


def _peer_v_kernel(idx_ref, c_ref, tab, y_ref, *crep):
    row = lax.broadcasted_iota(jnp.int32, (SUBLANES, LANES), 0)
    eye = (lax.broadcasted_iota(jnp.int32, (SLOTS, LANES), 0)
           == lax.broadcasted_iota(jnp.int32, (SLOTS, LANES), 1))

    def spread(t, crep):
        col = jnp.sum(jnp.where(eye, c_ref[pl.ds(t, 1), :], 0.0), axis=1, keepdims=True)
        crep[...] = jnp.broadcast_to(col, (SLOTS, LANES))

    def token(t, crep):
        acc_hi = jnp.zeros((SUBLANES, LANES), F32)
        acc_lo = jnp.zeros((SUBLANES, LANES), F32)
        for j, (ia, ib) in enumerate(_slot_pairs(idx_ref, t)):
            hi, lo = _load_pair(tab, ia, ib)
            cm = jnp.where(row < ROW_WORDS, crep[pl.ds(j, 1), :], crep[pl.ds(PAIRS + j, 1), :])
            acc_hi = acc_hi + cm * hi
            acc_lo = acc_lo + cm * lo
        y_ref[t] = jnp.concatenate([acc_hi[0:ROW_WORDS, :] + acc_hi[ROW_WORDS:, :],
                                    acc_lo[0:ROW_WORDS, :] + acc_lo[ROW_WORDS:, :]], axis=0)

    ahead = CREP_RING // 2
    for k in range(ahead):
        spread(k, crep[k])

    def ring_trip(p, carry):
        t0 = CREP_RING * p
        for k in range(CREP_RING):
            spread(jnp.minimum(t0 + k + ahead, TB - 1), crep[(k + ahead) % CREP_RING])
            token(t0 + k, crep[k])
        return carry

    lax.fori_loop(0, TB // CREP_RING, ring_trip, 0)


def _peer_v(idx, c, tab_v, first):
    n = c.shape[0]
    nb = n // TB
    fb = first // TB
    y = pl.pallas_call(
        _peer_v_kernel,
        out_shape=jax.ShapeDtypeStruct((n, SUBLANES, LANES), F32),
        grid=(nb,),
        in_specs=[pl.BlockSpec((None, 1, TB * SLOTS), lambda i: (fb + i, 0, 0), memory_space=pltpu.SMEM),
                  pl.BlockSpec((TB, SLOTS), lambda i: (i, 0)),
                  _const(tab_v.shape)],
        out_specs=pl.BlockSpec((TB, SUBLANES, LANES), lambda i: (i, 0, 0)),
        scratch_shapes=[pltpu.VMEM((SLOTS, LANES), F32)] * CREP_RING,
        compiler_params=_params(),
        name="peer_v",
    )(idx.reshape(idx.shape[0] // TB, 1, TB * SLOTS), c, tab_v)
    return y.reshape(n, D_MODEL)


def _sc_peer_v(idx, c, table):
    n = idx.shape[0]
    per_worker = n // SC_WORKERS
    batches = per_worker // SC_TOKENS
    groups = SC_TOKENS * SLOTS // SC_ROWS
    chunks = D_MODEL // SC_LANES
    mesh = plsc.VectorSubcoreMesh(core_axis_name="c", subcore_axis_name="s")

    def body(idx_hbm, c_hbm, tab_hbm, y_hbm, idx_v, c_v, rows_a, rows_b, y_v, sem_a, sem_b):
        wid = lax.axis_index("s") * 2 + lax.axis_index("c")
        zero = jnp.zeros((SC_LANES,), F32)

        def gather(g, rows, sem):
            return pltpu.make_async_copy(tab_hbm.at[idx_v.at[pl.ds(g * SC_ROWS, SC_ROWS)]], rows, sem)

        def accumulate(g, rows):
            tok = g // (SLOTS // SC_ROWS)
            ws = [plsc.load_gather(c_v, [jnp.full((SC_LANES,), g * SC_ROWS + r, jnp.int32)])
                  for r in range(SC_ROWS)]

            @pl.loop(0, chunks // SC_ACC)
            def _(cb):
                base = pl.multiple_of(cb * (SC_ACC * SC_LANES), SC_ACC * SC_LANES)
                acc = [zero] * SC_ACC
                for r in range(SC_ROWS):
                    for k in range(SC_ACC):
                        acc[k] = acc[k] + ws[r] * rows[r, pl.ds(base + k * SC_LANES, SC_LANES)]
                for k in range(SC_ACC):
                    plsc.addupdate(y_v.at[tok, pl.ds(base + k * SC_LANES, SC_LANES)], acc[k])

        @pl.loop(0, batches)
        def _(b):
            t0 = wid * per_worker + b * SC_TOKENS
            pltpu.sync_copy(idx_hbm.at[pl.ds(t0 * SLOTS, SC_TOKENS * SLOTS)], idx_v)
            pltpu.sync_copy(c_hbm.at[pl.ds(t0 * SLOTS, SC_TOKENS * SLOTS)], c_v)
            for t in range(SC_TOKENS):
                for ch in range(chunks):
                    y_v[t, pl.ds(ch * SC_LANES, SC_LANES)] = zero
            gather(0, rows_a, sem_a).start()

            @pl.loop(0, groups // 2)
            def _(h):
                g = 2 * h
                gather(g + 1, rows_b, sem_b).start()
                gather(g, rows_a, sem_a).wait()
                accumulate(g, rows_a)

                @pl.when(h + 1 < groups // 2)
                def _():
                    gather(g + 2, rows_a, sem_a).start()
                gather(g + 1, rows_b, sem_b).wait()
                accumulate(g + 1, rows_b)

            pltpu.sync_copy(y_v, y_hbm.at[pl.ds(t0, SC_TOKENS)])

    return pl.kernel(
        body,
        out_type=jax.ShapeDtypeStruct((n, D_MODEL), F32),
        mesh=mesh,
        scratch_types=[pltpu.VMEM((SC_TOKENS * SLOTS,), jnp.int32),
                       pltpu.VMEM((SC_TOKENS * SLOTS,), F32),
                       pltpu.VMEM((SC_ROWS, D_MODEL), F32),
                       pltpu.VMEM((SC_ROWS, D_MODEL), F32),
                       pltpu.VMEM((SC_TOKENS, D_MODEL), F32),
                       pltpu.SemaphoreType.DMA,
                       pltpu.SemaphoreType.DMA],
        compiler_params=pltpu.CompilerParams(needs_layout_passes=False),
        name="sc_peer_v",
    )(idx.reshape(n * SLOTS), c.reshape(n * SLOTS), table)


def _final_kernel(x1_ref, yh_ref, yt_ref, pp_ref, ps_ref, gple_ref, wg_ref, wp_ref, gfin_ref, op_ref, os_ref):
    x2 = x1_ref[...] + _split_rows(yh_ref, yt_ref, N_SC // TM)
    gate = jax.nn.sigmoid(_dot(_rms(x2, gple_ref[...]).astype(BF16), wg_ref[...]))
    x3 = x2 + gate * _dot(_group_rows(pp_ref, ps_ref).astype(BF16), wp_ref[...])
    out = _rms(x3, gfin_ref[...])
    is_prompt = pl.program_id(0) < P_TILES

    @pl.when(is_prompt)
    def _():
        op_ref[...] = out

    @pl.when(jnp.logical_not(is_prompt))
    def _():
        os_ref[...] = out


def _final(x1, y_head, y_tail, pp, ps, gple, wg, wp, gfin):
    return pl.pallas_call(
        _final_kernel,
        out_shape=[jax.ShapeDtypeStruct((N_P, D_MODEL), F32), jax.ShapeDtypeStruct((N_S, D_MODEL), F32)],
        grid=(N_TOK // TM,),
        in_specs=[_rows(D_MODEL), _rows_head(D_MODEL, N_SC // TM), _rows_tail(D_MODEL, N_SC // TM),
                  _rows_prompt(PLE_DIM), _rows_sample(PLE_DIM),
                  _const(gple.shape), _const(wg.shape), _const(wp.shape), _const(gfin.shape)],
        out_specs=[_rows_prompt(D_MODEL), _rows_sample(D_MODEL)],
        compiler_params=_params(),
        name="ple_final",
    )(x1, y_head, y_tail, pp, ps, gple, wg, wp, gfin)


def _rope_tables():
    half = ROPE // 2
    freqs = ROPE_THETA ** (-np.arange(half, dtype=np.float32) / half)
    pos = np.concatenate([np.arange(SEQ), np.tile(PAST_LEN + np.arange(DEC_SEQ), TM // DEC_SEQ)])
    ang = jnp.asarray(pos, F32)[:, None] * jnp.asarray(freqs, F32)[None, :]
    cos, sin = jnp.cos(ang), jnp.sin(ang)
    n = pos.shape[0]
    pad = jnp.zeros((n, LANES - NOPE - ROPE), F32)
    cos_t = jnp.concatenate([jnp.ones((n, NOPE), F32), cos, cos, pad], axis=1)
    sin_t = jnp.concatenate([jnp.zeros((n, NOPE), F32), sin, sin, pad], axis=1)
    return cos_t, sin_t


def _rot_cols(w):
    half = ROPE // 2
    return jnp.concatenate([-w[..., half:], w[..., :half]], axis=-1)


def _rel_bias(table, n_q, n_k):
    diag = np.arange(-(n_q - 1), n_k)
    line = table[:, np.clip(WINDOW_B - diag, -REL_CLIP, REL_CLIP) + REL_CLIP]
    return jnp.stack([line[:, n_q - 1 - i:n_q - 1 - i + n_k] for i in range(n_q)], axis=1)


def kernel(x_prompt, x_sample, cache_latent, cache_krope, cache_band_k, cache_band_v, p_prompt, p_sample, g_mix, w_in, g_q_lora, g_kv_lora, w_uq, w_uk, w_uv, rel_bias, w_a_proj, w_b_proj, w_out, g_ffn, w_query, sub_keys, expert_u, expert_v, g_ple, w_ple_gate, w_ple_proj, g_final):
    w = w_in[0]
    o_kr = Q_LORA + KV_LORA
    o_qb = o_kr + ROPE
    o_ga = o_qb + 3 * HB
    w_kr = w[:, o_kr:o_qb]
    z64 = jnp.zeros((D_MODEL, NOPE), F32)
    z32 = jnp.zeros((D_MODEL, LANES - NOPE - ROPE), F32)
    w1 = jnp.concatenate([w[:, :o_kr], z64, w_kr, z32, z64, _rot_cols(w_kr), z32], axis=1).astype(BF16)
    w2 = w[:, o_qb:o_ga].astype(BF16)
    w3 = w[:, o_ga:].astype(BF16)
    wq = w_uq[0].reshape(Q_LORA, H_A, NOPE + ROPE)
    zq64 = jnp.zeros((Q_LORA, H_A, NOPE), F32)
    zq32 = jnp.zeros((Q_LORA, H_A, LANES - NOPE - ROPE), F32)
    wuq1 = jnp.concatenate([wq, zq32], axis=-1).reshape(Q_LORA, H_A * LANES).astype(BF16)
    wuq2 = jnp.concatenate([zq64, _rot_cols(wq[..., NOPE:]), zq32], axis=-1).reshape(Q_LORA, H_A * LANES).astype(BF16)
    wukp = jnp.concatenate([w_uk[0], jnp.zeros((KV_LORA, H_A, LANES - NOPE), F32)], axis=-1)
    wukp = wukp.reshape(KV_LORA, H_A * LANES).astype(BF16)
    wuv = w_uv[0].reshape(KV_LORA, H_A * V_A).astype(BF16)
    wukt = jnp.transpose(w_uk[0], (1, 2, 0)).astype(BF16)
    wuvh = jnp.transpose(w_uv[0], (1, 0, 2)).astype(BF16)
    cos_t, sin_t = _rope_tables()
    row = lambda g: g.reshape(1, -1)

    xp = x_prompt.reshape(N_P, D_MODEL)
    xs = x_sample.reshape(N_S, D_MODEL)

    (lat, kr, q16, k16, v16, qb16, kb16, vb16, kb32, vb32, sga, sgb) = _inproj(
        xp, xs, row(g_mix[0]), w1, w2, w3, row(g_q_lora[0]), row(g_kv_lora[0]), wuq1, wuq2, wukp, wuv, cos_t, sin_t)

    oa_p = _mla_prompt(q16, k16, v16)
    oa_s = _mla_sample(q16, cache_latent[0], cache_krope[0], lat, kr, wukt, wuvh)

    pad = ((0, 0), (WINDOW_B, 0), (0, 0))
    kpad = jnp.pad(kb16[:N_P].reshape(BATCH, SEQ, HB), pad)
    vpad = jnp.pad(vb16[:N_P].reshape(BATCH, SEQ, HB), pad)
    tab = rel_bias[0]
    bias_p = _rel_bias(tab, CHUNK, BAND)
    bias_s = _rel_bias(tab, DEC_SEQ, WINDOW_B + DEC_SEQ)
    ob_p = _band_prompt(qb16, kpad, vpad, bias_p)
    ob_s, bk_s, bv_s = _band_sample(qb16, cache_band_k[0].reshape(DEC_BATCH, WINDOW_B, HB),
                                    cache_band_v[0].reshape(DEC_BATCH, WINDOW_B, HB),
                                    kb16, vb16, kb32, vb32, bias_s[:, :, :WINDOW_B], bias_s[:, :, WINDOW_B:])

    oa = jnp.concatenate([oa_p, oa_s], axis=0)
    ob = jnp.concatenate([ob_p, ob_s], axis=0)
    x1, h2, h16 = _mix_out(oa, ob, sga, sgb, xp, xs, w_a_proj[0].astype(BF16), w_b_proj[0].astype(BF16),
                           w_out[0].astype(BF16), row(g_ffn[0]))
    wqt = jnp.transpose(w_query[0]).astype(BF16)
    keys = sub_keys[0].reshape(2 * PEER_HEADS, N_KEYS, N_KEYS).astype(BF16)
    idx, gw = _peer_topk(h16, wqt, keys)

    tab_u = _pack_table(expert_u[0])
    tab_v = _pack_table(expert_v[0])
    c_sc = _peer_u(idx, h2, gw, tab_u, 0, N_SC)
    y_sc = _sc_peer_v(idx[:N_SC] // ROW_WORDS, c_sc, expert_v[0])
    c_tc = _peer_u(idx, h2, gw, tab_u, N_SC, N_TOK - N_SC)
    y_tc = _peer_v(idx, c_tc, tab_v, N_SC)

    out_p, out_s = _final(x1, y_sc, y_tc, p_prompt.reshape(N_P, PLE_DIM), p_sample.reshape(N_S, PLE_DIM), row(g_ple[0]),
                          w_ple_gate[0].astype(BF16), w_ple_proj[0].astype(BF16), row(g_final))

    y_prompt = out_p.reshape(BATCH, SEQ, D_MODEL)
    y_sample = out_s.reshape(DEC_BATCH, DEC_SEQ, D_MODEL)
    lat_p = lat[:N_P].reshape(1, BATCH, SEQ, KV_LORA)
    kr_p = kr[:N_P].reshape(1, BATCH, SEQ, ROPE)
    bk_p = kb32[:N_P].reshape(BATCH, SEQ, H_B, DH_B)[None, :, SEQ - WINDOW_B:]
    bv_p = vb32[:N_P].reshape(BATCH, SEQ, H_B, DH_B)[None, :, SEQ - WINDOW_B:]
    lat_s = lat[N_P:].reshape(1, DEC_BATCH, DEC_SEQ, KV_LORA)
    kr_s = kr[N_P:].reshape(1, DEC_BATCH, DEC_SEQ, ROPE)
    bk_s = bk_s.reshape(1, DEC_BATCH, WINDOW_B, H_B, DH_B)
    bv_s = bv_s.reshape(1, DEC_BATCH, WINDOW_B, H_B, DH_B)
    return (y_prompt, y_sample, lat_p, kr_p, bk_p, bv_p, lat_s, kr_s, bk_s, bv_s)
```

```python
import functools
import jax
import jax.numpy as jnp
import numpy as np
from jax import lax
from jax.experimental import pallas as pl
from jax.experimental.pallas import tpu as pltpu
from jax.experimental.pallas import tpu_sc as plsc

D_MODEL = 1024
BATCH = 16
SEQ = 2048
DEC_BATCH = 32
DEC_SEQ = 32
PAST_LEN = 4096
CHUNK = 64
EPS = 1e-6
H_A = 8
Q_LORA = 384
KV_LORA = 256
NOPE = 64
ROPE = 32
V_A = 64
ROPE_THETA = 10000.0
MLA_SCALE = (NOPE + ROPE) ** -0.5
H_B = 8
DH_B = 64
BAND_PREV = 8
BAND = (BAND_PREV + 1) * CHUNK
WINDOW_B = BAND_PREV * CHUNK
REL_CLIP = 256
BAND_SCALE = DH_B ** -0.5
PEER_HEADS = 8
N_KEYS = 128
N_EXPERTS = N_KEYS * N_KEYS
TOPK = 16
PLE_DIM = 256

N_P = BATCH * SEQ
N_S = DEC_BATCH * DEC_SEQ
N_TOK = N_P + N_S
HB = H_B * DH_B
HALF = D_MODEL // 2
SLOTS = PEER_HEADS * TOPK
LANES = 128
SUBLANES = 8
TM = 256
TQ = 128
KV_STEP = 512
TB = 128
CREP_RING = 4
ROW_WORDS = HALF // LANES
PAIRS = SLOTS // 2
STRIDE = 72
HEADS_PER_TRIP = 8
IDX_GROUP = 8
VMEM_LIMIT = 56 * 1024 * 1024
SC_WORKERS = 32
SC_LANES = 16
SC_TOKENS = 8
SC_ROWS = 32
SC_ACC = 8
N_SC = 18432
NEG = -1e30
BF16 = jnp.bfloat16
F32 = jnp.float32
NT = (((1,), (1,)), ((), ()))


def _params(n_axes=1):
    return pltpu.CompilerParams(dimension_semantics=("arbitrary",) * n_axes,
                                vmem_limit_bytes=VMEM_LIMIT)


def _const(shape):
    nd = len(shape)
    return pl.BlockSpec(shape, lambda *_: (0,) * nd, pipeline_mode=pl.Buffered(1))


def _rows(width, tile=TM):
    return pl.BlockSpec((tile, width), lambda i: (i, 0))


P_TILES = N_P // TM


def _rows_head(width, tiles):
    return pl.BlockSpec((TM, width), lambda i: (jnp.minimum(i, tiles - 1), 0))


def _rows_tail(width, tiles):
    return pl.BlockSpec((TM, width), lambda i: (jnp.maximum(i - tiles, 0), 0))


def _rows_prompt(width):
    return _rows_head(width, P_TILES)


def _rows_sample(width):
    return _rows_tail(width, P_TILES)


def _split_rows(head_ref, tail_ref, tiles):
    return jnp.where(pl.program_id(0) < tiles, head_ref[...], tail_ref[...])


def _group_rows(p_ref, s_ref):
    return _split_rows(p_ref, s_ref, P_TILES)


def _rms(x, g):
    return x * lax.rsqrt(jnp.mean(x * x, axis=-1, keepdims=True) + EPS) * g


def _dot(a, b):
    return jnp.dot(a, b, preferred_element_type=F32)


def _softmax_rows(s_list):
    m = functools.reduce(jnp.maximum, [jnp.max(s, axis=-1, keepdims=True) for s in s_list])
    p_list = [jnp.exp(s - m) for s in s_list]
    inv = 1.0 / functools.reduce(jnp.add, [jnp.sum(p, axis=-1, keepdims=True) for p in p_list])
    return [(p * inv).astype(BF16) for p in p_list]


def _inproj_kernel(xp_ref, xs_ref, gmix_ref, w1_ref, w2_ref, w3_ref, gq_ref, gkv_ref, wuq1_ref, wuq2_ref,
                   wukp_ref, wuv_ref, cos_ref, sin_ref,
                   lat_ref, kr_ref, q16_ref, k16_ref, v16_ref, qb16_ref, kb16_ref, vb16_ref,
                   kb32_ref, vb32_ref, sga_ref, sgb_ref):
    h = _rms(_group_rows(xp_ref, xs_ref), gmix_ref[...]).astype(BF16)
    cos = cos_ref[...]
    sin = sin_ref[...]
    z1 = _dot(h, w1_ref[...])
    k128 = z1[:, 640:768] * cos + z1[:, 768:896] * sin
    kr_ref[...] = k128[:, NOPE:NOPE + ROPE]
    cqn = _rms(z1[:, :Q_LORA], gq_ref[...]).astype(BF16)
    cos8 = jnp.concatenate([cos] * H_A, axis=1)
    sin8 = jnp.concatenate([sin] * H_A, axis=1)
    q = _dot(cqn, wuq1_ref[...]) * cos8 + _dot(cqn, wuq2_ref[...]) * sin8
    q16_ref[...] = q.astype(BF16)
    ckvn = _rms(z1[:, Q_LORA:Q_LORA + KV_LORA], gkv_ref[...])
    lat_ref[...] = ckvn
    ckvn16 = ckvn.astype(BF16)
    kk = _dot(ckvn16, wukp_ref[...]) + jnp.concatenate([k128] * H_A, axis=1)
    k16_ref[...] = kk.astype(BF16)
    v16_ref[...] = _dot(ckvn16, wuv_ref[...]).astype(BF16)
    z2 = _dot(h, w2_ref[...])
    qb16_ref[...] = z2[:, :HB].astype(BF16)
    kb = z2[:, HB:2 * HB]
    vb = z2[:, 2 * HB:]
    kb32_ref[...] = kb
    vb32_ref[...] = vb
    kb16_ref[...] = kb.astype(BF16)
    vb16_ref[...] = vb.astype(BF16)
    z3 = _dot(h, w3_ref[...])
    sga_ref[...] = jax.nn.sigmoid(z3[:, :D_MODEL])
    sgb_ref[...] = jax.nn.sigmoid(z3[:, D_MODEL:])


def _rope_rows():
    per_seq = SEQ // TM
    return pl.BlockSpec((TM, LANES), lambda i: (jnp.where(i < N_P // TM, i % per_seq, per_seq), 0))


def _inproj(xp, xs, gmix, w1, w2, w3, gq, gkv, wuq1, wuq2, wukp, wuv, cos, sin):
    n = N_TOK
    widths = [(KV_LORA, F32), (ROPE, F32), (H_A * LANES, BF16), (H_A * LANES, BF16), (H_A * V_A, BF16),
              (HB, BF16), (HB, BF16), (HB, BF16), (HB, F32), (HB, F32), (D_MODEL, F32), (D_MODEL, F32)]
    return pl.pallas_call(
        _inproj_kernel,
        out_shape=[jax.ShapeDtypeStruct((n, w), d) for w, d in widths],
        grid=(n // TM,),
        in_specs=[_rows_prompt(D_MODEL), _rows_sample(D_MODEL),
                  _const(gmix.shape), _const(w1.shape), _const(w2.shape), _const(w3.shape),
                  _const(gq.shape), _const(gkv.shape), _const(wuq1.shape), _const(wuq2.shape),
                  _const(wukp.shape), _const(wuv.shape), _rope_rows(), _rope_rows()],
        out_specs=[_rows(w) for w, _ in widths],
        compiler_params=_params(),
        name="inproj",
    )(xp, xs, gmix, w1, w2, w3, gq, gkv, wuq1, wuq2, wukp, wuv, cos, sin)


def _pair_select(o_even, o_odd):
    lane = lax.broadcasted_iota(jnp.int32, o_even.shape, 1)
    return jnp.where(lane < V_A, o_even, o_odd)


def _mla_prompt_tile(q_ref, k_ref, v_ref, o_ref, i, nk):
    q_chunk = (i * TQ + lax.broadcasted_iota(jnp.int32, (TQ, nk), 0)) // CHUNK
    k_chunk = lax.broadcasted_iota(jnp.int32, (TQ, nk), 1) // CHUNK
    mask = k_chunk <= q_chunk
    scores = []
    for hd in range(H_A):
        qh = q_ref[:, hd * LANES:(hd + 1) * LANES]
        kh = k_ref[0:nk, hd * LANES:(hd + 1) * LANES]
        s = lax.dot_general(qh, kh, NT, preferred_element_type=F32) * MLA_SCALE
        scores.append(jnp.where(mask, s, NEG))
    weights = [_softmax_rows([s])[0] for s in scores]
    outs = [_dot(weights[hd], v_ref[0:nk, (hd // 2) * LANES:(hd // 2 + 1) * LANES]) for hd in range(H_A)]
    pairs = [_pair_select(outs[2 * pr], outs[2 * pr + 1]) for pr in range(H_A // 2)]
    o_ref[...] = jnp.concatenate(pairs, axis=1).astype(BF16)


def _mla_prompt_kernel(q_ref, k_ref, v_ref, o_ref):
    i = pl.program_id(1)
    tiles_per_step = KV_STEP // TQ
    for grp in range(SEQ // KV_STEP):
        @pl.when(i // tiles_per_step == grp)
        def _():
            _mla_prompt_tile(q_ref, k_ref, v_ref, o_ref, i, (grp + 1) * KV_STEP)


def _mla_prompt(q16, k16, v16):
    nq = SEQ // TQ
    return pl.pallas_call(
        _mla_prompt_kernel,
        out_shape=jax.ShapeDtypeStruct((N_P, H_A * V_A), BF16),
        grid=(BATCH, nq),
        in_specs=[pl.BlockSpec((TQ, H_A * LANES), lambda b, i: (b * nq + i, 0)),
                  pl.BlockSpec((SEQ, H_A * LANES), lambda b, i: (b, 0)),
                  pl.BlockSpec((SEQ, H_A * V_A), lambda b, i: (b, 0))],
        out_specs=pl.BlockSpec((TQ, H_A * V_A), lambda b, i: (b * nq + i, 0)),
        compiler_params=_params(2),
        name="mla_prompt",
    )(q16, k16, v16)


def _mla_sample_kernel(q_ref, clat_ref, ckr_ref, nlat_ref, nkr_ref, wukt_ref, wuvh_ref, o_ref):
    q = q_ref[...].astype(F32)
    qlat, qrope = [], []
    for hd in range(H_A):
        qn = q[:, hd * LANES:hd * LANES + NOPE].astype(BF16)
        qlat.append(_dot(qn, wukt_ref[hd]).astype(BF16))
        qrope.append(q[:, hd * LANES + NOPE:hd * LANES + NOPE + ROPE].astype(BF16))
    qlat = jnp.concatenate(qlat, axis=0)
    qrope = jnp.concatenate(qrope, axis=0)
    clat = clat_ref[...].astype(BF16)
    ckr = ckr_ref[...].astype(BF16)
    nlat = nlat_ref[...].astype(BF16)
    nkr = nkr_ref[...].astype(BF16)
    s_c = (lax.dot_general(qlat, clat, NT, preferred_element_type=F32)
           + lax.dot_general(qrope, ckr, NT, preferred_element_type=F32)) * MLA_SCALE
    s_n = (lax.dot_general(qlat, nlat, NT, preferred_element_type=F32)
           + lax.dot_general(qrope, nkr, NT, preferred_element_type=F32)) * MLA_SCALE
    w_c, w_n = _softmax_rows([s_c, s_n])
    olat = (_dot(w_c, clat) + _dot(w_n, nlat)).astype(BF16)
    outs = [_dot(olat[hd * DEC_SEQ:(hd + 1) * DEC_SEQ, :], wuvh_ref[hd]) for hd in range(H_A)]
    o_ref[...] = jnp.concatenate(outs, axis=1).astype(BF16)


def _mla_sample(q16, cache_lat, cache_kr, lat, kr, wukt, wuvh):
    off = N_P // DEC_SEQ
    return pl.pallas_call(
        _mla_sample_kernel,
        out_shape=jax.ShapeDtypeStruct((N_S, H_A * V_A), BF16),
        grid=(DEC_BATCH,),
        in_specs=[pl.BlockSpec((DEC_SEQ, H_A * LANES), lambda b: (off + b, 0)),
                  pl.BlockSpec((None, PAST_LEN, KV_LORA), lambda b: (b, 0, 0)),
                  pl.BlockSpec((None, PAST_LEN, ROPE), lambda b: (b, 0, 0)),
                  pl.BlockSpec((DEC_SEQ, KV_LORA), lambda b: (off + b, 0)),
                  pl.BlockSpec((DEC_SEQ, ROPE), lambda b: (off + b, 0)),
                  _const(wukt.shape), _const(wuvh.shape)],
        out_specs=pl.BlockSpec((DEC_SEQ, H_A * V_A), lambda b: (b, 0)),
        compiler_params=_params(),
        name="mla_sample",
    )(q16, cache_lat, cache_kr, lat, kr, wukt, wuvh)


def _band_heads(q, blocks, bias_refs, valid):
    lane = lax.broadcasted_iota(jnp.int32, (q.shape[0], LANES), 1)
    scores = []
    for hd in range(H_B):
        sl = slice((hd // 2) * LANES, (hd // 2 + 1) * LANES)
        q2 = q[:, sl]
        own = (lane >= DH_B) if hd % 2 else (lane < DH_B)
        qm = jnp.where(own, q2, jnp.zeros_like(q2))
        ss = [lax.dot_general(qm, k[:, sl], NT, preferred_element_type=F32) * BAND_SCALE + b_ref[hd]
              for (k, _), b_ref in zip(blocks, bias_refs)]
        if valid is not None:
            ss = [jnp.where(valid, s, NEG) for s in ss]
        scores.append(ss)
    weights = [_softmax_rows(ss) for ss in scores]
    outs = []
    for hd in range(H_B):
        sl = slice((hd // 2) * LANES, (hd // 2 + 1) * LANES)
        outs.append(functools.reduce(jnp.add, [_dot(w, v[:, sl]) for w, (_, v) in zip(weights[hd], blocks)]))
    pairs = [_pair_select(outs[2 * pr], outs[2 * pr + 1]) for pr in range(H_B // 2)]
    return jnp.concatenate(pairs, axis=1).astype(BF16)


def _band_prompt_kernel(q_ref, k_ref, v_ref, bias_ref, o_ref):
    c = pl.program_id(1)
    start = pl.multiple_of(c * CHUNK, CHUNK)
    k = k_ref[pl.ds(start, BAND), :]
    v = v_ref[pl.ds(start, BAND), :]
    valid = (start - WINDOW_B + lax.broadcasted_iota(jnp.int32, (CHUNK, BAND), 1)) >= 0
    o_ref[...] = _band_heads(q_ref[...], [(k, v)], [bias_ref], valid)


def _band_prompt(qb16, kpad, vpad, bias):
    nc = SEQ // CHUNK
    return pl.pallas_call(
        _band_prompt_kernel,
        out_shape=jax.ShapeDtypeStruct((N_P, HB), BF16),
        grid=(BATCH, nc),
        in_specs=[pl.BlockSpec((CHUNK, HB), lambda b, c: (b * nc + c, 0)),
                  pl.BlockSpec((None, SEQ + WINDOW_B, HB), lambda b, c: (b, 0, 0)),
                  pl.BlockSpec((None, SEQ + WINDOW_B, HB), lambda b, c: (b, 0, 0)),
                  _const(bias.shape)],
        out_specs=pl.BlockSpec((CHUNK, HB), lambda b, c: (b * nc + c, 0)),
        compiler_params=_params(2),
        name="band_prompt",
    )(qb16, kpad, vpad, bias)


def _band_sample_kernel(q_ref, ck_ref, cv_ref, nk16_ref, nv16_ref, nk32_ref, nv32_ref, bias_c_ref, bias_n_ref,
                        o_ref, bk_ref, bv_ref):
    ck = ck_ref[...]
    cv = cv_ref[...]
    blocks = [(ck.astype(BF16), cv.astype(BF16)), (nk16_ref[...], nv16_ref[...])]
    o_ref[...] = _band_heads(q_ref[...], blocks, [bias_c_ref, bias_n_ref], None)
    keep = WINDOW_B - DEC_SEQ
    bk_ref[0:keep, :] = ck[DEC_SEQ:, :]
    bk_ref[keep:, :] = nk32_ref[...]
    bv_ref[0:keep, :] = cv[DEC_SEQ:, :]
    bv_ref[keep:, :] = nv32_ref[...]


def _band_sample(qb16, cache_k, cache_v, kb16, vb16, kb32, vb32, bias_c, bias_n):
    off = N_P // DEC_SEQ
    new = lambda: pl.BlockSpec((DEC_SEQ, HB), lambda b: (off + b, 0))
    cache = lambda: pl.BlockSpec((None, WINDOW_B, HB), lambda b: (b, 0, 0))
    return pl.pallas_call(
        _band_sample_kernel,
        out_shape=[jax.ShapeDtypeStruct((N_S, HB), BF16),
                   jax.ShapeDtypeStruct((DEC_BATCH, WINDOW_B, HB), F32),
                   jax.ShapeDtypeStruct((DEC_BATCH, WINDOW_B, HB), F32)],
        grid=(DEC_BATCH,),
        in_specs=[new(), cache(), cache(), new(), new(), new(), new(),
                  _const(bias_c.shape), _const(bias_n.shape)],
        out_specs=[pl.BlockSpec((DEC_SEQ, HB), lambda b: (b, 0)), cache(), cache()],
        compiler_params=_params(),
        name="band_sample",
    )(qb16, cache_k, cache_v, kb16, vb16, kb32, vb32, bias_c, bias_n)


def _mix_out_kernel(oa_ref, ob_ref, sga_ref, sgb_ref, xp_ref, xs_ref, wa_ref, wb_ref, wout_ref, gffn_ref,
                    x1_ref, h2_ref, h16_ref):
    merged = sga_ref[...] * _dot(oa_ref[...], wa_ref[...]) + sgb_ref[...] * _dot(ob_ref[...], wb_ref[...])
    x1 = _group_rows(xp_ref, xs_ref) + _dot(merged.astype(BF16), wout_ref[...])
    x1_ref[...] = x1
    h2 = _rms(x1, gffn_ref[...])
    h2_ref[...] = h2
    h16_ref[...] = h2.astype(BF16)


def _mix_out(oa, ob, sga, sgb, xp, xs, wa, wb, wout, gffn):
    n = N_TOK
    return pl.pallas_call(
        _mix_out_kernel,
        out_shape=[jax.ShapeDtypeStruct((n, D_MODEL), F32), jax.ShapeDtypeStruct((n, D_MODEL), F32),
                   jax.ShapeDtypeStruct((n, D_MODEL), BF16)],
        grid=(n // TM,),
        in_specs=[_rows(H_A * V_A), _rows(HB), _rows(D_MODEL), _rows(D_MODEL),
                  _rows_prompt(D_MODEL), _rows_sample(D_MODEL),
                  _const(wa.shape), _const(wb.shape), _const(wout.shape), _const(gffn.shape)],
        out_specs=[_rows(D_MODEL)] * 3,
        compiler_params=_params(),
        name="mix_out",
    )(oa, ob, sga, sgb, xp, xs, wa, wb, wout, gffn)


def _top16(s, n_rows):
    row = lax.broadcasted_iota(jnp.int32, (n_rows, TQ), 0).astype(F32)
    vals, ids = [], []
    for _ in range(TOPK):
        m = jnp.max(s, axis=0, keepdims=True)
        i = jnp.min(jnp.where(s == m, row, float(n_rows)), axis=0, keepdims=True)
        vals.append(m)
        ids.append(i)
        s = jnp.where(row == i, -jnp.inf, s)
    return jnp.concatenate(vals, axis=0), jnp.concatenate(ids, axis=0)


def _pair_candidates(f1, f2):
    h = SUBLANES
    blocks = [f1(0, 1, 0, h), f1(0, 1, h, 2 * h)]
    blocks += [f1(a, a + 1, 0, h) for a in range(1, h)]
    blocks += [f2(h, 2 * h, 0, 1)]
    return jnp.concatenate(blocks, axis=0)


def _peer_topk_kernel(h_ref, wqt_ref, keys_ref, idx_ref, g_ref, qt_ref, g_t, id_t):
    qt_ref[...] = lax.dot_general(wqt_ref[...], h_ref[...], NT, preferred_element_type=F32).astype(BF16)
    r8 = lax.broadcasted_iota(jnp.int32, (SUBLANES, TQ), 0).astype(F32)
    flat = lambda a0, a1, b0, b1: (r8 + float(b0)) + float(TOPK * a0)
    flat_t = lambda a0, a1, b0, b1: (r8 + float(a0)) * float(TOPK) + float(b0)
    cflat = _pair_candidates(flat, flat_t)

    def one_head(hd):
        tops = []
        for p in range(2):
            hp = hd * 2 + p
            qs = qt_ref[pl.ds(pl.multiple_of(hp * N_KEYS, N_KEYS), N_KEYS), :]
            tops.append(_top16(_dot(keys_ref[hp], qs), N_KEYS))
        (s1, i1), (s2, i2) = tops
        add = lambda x, y: (lambda a0, a1, b0, b1: x[a0:a1, :] + y[b0:b1, :])
        cand = _pair_candidates(add(s1, s2), add(s1, s2))
        e1 = i1 * float(N_KEYS)
        cidx = _pair_candidates(add(e1, i2), add(e1, i2))
        vals, ids = [], []
        for _ in range(TOPK):
            m = jnp.max(cand, axis=0, keepdims=True)
            c = jnp.min(jnp.where(cand == m, cflat, float(TOPK * TOPK)), axis=0, keepdims=True)
            sel = cflat == c
            vals.append(m)
            ids.append(jnp.sum(jnp.where(sel, cidx, 0.0), axis=0, keepdims=True))
            cand = jnp.where(sel, -jnp.inf, cand)
        best = jnp.concatenate(vals, axis=0)
        e = jnp.exp(best - best[0:1, :])
        g = e / jnp.sum(e, axis=0, keepdims=True)
        base = pl.multiple_of(hd * TOPK, TOPK)
        g_t[pl.ds(base, TOPK), :] = g
        id_t[pl.ds(base, TOPK), :] = jnp.concatenate(ids, axis=0)

    def head_group(grp, carry):
        for e in range(HEADS_PER_TRIP):
            one_head(grp * HEADS_PER_TRIP + e)
        return carry

    lax.fori_loop(0, PEER_HEADS // HEADS_PER_TRIP, head_group, 0)
    g_ref[...] = g_t[...].T
    idx_ref[...] = (id_t[...].T * float(ROW_WORDS)).astype(jnp.int32)


def _peer_topk(h16, wqt, keys, first, count):
    n = count
    nt = n // TQ
    fb = first // TQ
    return pl.pallas_call(
        _peer_topk_kernel,
        out_shape=[jax.ShapeDtypeStruct((n, SLOTS), jnp.int32),
                   jax.ShapeDtypeStruct((n, SLOTS), F32)],
        grid=(nt,),
        in_specs=[pl.BlockSpec((TQ, D_MODEL), lambda i: (fb + i, 0)), _const(wqt.shape), _const(keys.shape)],
        out_specs=[_rows(SLOTS, TQ)] * 2,
        scratch_shapes=[pltpu.VMEM((2 * PEER_HEADS * N_KEYS, TQ), BF16),
                        pltpu.VMEM((SLOTS, TQ), F32), pltpu.VMEM((SLOTS, TQ), F32)],
        compiler_params=_params(),
        name="peer_topk",
    )(h16, wqt, keys)


def _pack_table(t):
    b = lax.bitcast_convert_type(t.astype(BF16), jnp.uint16).astype(jnp.uint32)
    packed = (b[:, :HALF] << 16) | b[:, HALF:]
    return packed.reshape(N_EXPERTS * ROW_WORDS, LANES)


def _unpack(w):
    hi = lax.bitcast_convert_type(w & jnp.uint32(0xFFFF0000), F32)
    lo = lax.bitcast_convert_type(w << 16, F32)
    return hi, lo


def _slot_pairs(idx_ref, t):
    out = []
    for grp in range(PAIRS // IDX_GROUP):
        lo = idx_ref.at[0, pl.ds(t * SLOTS + grp * IDX_GROUP, IDX_GROUP)]
        hi = idx_ref.at[0, pl.ds(t * SLOTS + PAIRS + grp * IDX_GROUP, IDX_GROUP)]
        out.extend((lo[k], hi[k]) for k in range(IDX_GROUP))
    return out


def _load_pair(tab, ia, ib):
    wa = tab[pl.ds(pl.multiple_of(ia, ROW_WORDS), ROW_WORDS), :]
    wb = tab[pl.ds(pl.multiple_of(ib, ROW_WORDS), ROW_WORDS), :]
    return _unpack(jnp.concatenate([wa, wb], axis=0))


def _peer_u_kernel(idx_ref, x_ref, g_ref, tab, c_ref, prod_a, prod_b, rbuf, a_t):
    lane = lax.broadcasted_iota(jnp.int32, (SLOTS, LANES), 1)

    @pl.when(pl.program_id(0) == 0)
    def _():
        rbuf[...] = jnp.zeros_like(rbuf)
        a_t[...] = jnp.zeros_like(a_t)

    def reduce(slot, tok):
        col = jnp.sum(rbuf[slot], axis=1, keepdims=True)
        a_t[...] = jnp.where(lane == tok, col, a_t[...])

    def gather(t, slot, prod):
        xt = x_ref[t]
        x_hi = jnp.concatenate([xt[0:ROW_WORDS, :]] * 2, axis=0)
        x_lo = jnp.concatenate([xt[ROW_WORDS:, :]] * 2, axis=0)
        for j, (ia, ib) in enumerate(_slot_pairs(idx_ref, t)):
            hi, lo = _load_pair(tab, ia, ib)
            prod[pl.ds(j, SUBLANES, stride=STRIDE), :] = hi * x_hi + lo * x_lo
        halves = [functools.reduce(jnp.add, [prod[pl.ds((h * ROW_WORDS + s) * STRIDE, PAIRS), :]
                                             for s in range(ROW_WORDS)]) for h in range(2)]
        rbuf[slot] = jnp.concatenate(halves, axis=0)

    def two_tokens(p, carry):
        t0 = 2 * p
        reduce(0, t0 - 2)
        reduce(1, t0 - 1)
        gather(t0, 0, prod_a)
        gather(t0 + 1, 1, prod_b)
        return carry

    lax.fori_loop(0, TB // 2, two_tokens, 0)
    reduce(0, TB - 2)
    reduce(1, TB - 1)
    a = a_t[...].T[0:TB, :]
    gelu = 0.5 * a * (1.0 + lax.erf(a * (2.0 ** -0.5)))
    c_ref[...] = g_ref[...] * gelu


def _peer_u(idx, x, g, tab_u, first):
    n = x.shape[0]
    count = idx.shape[0]
    nb = count // TB
    fb = first // TB
    return pl.pallas_call(
        _peer_u_kernel,
        out_shape=jax.ShapeDtypeStruct((count, SLOTS), F32),
        grid=(nb,),
        in_specs=[pl.BlockSpec((None, 1, TB * SLOTS), lambda i: (i, 0, 0), memory_space=pltpu.SMEM),
                  pl.BlockSpec((TB, SUBLANES, LANES), lambda i: (fb + i, 0, 0)),
                  pl.BlockSpec((TB, SLOTS), lambda i: (i, 0)),
                  _const(tab_u.shape)],
        out_specs=pl.BlockSpec((TB, SLOTS), lambda i: (i, 0)),
        scratch_shapes=[pltpu.VMEM((SUBLANES * STRIDE, LANES), F32),
                        pltpu.VMEM((SUBLANES * STRIDE, LANES), F32),
                        pltpu.VMEM((2, SLOTS, LANES), F32),
                        pltpu.VMEM((SLOTS, LANES), F32)],
        compiler_params=_params(),
        name="peer_u",
    )(idx.reshape(nb, 1, TB * SLOTS), x.reshape(n, SUBLANES, LANES), g, tab_u)


def _peer_v_kernel(idx_ref, c_ref, tab, y_ref, *crep):
    row = lax.broadcasted_iota(jnp.int32, (SUBLANES, LANES), 0)
    eye = (lax.broadcasted_iota(jnp.int32, (SLOTS, LANES), 0)
           == lax.broadcasted_iota(jnp.int32, (SLOTS, LANES), 1))

    def spread(t, crep):
        col = jnp.sum(jnp.where(eye, c_ref[pl.ds(t, 1), :], 0.0), axis=1, keepdims=True)
        crep[...] = jnp.broadcast_to(col, (SLOTS, LANES))

    def token(t, crep):
        acc_hi = jnp.zeros((SUBLANES, LANES), F32)
        acc_lo = jnp.zeros((SUBLANES, LANES), F32)
        for j, (ia, ib) in enumerate(_slot_pairs(idx_ref, t)):
            hi, lo = _load_pair(tab, ia, ib)
            cm = jnp.where(row < ROW_WORDS, crep[pl.ds(j, 1), :], crep[pl.ds(PAIRS + j, 1), :])
            acc_hi = acc_hi + cm * hi
            acc_lo = acc_lo + cm * lo
        y_ref[t] = jnp.concatenate([acc_hi[0:ROW_WORDS, :] + acc_hi[ROW_WORDS:, :],
                                    acc_lo[0:ROW_WORDS, :] + acc_lo[ROW_WORDS:, :]], axis=0)

    ahead = CREP_RING // 2
    for k in range(ahead):
        spread(k, crep[k])

    def ring_trip(p, carry):
        t0 = CREP_RING * p
        for k in range(CREP_RING):
            spread(jnp.minimum(t0 + k + ahead, TB - 1), crep[(k + ahead) % CREP_RING])
            token(t0 + k, crep[k])
        return carry

    lax.fori_loop(0, TB // CREP_RING, ring_trip, 0)


def _peer_v(idx, c, tab_v):
    n = c.shape[0]
    nb = n // TB
    y = pl.pallas_call(
        _peer_v_kernel,
        out_shape=jax.ShapeDtypeStruct((n, SUBLANES, LANES), F32),
        grid=(nb,),
        in_specs=[pl.BlockSpec((None, 1, TB * SLOTS), lambda i: (i, 0, 0), memory_space=pltpu.SMEM),
                  pl.BlockSpec((TB, SLOTS), lambda i: (i, 0)),
                  _const(tab_v.shape)],
        out_specs=pl.BlockSpec((TB, SUBLANES, LANES), lambda i: (i, 0, 0)),
        scratch_shapes=[pltpu.VMEM((SLOTS, LANES), F32)] * CREP_RING,
        compiler_params=_params(),
        name="peer_v",
    )(idx.reshape(nb, 1, TB * SLOTS), c, tab_v)
    return y.reshape(n, D_MODEL)


def _sc_peer_v(idx, c, table):
    n = idx.shape[0]
    per_worker = n // SC_WORKERS
    batches = per_worker // SC_TOKENS
    groups = SC_TOKENS * SLOTS // SC_ROWS
    chunks = D_MODEL // SC_LANES
    mesh = plsc.VectorSubcoreMesh(core_axis_name="c", subcore_axis_name="s")

    def body(idx_hbm, c_hbm, tab_hbm, y_hbm, idx_v, c_v, rows_a, rows_b, y_v, sem_a, sem_b):
        wid = lax.axis_index("s") * 2 + lax.axis_index("c")
        zero = jnp.zeros((SC_LANES,), F32)

        def gather(g, rows, sem):
            return pltpu.make_async_copy(tab_hbm.at[idx_v.at[pl.ds(g * SC_ROWS, SC_ROWS)]], rows, sem)

        def accumulate(g, rows):
            tok = g // (SLOTS // SC_ROWS)
            ws = [plsc.load_gather(c_v, [jnp.full((SC_LANES,), g * SC_ROWS + r, jnp.int32)])
                  for r in range(SC_ROWS)]

            @pl.loop(0, chunks // SC_ACC)
            def _(cb):
                base = pl.multiple_of(cb * (SC_ACC * SC_LANES), SC_ACC * SC_LANES)
                acc = [zero] * SC_ACC
                for r in range(SC_ROWS):
                    for k in range(SC_ACC):
                        acc[k] = acc[k] + ws[r] * rows[r, pl.ds(base + k * SC_LANES, SC_LANES)]
                for k in range(SC_ACC):
                    plsc.addupdate(y_v.at[tok, pl.ds(base + k * SC_LANES, SC_LANES)], acc[k])

        @pl.loop(0, batches)
        def _(b):
            t0 = wid * per_worker + b * SC_TOKENS
            pltpu.sync_copy(idx_hbm.at[pl.ds(t0 * SLOTS, SC_TOKENS * SLOTS)], idx_v)
            pltpu.sync_copy(c_hbm.at[pl.ds(t0 * SLOTS, SC_TOKENS * SLOTS)], c_v)
            for t in range(SC_TOKENS):
                for ch in range(chunks):
                    y_v[t, pl.ds(ch * SC_LANES, SC_LANES)] = zero
            gather(0, rows_a, sem_a).start()

            @pl.loop(0, groups // 2)
            def _(h):
                g = 2 * h
                gather(g + 1, rows_b, sem_b).start()
                gather(g, rows_a, sem_a).wait()
                accumulate(g, rows_a)

                @pl.when(h + 1 < groups // 2)
                def _():
                    gather(g + 2, rows_a, sem_a).start()
                gather(g + 1, rows_b, sem_b).wait()
                accumulate(g + 1, rows_b)

            pltpu.sync_copy(y_v, y_hbm.at[pl.ds(t0, SC_TOKENS)])

    return pl.kernel(
        body,
        out_type=jax.ShapeDtypeStruct((n, D_MODEL), F32),
        mesh=mesh,
        scratch_types=[pltpu.VMEM((SC_TOKENS * SLOTS,), jnp.int32),
                       pltpu.VMEM((SC_TOKENS * SLOTS,), F32),
                       pltpu.VMEM((SC_ROWS, D_MODEL), F32),
                       pltpu.VMEM((SC_ROWS, D_MODEL), F32),
                       pltpu.VMEM((SC_TOKENS, D_MODEL), F32),
                       pltpu.SemaphoreType.DMA,
                       pltpu.SemaphoreType.DMA],
        compiler_params=pltpu.CompilerParams(needs_layout_passes=False),
        name="sc_peer_v",
    )(idx.reshape(n * SLOTS), c.reshape(n * SLOTS), table)


def _final_kernel(x1_ref, yh_ref, yt_ref, pp_ref, ps_ref, gple_ref, wg_ref, wp_ref, gfin_ref, op_ref, os_ref):
    x2 = x1_ref[...] + _split_rows(yh_ref, yt_ref, N_SC // TM)
    gate = jax.nn.sigmoid(_dot(_rms(x2, gple_ref[...]).astype(BF16), wg_ref[...]))
    x3 = x2 + gate * _dot(_group_rows(pp_ref, ps_ref).astype(BF16), wp_ref[...])
    out = _rms(x3, gfin_ref[...])
    is_prompt = pl.program_id(0) < P_TILES

    @pl.when(is_prompt)
    def _():
        op_ref[...] = out

    @pl.when(jnp.logical_not(is_prompt))
    def _():
        os_ref[...] = out


def _final(x1, y_head, y_tail, pp, ps, gple, wg, wp, gfin):
    return pl.pallas_call(
        _final_kernel,
        out_shape=[jax.ShapeDtypeStruct((N_P, D_MODEL), F32), jax.ShapeDtypeStruct((N_S, D_MODEL), F32)],
        grid=(N_TOK // TM,),
        in_specs=[_rows(D_MODEL), _rows_head(D_MODEL, N_SC // TM), _rows_tail(D_MODEL, N_SC // TM),
                  _rows_prompt(PLE_DIM), _rows_sample(PLE_DIM),
                  _const(gple.shape), _const(wg.shape), _const(wp.shape), _const(gfin.shape)],
        out_specs=[_rows_prompt(D_MODEL), _rows_sample(D_MODEL)],
        compiler_params=_params(),
        name="ple_final",
    )(x1, y_head, y_tail, pp, ps, gple, wg, wp, gfin)


def _rope_tables():
    half = ROPE // 2
    freqs = ROPE_THETA ** (-np.arange(half, dtype=np.float32) / half)
    pos = np.concatenate([np.arange(SEQ), np.tile(PAST_LEN + np.arange(DEC_SEQ), TM // DEC_SEQ)])
    ang = jnp.asarray(pos, F32)[:, None] * jnp.asarray(freqs, F32)[None, :]
    cos, sin = jnp.cos(ang), jnp.sin(ang)
    n = pos.shape[0]
    pad = jnp.zeros((n, LANES - NOPE - ROPE), F32)
    cos_t = jnp.concatenate([jnp.ones((n, NOPE), F32), cos, cos, pad], axis=1)
    sin_t = jnp.concatenate([jnp.zeros((n, NOPE), F32), sin, sin, pad], axis=1)
    return cos_t, sin_t


def _rot_cols(w):
    half = ROPE // 2
    return jnp.concatenate([-w[..., half:], w[..., :half]], axis=-1)


def _rel_bias(table, n_q, n_k):
    diag = np.arange(-(n_q - 1), n_k)
    line = table[:, np.clip(WINDOW_B - diag, -REL_CLIP, REL_CLIP) + REL_CLIP]
    return jnp.stack([line[:, n_q - 1 - i:n_q - 1 - i + n_k] for i in range(n_q)], axis=1)


def kernel(x_prompt, x_sample, cache_latent, cache_krope, cache_band_k, cache_band_v, p_prompt, p_sample, g_mix, w_in, g_q_lora, g_kv_lora, w_uq, w_uk, w_uv, rel_bias, w_a_proj, w_b_proj, w_out, g_ffn, w_query, sub_keys, expert_u, expert_v, g_ple, w_ple_gate, w_ple_proj, g_final):
    w = w_in[0]
    o_kr = Q_LORA + KV_LORA
    o_qb = o_kr + ROPE
    o_ga = o_qb + 3 * HB
    w_kr = w[:, o_kr:o_qb]
    z64 = jnp.zeros((D_MODEL, NOPE), F32)
    z32 = jnp.zeros((D_MODEL, LANES - NOPE - ROPE), F32)
    w1 = jnp.concatenate([w[:, :o_kr], z64, w_kr, z32, z64, _rot_cols(w_kr), z32], axis=1).astype(BF16)
    w2 = w[:, o_qb:o_ga].astype(BF16)
    w3 = w[:, o_ga:].astype(BF16)
    wq = w_uq[0].reshape(Q_LORA, H_A, NOPE + ROPE)
    zq64 = jnp.zeros((Q_LORA, H_A, NOPE), F32)
    zq32 = jnp.zeros((Q_LORA, H_A, LANES - NOPE - ROPE), F32)
    wuq1 = jnp.concatenate([wq, zq32], axis=-1).reshape(Q_LORA, H_A * LANES).astype(BF16)
    wuq2 = jnp.concatenate([zq64, _rot_cols(wq[..., NOPE:]), zq32], axis=-1).reshape(Q_LORA, H_A * LANES).astype(BF16)
    wukp = jnp.concatenate([w_uk[0], jnp.zeros((KV_LORA, H_A, LANES - NOPE), F32)], axis=-1)
    wukp = wukp.reshape(KV_LORA, H_A * LANES).astype(BF16)
    wuv = w_uv[0].reshape(KV_LORA, H_A * V_A).astype(BF16)
    wukt = jnp.transpose(w_uk[0], (1, 2, 0)).astype(BF16)
    wuvh = jnp.transpose(w_uv[0], (1, 0, 2)).astype(BF16)
    cos_t, sin_t = _rope_tables()
    row = lambda g: g.reshape(1, -1)

    xp = x_prompt.reshape(N_P, D_MODEL)
    xs = x_sample.reshape(N_S, D_MODEL)

    (lat, kr, q16, k16, v16, qb16, kb16, vb16, kb32, vb32, sga, sgb) = _inproj(
        xp, xs, row(g_mix[0]), w1, w2, w3, row(g_q_lora[0]), row(g_kv_lora[0]), wuq1, wuq2, wukp, wuv, cos_t, sin_t)

    oa_p = _mla_prompt(q16, k16, v16)
    oa_s = _mla_sample(q16, cache_latent[0], cache_krope[0], lat, kr, wukt, wuvh)

    pad = ((0, 0), (WINDOW_B, 0), (0, 0))
    kpad = jnp.pad(kb16[:N_P].reshape(BATCH, SEQ, HB), pad)
    vpad = jnp.pad(vb16[:N_P].reshape(BATCH, SEQ, HB), pad)
    tab = rel_bias[0]
    bias_p = _rel_bias(tab, CHUNK, BAND)
    bias_s = _rel_bias(tab, DEC_SEQ, WINDOW_B + DEC_SEQ)
    ob_p = _band_prompt(qb16, kpad, vpad, bias_p)
    ob_s, bk_s, bv_s = _band_sample(qb16, cache_band_k[0].reshape(DEC_BATCH, WINDOW_B, HB),
                                    cache_band_v[0].reshape(DEC_BATCH, WINDOW_B, HB),
                                    kb16, vb16, kb32, vb32, bias_s[:, :, :WINDOW_B], bias_s[:, :, WINDOW_B:])

    oa = jnp.concatenate([oa_p, oa_s], axis=0)
    ob = jnp.concatenate([ob_p, ob_s], axis=0)
    x1, h2, h16 = _mix_out(oa, ob, sga, sgb, xp, xs, w_a_proj[0].astype(BF16), w_b_proj[0].astype(BF16),
                           w_out[0].astype(BF16), row(g_ffn[0]))
    wqt = jnp.transpose(w_query[0]).astype(BF16)
    keys = sub_keys[0].reshape(2 * PEER_HEADS, N_KEYS, N_KEYS).astype(BF16)

    tab_u = _pack_table(expert_u[0])
    tab_v = _pack_table(expert_v[0])
    idx_sc, gw_sc = _peer_topk(h16, wqt, keys, 0, N_SC)
    c_sc = _peer_u(idx_sc, h2, gw_sc, tab_u, 0)
    y_sc = _sc_peer_v(idx_sc // ROW_WORDS, c_sc, expert_v[0])
    idx_tc, gw_tc = _peer_topk(h16, wqt, keys, N_SC, N_TOK - N_SC)
    c_tc = _peer_u(idx_tc, h2, gw_tc, tab_u, N_SC)
    y_tc = _peer_v(idx_tc, c_tc, tab_v)

    out_p, out_s = _final(x1, y_sc, y_tc, p_prompt.reshape(N_P, PLE_DIM), p_sample.reshape(N_S, PLE_DIM), row(g_ple[0]),
                          w_ple_gate[0].astype(BF16), w_ple_proj[0].astype(BF16), row(g_final))

    y_prompt = out_p.reshape(BATCH, SEQ, D_MODEL)
    y_sample = out_s.reshape(DEC_BATCH, DEC_SEQ, D_MODEL)
    lat_p = lat[:N_P].reshape(1, BATCH, SEQ, KV_LORA)
    kr_p = kr[:N_P].reshape(1, BATCH, SEQ, ROPE)
    bk_p = kb32[:N_P].reshape(BATCH, SEQ, H_B, DH_B)[None, :, SEQ - WINDOW_B:]
    bv_p = vb32[:N_P].reshape(BATCH, SEQ, H_B, DH_B)[None, :, SEQ - WINDOW_B:]
    lat_s = lat[N_P:].reshape(1, DEC_BATCH, DEC_SEQ, KV_LORA)
    kr_s = kr[N_P:].reshape(1, DEC_BATCH, DEC_SEQ, ROPE)
    bk_s = bk_s.reshape(1, DEC_BATCH, WINDOW_B, H_B, DH_B)
    bv_s = bv_s.reshape(1, DEC_BATCH, WINDOW_B, H_B, DH_B)
    return (y_prompt, y_sample, lat_p, kr_p, bk_p, bv_p, lat_s, kr_s, bk_s, bv_s)
```

```python
import functools
import jax
import jax.numpy as jnp
import numpy as np
from jax import lax
from jax.experimental import pallas as pl
from jax.experimental.pallas import tpu as pltpu
from jax.experimental.pallas import tpu_sc as plsc

D_MODEL = 1024
BATCH = 16
SEQ = 2048
DEC_BATCH = 32
DEC_SEQ = 32
PAST_LEN = 4096
CHUNK = 64
EPS = 1e-6
H_A = 8
Q_LORA = 384
KV_LORA = 256
NOPE = 64
ROPE = 32
V_A = 64
ROPE_THETA = 10000.0
MLA_SCALE = (NOPE + ROPE) ** -0.5
H_B = 8
DH_B = 64
BAND_PREV = 8
BAND = (BAND_PREV + 1) * CHUNK
WINDOW_B = BAND_PREV * CHUNK
REL_CLIP = 256
BAND_SCALE = DH_B ** -0.5
PEER_HEADS = 8
N_KEYS = 128
N_EXPERTS = N_KEYS * N_KEYS
TOPK = 16
PLE_DIM = 256

N_P = BATCH * SEQ
N_S = DEC_BATCH * DEC_SEQ
N_TOK = N_P + N_S
HB = H_B * DH_B
HALF = D_MODEL // 2
SLOTS = PEER_HEADS * TOPK
LANES = 128
SUBLANES = 8
TM = 256
TQ = 128
KV_STEP = 256
TB = 128
CREP_RING = 4
ROW_WORDS = HALF // LANES
PAIRS = SLOTS // 2
STRIDE = 72
HEADS_PER_TRIP = 8
IDX_GROUP = 8
VMEM_LIMIT = 56 * 1024 * 1024
SC_WORKERS = 32
SC_LANES = 16
SC_TOKENS = 8
SC_ROWS = 32
SC_ACC = 8
N_SC = 18432
NEG = -1e30
BF16 = jnp.bfloat16
F32 = jnp.float32
NT = (((1,), (1,)), ((), ()))


def _params(n_axes=1):
    return pltpu.CompilerParams(dimension_semantics=("arbitrary",) * n_axes,
                                vmem_limit_bytes=VMEM_LIMIT)


def _const(shape):
    nd = len(shape)
    return pl.BlockSpec(shape, lambda *_: (0,) * nd, pipeline_mode=pl.Buffered(1))


def _rows(width, tile=TM):
    return pl.BlockSpec((tile, width), lambda i: (i, 0))


P_TILES = N_P // TM


def _rows_head(width, tiles):
    return pl.BlockSpec((TM, width), lambda i: (jnp.minimum(i, tiles - 1), 0))


def _rows_tail(width, tiles):
    return pl.BlockSpec((TM, width), lambda i: (jnp.maximum(i - tiles, 0), 0))


def _rows_prompt(width):
    return _rows_head(width, P_TILES)


def _rows_sample(width):
    return _rows_tail(width, P_TILES)


def _split_rows(head_ref, tail_ref, tiles):
    return jnp.where(pl.program_id(0) < tiles, head_ref[...], tail_ref[...])


def _group_rows(p_ref, s_ref):
    return _split_rows(p_ref, s_ref, P_TILES)


def _rms(x, g):
    return x * lax.rsqrt(jnp.mean(x * x, axis=-1, keepdims=True) + EPS) * g


def _dot(a, b):
    return jnp.dot(a, b, preferred_element_type=F32)


def _softmax_rows(s_list):
    m = functools.reduce(jnp.maximum, [jnp.max(s, axis=-1, keepdims=True) for s in s_list])
    p_list = [jnp.exp(s - m) for s in s_list]
    inv = 1.0 / functools.reduce(jnp.add, [jnp.sum(p, axis=-1, keepdims=True) for p in p_list])
    return [(p * inv).astype(BF16) for p in p_list]


def _inproj_kernel(xp_ref, xs_ref, gmix_ref, w1_ref, w2_ref, w3_ref, gq_ref, gkv_ref, wuq1_ref, wuq2_ref,
                   wukp_ref, wuv_ref, cos_ref, sin_ref,
                   lat_ref, kr_ref, q16_ref, k16_ref, v16_ref, qb16_ref, kb16_ref, vb16_ref,
                   kb32_ref, vb32_ref, sga_ref, sgb_ref):
    h = _rms(_group_rows(xp_ref, xs_ref), gmix_ref[...]).astype(BF16)
    cos = cos_ref[...]
    sin = sin_ref[...]
    z1 = _dot(h, w1_ref[...])
    k128 = z1[:, 640:768] * cos + z1[:, 768:896] * sin
    kr_ref[...] = k128[:, NOPE:NOPE + ROPE]
    cqn = _rms(z1[:, :Q_LORA], gq_ref[...]).astype(BF16)
    cos8 = jnp.concatenate([cos] * H_A, axis=1)
    sin8 = jnp.concatenate([sin] * H_A, axis=1)
    q = _dot(cqn, wuq1_ref[...]) * cos8 + _dot(cqn, wuq2_ref[...]) * sin8
    q16_ref[...] = q.astype(BF16)
    ckvn = _rms(z1[:, Q_LORA:Q_LORA + KV_LORA], gkv_ref[...])
    lat_ref[...] = ckvn
    ckvn16 = ckvn.astype(BF16)
    kk = _dot(ckvn16, wukp_ref[...]) + jnp.concatenate([k128] * H_A, axis=1)
    k16_ref[...] = kk.astype(BF16)
    v16_ref[...] = _dot(ckvn16, wuv_ref[...]).astype(BF16)
    z2 = _dot(h, w2_ref[...])
    qb16_ref[...] = z2[:, :HB].astype(BF16)
    kb = z2[:, HB:2 * HB]
    vb = z2[:, 2 * HB:]
    kb32_ref[...] = kb
    vb32_ref[...] = vb
    kb16_ref[...] = kb.astype(BF16)
    vb16_ref[...] = vb.astype(BF16)
    z3 = _dot(h, w3_ref[...])
    sga_ref[...] = jax.nn.sigmoid(z3[:, :D_MODEL])
    sgb_ref[...] = jax.nn.sigmoid(z3[:, D_MODEL:])


def _rope_rows():
    per_seq = SEQ // TM
    return pl.BlockSpec((TM, LANES), lambda i: (jnp.where(i < N_P // TM, i % per_seq, per_seq), 0))


def _inproj(xp, xs, gmix, w1, w2, w3, gq, gkv, wuq1, wuq2, wukp, wuv, cos, sin):
    n = N_TOK
    widths = [(KV_LORA, F32), (ROPE, F32), (H_A * LANES, BF16), (H_A * LANES, BF16), (H_A * V_A, BF16),
              (HB, BF16), (HB, BF16), (HB, BF16), (HB, F32), (HB, F32), (D_MODEL, F32), (D_MODEL, F32)]
    return pl.pallas_call(
        _inproj_kernel,
        out_shape=[jax.ShapeDtypeStruct((n, w), d) for w, d in widths],
        grid=(n // TM,),
        in_specs=[_rows_prompt(D_MODEL), _rows_sample(D_MODEL),
                  _const(gmix.shape), _const(w1.shape), _const(w2.shape), _const(w3.shape),
                  _const(gq.shape), _const(gkv.shape), _const(wuq1.shape), _const(wuq2.shape),
                  _const(wukp.shape), _const(wuv.shape), _rope_rows(), _rope_rows()],
        out_specs=[_rows(w) for w, _ in widths],
        compiler_params=_params(),
        name="inproj",
    )(xp, xs, gmix, w1, w2, w3, gq, gkv, wuq1, wuq2, wukp, wuv, cos, sin)


def _pair_select(o_even, o_odd):
    lane = lax.broadcasted_iota(jnp.int32, o_even.shape, 1)
    return jnp.where(lane < V_A, o_even, o_odd)


def _mla_prompt_tile(q_ref, k_ref, v_ref, o_ref, i, nk):
    q_chunk = (i * TQ + lax.broadcasted_iota(jnp.int32, (TQ, nk), 0)) // CHUNK
    k_chunk = lax.broadcasted_iota(jnp.int32, (TQ, nk), 1) // CHUNK
    mask = k_chunk <= q_chunk
    scores = []
    for hd in range(H_A):
        qh = q_ref[:, hd * LANES:(hd + 1) * LANES]
        kh = k_ref[0:nk, hd * LANES:(hd + 1) * LANES]
        s = lax.dot_general(qh, kh, NT, preferred_element_type=F32) * MLA_SCALE
        scores.append(jnp.where(mask, s, NEG))
    weights = [_softmax_rows([s])[0] for s in scores]
    outs = [_dot(weights[hd], v_ref[0:nk, (hd // 2) * LANES:(hd // 2 + 1) * LANES]) for hd in range(H_A)]
    pairs = [_pair_select(outs[2 * pr], outs[2 * pr + 1]) for pr in range(H_A // 2)]
    o_ref[...] = jnp.concatenate(pairs, axis=1).astype(BF16)


def _mla_prompt_kernel(q_ref, k_ref, v_ref, o_ref):
    i = pl.program_id(1)
    tiles_per_step = KV_STEP // TQ
    for grp in range(SEQ // KV_STEP):
        @pl.when(i // tiles_per_step == grp)
        def _():
            _mla_prompt_tile(q_ref, k_ref, v_ref, o_ref, i, (grp + 1) * KV_STEP)


def _mla_prompt(q16, k16, v16):
    nq = SEQ // TQ
    return pl.pallas_call(
        _mla_prompt_kernel,
        out_shape=jax.ShapeDtypeStruct((N_P, H_A * V_A), BF16),
        grid=(BATCH, nq),
        in_specs=[pl.BlockSpec((TQ, H_A * LANES), lambda b, i: (b * nq + i, 0)),
                  pl.BlockSpec((SEQ, H_A * LANES), lambda b, i: (b, 0)),
                  pl.BlockSpec((SEQ, H_A * V_A), lambda b, i: (b, 0))],
        out_specs=pl.BlockSpec((TQ, H_A * V_A), lambda b, i: (b * nq + i, 0)),
        compiler_params=_params(2),
        name="mla_prompt",
    )(q16, k16, v16)


def _mla_sample_kernel(q_ref, clat_ref, ckr_ref, nlat_ref, nkr_ref, wukt_ref, wuvh_ref, o_ref):
    q = q_ref[...].astype(F32)
    qlat, qrope = [], []
    for hd in range(H_A):
        qn = q[:, hd * LANES:hd * LANES + NOPE].astype(BF16)
        qlat.append(_dot(qn, wukt_ref[hd]).astype(BF16))
        qrope.append(q[:, hd * LANES + NOPE:hd * LANES + NOPE + ROPE].astype(BF16))
    qlat = jnp.concatenate(qlat, axis=0)
    qrope = jnp.concatenate(qrope, axis=0)
    clat = clat_ref[...].astype(BF16)
    ckr = ckr_ref[...].astype(BF16)
    nlat = nlat_ref[...].astype(BF16)
    nkr = nkr_ref[...].astype(BF16)
    s_c = (lax.dot_general(qlat, clat, NT, preferred_element_type=F32)
           + lax.dot_general(qrope, ckr, NT, preferred_element_type=F32)) * MLA_SCALE
    s_n = (lax.dot_general(qlat, nlat, NT, preferred_element_type=F32)
           + lax.dot_general(qrope, nkr, NT, preferred_element_type=F32)) * MLA_SCALE
    w_c, w_n = _softmax_rows([s_c, s_n])
    olat = (_dot(w_c, clat) + _dot(w_n, nlat)).astype(BF16)
    outs = [_dot(olat[hd * DEC_SEQ:(hd + 1) * DEC_SEQ, :], wuvh_ref[hd]) for hd in range(H_A)]
    o_ref[...] = jnp.concatenate(outs, axis=1).astype(BF16)


def _mla_sample(q16, cache_lat, cache_kr, lat, kr, wukt, wuvh):
    off = N_P // DEC_SEQ
    return pl.pallas_call(
        _mla_sample_kernel,
        out_shape=jax.ShapeDtypeStruct((N_S, H_A * V_A), BF16),
        grid=(DEC_BATCH,),
        in_specs=[pl.BlockSpec((DEC_SEQ, H_A * LANES), lambda b: (off + b, 0)),
                  pl.BlockSpec((None, PAST_LEN, KV_LORA), lambda b: (b, 0, 0)),
                  pl.BlockSpec((None, PAST_LEN, ROPE), lambda b: (b, 0, 0)),
                  pl.BlockSpec((DEC_SEQ, KV_LORA), lambda b: (off + b, 0)),
                  pl.BlockSpec((DEC_SEQ, ROPE), lambda b: (off + b, 0)),
                  _const(wukt.shape), _const(wuvh.shape)],
        out_specs=pl.BlockSpec((DEC_SEQ, H_A * V_A), lambda b: (b, 0)),
        compiler_params=_params(),
        name="mla_sample",
    )(q16, cache_lat, cache_kr, lat, kr, wukt, wuvh)


def _band_heads(q, blocks, bias_refs, valid):
    lane = lax.broadcasted_iota(jnp.int32, (q.shape[0], LANES), 1)
    scores = []
    for hd in range(H_B):
        sl = slice((hd // 2) * LANES, (hd // 2 + 1) * LANES)
        q2 = q[:, sl]
        own = (lane >= DH_B) if hd % 2 else (lane < DH_B)
        qm = jnp.where(own, q2, jnp.zeros_like(q2))
        ss = [lax.dot_general(qm, k[:, sl], NT, preferred_element_type=F32) * BAND_SCALE + b_ref[hd]
              for (k, _), b_ref in zip(blocks, bias_refs)]
        if valid is not None:
            ss = [jnp.where(valid, s, NEG) for s in ss]
        scores.append(ss)
    weights = [_softmax_rows(ss) for ss in scores]
    outs = []
    for hd in range(H_B):
        sl = slice((hd // 2) * LANES, (hd // 2 + 1) * LANES)
        outs.append(functools.reduce(jnp.add, [_dot(w, v[:, sl]) for w, (_, v) in zip(weights[hd], blocks)]))
    pairs = [_pair_select(outs[2 * pr], outs[2 * pr + 1]) for pr in range(H_B // 2)]
    return jnp.concatenate(pairs, axis=1).astype(BF16)


def _band_prompt_kernel(q_ref, k_ref, v_ref, bias_ref, o_ref):
    c = pl.program_id(1)
    start = pl.multiple_of(c * CHUNK, CHUNK)
    k = k_ref[pl.ds(start, BAND), :]
    v = v_ref[pl.ds(start, BAND), :]
    valid = (start - WINDOW_B + lax.broadcasted_iota(jnp.int32, (CHUNK, BAND), 1)) >= 0
    o_ref[...] = _band_heads(q_ref[...], [(k, v)], [bias_ref], valid)


def _band_prompt(qb16, kpad, vpad, bias):
    nc = SEQ // CHUNK
    return pl.pallas_call(
        _band_prompt_kernel,
        out_shape=jax.ShapeDtypeStruct((N_P, HB), BF16),
        grid=(BATCH, nc),
        in_specs=[pl.BlockSpec((CHUNK, HB), lambda b, c: (b * nc + c, 0)),
                  pl.BlockSpec((None, SEQ + WINDOW_B, HB), lambda b, c: (b, 0, 0)),
                  pl.BlockSpec((None, SEQ + WINDOW_B, HB), lambda b, c: (b, 0, 0)),
                  _const(bias.shape)],
        out_specs=pl.BlockSpec((CHUNK, HB), lambda b, c: (b * nc + c, 0)),
        compiler_params=_params(2),
        name="band_prompt",
    )(qb16, kpad, vpad, bias)


def _band_sample_kernel(q_ref, ck_ref, cv_ref, nk16_ref, nv16_ref, nk32_ref, nv32_ref, bias_c_ref, bias_n_ref,
                        o_ref, bk_ref, bv_ref):
    ck = ck_ref[...]
    cv = cv_ref[...]
    blocks = [(ck.astype(BF16), cv.astype(BF16)), (nk16_ref[...], nv16_ref[...])]
    o_ref[...] = _band_heads(q_ref[...], blocks, [bias_c_ref, bias_n_ref], None)
    keep = WINDOW_B - DEC_SEQ
    bk_ref[0:keep, :] = ck[DEC_SEQ:, :]
    bk_ref[keep:, :] = nk32_ref[...]
    bv_ref[0:keep, :] = cv[DEC_SEQ:, :]
    bv_ref[keep:, :] = nv32_ref[...]


def _band_sample(qb16, cache_k, cache_v, kb16, vb16, kb32, vb32, bias_c, bias_n):
    off = N_P // DEC_SEQ
    new = lambda: pl.BlockSpec((DEC_SEQ, HB), lambda b: (off + b, 0))
    cache = lambda: pl.BlockSpec((None, WINDOW_B, HB), lambda b: (b, 0, 0))
    return pl.pallas_call(
        _band_sample_kernel,
        out_shape=[jax.ShapeDtypeStruct((N_S, HB), BF16),
                   jax.ShapeDtypeStruct((DEC_BATCH, WINDOW_B, HB), F32),
                   jax.ShapeDtypeStruct((DEC_BATCH, WINDOW_B, HB), F32)],
        grid=(DEC_BATCH,),
        in_specs=[new(), cache(), cache(), new(), new(), new(), new(),
                  _const(bias_c.shape), _const(bias_n.shape)],
        out_specs=[pl.BlockSpec((DEC_SEQ, HB), lambda b: (b, 0)), cache(), cache()],
        compiler_params=_params(),
        name="band_sample",
    )(qb16, cache_k, cache_v, kb16, vb16, kb32, vb32, bias_c, bias_n)


def _mix_out_kernel(oa_ref, ob_ref, sga_ref, sgb_ref, xp_ref, xs_ref, wa_ref, wb_ref, wout_ref, gffn_ref,
                    x1_ref, h2_ref, h16_ref):
    merged = sga_ref[...] * _dot(oa_ref[...], wa_ref[...]) + sgb_ref[...] * _dot(ob_ref[...], wb_ref[...])
    x1 = _group_rows(xp_ref, xs_ref) + _dot(merged.astype(BF16), wout_ref[...])
    x1_ref[...] = x1
    h2 = _rms(x1, gffn_ref[...])
    h2_ref[...] = h2
    h16_ref[...] = h2.astype(BF16)


def _mix_out(oa, ob, sga, sgb, xp, xs, wa, wb, wout, gffn):
    n = N_TOK
    return pl.pallas_call(
        _mix_out_kernel,
        out_shape=[jax.ShapeDtypeStruct((n, D_MODEL), F32), jax.ShapeDtypeStruct((n, D_MODEL), F32),
                   jax.ShapeDtypeStruct((n, D_MODEL), BF16)],
        grid=(n // TM,),
        in_specs=[_rows(H_A * V_A), _rows(HB), _rows(D_MODEL), _rows(D_MODEL),
                  _rows_prompt(D_MODEL), _rows_sample(D_MODEL),
                  _const(wa.shape), _const(wb.shape), _const(wout.shape), _const(gffn.shape)],
        out_specs=[_rows(D_MODEL)] * 3,
        compiler_params=_params(),
        name="mix_out",
    )(oa, ob, sga, sgb, xp, xs, wa, wb, wout, gffn)


def _top16(s, n_rows):
    row = lax.broadcasted_iota(jnp.int32, (n_rows, TQ), 0).astype(F32)
    vals, ids = [], []
    for _ in range(TOPK):
        m = jnp.max(s, axis=0, keepdims=True)
        i = jnp.min(jnp.where(s == m, row, float(n_rows)), axis=0, keepdims=True)
        vals.append(m)
        ids.append(i)
        s = jnp.where(row == i, -jnp.inf, s)
    return jnp.concatenate(vals, axis=0), jnp.concatenate(ids, axis=0)


def _pair_candidates(f1, f2):
    h = SUBLANES
    blocks = [f1(0, 1, 0, h), f1(0, 1, h, 2 * h)]
    blocks += [f1(a, a + 1, 0, h) for a in range(1, h)]
    blocks += [f2(h, 2 * h, 0, 1)]
    return jnp.concatenate(blocks, axis=0)


def _peer_topk_kernel(h_ref, wqt_ref, keys_ref, idx_ref, g_ref, qt_ref, g_t, id_t):
    qt_ref[...] = lax.dot_general(wqt_ref[...], h_ref[...], NT, preferred_element_type=F32).astype(BF16)
    r8 = lax.broadcasted_iota(jnp.int32, (SUBLANES, TQ), 0).astype(F32)
    flat = lambda a0, a1, b0, b1: (r8 + float(b0)) + float(TOPK * a0)
    flat_t = lambda a0, a1, b0, b1: (r8 + float(a0)) * float(TOPK) + float(b0)
    cflat = _pair_candidates(flat, flat_t)

    def one_head(hd):
        tops = []
        for p in range(2):
            hp = hd * 2 + p
            qs = qt_ref[pl.ds(pl.multiple_of(hp * N_KEYS, N_KEYS), N_KEYS), :]
            tops.append(_top16(_dot(keys_ref[hp], qs), N_KEYS))
        (s1, i1), (s2, i2) = tops
        add = lambda x, y: (lambda a0, a1, b0, b1: x[a0:a1, :] + y[b0:b1, :])
        cand = _pair_candidates(add(s1, s2), add(s1, s2))
        e1 = i1 * float(N_KEYS)
        cidx = _pair_candidates(add(e1, i2), add(e1, i2))
        vals, ids = [], []
        for _ in range(TOPK):
            m = jnp.max(cand, axis=0, keepdims=True)
            c = jnp.min(jnp.where(cand == m, cflat, float(TOPK * TOPK)), axis=0, keepdims=True)
            sel = cflat == c
            vals.append(m)
            ids.append(jnp.sum(jnp.where(sel, cidx, 0.0), axis=0, keepdims=True))
            cand = jnp.where(sel, -jnp.inf, cand)
        best = jnp.concatenate(vals, axis=0)
        e = jnp.exp(best - best[0:1, :])
        g = e / jnp.sum(e, axis=0, keepdims=True)
        base = pl.multiple_of(hd * TOPK, TOPK)
        g_t[pl.ds(base, TOPK), :] = g
        id_t[pl.ds(base, TOPK), :] = jnp.concatenate(ids, axis=0)

    def head_group(grp, carry):
        for e in range(HEADS_PER_TRIP):
            one_head(grp * HEADS_PER_TRIP + e)
        return carry

    lax.fori_loop(0, PEER_HEADS // HEADS_PER_TRIP, head_group, 0)
    g_ref[...] = g_t[...].T
    idx_ref[...] = (id_t[...].T * float(ROW_WORDS)).astype(jnp.int32)


def _peer_topk(h16, wqt, keys, first, count):
    n = count
    nt = n // TQ
    fb = first // TQ
    return pl.pallas_call(
        _peer_topk_kernel,
        out_shape=[jax.ShapeDtypeStruct((n, SLOTS), jnp.int32),
                   jax.ShapeDtypeStruct((n, SLOTS), F32)],
        grid=(nt,),
        in_specs=[pl.BlockSpec((TQ, D_MODEL), lambda i: (fb + i, 0)), _const(wqt.shape), _const(keys.shape)],
        out_specs=[_rows(SLOTS, TQ)] * 2,
        scratch_shapes=[pltpu.VMEM((2 * PEER_HEADS * N_KEYS, TQ), BF16),
                        pltpu.VMEM((SLOTS, TQ), F32), pltpu.VMEM((SLOTS, TQ), F32)],
        compiler_params=_params(),
        name="peer_topk",
    )(h16, wqt, keys)


def _pack_table(t):
    b = lax.bitcast_convert_type(t.astype(BF16), jnp.uint16).astype(jnp.uint32)
    packed = (b[:, :HALF] << 16) | b[:, HALF:]
    return packed.reshape(N_EXPERTS * ROW_WORDS, LANES)


def _unpack(w):
    hi = lax.bitcast_convert_type(w & jnp.uint32(0xFFFF0000), F32)
    lo = lax.bitcast_convert_type(w << 16, F32)
    return hi, lo


def _slot_pairs(idx_ref, t):
    out = []
    for grp in range(PAIRS // IDX_GROUP):
        lo = idx_ref.at[0, pl.ds(t * SLOTS + grp * IDX_GROUP, IDX_GROUP)]
        hi = idx_ref.at[0, pl.ds(t * SLOTS + PAIRS + grp * IDX_GROUP, IDX_GROUP)]
        out.extend((lo[k], hi[k]) for k in range(IDX_GROUP))
    return out


def _load_pair(tab, ia, ib):
    wa = tab[pl.ds(pl.multiple_of(ia, ROW_WORDS), ROW_WORDS), :]
    wb = tab[pl.ds(pl.multiple_of(ib, ROW_WORDS), ROW_WORDS), :]
    return _unpack(jnp.concatenate([wa, wb], axis=0))


def _peer_u_kernel(idx_ref, x_ref, g_ref, tab, c_ref, prod_a, prod_b, rbuf, a_t):
    lane = lax.broadcasted_iota(jnp.int32, (SLOTS, LANES), 1)

    @pl.when(pl.program_id(0) == 0)
    def _():
        rbuf[...] = jnp.zeros_like(rbuf)
        a_t[...] = jnp.zeros_like(a_t)

    def reduce(slot, tok):
        col = jnp.sum(rbuf[slot], axis=1, keepdims=True)
        a_t[...] = jnp.where(lane == tok, col, a_t[...])

    def gather(t, slot, prod):
        xt = x_ref[t]
        x_hi = jnp.concatenate([xt[0:ROW_WORDS, :]] * 2, axis=0)
        x_lo = jnp.concatenate([xt[ROW_WORDS:, :]] * 2, axis=0)
        for j, (ia, ib) in enumerate(_slot_pairs(idx_ref, t)):
            hi, lo = _load_pair(tab, ia, ib)
            prod[pl.ds(j, SUBLANES, stride=STRIDE), :] = hi * x_hi + lo * x_lo
        halves = [functools.reduce(jnp.add, [prod[pl.ds((h * ROW_WORDS + s) * STRIDE, PAIRS), :]
                                             for s in range(ROW_WORDS)]) for h in range(2)]
        rbuf[slot] = jnp.concatenate(halves, axis=0)

    def two_tokens(p, carry):
        t0 = 2 * p
        reduce(0, t0 - 2)
        reduce(1, t0 - 1)
        gather(t0, 0, prod_a)
        gather(t0 + 1, 1, prod_b)
        return carry

    lax.fori_loop(0, TB // 2, two_tokens, 0)
    reduce(0, TB - 2)
    reduce(1, TB - 1)
    a = a_t[...].T[0:TB, :]
    gelu = 0.5 * a * (1.0 + lax.erf(a * (2.0 ** -0.5)))
    c_ref[...] = g_ref[...] * gelu


def _peer_u(idx, x, g, tab_u, first):
    n = x.shape[0]
    count = idx.shape[0]
    nb = count // TB
    fb = first // TB
    return pl.pallas_call(
        _peer_u_kernel,
        out_shape=jax.ShapeDtypeStruct((count, SLOTS), F32),
        grid=(nb,),
        in_specs=[pl.BlockSpec((None, 1, TB * SLOTS), lambda i: (i, 0, 0), memory_space=pltpu.SMEM),
                  pl.BlockSpec((TB, SUBLANES, LANES), lambda i: (fb + i, 0, 0)),
                  pl.BlockSpec((TB, SLOTS), lambda i: (i, 0)),
                  _const(tab_u.shape)],
        out_specs=pl.BlockSpec((TB, SLOTS), lambda i: (i, 0)),
        scratch_shapes=[pltpu.VMEM((SUBLANES * STRIDE, LANES), F32),
                        pltpu.VMEM((SUBLANES * STRIDE, LANES), F32),
                        pltpu.VMEM((2, SLOTS, LANES), F32),
                        pltpu.VMEM((SLOTS, LANES), F32)],
        compiler_params=_params(),
        name="peer_u",
    )(idx.reshape(nb, 1, TB * SLOTS), x.reshape(n, SUBLANES, LANES), g, tab_u)


def _peer_v_kernel(idx_ref, c_ref, tab, y_ref, *crep):
    row = lax.broadcasted_iota(jnp.int32, (SUBLANES, LANES), 0)
    eye = (lax.broadcasted_iota(jnp.int32, (SLOTS, LANES), 0)
           == lax.broadcasted_iota(jnp.int32, (SLOTS, LANES), 1))

    def spread(t, crep):
        col = jnp.sum(jnp.where(eye, c_ref[pl.ds(t, 1), :], 0.0), axis=1, keepdims=True)
        crep[...] = jnp.broadcast_to(col, (SLOTS, LANES))

    def token(t, crep):
        acc_hi = jnp.zeros((SUBLANES, LANES), F32)
        acc_lo = jnp.zeros((SUBLANES, LANES), F32)
        for j, (ia, ib) in enumerate(_slot_pairs(idx_ref, t)):
            hi, lo = _load_pair(tab, ia, ib)
            cm = jnp.where(row < ROW_WORDS, crep[pl.ds(j, 1), :], crep[pl.ds(PAIRS + j, 1), :])
            acc_hi = acc_hi + cm * hi
            acc_lo = acc_lo + cm * lo
        y_ref[t] = jnp.concatenate([acc_hi[0:ROW_WORDS, :] + acc_hi[ROW_WORDS:, :],
                                    acc_lo[0:ROW_WORDS, :] + acc_lo[ROW_WORDS:, :]], axis=0)

    ahead = CREP_RING // 2
    for k in range(ahead):
        spread(k, crep[k])

    def ring_trip(p, carry):
        t0 = CREP_RING * p
        for k in range(CREP_RING):
            spread(jnp.minimum(t0 + k + ahead, TB - 1), crep[(k + ahead) % CREP_RING])
            token(t0 + k, crep[k])
        return carry

    lax.fori_loop(0, TB // CREP_RING, ring_trip, 0)


def _peer_v(idx, c, tab_v):
    n = c.shape[0]
    nb = n // TB
    y = pl.pallas_call(
        _peer_v_kernel,
        out_shape=jax.ShapeDtypeStruct((n, SUBLANES, LANES), F32),
        grid=(nb,),
        in_specs=[pl.BlockSpec((None, 1, TB * SLOTS), lambda i: (i, 0, 0), memory_space=pltpu.SMEM),
                  pl.BlockSpec((TB, SLOTS), lambda i: (i, 0)),
                  _const(tab_v.shape)],
        out_specs=pl.BlockSpec((TB, SUBLANES, LANES), lambda i: (i, 0, 0)),
        scratch_shapes=[pltpu.VMEM((SLOTS, LANES), F32)] * CREP_RING,
        compiler_params=_params(),
        name="peer_v",
    )(idx.reshape(nb, 1, TB * SLOTS), c, tab_v)
    return y.reshape(n, D_MODEL)


def _sc_peer_v(idx, c, table):
    n = idx.shape[0]
    per_worker = n // SC_WORKERS
    batches = per_worker // SC_TOKENS
    groups = SC_TOKENS * SLOTS // SC_ROWS
    chunks = D_MODEL // SC_LANES
    mesh = plsc.VectorSubcoreMesh(core_axis_name="c", subcore_axis_name="s")

    def body(idx_hbm, c_hbm, tab_hbm, y_hbm, idx_v, c_v, rows_a, rows_b, y_v, sem_a, sem_b):
        wid = lax.axis_index("s") * 2 + lax.axis_index("c")
        zero = jnp.zeros((SC_LANES,), F32)

        def gather(g, rows, sem):
            return pltpu.make_async_copy(tab_hbm.at[idx_v.at[pl.ds(g * SC_ROWS, SC_ROWS)]], rows, sem)

        def accumulate(g, rows):
            tok = g // (SLOTS // SC_ROWS)
            ws = [plsc.load_gather(c_v, [jnp.full((SC_LANES,), g * SC_ROWS + r, jnp.int32)])
                  for r in range(SC_ROWS)]

            @pl.loop(0, chunks // SC_ACC)
            def _(cb):
                base = pl.multiple_of(cb * (SC_ACC * SC_LANES), SC_ACC * SC_LANES)
                acc = [zero] * SC_ACC
                for r in range(SC_ROWS):
                    for k in range(SC_ACC):
                        acc[k] = acc[k] + ws[r] * rows[r, pl.ds(base + k * SC_LANES, SC_LANES)]
                for k in range(SC_ACC):
                    plsc.addupdate(y_v.at[tok, pl.ds(base + k * SC_LANES, SC_LANES)], acc[k])

        @pl.loop(0, batches)
        def _(b):
            t0 = wid * per_worker + b * SC_TOKENS
            pltpu.sync_copy(idx_hbm.at[pl.ds(t0 * SLOTS, SC_TOKENS * SLOTS)], idx_v)
            pltpu.sync_copy(c_hbm.at[pl.ds(t0 * SLOTS, SC_TOKENS * SLOTS)], c_v)
            for t in range(SC_TOKENS):
                for ch in range(chunks):
                    y_v[t, pl.ds(ch * SC_LANES, SC_LANES)] = zero
            gather(0, rows_a, sem_a).start()

            @pl.loop(0, groups // 2)
            def _(h):
                g = 2 * h
                gather(g + 1, rows_b, sem_b).start()
                gather(g, rows_a, sem_a).wait()
                accumulate(g, rows_a)

                @pl.when(h + 1 < groups // 2)
                def _():
                    gather(g + 2, rows_a, sem_a).start()
                gather(g + 1, rows_b, sem_b).wait()
                accumulate(g + 1, rows_b)

            pltpu.sync_copy(y_v, y_hbm.at[pl.ds(t0, SC_TOKENS)])

    return pl.kernel(
        body,
        out_type=jax.ShapeDtypeStruct((n, D_MODEL), F32),
        mesh=mesh,
        scratch_types=[pltpu.VMEM((SC_TOKENS * SLOTS,), jnp.int32),
                       pltpu.VMEM((SC_TOKENS * SLOTS,), F32),
                       pltpu.VMEM((SC_ROWS, D_MODEL), F32),
                       pltpu.VMEM((SC_ROWS, D_MODEL), F32),
                       pltpu.VMEM((SC_TOKENS, D_MODEL), F32),
                       pltpu.SemaphoreType.DMA,
                       pltpu.SemaphoreType.DMA],
        compiler_params=pltpu.CompilerParams(needs_layout_passes=False),
        name="sc_peer_v",
    )(idx.reshape(n * SLOTS), c.reshape(n * SLOTS), table)


def _final_kernel(x1_ref, yh_ref, yt_ref, pp_ref, ps_ref, gple_ref, wg_ref, wp_ref, gfin_ref, op_ref, os_ref):
    x2 = x1_ref[...] + _split_rows(yh_ref, yt_ref, N_SC // TM)
    gate = jax.nn.sigmoid(_dot(_rms(x2, gple_ref[...]).astype(BF16), wg_ref[...]))
    x3 = x2 + gate * _dot(_group_rows(pp_ref, ps_ref).astype(BF16), wp_ref[...])
    out = _rms(x3, gfin_ref[...])
    is_prompt = pl.program_id(0) < P_TILES

    @pl.when(is_prompt)
    def _():
        op_ref[...] = out

    @pl.when(jnp.logical_not(is_prompt))
    def _():
        os_ref[...] = out


def _final(x1, y_head, y_tail, pp, ps, gple, wg, wp, gfin):
    return pl.pallas_call(
        _final_kernel,
        out_shape=[jax.ShapeDtypeStruct((N_P, D_MODEL), F32), jax.ShapeDtypeStruct((N_S, D_MODEL), F32)],
        grid=(N_TOK // TM,),
        in_specs=[_rows(D_MODEL), _rows_head(D_MODEL, N_SC // TM), _rows_tail(D_MODEL, N_SC // TM),
                  _rows_prompt(PLE_DIM), _rows_sample(PLE_DIM),
                  _const(gple.shape), _const(wg.shape), _const(wp.shape), _const(gfin.shape)],
        out_specs=[_rows_prompt(D_MODEL), _rows_sample(D_MODEL)],
        compiler_params=_params(),
        name="ple_final",
    )(x1, y_head, y_tail, pp, ps, gple, wg, wp, gfin)


def _rope_tables():
    half = ROPE // 2
    freqs = ROPE_THETA ** (-np.arange(half, dtype=np.float32) / half)
    pos = np.concatenate([np.arange(SEQ), np.tile(PAST_LEN + np.arange(DEC_SEQ), TM // DEC_SEQ)])
    ang = jnp.asarray(pos, F32)[:, None] * jnp.asarray(freqs, F32)[None, :]
    cos, sin = jnp.cos(ang), jnp.sin(ang)
    n = pos.shape[0]
    pad = jnp.zeros((n, LANES - NOPE - ROPE), F32)
    cos_t = jnp.concatenate([jnp.ones((n, NOPE), F32), cos, cos, pad], axis=1)
    sin_t = jnp.concatenate([jnp.zeros((n, NOPE), F32), sin, sin, pad], axis=1)
    return cos_t, sin_t


def _rot_cols(w):
    half = ROPE // 2
    return jnp.concatenate([-w[..., half:], w[..., :half]], axis=-1)


def _rel_bias(table, n_q, n_k):
    diag = np.arange(-(n_q - 1), n_k)
    line = table[:, np.clip(WINDOW_B - diag, -REL_CLIP, REL_CLIP) + REL_CLIP]
    return jnp.stack([line[:, n_q - 1 - i:n_q - 1 - i + n_k] for i in range(n_q)], axis=1)


def kernel(x_prompt, x_sample, cache_latent, cache_krope, cache_band_k, cache_band_v, p_prompt, p_sample, g_mix, w_in, g_q_lora, g_kv_lora, w_uq, w_uk, w_uv, rel_bias, w_a_proj, w_b_proj, w_out, g_ffn, w_query, sub_keys, expert_u, expert_v, g_ple, w_ple_gate, w_ple_proj, g_final):
    w = w_in[0]
    o_kr = Q_LORA + KV_LORA
    o_qb = o_kr + ROPE
    o_ga = o_qb + 3 * HB
    w_kr = w[:, o_kr:o_qb]
    z64 = jnp.zeros((D_MODEL, NOPE), F32)
    z32 = jnp.zeros((D_MODEL, LANES - NOPE - ROPE), F32)
    w1 = jnp.concatenate([w[:, :o_kr], z64, w_kr, z32, z64, _rot_cols(w_kr), z32], axis=1).astype(BF16)
    w2 = w[:, o_qb:o_ga].astype(BF16)
    w3 = w[:, o_ga:].astype(BF16)
    wq = w_uq[0].reshape(Q_LORA, H_A, NOPE + ROPE)
    zq64 = jnp.zeros((Q_LORA, H_A, NOPE), F32)
    zq32 = jnp.zeros((Q_LORA, H_A, LANES - NOPE - ROPE), F32)
    wuq1 = jnp.concatenate([wq, zq32], axis=-1).reshape(Q_LORA, H_A * LANES).astype(BF16)
    wuq2 = jnp.concatenate([zq64, _rot_cols(wq[..., NOPE:]), zq32], axis=-1).reshape(Q_LORA, H_A * LANES).astype(BF16)
    wukp = jnp.concatenate([w_uk[0], jnp.zeros((KV_LORA, H_A, LANES - NOPE), F32)], axis=-1)
    wukp = wukp.reshape(KV_LORA, H_A * LANES).astype(BF16)
    wuv = w_uv[0].reshape(KV_LORA, H_A * V_A).astype(BF16)
    wukt = jnp.transpose(w_uk[0], (1, 2, 0)).astype(BF16)
    wuvh = jnp.transpose(w_uv[0], (1, 0, 2)).astype(BF16)
    cos_t, sin_t = _rope_tables()
    row = lambda g: g.reshape(1, -1)

    xp = x_prompt.reshape(N_P, D_MODEL)
    xs = x_sample.reshape(N_S, D_MODEL)

    (lat, kr, q16, k16, v16, qb16, kb16, vb16, kb32, vb32, sga, sgb) = _inproj(
        xp, xs, row(g_mix[0]), w1, w2, w3, row(g_q_lora[0]), row(g_kv_lora[0]), wuq1, wuq2, wukp, wuv, cos_t, sin_t)

    oa_p = _mla_prompt(q16, k16, v16)
    oa_s = _mla_sample(q16, cache_latent[0], cache_krope[0], lat, kr, wukt, wuvh)

    pad = ((0, 0), (WINDOW_B, 0), (0, 0))
    kpad = jnp.pad(kb16[:N_P].reshape(BATCH, SEQ, HB), pad)
    vpad = jnp.pad(vb16[:N_P].reshape(BATCH, SEQ, HB), pad)
    tab = rel_bias[0]
    bias_p = _rel_bias(tab, CHUNK, BAND)
    bias_s = _rel_bias(tab, DEC_SEQ, WINDOW_B + DEC_SEQ)
    ob_p = _band_prompt(qb16, kpad, vpad, bias_p)
    ob_s, bk_s, bv_s = _band_sample(qb16, cache_band_k[0].reshape(DEC_BATCH, WINDOW_B, HB),
                                    cache_band_v[0].reshape(DEC_BATCH, WINDOW_B, HB),
                                    kb16, vb16, kb32, vb32, bias_s[:, :, :WINDOW_B], bias_s[:, :, WINDOW_B:])

    oa = jnp.concatenate([oa_p, oa_s], axis=0)
    ob = jnp.concatenate([ob_p, ob_s], axis=0)
    x1, h2, h16 = _mix_out(oa, ob, sga, sgb, xp, xs, w_a_proj[0].astype(BF16), w_b_proj[0].astype(BF16),
                           w_out[0].astype(BF16), row(g_ffn[0]))
    wqt = jnp.transpose(w_query[0]).astype(BF16)
    keys = sub_keys[0].reshape(2 * PEER_HEADS, N_KEYS, N_KEYS).astype(BF16)

    tab_u = _pack_table(expert_u[0])
    tab_v = _pack_table(expert_v[0])
    idx_sc, gw_sc = _peer_topk(h16, wqt, keys, 0, N_SC)
    c_sc = _peer_u(idx_sc, h2, gw_sc, tab_u, 0)
    y_sc = _sc_peer_v(idx_sc // ROW_WORDS, c_sc, expert_v[0])
    idx_tc, gw_tc = _peer_topk(h16, wqt, keys, N_SC, N_TOK - N_SC)
    c_tc = _peer_u(idx_tc, h2, gw_tc, tab_u, N_SC)
    y_tc = _peer_v(idx_tc, c_tc, tab_v)

    out_p, out_s = _final(x1, y_sc, y_tc, p_prompt.reshape(N_P, PLE_DIM), p_sample.reshape(N_S, PLE_DIM), row(g_ple[0]),
                          w_ple_gate[0].astype(BF16), w_ple_proj[0].astype(BF16), row(g_final))

    y_prompt = out_p.reshape(BATCH, SEQ, D_MODEL)
    y_sample = out_s.reshape(DEC_BATCH, DEC_SEQ, D_MODEL)
    lat_p = lat[:N_P].reshape(1, BATCH, SEQ, KV_LORA)
    kr_p = kr[:N_P].reshape(1, BATCH, SEQ, ROPE)
    bk_p = kb32[:N_P].reshape(BATCH, SEQ, H_B, DH_B)[None, :, SEQ - WINDOW_B:]
    bv_p = vb32[:N_P].reshape(BATCH, SEQ, H_B, DH_B)[None, :, SEQ - WINDOW_B:]
    lat_s = lat[N_P:].reshape(1, DEC_BATCH, DEC_SEQ, KV_LORA)
    kr_s = kr[N_P:].reshape(1, DEC_BATCH, DEC_SEQ, ROPE)
    bk_s = bk_s.reshape(1, DEC_BATCH, WINDOW_B, H_B, DH_B)
    bv_s = bv_s.reshape(1, DEC_BATCH, WINDOW_B, H_B, DH_B)
    return (y_prompt, y_sample, lat_p, kr_p, bk_p, bv_p, lat_s, kr_s, bk_s, bv_s)
```

```python
import functools
import jax
import jax.numpy as jnp
import numpy as np
from jax import lax
from jax.experimental import pallas as pl
from jax.experimental.pallas import tpu as pltpu
from jax.experimental.pallas import tpu_sc as plsc

D_MODEL = 1024
BATCH = 16
SEQ = 2048
DEC_BATCH = 32
DEC_SEQ = 32
PAST_LEN = 4096
CHUNK = 64
EPS = 1e-6
H_A = 8
Q_LORA = 384
KV_LORA = 256
NOPE = 64
ROPE = 32
V_A = 64
ROPE_THETA = 10000.0
MLA_SCALE = (NOPE + ROPE) ** -0.5
H_B = 8
DH_B = 64
BAND_PREV = 8
BAND = (BAND_PREV + 1) * CHUNK
WINDOW_B = BAND_PREV * CHUNK
REL_CLIP = 256
BAND_SCALE = DH_B ** -0.5
PEER_HEADS = 8
N_KEYS = 128
N_EXPERTS = N_KEYS * N_KEYS
TOPK = 16
PLE_DIM = 256

N_P = BATCH * SEQ
N_S = DEC_BATCH * DEC_SEQ
N_TOK = N_P + N_S
HB = H_B * DH_B
HALF = D_MODEL // 2
SLOTS = PEER_HEADS * TOPK
LANES = 128
SUBLANES = 8
TM = 256
TQ = 128
BAND_CHUNKS = 4
KV_STEP = 256
TB = 128
CREP_RING = 4
ROW_WORDS = HALF // LANES
PAIRS = SLOTS // 2
STRIDE = 72
HEADS_PER_TRIP = 8
IDX_GROUP = 8
VMEM_LIMIT = 56 * 1024 * 1024
SC_WORKERS = 32
SC_LANES = 16
SC_TOKENS = 8
SC_ROWS = 32
SC_ACC = 8
N_SC = 18432
NEG = -1e30
BF16 = jnp.bfloat16
F32 = jnp.float32
NT = (((1,), (1,)), ((), ()))


def _params(n_axes=1):
    return pltpu.CompilerParams(dimension_semantics=("arbitrary",) * n_axes,
                                vmem_limit_bytes=VMEM_LIMIT)


def _const(shape):
    nd = len(shape)
    return pl.BlockSpec(shape, lambda *_: (0,) * nd, pipeline_mode=pl.Buffered(1))


def _rows(width, tile=TM):
    return pl.BlockSpec((tile, width), lambda i: (i, 0))


P_TILES = N_P // TM


def _rows_head(width, tiles):
    return pl.BlockSpec((TM, width), lambda i: (jnp.minimum(i, tiles - 1), 0))


def _rows_tail(width, tiles):
    return pl.BlockSpec((TM, width), lambda i: (jnp.maximum(i - tiles, 0), 0))


def _rows_prompt(width):
    return _rows_head(width, P_TILES)


def _rows_sample(width):
    return _rows_tail(width, P_TILES)


def _split_rows(head_ref, tail_ref, tiles):
    return jnp.where(pl.program_id(0) < tiles, head_ref[...], tail_ref[...])


def _group_rows(p_ref, s_ref):
    return _split_rows(p_ref, s_ref, P_TILES)


def _rms(x, g):
    return x * lax.rsqrt(jnp.mean(x * x, axis=-1, keepdims=True) + EPS) * g


def _dot(a, b):
    return jnp.dot(a, b, preferred_element_type=F32)


def _softmax_rows(s_list):
    m = functools.reduce(jnp.maximum, [jnp.max(s, axis=-1, keepdims=True) for s in s_list])
    p_list = [jnp.exp(s - m) for s in s_list]
    inv = 1.0 / functools.reduce(jnp.add, [jnp.sum(p, axis=-1, keepdims=True) for p in p_list])
    return [(p * inv).astype(BF16) for p in p_list]


def _inproj_kernel(xp_ref, xs_ref, gmix_ref, w1_ref, w2_ref, w3_ref, gq_ref, gkv_ref, wuq1_ref, wuq2_ref,
                   wukp_ref, wuv_ref, cos_ref, sin_ref,
                   lat_ref, kr_ref, q16_ref, k16_ref, v16_ref, qb16_ref, kb16_ref, vb16_ref,
                   kb32_ref, vb32_ref, sga_ref, sgb_ref):
    h = _rms(_group_rows(xp_ref, xs_ref), gmix_ref[...]).astype(BF16)
    cos = cos_ref[...]
    sin = sin_ref[...]
    z1 = _dot(h, w1_ref[...])
    k128 = z1[:, 640:768] * cos + z1[:, 768:896] * sin
    kr_ref[...] = k128[:, NOPE:NOPE + ROPE]
    cqn = _rms(z1[:, :Q_LORA], gq_ref[...]).astype(BF16)
    cos8 = jnp.concatenate([cos] * H_A, axis=1)
    sin8 = jnp.concatenate([sin] * H_A, axis=1)
    q = _dot(cqn, wuq1_ref[...]) * cos8 + _dot(cqn, wuq2_ref[...]) * sin8
    q16_ref[...] = q.astype(BF16)
    ckvn = _rms(z1[:, Q_LORA:Q_LORA + KV_LORA], gkv_ref[...])
    lat_ref[...] = ckvn
    ckvn16 = ckvn.astype(BF16)
    kk = _dot(ckvn16, wukp_ref[...]) + jnp.concatenate([k128] * H_A, axis=1)
    k16_ref[...] = kk.astype(BF16)
    v16_ref[...] = _dot(ckvn16, wuv_ref[...]).astype(BF16)
    z2 = _dot(h, w2_ref[...])
    qb16_ref[...] = z2[:, :HB].astype(BF16)
    kb = z2[:, HB:2 * HB]
    vb = z2[:, 2 * HB:]
    kb32_ref[...] = kb
    vb32_ref[...] = vb
    kb16_ref[...] = kb.astype(BF16)
    vb16_ref[...] = vb.astype(BF16)
    z3 = _dot(h, w3_ref[...])
    sga_ref[...] = jax.nn.sigmoid(z3[:, :D_MODEL])
    sgb_ref[...] = jax.nn.sigmoid(z3[:, D_MODEL:])


def _rope_rows():
    per_seq = SEQ // TM
    return pl.BlockSpec((TM, LANES), lambda i: (jnp.where(i < N_P // TM, i % per_seq, per_seq), 0))


def _inproj(xp, xs, gmix, w1, w2, w3, gq, gkv, wuq1, wuq2, wukp, wuv, cos, sin):
    n = N_TOK
    widths = [(KV_LORA, F32), (ROPE, F32), (H_A * LANES, BF16), (H_A * LANES, BF16), (H_A * V_A, BF16),
              (HB, BF16), (HB, BF16), (HB, BF16), (HB, F32), (HB, F32), (D_MODEL, F32), (D_MODEL, F32)]
    return pl.pallas_call(
        _inproj_kernel,
        out_shape=[jax.ShapeDtypeStruct((n, w), d) for w, d in widths],
        grid=(n // TM,),
        in_specs=[_rows_prompt(D_MODEL), _rows_sample(D_MODEL),
                  _const(gmix.shape), _const(w1.shape), _const(w2.shape), _const(w3.shape),
                  _const(gq.shape), _const(gkv.shape), _const(wuq1.shape), _const(wuq2.shape),
                  _const(wukp.shape), _const(wuv.shape), _rope_rows(), _rope_rows()],
        out_specs=[_rows(w) for w, _ in widths],
        compiler_params=_params(),
        name="inproj",
    )(xp, xs, gmix, w1, w2, w3, gq, gkv, wuq1, wuq2, wukp, wuv, cos, sin)


def _pair_select(o_even, o_odd):
    lane = lax.broadcasted_iota(jnp.int32, o_even.shape, 1)
    return jnp.where(lane < V_A, o_even, o_odd)


def _mla_prompt_tile(q_ref, k_ref, v_ref, o_ref, i, nk):
    q_chunk = (i * TQ + lax.broadcasted_iota(jnp.int32, (TQ, nk), 0)) // CHUNK
    k_chunk = lax.broadcasted_iota(jnp.int32, (TQ, nk), 1) // CHUNK
    mask = k_chunk <= q_chunk
    scores = []
    for hd in range(H_A):
        qh = q_ref[:, hd * LANES:(hd + 1) * LANES]
        kh = k_ref[0:nk, hd * LANES:(hd + 1) * LANES]
        s = lax.dot_general(qh, kh, NT, preferred_element_type=F32) * MLA_SCALE
        scores.append(jnp.where(mask, s, NEG))
    weights = [_softmax_rows([s])[0] for s in scores]
    outs = [_dot(weights[hd], v_ref[0:nk, (hd // 2) * LANES:(hd // 2 + 1) * LANES]) for hd in range(H_A)]
    pairs = [_pair_select(outs[2 * pr], outs[2 * pr + 1]) for pr in range(H_A // 2)]
    o_ref[...] = jnp.concatenate(pairs, axis=1).astype(BF16)


def _mla_prompt_kernel(q_ref, k_ref, v_ref, o_ref):
    i = pl.program_id(1)
    tiles_per_step = KV_STEP // TQ
    for grp in range(SEQ // KV_STEP):
        @pl.when(i // tiles_per_step == grp)
        def _():
            _mla_prompt_tile(q_ref, k_ref, v_ref, o_ref, i, (grp + 1) * KV_STEP)


def _mla_prompt(q16, k16, v16):
    nq = SEQ // TQ
    return pl.pallas_call(
        _mla_prompt_kernel,
        out_shape=jax.ShapeDtypeStruct((N_P, H_A * V_A), BF16),
        grid=(BATCH, nq),
        in_specs=[pl.BlockSpec((TQ, H_A * LANES), lambda b, i: (b * nq + i, 0)),
                  pl.BlockSpec((SEQ, H_A * LANES), lambda b, i: (b, 0)),
                  pl.BlockSpec((SEQ, H_A * V_A), lambda b, i: (b, 0))],
        out_specs=pl.BlockSpec((TQ, H_A * V_A), lambda b, i: (b * nq + i, 0)),
        compiler_params=_params(2),
        name="mla_prompt",
    )(q16, k16, v16)


def _mla_sample_kernel(q_ref, clat_ref, ckr_ref, nlat_ref, nkr_ref, wukt_ref, wuvh_ref, o_ref):
    q = q_ref[...].astype(F32)
    qlat, qrope = [], []
    for hd in range(H_A):
        qn = q[:, hd * LANES:hd * LANES + NOPE].astype(BF16)
        qlat.append(_dot(qn, wukt_ref[hd]).astype(BF16))
        qrope.append(q[:, hd * LANES + NOPE:hd * LANES + NOPE + ROPE].astype(BF16))
    qlat = jnp.concatenate(qlat, axis=0)
    qrope = jnp.concatenate(qrope, axis=0)
    clat = clat_ref[...].astype(BF16)
    ckr = ckr_ref[...].astype(BF16)
    nlat = nlat_ref[...].astype(BF16)
    nkr = nkr_ref[...].astype(BF16)
    s_c = (lax.dot_general(qlat, clat, NT, preferred_element_type=F32)
           + lax.dot_general(qrope, ckr, NT, preferred_element_type=F32)) * MLA_SCALE
    s_n = (lax.dot_general(qlat, nlat, NT, preferred_element_type=F32)
           + lax.dot_general(qrope, nkr, NT, preferred_element_type=F32)) * MLA_SCALE
    w_c, w_n = _softmax_rows([s_c, s_n])
    olat = (_dot(w_c, clat) + _dot(w_n, nlat)).astype(BF16)
    outs = [_dot(olat[hd * DEC_SEQ:(hd + 1) * DEC_SEQ, :], wuvh_ref[hd]) for hd in range(H_A)]
    o_ref[...] = jnp.concatenate(outs, axis=1).astype(BF16)


def _mla_sample(q16, cache_lat, cache_kr, lat, kr, wukt, wuvh):
    off = N_P // DEC_SEQ
    return pl.pallas_call(
        _mla_sample_kernel,
        out_shape=jax.ShapeDtypeStruct((N_S, H_A * V_A), BF16),
        grid=(DEC_BATCH,),
        in_specs=[pl.BlockSpec((DEC_SEQ, H_A * LANES), lambda b: (off + b, 0)),
                  pl.BlockSpec((None, PAST_LEN, KV_LORA), lambda b: (b, 0, 0)),
                  pl.BlockSpec((None, PAST_LEN, ROPE), lambda b: (b, 0, 0)),
                  pl.BlockSpec((DEC_SEQ, KV_LORA), lambda b: (off + b, 0)),
                  pl.BlockSpec((DEC_SEQ, ROPE), lambda b: (off + b, 0)),
                  _const(wukt.shape), _const(wuvh.shape)],
        out_specs=pl.BlockSpec((DEC_SEQ, H_A * V_A), lambda b: (b, 0)),
        compiler_params=_params(),
        name="mla_sample",
    )(q16, cache_lat, cache_kr, lat, kr, wukt, wuvh)


def _band_heads(q, blocks, bias_refs, valid):
    lane = lax.broadcasted_iota(jnp.int32, (q.shape[0], LANES), 1)
    scores = []
    for hd in range(H_B):
        sl = slice((hd // 2) * LANES, (hd // 2 + 1) * LANES)
        q2 = q[:, sl]
        own = (lane >= DH_B) if hd % 2 else (lane < DH_B)
        qm = jnp.where(own, q2, jnp.zeros_like(q2))
        ss = [lax.dot_general(qm, k[:, sl], NT, preferred_element_type=F32) * BAND_SCALE + b_ref[hd]
              for (k, _), b_ref in zip(blocks, bias_refs)]
        if valid is not None:
            ss = [jnp.where(valid, s, NEG) for s in ss]
        scores.append(ss)
    weights = [_softmax_rows(ss) for ss in scores]
    outs = []
    for hd in range(H_B):
        sl = slice((hd // 2) * LANES, (hd // 2 + 1) * LANES)
        outs.append(functools.reduce(jnp.add, [_dot(w, v[:, sl]) for w, (_, v) in zip(weights[hd], blocks)]))
    pairs = [_pair_select(outs[2 * pr], outs[2 * pr + 1]) for pr in range(H_B // 2)]
    return jnp.concatenate(pairs, axis=1).astype(BF16)


def _band_prompt_kernel(q_ref, k_ref, v_ref, bias_ref, o_ref):
    for e in range(BAND_CHUNKS):
        c = pl.program_id(1) * BAND_CHUNKS + e
        start = pl.multiple_of(c * CHUNK, CHUNK)
        k = k_ref[pl.ds(start, BAND), :]
        v = v_ref[pl.ds(start, BAND), :]
        valid = (start - WINDOW_B + lax.broadcasted_iota(jnp.int32, (CHUNK, BAND), 1)) >= 0
        rows = slice(e * CHUNK, (e + 1) * CHUNK)
        o_ref[rows, :] = _band_heads(q_ref[rows, :], [(k, v)], [bias_ref], valid)


def _band_prompt(qb16, kpad, vpad, bias):
    nc = SEQ // (CHUNK * BAND_CHUNKS)
    return pl.pallas_call(
        _band_prompt_kernel,
        out_shape=jax.ShapeDtypeStruct((N_P, HB), BF16),
        grid=(BATCH, nc),
        in_specs=[pl.BlockSpec((CHUNK * BAND_CHUNKS, HB), lambda b, c: (b * nc + c, 0)),
                  pl.BlockSpec((None, SEQ + WINDOW_B, HB), lambda b, c: (b, 0, 0)),
                  pl.BlockSpec((None, SEQ + WINDOW_B, HB), lambda b, c: (b, 0, 0)),
                  _const(bias.shape)],
        out_specs=pl.BlockSpec((CHUNK * BAND_CHUNKS, HB), lambda b, c: (b * nc + c, 0)),
        compiler_params=_params(2),
        name="band_prompt",
    )(qb16, kpad, vpad, bias)


def _band_sample_kernel(q_ref, ck_ref, cv_ref, nk16_ref, nv16_ref, nk32_ref, nv32_ref, bias_c_ref, bias_n_ref,
                        o_ref, bk_ref, bv_ref):
    ck = ck_ref[...]
    cv = cv_ref[...]
    blocks = [(ck.astype(BF16), cv.astype(BF16)), (nk16_ref[...], nv16_ref[...])]
    o_ref[...] = _band_heads(q_ref[...], blocks, [bias_c_ref, bias_n_ref], None)
    keep = WINDOW_B - DEC_SEQ
    bk_ref[0:keep, :] = ck[DEC_SEQ:, :]
    bk_ref[keep:, :] = nk32_ref[...]
    bv_ref[0:keep, :] = cv[DEC_SEQ:, :]
    bv_ref[keep:, :] = nv32_ref[...]


def _band_sample(qb16, cache_k, cache_v, kb16, vb16, kb32, vb32, bias_c, bias_n):
    off = N_P // DEC_SEQ
    new = lambda: pl.BlockSpec((DEC_SEQ, HB), lambda b: (off + b, 0))
    cache = lambda: pl.BlockSpec((None, WINDOW_B, HB), lambda b: (b, 0, 0))
    return pl.pallas_call(
        _band_sample_kernel,
        out_shape=[jax.ShapeDtypeStruct((N_S, HB), BF16),
                   jax.ShapeDtypeStruct((DEC_BATCH, WINDOW_B, HB), F32),
                   jax.ShapeDtypeStruct((DEC_BATCH, WINDOW_B, HB), F32)],
        grid=(DEC_BATCH,),
        in_specs=[new(), cache(), cache(), new(), new(), new(), new(),
                  _const(bias_c.shape), _const(bias_n.shape)],
        out_specs=[pl.BlockSpec((DEC_SEQ, HB), lambda b: (b, 0)), cache(), cache()],
        compiler_params=_params(),
        name="band_sample",
    )(qb16, cache_k, cache_v, kb16, vb16, kb32, vb32, bias_c, bias_n)


def _mix_out_kernel(oa_ref, ob_ref, sga_ref, sgb_ref, xp_ref, xs_ref, wa_ref, wb_ref, wout_ref, gffn_ref,
                    x1_ref, h2_ref, h16_ref):
    merged = sga_ref[...] * _dot(oa_ref[...], wa_ref[...]) + sgb_ref[...] * _dot(ob_ref[...], wb_ref[...])
    x1 = _group_rows(xp_ref, xs_ref) + _dot(merged.astype(BF16), wout_ref[...])
    x1_ref[...] = x1
    h2 = _rms(x1, gffn_ref[...])
    h2_ref[...] = h2
    h16_ref[...] = h2.astype(BF16)


def _mix_out(oa, ob, sga, sgb, xp, xs, wa, wb, wout, gffn):
    n = N_TOK
    return pl.pallas_call(
        _mix_out_kernel,
        out_shape=[jax.ShapeDtypeStruct((n, D_MODEL), F32), jax.ShapeDtypeStruct((n, D_MODEL), F32),
                   jax.ShapeDtypeStruct((n, D_MODEL), BF16)],
        grid=(n // TM,),
        in_specs=[_rows(H_A * V_A), _rows(HB), _rows(D_MODEL), _rows(D_MODEL),
                  _rows_prompt(D_MODEL), _rows_sample(D_MODEL),
                  _const(wa.shape), _const(wb.shape), _const(wout.shape), _const(gffn.shape)],
        out_specs=[_rows(D_MODEL)] * 3,
        compiler_params=_params(),
        name="mix_out",
    )(oa, ob, sga, sgb, xp, xs, wa, wb, wout, gffn)


def _top16(s, n_rows):
    row = lax.broadcasted_iota(jnp.int32, (n_rows, TQ), 0).astype(F32)
    vals, ids = [], []
    for _ in range(TOPK):
        m = jnp.max(s, axis=0, keepdims=True)
        i = jnp.min(jnp.where(s == m, row, float(n_rows)), axis=0, keepdims=True)
        vals.append(m)
        ids.append(i)
        s = jnp.where(row == i, -jnp.inf, s)
    return jnp.concatenate(vals, axis=0), jnp.concatenate(ids, axis=0)


def _pair_candidates(f1, f2):
    h = SUBLANES
    blocks = [f1(0, 1, 0, h), f1(0, 1, h, 2 * h)]
    blocks += [f1(a, a + 1, 0, h) for a in range(1, h)]
    blocks += [f2(h, 2 * h, 0, 1)]
    return jnp.concatenate(blocks, axis=0)


def _peer_topk_kernel(h_ref, wqt_ref, keys_ref, idx_ref, g_ref, qt_ref, g_t, id_t):
    qt_ref[...] = lax.dot_general(wqt_ref[...], h_ref[...], NT, preferred_element_type=F32).astype(BF16)
    r8 = lax.broadcasted_iota(jnp.int32, (SUBLANES, TQ), 0).astype(F32)
    flat = lambda a0, a1, b0, b1: (r8 + float(b0)) + float(TOPK * a0)
    flat_t = lambda a0, a1, b0, b1: (r8 + float(a0)) * float(TOPK) + float(b0)
    cflat = _pair_candidates(flat, flat_t)

    def one_head(hd):
        tops = []
        for p in range(2):
            hp = hd * 2 + p
            qs = qt_ref[pl.ds(pl.multiple_of(hp * N_KEYS, N_KEYS), N_KEYS), :]
            tops.append(_top16(_dot(keys_ref[hp], qs), N_KEYS))
        (s1, i1), (s2, i2) = tops
        add = lambda x, y: (lambda a0, a1, b0, b1: x[a0:a1, :] + y[b0:b1, :])
        cand = _pair_candidates(add(s1, s2), add(s1, s2))
        e1 = i1 * float(N_KEYS)
        cidx = _pair_candidates(add(e1, i2), add(e1, i2))
        vals, ids = [], []
        for _ in range(TOPK):
            m = jnp.max(cand, axis=0, keepdims=True)
            c = jnp.min(jnp.where(cand == m, cflat, float(TOPK * TOPK)), axis=0, keepdims=True)
            sel = cflat == c
            vals.append(m)
            ids.append(jnp.sum(jnp.where(sel, cidx, 0.0), axis=0, keepdims=True))
            cand = jnp.where(sel, -jnp.inf, cand)
        best = jnp.concatenate(vals, axis=0)
        e = jnp.exp(best - best[0:1, :])
        g = e / jnp.sum(e, axis=0, keepdims=True)
        base = pl.multiple_of(hd * TOPK, TOPK)
        g_t[pl.ds(base, TOPK), :] = g
        id_t[pl.ds(base, TOPK), :] = jnp.concatenate(ids, axis=0)

    def head_group(grp, carry):
        for e in range(HEADS_PER_TRIP):
            one_head(grp * HEADS_PER_TRIP + e)
        return carry

    lax.fori_loop(0, PEER_HEADS // HEADS_PER_TRIP, head_group, 0)
    g_ref[...] = g_t[...].T
    idx_ref[...] = (id_t[...].T * float(ROW_WORDS)).astype(jnp.int32)


def _peer_topk(h16, wqt, keys, first, count):
    n = count
    nt = n // TQ
    fb = first // TQ
    return pl.pallas_call(
        _peer_topk_kernel,
        out_shape=[jax.ShapeDtypeStruct((n, SLOTS), jnp.int32),
                   jax.ShapeDtypeStruct((n, SLOTS), F32)],
        grid=(nt,),
        in_specs=[pl.BlockSpec((TQ, D_MODEL), lambda i: (fb + i, 0)), _const(wqt.shape), _const(keys.shape)],
        out_specs=[_rows(SLOTS, TQ)] * 2,
        scratch_shapes=[pltpu.VMEM((2 * PEER_HEADS * N_KEYS, TQ), BF16),
                        pltpu.VMEM((SLOTS, TQ), F32), pltpu.VMEM((SLOTS, TQ), F32)],
        compiler_params=_params(),
        name="peer_topk",
    )(h16, wqt, keys)


def _pack_table(t):
    b = lax.bitcast_convert_type(t.astype(BF16), jnp.uint16).astype(jnp.uint32)
    packed = (b[:, :HALF] << 16) | b[:, HALF:]
    return packed.reshape(N_EXPERTS * ROW_WORDS, LANES)


def _unpack(w):
    hi = lax.bitcast_convert_type(w & jnp.uint32(0xFFFF0000), F32)
    lo = lax.bitcast_convert_type(w << 16, F32)
    return hi, lo


def _slot_pairs(idx_ref, t):
    out = []
    for grp in range(PAIRS // IDX_GROUP):
        lo = idx_ref.at[0, pl.ds(t * SLOTS + grp * IDX_GROUP, IDX_GROUP)]
        hi = idx_ref.at[0, pl.ds(t * SLOTS + PAIRS + grp * IDX_GROUP, IDX_GROUP)]
        out.extend((lo[k], hi[k]) for k in range(IDX_GROUP))
    return out


def _load_pair(tab, ia, ib):
    wa = tab[pl.ds(pl.multiple_of(ia, ROW_WORDS), ROW_WORDS), :]
    wb = tab[pl.ds(pl.multiple_of(ib, ROW_WORDS), ROW_WORDS), :]
    return _unpack(jnp.concatenate([wa, wb], axis=0))


def _peer_u_kernel(idx_ref, x_ref, g_ref, tab, c_ref, prod_a, prod_b, rbuf, a_t):
    lane = lax.broadcasted_iota(jnp.int32, (SLOTS, LANES), 1)

    @pl.when(pl.program_id(0) == 0)
    def _():
        rbuf[...] = jnp.zeros_like(rbuf)
        a_t[...] = jnp.zeros_like(a_t)

    def reduce(slot, tok):
        col = jnp.sum(rbuf[slot], axis=1, keepdims=True)
        a_t[...] = jnp.where(lane == tok, col, a_t[...])

    def gather(t, slot, prod):
        xt = x_ref[t]
        x_hi = jnp.concatenate([xt[0:ROW_WORDS, :]] * 2, axis=0)
        x_lo = jnp.concatenate([xt[ROW_WORDS:, :]] * 2, axis=0)
        for j, (ia, ib) in enumerate(_slot_pairs(idx_ref, t)):
            hi, lo = _load_pair(tab, ia, ib)
            prod[pl.ds(j, SUBLANES, stride=STRIDE), :] = hi * x_hi + lo * x_lo
        halves = [functools.reduce(jnp.add, [prod[pl.ds((h * ROW_WORDS + s) * STRIDE, PAIRS), :]
                                             for s in range(ROW_WORDS)]) for h in range(2)]
        rbuf[slot] = jnp.concatenate(halves, axis=0)

    def two_tokens(p, carry):
        t0 = 2 * p
        reduce(0, t0 - 2)
        reduce(1, t0 - 1)
        gather(t0, 0, prod_a)
        gather(t0 + 1, 1, prod_b)
        return carry

    lax.fori_loop(0, TB // 2, two_tokens, 0)
    reduce(0, TB - 2)
    reduce(1, TB - 1)
    a = a_t[...].T[0:TB, :]
    gelu = 0.5 * a * (1.0 + lax.erf(a * (2.0 ** -0.5)))
    c_ref[...] = g_ref[...] * gelu


def _peer_u(idx, x, g, tab_u, first):
    n = x.shape[0]
    count = idx.shape[0]
    nb = count // TB
    fb = first // TB
    return pl.pallas_call(
        _peer_u_kernel,
        out_shape=jax.ShapeDtypeStruct((count, SLOTS), F32),
        grid=(nb,),
        in_specs=[pl.BlockSpec((None, 1, TB * SLOTS), lambda i: (i, 0, 0), memory_space=pltpu.SMEM),
                  pl.BlockSpec((TB, SUBLANES, LANES), lambda i: (fb + i, 0, 0)),
                  pl.BlockSpec((TB, SLOTS), lambda i: (i, 0)),
                  _const(tab_u.shape)],
        out_specs=pl.BlockSpec((TB, SLOTS), lambda i: (i, 0)),
        scratch_shapes=[pltpu.VMEM((SUBLANES * STRIDE, LANES), F32),
                        pltpu.VMEM((SUBLANES * STRIDE, LANES), F32),
                        pltpu.VMEM((2, SLOTS, LANES), F32),
                        pltpu.VMEM((SLOTS, LANES), F32)],
        compiler_params=_params(),
        name="peer_u",
    )(idx.reshape(nb, 1, TB * SLOTS), x.reshape(n, SUBLANES, LANES), g, tab_u)


def _peer_v_kernel(idx_ref, c_ref, tab, y_ref, *crep):
    row = lax.broadcasted_iota(jnp.int32, (SUBLANES, LANES), 0)
    eye = (lax.broadcasted_iota(jnp.int32, (SLOTS, LANES), 0)
           == lax.broadcasted_iota(jnp.int32, (SLOTS, LANES), 1))

    def spread(t, crep):
        col = jnp.sum(jnp.where(eye, c_ref[pl.ds(t, 1), :], 0.0), axis=1, keepdims=True)
        crep[...] = jnp.broadcast_to(col, (SLOTS, LANES))

    def token(t, crep):
        acc_hi = jnp.zeros((SUBLANES, LANES), F32)
        acc_lo = jnp.zeros((SUBLANES, LANES), F32)
        for j, (ia, ib) in enumerate(_slot_pairs(idx_ref, t)):
            hi, lo = _load_pair(tab, ia, ib)
            cm = jnp.where(row < ROW_WORDS, crep[pl.ds(j, 1), :], crep[pl.ds(PAIRS + j, 1), :])
            acc_hi = acc_hi + cm * hi
            acc_lo = acc_lo + cm * lo
        y_ref[t] = jnp.concatenate([acc_hi[0:ROW_WORDS, :] + acc_hi[ROW_WORDS:, :],
                                    acc_lo[0:ROW_WORDS, :] + acc_lo[ROW_WORDS:, :]], axis=0)

    ahead = CREP_RING // 2
    for k in range(ahead):
        spread(k, crep[k])

    def ring_trip(p, carry):
        t0 = CREP_RING * p
        for k in range(CREP_RING):
            spread(jnp.minimum(t0 + k + ahead, TB - 1), crep[(k + ahead) % CREP_RING])
            token(t0 + k, crep[k])
        return carry

    lax.fori_loop(0, TB // CREP_RING, ring_trip, 0)


def _peer_v(idx, c, tab_v):
    n = c.shape[0]
    nb = n // TB
    y = pl.pallas_call(
        _peer_v_kernel,
        out_shape=jax.ShapeDtypeStruct((n, SUBLANES, LANES), F32),
        grid=(nb,),
        in_specs=[pl.BlockSpec((None, 1, TB * SLOTS), lambda i: (i, 0, 0), memory_space=pltpu.SMEM),
                  pl.BlockSpec((TB, SLOTS), lambda i: (i, 0)),
                  _const(tab_v.shape)],
        out_specs=pl.BlockSpec((TB, SUBLANES, LANES), lambda i: (i, 0, 0)),
        scratch_shapes=[pltpu.VMEM((SLOTS, LANES), F32)] * CREP_RING,
        compiler_params=_params(),
        name="peer_v",
    )(idx.reshape(nb, 1, TB * SLOTS), c, tab_v)
    return y.reshape(n, D_MODEL)


def _sc_peer_v(idx, c, table):
    n = idx.shape[0]
    per_worker = n // SC_WORKERS
    batches = per_worker // SC_TOKENS
    groups = SC_TOKENS * SLOTS // SC_ROWS
    chunks = D_MODEL // SC_LANES
    mesh = plsc.VectorSubcoreMesh(core_axis_name="c", subcore_axis_name="s")

    def body(idx_hbm, c_hbm, tab_hbm, y_hbm, idx_v, c_v, rows_a, rows_b, y_v, sem_a, sem_b):
        wid = lax.axis_index("s") * 2 + lax.axis_index("c")
        zero = jnp.zeros((SC_LANES,), F32)

        def gather(g, rows, sem):
            return pltpu.make_async_copy(tab_hbm.at[idx_v.at[pl.ds(g * SC_ROWS, SC_ROWS)]], rows, sem)

        def accumulate(g, rows):
            tok = g // (SLOTS // SC_ROWS)
            ws = [plsc.load_gather(c_v, [jnp.full((SC_LANES,), g * SC_ROWS + r, jnp.int32)])
                  for r in range(SC_ROWS)]

            @pl.loop(0, chunks // SC_ACC)
            def _(cb):
                base = pl.multiple_of(cb * (SC_ACC * SC_LANES), SC_ACC * SC_LANES)
                acc = [zero] * SC_ACC
                for r in range(SC_ROWS):
                    for k in range(SC_ACC):
                        acc[k] = acc[k] + ws[r] * rows[r, pl.ds(base + k * SC_LANES, SC_LANES)]
                for k in range(SC_ACC):
                    plsc.addupdate(y_v.at[tok, pl.ds(base + k * SC_LANES, SC_LANES)], acc[k])

        @pl.loop(0, batches)
        def _(b):
            t0 = wid * per_worker + b * SC_TOKENS
            pltpu.sync_copy(idx_hbm.at[pl.ds(t0 * SLOTS, SC_TOKENS * SLOTS)], idx_v)
            pltpu.sync_copy(c_hbm.at[pl.ds(t0 * SLOTS, SC_TOKENS * SLOTS)], c_v)
            for t in range(SC_TOKENS):
                for ch in range(chunks):
                    y_v[t, pl.ds(ch * SC_LANES, SC_LANES)] = zero
            gather(0, rows_a, sem_a).start()

            @pl.loop(0, groups // 2)
            def _(h):
                g = 2 * h
                gather(g + 1, rows_b, sem_b).start()
                gather(g, rows_a, sem_a).wait()
                accumulate(g, rows_a)

                @pl.when(h + 1 < groups // 2)
                def _():
                    gather(g + 2, rows_a, sem_a).start()
                gather(g + 1, rows_b, sem_b).wait()
                accumulate(g + 1, rows_b)

            pltpu.sync_copy(y_v, y_hbm.at[pl.ds(t0, SC_TOKENS)])

    return pl.kernel(
        body,
        out_type=jax.ShapeDtypeStruct((n, D_MODEL), F32),
        mesh=mesh,
        scratch_types=[pltpu.VMEM((SC_TOKENS * SLOTS,), jnp.int32),
                       pltpu.VMEM((SC_TOKENS * SLOTS,), F32),
                       pltpu.VMEM((SC_ROWS, D_MODEL), F32),
                       pltpu.VMEM((SC_ROWS, D_MODEL), F32),
                       pltpu.VMEM((SC_TOKENS, D_MODEL), F32),
                       pltpu.SemaphoreType.DMA,
                       pltpu.SemaphoreType.DMA],
        compiler_params=pltpu.CompilerParams(needs_layout_passes=False),
        name="sc_peer_v",
    )(idx.reshape(n * SLOTS), c.reshape(n * SLOTS), table)


def _final_kernel(x1_ref, yh_ref, yt_ref, pp_ref, ps_ref, gple_ref, wg_ref, wp_ref, gfin_ref, op_ref, os_ref):
    x2 = x1_ref[...] + _split_rows(yh_ref, yt_ref, N_SC // TM)
    gate = jax.nn.sigmoid(_dot(_rms(x2, gple_ref[...]).astype(BF16), wg_ref[...]))
    x3 = x2 + gate * _dot(_group_rows(pp_ref, ps_ref).astype(BF16), wp_ref[...])
    out = _rms(x3, gfin_ref[...])
    is_prompt = pl.program_id(0) < P_TILES

    @pl.when(is_prompt)
    def _():
        op_ref[...] = out

    @pl.when(jnp.logical_not(is_prompt))
    def _():
        os_ref[...] = out


def _final(x1, y_head, y_tail, pp, ps, gple, wg, wp, gfin):
    return pl.pallas_call(
        _final_kernel,
        out_shape=[jax.ShapeDtypeStruct((N_P, D_MODEL), F32), jax.ShapeDtypeStruct((N_S, D_MODEL), F32)],
        grid=(N_TOK // TM,),
        in_specs=[_rows(D_MODEL), _rows_head(D_MODEL, N_SC // TM), _rows_tail(D_MODEL, N_SC // TM),
                  _rows_prompt(PLE_DIM), _rows_sample(PLE_DIM),
                  _const(gple.shape), _const(wg.shape), _const(wp.shape), _const(gfin.shape)],
        out_specs=[_rows_prompt(D_MODEL), _rows_sample(D_MODEL)],
        compiler_params=_params(),
        name="ple_final",
    )(x1, y_head, y_tail, pp, ps, gple, wg, wp, gfin)


def _rope_tables():
    half = ROPE // 2
    freqs = ROPE_THETA ** (-np.arange(half, dtype=np.float32) / half)
    pos = np.concatenate([np.arange(SEQ), np.tile(PAST_LEN + np.arange(DEC_SEQ), TM // DEC_SEQ)])
    ang = jnp.asarray(pos, F32)[:, None] * jnp.asarray(freqs, F32)[None, :]
    cos, sin = jnp.cos(ang), jnp.sin(ang)
    n = pos.shape[0]
    pad = jnp.zeros((n, LANES - NOPE - ROPE), F32)
    cos_t = jnp.concatenate([jnp.ones((n, NOPE), F32), cos, cos, pad], axis=1)
    sin_t = jnp.concatenate([jnp.zeros((n, NOPE), F32), sin, sin, pad], axis=1)
    return cos_t, sin_t


def _rot_cols(w):
    half = ROPE // 2
    return jnp.concatenate([-w[..., half:], w[..., :half]], axis=-1)


def _rel_bias(table, n_q, n_k):
    diag = np.arange(-(n_q - 1), n_k)
    line = table[:, np.clip(WINDOW_B - diag, -REL_CLIP, REL_CLIP) + REL_CLIP]
    return jnp.stack([line[:, n_q - 1 - i:n_q - 1 - i + n_k] for i in range(n_q)], axis=1)


def kernel(x_prompt, x_sample, cache_latent, cache_krope, cache_band_k, cache_band_v, p_prompt, p_sample, g_mix, w_in, g_q_lora, g_kv_lora, w_uq, w_uk, w_uv, rel_bias, w_a_proj, w_b_proj, w_out, g_ffn, w_query, sub_keys, expert_u, expert_v, g_ple, w_ple_gate, w_ple_proj, g_final):
    w = w_in[0]
    o_kr = Q_LORA + KV_LORA
    o_qb = o_kr + ROPE
    o_ga = o_qb + 3 * HB
    w_kr = w[:, o_kr:o_qb]
    z64 = jnp.zeros((D_MODEL, NOPE), F32)
    z32 = jnp.zeros((D_MODEL, LANES - NOPE - ROPE), F32)
    w1 = jnp.concatenate([w[:, :o_kr], z64, w_kr, z32, z64, _rot_cols(w_kr), z32], axis=1).astype(BF16)
    w2 = w[:, o_qb:o_ga].astype(BF16)
    w3 = w[:, o_ga:].astype(BF16)
    wq = w_uq[0].reshape(Q_LORA, H_A, NOPE + ROPE)
    zq64 = jnp.zeros((Q_LORA, H_A, NOPE), F32)
    zq32 = jnp.zeros((Q_LORA, H_A, LANES - NOPE - ROPE), F32)
    wuq1 = jnp.concatenate([wq, zq32], axis=-1).reshape(Q_LORA, H_A * LANES).astype(BF16)
    wuq2 = jnp.concatenate([zq64, _rot_cols(wq[..., NOPE:]), zq32], axis=-1).reshape(Q_LORA, H_A * LANES).astype(BF16)
    wukp = jnp.concatenate([w_uk[0], jnp.zeros((KV_LORA, H_A, LANES - NOPE), F32)], axis=-1)
    wukp = wukp.reshape(KV_LORA, H_A * LANES).astype(BF16)
    wuv = w_uv[0].reshape(KV_LORA, H_A * V_A).astype(BF16)
    wukt = jnp.transpose(w_uk[0], (1, 2, 0)).astype(BF16)
    wuvh = jnp.transpose(w_uv[0], (1, 0, 2)).astype(BF16)
    cos_t, sin_t = _rope_tables()
    row = lambda g: g.reshape(1, -1)

    xp = x_prompt.reshape(N_P, D_MODEL)
    xs = x_sample.reshape(N_S, D_MODEL)

    (lat, kr, q16, k16, v16, qb16, kb16, vb16, kb32, vb32, sga, sgb) = _inproj(
        xp, xs, row(g_mix[0]), w1, w2, w3, row(g_q_lora[0]), row(g_kv_lora[0]), wuq1, wuq2, wukp, wuv, cos_t, sin_t)

    oa_p = _mla_prompt(q16, k16, v16)
    oa_s = _mla_sample(q16, cache_latent[0], cache_krope[0], lat, kr, wukt, wuvh)

    pad = ((0, 0), (WINDOW_B, 0), (0, 0))
    kpad = jnp.pad(kb16[:N_P].reshape(BATCH, SEQ, HB), pad)
    vpad = jnp.pad(vb16[:N_P].reshape(BATCH, SEQ, HB), pad)
    tab = rel_bias[0]
    bias_p = _rel_bias(tab, CHUNK, BAND)
    bias_s = _rel_bias(tab, DEC_SEQ, WINDOW_B + DEC_SEQ)
    ob_p = _band_prompt(qb16, kpad, vpad, bias_p)
    ob_s, bk_s, bv_s = _band_sample(qb16, cache_band_k[0].reshape(DEC_BATCH, WINDOW_B, HB),
                                    cache_band_v[0].reshape(DEC_BATCH, WINDOW_B, HB),
                                    kb16, vb16, kb32, vb32, bias_s[:, :, :WINDOW_B], bias_s[:, :, WINDOW_B:])

    oa = jnp.concatenate([oa_p, oa_s], axis=0)
    ob = jnp.concatenate([ob_p, ob_s], axis=0)
    x1, h2, h16 = _mix_out(oa, ob, sga, sgb, xp, xs, w_a_proj[0].astype(BF16), w_b_proj[0].astype(BF16),
                           w_out[0].astype(BF16), row(g_ffn[0]))
    wqt = jnp.transpose(w_query[0]).astype(BF16)
    keys = sub_keys[0].reshape(2 * PEER_HEADS, N_KEYS, N_KEYS).astype(BF16)

    tab_u = _pack_table(expert_u[0])
    tab_v = _pack_table(expert_v[0])
    idx_sc, gw_sc = _peer_topk(h16, wqt, keys, 0, N_SC)
    c_sc = _peer_u(idx_sc, h2, gw_sc, tab_u, 0)
    y_sc = _sc_peer_v(idx_sc // ROW_WORDS, c_sc, expert_v[0])
    idx_tc, gw_tc = _peer_topk(h16, wqt, keys, N_SC, N_TOK - N_SC)
    c_tc = _peer_u(idx_tc, h2, gw_tc, tab_u, N_SC)
    y_tc = _peer_v(idx_tc, c_tc, tab_v)

    out_p, out_s = _final(x1, y_sc, y_tc, p_prompt.reshape(N_P, PLE_DIM), p_sample.reshape(N_S, PLE_DIM), row(g_ple[0]),
                          w_ple_gate[0].astype(BF16), w_ple_proj[0].astype(BF16), row(g_final))

    y_prompt = out_p.reshape(BATCH, SEQ, D_MODEL)
    y_sample = out_s.reshape(DEC_BATCH, DEC_SEQ, D_MODEL)
    lat_p = lat[:N_P].reshape(1, BATCH, SEQ, KV_LORA)
    kr_p = kr[:N_P].reshape(1, BATCH, SEQ, ROPE)
    bk_p = kb32[:N_P].reshape(BATCH, SEQ, H_B, DH_B)[None, :, SEQ - WINDOW_B:]
    bv_p = vb32[:N_P].reshape(BATCH, SEQ, H_B, DH_B)[None, :, SEQ - WINDOW_B:]
    lat_s = lat[N_P:].reshape(1, DEC_BATCH, DEC_SEQ, KV_LORA)
    kr_s = kr[N_P:].reshape(1, DEC_BATCH, DEC_SEQ, ROPE)
    bk_s = bk_s.reshape(1, DEC_BATCH, WINDOW_B, H_B, DH_B)
    bv_s = bv_s.reshape(1, DEC_BATCH, WINDOW_B, H_B, DH_B)
    return (y_prompt, y_sample, lat_p, kr_p, bk_p, bv_p, lat_s, kr_s, bk_s, bv_s)
```
